```python
import math
import jax, jax.numpy as jnp
from jax import lax
import numpy as np

D_MODEL = 1024
BATCH = 1
SEQ = 16384
DEPTH = 2

GRID_W = 64
CTX_LEN = 256
D_SSM = 512
D_FNET = 512
SSM_GROUP = 16
N_SSM_GROUPS = D_SSM // SSM_GROUP
STATE = 64
N_FNET_HEADS = 4
FNET_HEAD = D_FNET // N_FNET_HEADS
D_FF = 2816
N_MOD = 9
EPS = 1e-6
DT_MIN = 1e-3
DT_MAX = 1e-1
POS_BASE = 10000.0

kernel_name = "hybrid_s5_fnet_macaron_adaln_block"


def _rmsnorm(x, g):
    xf = x.astype(jnp.float32)
    y = xf * lax.rsqrt(jnp.mean(xf * xf, axis=-1, keepdims=True) + EPS)
    return (y * g.astype(jnp.float32)).astype(x.dtype)


def _pre(x, g, mod, k):
    return _rmsnorm(x, g) * (1.0 + mod[:, 3 * k + 1][:, None]) + mod[:, 3 * k][:, None]


def _post(x, y, g, mod, k, weight):
    return x + weight * mod[:, 3 * k + 2][:, None] * _rmsnorm(y, g)


def _swiglu(h, w_gu, w_down):
    gate, up = jnp.split(h @ w_gu, 2, axis=-1)
    return (jax.nn.silu(gate) * up) @ w_down


def _pos_2d(length, dtype):
    rows = length // GRID_W
    row = jnp.broadcast_to(jnp.arange(rows, dtype=jnp.float32)[:, None], (rows, GRID_W)).reshape(-1)
    col = jnp.broadcast_to(jnp.arange(GRID_W, dtype=jnp.float32)[None, :], (rows, GRID_W)).reshape(-1)
    quarter = D_MODEL // 4
    omega = 1.0 / (POS_BASE ** (jnp.arange(quarter, dtype=jnp.float32) / quarter))

    def enc(p):
        ang = p[:, None] * omega[None, :]
        return jnp.concatenate([jnp.sin(ang), jnp.cos(ang)], axis=-1)

    return jnp.concatenate([enc(row), enc(col)], axis=-1).astype(dtype)


def _scan_op(e1, e2):
    a1, b1 = e1
    a2, b2 = e2
    return a1 * a2, a2 * b1 + b2


def _discretize(lam_re, lam_im, log_dt, b_re, b_im, c_re, c_im):
    f32 = jnp.float32
    lam = lax.complex(lam_re.astype(f32), lam_im.astype(f32))
    dt = jnp.exp(log_dt.astype(f32))[:, None]
    lam_bar = jnp.exp(lam * dt)
    b = lax.complex(b_re.astype(f32), b_im.astype(f32))
    b_bar = ((lam_bar - 1.0) / lam)[..., None] * b
    c_mat = lax.complex(c_re.astype(f32), c_im.astype(f32))
    return lam_bar, b_bar, c_mat


def _s5_states(u, lam_bar, b_bar, s0):
    bu = jnp.einsum('gph,blgh->blgp', b_bar, u.astype(jnp.complex64))
    if s0 is not None:
        bu = bu.at[:, 0].add(lam_bar * s0)
    a = jnp.broadcast_to(lam_bar, bu.shape)
    _, xs = lax.associative_scan(_scan_op, (a, bu), axis=1)
    return xs


def _readout(c_mat, xs):
    return jnp.einsum('ghp,blgp->blgh', c_mat, xs).real


def _glu(y, w_glu):
    h = jax.nn.gelu(y)
    return h * jax.nn.sigmoid(h @ w_glu)


def _s5_mixer(u_lat, u_ctx, lam_re, lam_im, log_dt, b_re, b_im, c_re, c_im, d_skip, w_glu, need_ctx):
    bl, L, _ = u_lat.shape
    bc, Lc, _ = u_ctx.shape
    ul = u_lat.reshape(bl, L, N_SSM_GROUPS, SSM_GROUP)
    uc = u_ctx.reshape(bc, Lc, N_SSM_GROUPS, SSM_GROUP)
    y_lat = d_skip * u_lat
    y_ctx = d_skip * u_ctx if need_ctx else None
    for direction in range(2):
        lam_bar, b_bar, c_mat = _discretize(lam_re[direction], lam_im[direction], log_dt[direction],
                                            b_re[direction], b_im[direction], c_re[direction], c_im[direction])
        rev = direction == 1
        uc_d = jnp.flip(uc, axis=1) if rev else uc
        ul_d = jnp.flip(ul, axis=1) if rev else ul
        xs_c = _s5_states(uc_d, lam_bar, b_bar, None)
        xs_l = _s5_states(ul_d, lam_bar, b_bar, xs_c[:, -1])
        yl = _readout(c_mat, xs_l)
        if rev:
            yl = jnp.flip(yl, axis=1)
        y_lat = y_lat + yl.reshape(bl, L, D_SSM).astype(u_lat.dtype)
        if need_ctx:
            yc = _readout(c_mat, xs_c)
            if rev:
                yc = jnp.flip(yc, axis=1)
            y_ctx = y_ctx + yc.reshape(bc, Lc, D_SSM).astype(u_ctx.dtype)
    out_lat = _glu(y_lat, w_glu)
    out_ctx = _glu(y_ctx, w_glu) if need_ctx else None
    return out_lat, out_ctx


def _fnet(h, w_fmix):
    b, L, _ = h.shape
    hf = h.astype(jnp.float32).reshape(b, L, N_FNET_HEADS, FNET_HEAD)
    f = jnp.fft.fft2(hf, axes=(1, 3), norm="ortho").real.astype(h.dtype)
    return jnp.einsum('blnc,ncd->blnd', f, w_fmix).reshape(b, L, D_FNET)


def setup_inputs(seed: int = 0) -> dict:
    key = jax.random.key(seed)
    ks = jax.random.split(key, 24)
    f32 = jnp.float32
    D = D_MODEL
    G, P, H = N_SSM_GROUPS, STATE, SSM_GROUP
    nrm = lambda k, shape, s: jax.random.normal(k, shape, f32) * s
    n_idx = jnp.arange(P, dtype=f32)
    return {
        "x": nrm(ks[0], (BATCH, SEQ, D), 1.0),
        "c": nrm(ks[1], (BATCH, D), 1.0),
        "ctx": nrm(ks[2], (BATCH, CTX_LEN, D), 1.0),
        "c_ctx": nrm(ks[3], (D,), 1.0),
        "w_ada": nrm(ks[4], (DEPTH, D, N_MOD * D), 0.5 * D ** -0.5),
        "b_ada": nrm(ks[5], (DEPTH, N_MOD * D), 0.01),
        "norm_g": 1.0 + nrm(ks[6], (DEPTH, 6, D), 0.02),
        "ffn1_gu": nrm(ks[7], (DEPTH, D, 2 * D_FF), D ** -0.5),
        "ffn1_down": nrm(ks[8], (DEPTH, D_FF, D), D_FF ** -0.5),
        "ffn2_gu": nrm(ks[9], (DEPTH, D, 2 * D_FF), D ** -0.5),
        "ffn2_down": nrm(ks[10], (DEPTH, D_FF, D), D_FF ** -0.5),
        "w_in": nrm(ks[11], (DEPTH, D, D_SSM + D_FNET), D ** -0.5),
        "w_out": nrm(ks[12], (DEPTH, D_SSM + D_FNET, D), (D_SSM + D_FNET) ** -0.5),
        "ssm_lam_re": -0.5 + nrm(ks[13], (DEPTH, 2, G, P), 0.01),
        "ssm_lam_im": math.pi * n_idx + nrm(ks[14], (DEPTH, 2, G, P), 0.01),
        "ssm_log_dt": jax.random.uniform(ks[15], (DEPTH, 2, G), f32, math.log(DT_MIN), math.log(DT_MAX)),
        "ssm_b_re": nrm(ks[16], (DEPTH, 2, G, P, H), (2 * H) ** -0.5),
        "ssm_b_im": nrm(ks[17], (DEPTH, 2, G, P, H), (2 * H) ** -0.5),
        "ssm_c_re": nrm(ks[18], (DEPTH, 2, G, H, P), 2 ** -0.5),
        "ssm_c_im": nrm(ks[19], (DEPTH, 2, G, H, P), 2 ** -0.5),
        "ssm_d": nrm(ks[20], (DEPTH, D_SSM), 1.0),
        "w_glu": nrm(ks[21], (DEPTH, D_SSM, D_SSM), D_SSM ** -0.5),
        "w_fmix": nrm(ks[22], (DEPTH, N_FNET_HEADS, FNET_HEAD, FNET_HEAD), FNET_HEAD ** -0.5),
    }


def reference(x, c, ctx, c_ctx, w_ada, b_ada, norm_g, ffn1_gu, ffn1_down, ffn2_gu, ffn2_down,
              w_in, w_out, ssm_lam_re, ssm_lam_im, ssm_log_dt, ssm_b_re, ssm_b_im, ssm_c_re, ssm_c_im,
              ssm_d, w_glu, w_fmix):
    b, L, D = x.shape
    x = x + _pos_2d(L, x.dtype)[None]
    xc = ctx
    for l in range(DEPTH):
        last = l == DEPTH - 1
        mod_lat = (jax.nn.silu(c) @ w_ada[l] + b_ada[l]).reshape(b, N_MOD, D)
        mod_ctx = (jax.nn.silu(c_ctx) @ w_ada[l] + b_ada[l]).reshape(1, N_MOD, D)
        g = norm_g[l]
        x = _post(x, _swiglu(_pre(x, g[0], mod_lat, 0), ffn1_gu[l], ffn1_down[l]), g[1], mod_lat, 0, 0.5)
        xc = _post(xc, _swiglu(_pre(xc, g[0], mod_ctx, 0), ffn1_gu[l], ffn1_down[l]), g[1], mod_ctx, 0, 0.5)
        h = _pre(x, g[2], mod_lat, 1) @ w_in[l]
        hc = _pre(xc, g[2], mod_ctx, 1) @ w_in[l]
        y_ssm, yc_ssm = _s5_mixer(h[..., :D_SSM], hc[..., :D_SSM], ssm_lam_re[l], ssm_lam_im[l],
                                  ssm_log_dt[l], ssm_b_re[l], ssm_b_im[l], ssm_c_re[l], ssm_c_im[l],
                                  ssm_d[l], w_glu[l], not last)
        y_f = _fnet(h[..., D_SSM:], w_fmix[l])
        y = jnp.concatenate([y_ssm, y_f], axis=-1) @ w_out[l]
        x = _post(x, y, g[3], mod_lat, 1, 1.0)
        if not last:
            yc_f = _fnet(hc[..., D_SSM:], w_fmix[l])
            yc = jnp.concatenate([yc_ssm, yc_f], axis=-1) @ w_out[l]
            xc = _post(xc, yc, g[3], mod_ctx, 1, 1.0)
        x = _post(x, _swiglu(_pre(x, g[4], mod_lat, 2), ffn2_gu[l], ffn2_down[l]), g[5], mod_lat, 2, 0.5)
        if not last:
            xc = _post(xc, _swiglu(_pre(xc, g[4], mod_ctx, 2), ffn2_gu[l], ffn2_down[l]), g[5], mod_ctx, 2, 0.5)
    return x
```

```python
import functools
import math

import numpy as np
import jax
import jax.numpy as jnp
from jax import lax
from jax.experimental import pallas as pl
from jax.experimental.pallas import tpu as pltpu

F32 = jnp.float32
BF16 = jnp.bfloat16

LANE = 128
SUBLANE = 8
VMEM_LIMIT = 56 * 1024 * 1024

EPS = 1e-6
GRID_W = 64
POS_BASE = 10000.0
N_MOD = 9
SSM_GROUP = 16
STATE = 64
N_FNET_HEADS = 4
CHUNK = 16
TM = 256


def _dot(a, b):
    return jnp.dot(a, b, preferred_element_type=F32)


def _params(sem=None):
    return pltpu.CompilerParams(dimension_semantics=sem, vmem_limit_bytes=VMEM_LIMIT)


def _const_spec(shape):
    nd = len(shape)
    return pl.BlockSpec(shape, lambda *_: (0,) * nd, pipeline_mode=pl.Buffered(1))


def _rms(x, g):
    ms = jnp.mean(x * x, axis=-1, keepdims=True)
    return x * lax.rsqrt(ms + EPS) * g


def _pre(x, g, mod_ref, k):
    return _rms(x, g) * (1.0 + mod_ref[0, 3 * k + 1:3 * k + 2, :]) + mod_ref[0, 3 * k:3 * k + 1, :]


def _post(x, y, g, mod_ref, k, weight):
    return x + (weight * mod_ref[0, 3 * k + 2:3 * k + 3, :]) * _rms(y, g)


def _sigmoid(x):
    return 1.0 / (1.0 + jnp.exp(-x))


def _gelu_tanh(x):
    return 0.5 * x * (1.0 + jnp.tanh(math.sqrt(2.0 / math.pi) * (x + 0.044715 * (x * x * x))))


def _mod_kernel(cb_ref, w_ref, b_ref, o_ref, *, d_model, tn):
    nb = tn // LANE

    def body(t, accs):
        d0 = pl.multiple_of(t * SUBLANE, SUBLANE)
        s = []
        for r in range(2):
            cv = cb_ref[r, pl.ds(d0, SUBLANE), :]
            s.append(cv * _sigmoid(cv))
        new = list(accs)
        for j in range(nb):
            w = w_ref[0, pl.ds(d0, SUBLANE), j * LANE:(j + 1) * LANE]
            for r in range(2):
                new[r * nb + j] = accs[r * nb + j] + w * s[r]
        return tuple(new)

    init = tuple(jnp.zeros((SUBLANE, LANE), F32) for _ in range(2 * nb))
    accs = lax.fori_loop(0, d_model // SUBLANE, body, init)
    for r in range(2):
        for j in range(nb):
            o_ref[0, r:r + 1, j * LANE:(j + 1) * LANE] = (
                jnp.sum(accs[r * nb + j], axis=0, keepdims=True) + b_ref[0, :, j * LANE:(j + 1) * LANE])


def _modulation(c_ctx, c, w_ada, b_ada):
    depth, d_model, n_out = w_ada.shape
    tn = 9 * LANE
    assert n_out % tn == 0
    cb = jnp.broadcast_to(jnp.stack([c_ctx, c[0]])[:, :, None], (2, d_model, LANE))
    out = pl.pallas_call(
        functools.partial(_mod_kernel, d_model=d_model, tn=tn),
        out_shape=jax.ShapeDtypeStruct((depth, 2, n_out), F32),
        grid=(depth, n_out // tn),
        in_specs=[
            pl.BlockSpec((2, d_model, LANE), lambda l, j: (0, 0, 0)),
            pl.BlockSpec((1, d_model, tn), lambda l, j: (l, 0, j)),
            pl.BlockSpec((1, 1, tn), lambda l, j: (l, 0, j)),
        ],
        out_specs=pl.BlockSpec((1, 2, tn), lambda l, j: (l, 0, j)),
        compiler_params=_params(("arbitrary", "arbitrary")),
        name="adaln_mod",
    )(cb, w_ada, b_ada.reshape(depth, 1, n_out))
    return out.reshape(depth, 2, N_MOD, d_model)


def _ffn_body(x, mod_ref, g_ref, wg_ref, wu_ref, wd_ref, k, gi):
    h = _pre(x, g_ref[gi:gi + 1, :], mod_ref, k).astype(BF16)
    gate = _dot(h, wg_ref[...])
    up = _dot(h, wu_ref[...])
    act = (gate * _sigmoid(gate) * up).astype(BF16)
    y = _dot(act, wd_ref[...])
    return _post(x, y, g_ref[gi + 1:gi + 2, :], mod_ref, k, 0.5)


def _ffn_kernel(x_ref, mod_ref, g_ref, wg_ref, wu_ref, wd_ref, o_ref, *, k, gi):
    o_ref[...] = _ffn_body(x_ref[...], mod_ref, g_ref, wg_ref, wu_ref, wd_ref, k, gi)


def _ffn_first_kernel(ctx_ref, x_ref, renc_ref, cenc_ref, mod_ref, g_ref, wg_ref, wu_ref, wd_ref,
                      o_ref, xs_ref, *, k, gi, n_ctx_tiles, tm):
    i = pl.program_id(0)
    half = renc_ref.shape[1]
    rows_per_tile = tm // GRID_W

    @pl.when(i < n_ctx_tiles)
    def _():
        xs_ref[...] = ctx_ref[...]

    @pl.when(i >= n_ctx_tiles)
    def _():
        r0 = (i - n_ctx_tiles) * rows_per_tile
        xs_ref[:, half:] = x_ref[:, half:] + cenc_ref[...]
        for q in range(rows_per_tile):
            xs_ref[q * GRID_W:(q + 1) * GRID_W, :half] = (
                x_ref[q * GRID_W:(q + 1) * GRID_W, :half] + renc_ref[pl.ds(r0 + q, 1), :])

    o_ref[...] = _ffn_body(xs_ref[...], mod_ref, g_ref, wg_ref, wu_ref, wd_ref, k, gi)


def _mod_spec(d_model, n_ctx_tiles, off):
    return pl.BlockSpec((1, N_MOD, d_model),
                        lambda i: (jnp.where(i + off >= n_ctx_tiles, 1, 0), 0, 0))


def _ffn(xa, mod, g, wg, wu, wd, *, k, gi, n_ctx_tiles, row_off):
    n, d = xa.shape
    f = wg.shape[1]
    nt = n // TM - row_off
    return pl.pallas_call(
        functools.partial(_ffn_kernel, k=k, gi=gi),
        out_shape=jax.ShapeDtypeStruct((nt * TM, d), F32),
        grid=(nt,),
        in_specs=[
            pl.BlockSpec((TM, d), lambda i: (i + row_off, 0)),
            _mod_spec(d, n_ctx_tiles, row_off),
            _const_spec(g.shape),
            _const_spec((d, f)), _const_spec((d, f)), _const_spec((f, d)),
        ],
        out_specs=pl.BlockSpec((TM, d), lambda i: (i, 0)),
        compiler_params=_params(("arbitrary",)),
        name=f"ffn_k{k}",
    )(xa, mod, g, wg, wu, wd)


def _ffn_first(ctx2, x2, renc, cenc, mod, g, wg, wu, wd, *, k, gi):
    lc, d = ctx2.shape
    l = x2.shape[0]
    f = wg.shape[1]
    n_ctx_tiles = lc // TM
    nt = (lc + l) // TM
    return pl.pallas_call(
        functools.partial(_ffn_first_kernel, k=k, gi=gi, n_ctx_tiles=n_ctx_tiles, tm=TM),
        out_shape=jax.ShapeDtypeStruct((lc + l, d), F32),
        grid=(nt,),
        in_specs=[
            pl.BlockSpec((TM, d), lambda i: (jnp.minimum(i, n_ctx_tiles - 1), 0)),
            pl.BlockSpec((TM, d), lambda i: (jnp.maximum(i - n_ctx_tiles, 0), 0)),
            _const_spec(renc.shape), _const_spec(cenc.shape),
            _mod_spec(d, n_ctx_tiles, 0),
            _const_spec(g.shape),
            _const_spec((d, f)), _const_spec((d, f)), _const_spec((f, d)),
        ],
        out_specs=pl.BlockSpec((TM, d), lambda i: (i, 0)),
        scratch_shapes=[pltpu.VMEM((TM, d), F32)],
        compiler_params=_params(("arbitrary",)),
        name="ffn_first",
    )(ctx2, x2, renc, cenc, mod, g, wg, wu, wd)


def _inproj_kernel(x_ref, mod_ref, g_ref, win_ref, cs_ref, u_ref, p_ref, q_ref, *, d_ssm, head):
    h = _pre(x_ref[...], g_ref[2:3, :], mod_ref, 1).astype(BF16)
    hh = _dot(h, win_ref[...])
    u_ref[...] = hh[:, :d_ssm].astype(BF16)
    for n in range(N_FNET_HEADS):
        lo = d_ssm + n * head
        pq = _dot(hh[:, lo:lo + head].astype(BF16), cs_ref[...])
        p_ref[:, n * head:(n + 1) * head] = pq[:, :head].astype(BF16)
        q_ref[:, n * head:(n + 1) * head] = pq[:, head:].astype(BF16)


def _inproj(xa, mod, g, win, cs, *, n_ctx_tiles, d_ssm):
    n, d = xa.shape
    d_f = win.shape[1] - d_ssm
    head = d_f // N_FNET_HEADS
    nt = n // TM
    return pl.pallas_call(
        functools.partial(_inproj_kernel, d_ssm=d_ssm, head=head),
        out_shape=(jax.ShapeDtypeStruct((n, d_ssm), BF16),
                   jax.ShapeDtypeStruct((n, d_f), BF16),
                   jax.ShapeDtypeStruct((n, d_f), BF16)),
        grid=(nt,),
        in_specs=[
            pl.BlockSpec((TM, d), lambda i: (i, 0)),
            _mod_spec(d, n_ctx_tiles, 0),
            _const_spec(g.shape),
            _const_spec(win.shape), _const_spec(cs.shape),
        ],
        out_specs=(pl.BlockSpec((TM, d_ssm), lambda i: (i, 0)),
                   pl.BlockSpec((TM, d_f), lambda i: (i, 0)),
                   pl.BlockSpec((TM, d_f), lambda i: (i, 0))),
        compiler_params=_params(("arbitrary",)),
        name="in_proj",
    )(xa, mod, g, win, cs)


def _cmul_add(ar, ai, xr, xi, zr, zi):
    return ar * xr - ai * xi + zr, ar * xi + ai * xr + zi


def _ssm_kernel(ul_ref, uc_ref, bt_ref, wz_ref, wy_ref, mu_ref, dsk_ref, yl_ref, yc_ref,
                z_ref, xp_ref, xc_ref, *, nseg, ncc):
    w = LANE
    wz = wz_ref[0]
    wy = wy_ref[0]
    mu = [mu_ref[0, 0, c:c + 1, :] for c in range(4)]
    mun = [mu_ref[0, 1, c:c + 1, :] for c in range(4)]

    ucat_c = jnp.concatenate([uc_ref[0], uc_ref[1]], axis=1)
    zc = _dot(ucat_c, wz)
    sf = (jnp.zeros((1, w), F32), jnp.zeros((1, w), F32))
    sb = (jnp.zeros((1, w), F32), jnp.zeros((1, w), F32))
    for j in range(ncc):
        jb = ncc - 1 - j
        xc_ref[j:j + 1, 0:w] = sf[0]
        xc_ref[j:j + 1, w:2 * w] = sf[1]
        xc_ref[jb:jb + 1, 2 * w:3 * w] = sb[0]
        xc_ref[jb:jb + 1, 3 * w:4 * w] = sb[1]
        sf = _cmul_add(mu[0], mu[1], sf[0], sf[1], zc[j:j + 1, 0:w], zc[j:j + 1, w:2 * w])
        sb = _cmul_add(mu[2], mu[3], sb[0], sb[1], zc[jb:jb + 1, 2 * w:3 * w], zc[jb:jb + 1, 3 * w:4 * w])

    ucat = jnp.concatenate([ul_ref[0], ul_ref[1]], axis=1)
    z_ref[...] = _dot(ucat, wz)

    mub = [jnp.broadcast_to(m, (SUBLANE, w)) for m in mu]

    def load_z(i, c):
        return z_ref[pl.ds(pl.multiple_of(i * SUBLANE, SUBLANE), SUBLANE), c * w:(c + 1) * w]

    def step(i, st):
        ib = nseg - 1 - i
        fr, fi = _cmul_add(mub[0], mub[1], st[0], st[1], load_z(i, 0), load_z(i, 1))
        br, bi = _cmul_add(mub[2], mub[3], st[2], st[3], load_z(ib, 2), load_z(ib, 3))
        return fr, fi, br, bi

    zero = jnp.zeros((SUBLANE, w), F32)
    fin = lax.fori_loop(0, nseg, step, (zero, zero, zero, zero))

    cf, cb = sf, sb
    rows_f, rows_b = [None] * SUBLANE, [None] * SUBLANE
    for s in range(SUBLANE):
        sb_idx = SUBLANE - 1 - s
        rows_f[s] = cf
        rows_b[sb_idx] = cb
        cf = _cmul_add(mun[0], mun[1], cf[0], cf[1], fin[0][s:s + 1, :], fin[1][s:s + 1, :])
        cb = _cmul_add(mun[2], mun[3], cb[0], cb[1], fin[2][sb_idx:sb_idx + 1, :], fin[3][sb_idx:sb_idx + 1, :])
    carry = (jnp.concatenate([r[0] for r in rows_f], axis=0), jnp.concatenate([r[1] for r in rows_f], axis=0),
             jnp.concatenate([r[0] for r in rows_b], axis=0), jnp.concatenate([r[1] for r in rows_b], axis=0))

    def step2(i, st):
        ib = nseg - 1 - i
        rf = pl.ds(pl.multiple_of(i * SUBLANE, SUBLANE), SUBLANE)
        rb = pl.ds(pl.multiple_of(ib * SUBLANE, SUBLANE), SUBLANE)
        xp_ref[rf, 0:w] = st[0]
        xp_ref[rf, w:2 * w] = st[1]
        xp_ref[rb, 2 * w:3 * w] = st[2]
        xp_ref[rb, 3 * w:4 * w] = st[3]
        return step(i, st)

    lax.fori_loop(0, nseg, step2, carry)

    ycorr = _dot(xp_ref[...].astype(BF16), wy)
    ycorr_c = _dot(xc_ref[...].astype(BF16), wy)
    half = ul_ref.shape[2]
    for gq in range(2):
        bt = bt_ref[0, gq]
        dsk = dsk_ref[0, gq:gq + 1, :]
        ug = ul_ref[gq]
        yl_ref[gq] = (_dot(ug, bt) + ycorr[:, gq * half:(gq + 1) * half] + ug.astype(F32) * dsk).astype(BF16)
        ugc = uc_ref[gq]
        yc_ref[gq] = (_dot(ugc, bt) + ycorr_c[:, gq * half:(gq + 1) * half] + ugc.astype(F32) * dsk).astype(BF16)


def _ssm_weights(lam_re, lam_im, log_dt, b_re, b_im, c_re, c_im, d_skip, nseg):
    hp = lax.Precision.HIGHEST
    t = CHUNK
    ng, ns = lam_re.shape[1], lam_re.shape[2]
    nh = b_re.shape[-1]
    npair = ng // 2
    dt = jnp.exp(log_dt)[..., None]
    a, b = lam_re * dt, lam_im * dt

    def lam_pow(m):
        mm = jnp.asarray(m, F32).reshape(-1, 1, 1, 1)
        mag = jnp.exp(a[None] * mm)
        return mag * jnp.cos(b[None] * mm), mag * jnp.sin(b[None] * mm)

    pr, pi = lam_pow(np.arange(t + 1))
    nr, ni = pr[1] - 1.0, pi[1]
    den = lam_re * lam_re + lam_im * lam_im
    qr, qi = (nr * lam_re + ni * lam_im) / den, (ni * lam_re - nr * lam_im) / den
    bb_re = qr[..., None] * b_re - qi[..., None] * b_im
    bb_im = qr[..., None] * b_im + qi[..., None] * b_re
    e_re = pr[..., None] * bb_re[None] - pi[..., None] * bb_im[None]
    e_im = pr[..., None] * bb_im[None] + pi[..., None] * bb_re[None]
    kk = (jnp.einsum('dghp,mdgpk->mdghk', c_re, e_re, precision=hp)
          - jnp.einsum('dghp,mdgpk->mdghk', c_im, e_im, precision=hp))[:t]

    ii = np.arange(t)[:, None]
    jj = np.arange(t)[None, :]
    kf = jnp.where((jj >= ii)[:, :, None, None, None], kk[:, 0][np.clip(jj - ii, 0, t - 1)], 0.0)
    kb = jnp.where((ii >= jj)[:, :, None, None, None], kk[:, 1][np.clip(ii - jj, 0, t - 1)], 0.0)
    ktot = kf + kb
    bt = ktot.transpose(2, 0, 4, 1, 3).reshape(npair, 2, t * nh, t * nh)

    def wz_part(e, d):
        sel = e[:t][::-1, d] if d == 0 else e[:t, d]
        return sel.transpose(1, 0, 3, 2).reshape(ng, t * nh, ns)
    wz_cols = [wz_part(e_re, 0), wz_part(e_im, 0), wz_part(e_re, 1), wz_part(e_im, 1)]
    eye2 = jnp.eye(2, dtype=F32)
    wz = jnp.stack(wz_cols, axis=1).reshape(npair, 2, 4, t * nh, ns)
    wz = jnp.einsum('aqcrp,qs->aqrcsp', wz, eye2).reshape(npair, 2 * t * nh, 4 * 2 * ns)

    def wy_part(d):
        sl = slice(1, t + 1)
        p_r, p_i = pr[sl, d], pi[sl, d]
        if d == 1:
            p_r, p_i = p_r[::-1], p_i[::-1]
        w_re = c_re[d][None] * p_r[:, :, None, :] - c_im[d][None] * p_i[:, :, None, :]
        w_im = c_re[d][None] * p_i[:, :, None, :] + c_im[d][None] * p_r[:, :, None, :]
        to_rows = lambda v: v.transpose(1, 3, 0, 2).reshape(ng, ns, t * nh)
        return to_rows(w_re), to_rows(-w_im)
    wy_rows = [*wy_part(0), *wy_part(1)]
    wy = jnp.stack(wy_rows, axis=1).reshape(npair, 2, 4, ns, t * nh)
    wy = jnp.einsum('aqcpn,qs->acqpsn', wy, eye2).reshape(npair, 4 * 2 * ns, 2 * t * nh)

    mr, mi = lam_pow([t, t * nseg])
    mu = jnp.stack([mr[:, 0], mi[:, 0], mr[:, 1], mi[:, 1]], axis=1)
    mu = mu.reshape(2, 4, npair, 2 * ns).transpose(2, 0, 1, 3)
    dsk = jnp.tile(d_skip.reshape(ng, 1, nh), (1, t, 1)).reshape(npair, 2, t * nh)
    return bt.astype(BF16), wz.astype(BF16), wy.astype(BF16), mu, dsk


def _ssm(u, lc, sw):
    bt, wz, wy, mu, dsk = sw
    n, d_ssm = u.shape
    l = n - lc
    ng = d_ssm // SSM_GROUP
    npair = ng // 2
    cw = CHUNK * SSM_GROUP
    ncc, nc = lc // CHUNK, l // CHUNK
    nseg = nc // SUBLANE
    assert nseg * SUBLANE == nc and ncc * CHUNK == lc
    uc = u[:lc].reshape(ncc, CHUNK, ng, SSM_GROUP).transpose(2, 0, 1, 3).reshape(ng, ncc, cw)
    ul = u[lc:].reshape(SUBLANE, nseg, CHUNK, ng, SSM_GROUP).transpose(3, 1, 0, 2, 4).reshape(ng, nc, cw)
    yl, yc = pl.pallas_call(
        functools.partial(_ssm_kernel, nseg=nseg, ncc=ncc),
        out_shape=(jax.ShapeDtypeStruct((ng, nc, cw), BF16), jax.ShapeDtypeStruct((ng, ncc, cw), BF16)),
        grid=(npair,),
        in_specs=[
            pl.BlockSpec((2, nc, cw), lambda p: (p, 0, 0)),
            pl.BlockSpec((2, ncc, cw), lambda p: (p, 0, 0)),
            pl.BlockSpec((1, 2, cw, cw), lambda p: (p, 0, 0, 0)),
            pl.BlockSpec((1, 2 * cw, 2 * cw), lambda p: (p, 0, 0)),
            pl.BlockSpec((1, 2 * cw, 2 * cw), lambda p: (p, 0, 0)),
            pl.BlockSpec((1, 2, 4, LANE), lambda p: (p, 0, 0, 0)),
            pl.BlockSpec((1, 2, cw), lambda p: (p, 0, 0)),
        ],
        out_specs=(pl.BlockSpec((2, nc, cw), lambda p: (p, 0, 0)),
                   pl.BlockSpec((2, ncc, cw), lambda p: (p, 0, 0))),
        scratch_shapes=[pltpu.VMEM((nc, 2 * cw), F32), pltpu.VMEM((nc, 2 * cw), F32),
                        pltpu.VMEM((ncc, 2 * cw), F32)],
        compiler_params=_params(("arbitrary",)),
        name="s5_scan",
    )(ul, uc, bt, wz, wy, mu, dsk)
    y_lat = yl.reshape(ng, nseg, SUBLANE, CHUNK, SSM_GROUP).transpose(2, 1, 3, 0, 4).reshape(l, d_ssm)
    y_ctx = yc.reshape(ng, ncc, CHUNK, SSM_GROUP).transpose(1, 2, 0, 3).reshape(lc, d_ssm)
    return y_ctx, y_lat


def _fft1_kernel(p_ref, q_ref, m_ref, o_ref):
    o_ref[...] = _dot(m_ref[...], jnp.concatenate([p_ref[...], q_ref[...]], axis=0)).astype(BF16)


def _fft3_kernel(y_ref, g_ref, o_ref, *, kb):
    for j in range(kb):
        o_ref[j] = _dot(g_ref[j], y_ref[j]).astype(BF16)


def _fft_ctx_kernel(p_ref, q_ref, m_ref, o_ref):
    o_ref[...] = _dot(m_ref[...], jnp.concatenate([p_ref[...], q_ref[...]], axis=0)).astype(BF16)


def _fnet_tables(l, lc):
    r = math.isqrt(l)
    assert r * r == l
    k = np.arange(r)
    th = 2.0 * np.pi * (np.outer(k, k) % r) / r
    c, s = np.cos(th), np.sin(th)
    m1 = np.block([[c, -s], [-s, -c]])
    ta = 2.0 * np.pi * np.outer(k, k) / l
    ca, sa = jnp.asarray(np.cos(ta), F32)[:, None, :], jnp.asarray(np.sin(ta), F32)[:, None, :]
    cb, sb = jnp.asarray(c, F32)[None, :, :], jnp.asarray(s, F32)[None, :, :]
    scale = 1.0 / math.sqrt(l)
    g2 = jnp.concatenate([(ca * cb - sa * sb) * scale, (sa * cb + ca * sb) * scale], axis=-1)
    kc = np.arange(lc)
    thc = 2.0 * np.pi * (np.outer(kc, kc) % lc) / lc
    mc = np.concatenate([np.cos(thc), -np.sin(thc)], axis=1) / math.sqrt(lc)
    return (jnp.asarray(m1, F32).astype(BF16), g2.astype(BF16), jnp.asarray(mc, F32).astype(BF16))


def _fnet_lat(p, q, lc, m1, g2):
    n, c = p.shape
    l = n - lc
    r = m1.shape[0] // 2
    nn = r * c
    tn = SUBLANE * c
    p2 = p[lc:].reshape(r, nn)
    q2 = q[lc:].reshape(r, nn)
    y1 = pl.pallas_call(
        _fft1_kernel,
        out_shape=jax.ShapeDtypeStruct((2 * r, nn), BF16),
        grid=(nn // tn,),
        in_specs=[pl.BlockSpec((r, tn), lambda j: (0, j)),
                  pl.BlockSpec((r, tn), lambda j: (0, j)),
                  _const_spec(m1.shape)],
        out_specs=pl.BlockSpec((2 * r, tn), lambda j: (0, j)),
        compiler_params=_params(("arbitrary",)),
        name="fft_stage1",
    )(p2, q2, m1)
    y1 = y1.reshape(2, r, r, c).transpose(1, 0, 2, 3).reshape(r, 2 * r, c)
    kb = 4
    out = pl.pallas_call(
        functools.partial(_fft3_kernel, kb=kb),
        out_shape=jax.ShapeDtypeStruct((r, r, c), BF16),
        grid=(r // kb,),
        in_specs=[pl.BlockSpec((kb, 2 * r, c), lambda j: (j, 0, 0)),
                  pl.BlockSpec((kb, r, 2 * r), lambda j: (j, 0, 0))],
        out_specs=pl.BlockSpec((kb, r, c), lambda j: (j, 0, 0)),
        compiler_params=_params(("arbitrary",)),
        name="fft_stage2",
    )(y1, g2)
    return out.transpose(1, 0, 2).reshape(l, c)


def _fnet_ctx(p, q, lc, mc):
    c = p.shape[1]
    return pl.pallas_call(
        _fft_ctx_kernel,
        out_shape=jax.ShapeDtypeStruct((lc, c), BF16),
        grid=(1,),
        in_specs=[pl.BlockSpec((lc, c), lambda i: (0, 0)),
                  pl.BlockSpec((lc, c), lambda i: (0, 0)),
                  _const_spec(mc.shape)],
        out_specs=pl.BlockSpec((lc, c), lambda i: (0, 0)),
        compiler_params=_params(("arbitrary",)),
        name="fft_ctx",
    )(p, q, mc)


def _outproj_kernel(x_ref, ysc_ref, ysl_ref, yfc_ref, yfl_ref, mod_ref, g_ref, wglu_ref, wfm_ref, wout_ref,
                    o_ref, *, n_ctx_tiles, row_off, d_ssm, head):
    i = pl.program_id(0) + row_off
    is_ctx = i < n_ctx_tiles
    ys = jnp.where(is_ctx, ysc_ref[...], ysl_ref[...]).astype(F32)
    yf = jnp.where(is_ctx, yfc_ref[...], yfl_ref[...])
    h = _gelu_tanh(ys)
    s = h * _sigmoid(_dot(h.astype(BF16), wglu_ref[...]))
    y = _dot(s.astype(BF16), wout_ref[:d_ssm, :])
    for n in range(N_FNET_HEADS):
        fm = _dot(yf[:, n * head:(n + 1) * head], wfm_ref[n])
        y = y + _dot(fm.astype(BF16), wout_ref[d_ssm + n * head:d_ssm + (n + 1) * head, :])
    o_ref[...] = _post(x_ref[...], y, g_ref[3:4, :], mod_ref, 1, 1.0)


def _outproj(xa, ysc, ysl, yfc, yfl, mod, g, wglu, wfm, wout, *, n_ctx_tiles, row_off):
    n, d = xa.shape
    d_ssm = ysl.shape[1]
    d_f = yfl.shape[1]
    head = d_f // N_FNET_HEADS
    nt = n // TM - row_off
    lat_idx = lambda i: (jnp.maximum(i + row_off - n_ctx_tiles, 0), 0)
    ctx_idx = lambda i: (jnp.minimum(i + row_off, n_ctx_tiles - 1), 0)
    return pl.pallas_call(
        functools.partial(_outproj_kernel, n_ctx_tiles=n_ctx_tiles, row_off=row_off, d_ssm=d_ssm, head=head),
        out_shape=jax.ShapeDtypeStruct((nt * TM, d), F32),
        grid=(nt,),
        in_specs=[
            pl.BlockSpec((TM, d), lambda i: (i + row_off, 0)),
            pl.BlockSpec((TM, d_ssm), ctx_idx), pl.BlockSpec((TM, d_ssm), lat_idx),
            pl.BlockSpec((TM, d_f), ctx_idx), pl.BlockSpec((TM, d_f), lat_idx),
            _mod_spec(d, n_ctx_tiles, row_off),
            _const_spec(g.shape),
            _const_spec(wglu.shape), _const_spec(wfm.shape), _const_spec(wout.shape),
        ],
        out_specs=pl.BlockSpec((TM, d), lambda i: (i, 0)),
        compiler_params=_params(("arbitrary",)),
        name="out_proj",
    )(xa, ysc, ysl, yfc, yfl, mod, g, wglu, wfm, wout)


def _pos_tables(l, d):
    quarter = d // 4
    omega = 1.0 / (POS_BASE ** (jnp.arange(quarter, dtype=F32) / quarter))

    def enc(pv):
        ang = pv[:, None] * omega[None, :]
        return jnp.concatenate([jnp.sin(ang), jnp.cos(ang)], axis=-1)

    renc = enc(jnp.arange(l // GRID_W, dtype=F32))
    cenc = jnp.tile(enc(jnp.arange(GRID_W, dtype=F32)), (TM // GRID_W, 1))
    return renc, cenc


def kernel(x, c, ctx, c_ctx, w_ada, b_ada, norm_g, ffn1_gu, ffn1_down, ffn2_gu, ffn2_down, w_in, w_out,
           ssm_lam_re, ssm_lam_im, ssm_log_dt, ssm_b_re, ssm_b_im, ssm_c_re, ssm_c_im, ssm_d, w_glu, w_fmix):
    bsz, l, d = x.shape
    lc = ctx.shape[1]
    depth = w_ada.shape[0]
    d_ff = ffn1_down.shape[1]
    d_ssm = w_glu.shape[1]
    head = w_fmix.shape[2]
    assert bsz == 1 and lc == TM and l % TM == 0 and TM % GRID_W == 0
    n_ctx_tiles = lc // TM
    nseg = (l // CHUNK) // SUBLANE

    mods = _modulation(c_ctx, c, w_ada, b_ada)
    renc, cenc = _pos_tables(l, d)
    m1, g2, mc = _fnet_tables(l, lc)
    kc = np.arange(head)
    thc = 2.0 * np.pi * (np.outer(kc, kc) % head) / head
    cs = jnp.asarray(np.concatenate([np.cos(thc), np.sin(thc)], axis=1) / math.sqrt(head), F32).astype(BF16)

    xa = None
    for li in range(depth):
        last = li == depth - 1
        mod = mods[li]
        g = norm_g[li]
        wg1, wu1 = ffn1_gu[li, :, :d_ff].astype(BF16), ffn1_gu[li, :, d_ff:].astype(BF16)
        wg2, wu2 = ffn2_gu[li, :, :d_ff].astype(BF16), ffn2_gu[li, :, d_ff:].astype(BF16)
        wd1, wd2 = ffn1_down[li].astype(BF16), ffn2_down[li].astype(BF16)
        sw = _ssm_weights(ssm_lam_re[li], ssm_lam_im[li], ssm_log_dt[li], ssm_b_re[li], ssm_b_im[li],
                          ssm_c_re[li], ssm_c_im[li], ssm_d[li], nseg)

        if li == 0:
            xa = _ffn_first(ctx[0], x[0], renc, cenc, mod, g, wg1, wu1, wd1, k=0, gi=0)
        else:
            xa = _ffn(xa, mod, g, wg1, wu1, wd1, k=0, gi=0, n_ctx_tiles=n_ctx_tiles, row_off=0)

        u, p, q = _inproj(xa, mod, g, w_in[li].astype(BF16), cs, n_ctx_tiles=n_ctx_tiles, d_ssm=d_ssm)
        ys_c, ys_l = _ssm(u, lc, sw)
        yf_l = _fnet_lat(p, q, lc, m1, g2)
        yf_c = _fnet_ctx(p, q, lc, mc)
        row_off = n_ctx_tiles if last else 0
        xa = _outproj(xa, ys_c, ys_l, yf_c, yf_l, mod, g, w_glu[li].astype(BF16), w_fmix[li].astype(BF16),
                      w_out[li].astype(BF16), n_ctx_tiles=n_ctx_tiles, row_off=row_off)
        xa = _ffn(xa, mod, g, wg2, wu2, wd2, k=2, gi=4,
                  n_ctx_tiles=0 if last else n_ctx_tiles, row_off=0)
    return xa[None] if xa.shape[0] == l else xa[lc:][None]
```

```python
import functools
import math

import numpy as np
import jax
import jax.numpy as jnp
from jax import lax
from jax.experimental import pallas as pl
from jax.experimental.pallas import tpu as pltpu

F32 = jnp.float32
BF16 = jnp.bfloat16

LANE = 128
SUBLANE = 8
VMEM_LIMIT = 56 * 1024 * 1024

EPS = 1e-6
GRID_W = 64
POS_BASE = 10000.0
N_MOD = 9
SSM_GROUP = 16
STATE = 64
N_FNET_HEADS = 4
CHUNK = 16
TM = 256


def _dot(a, b):
    return jnp.dot(a, b, preferred_element_type=F32)


def _params(sem=None):
    return pltpu.CompilerParams(dimension_semantics=sem, vmem_limit_bytes=VMEM_LIMIT)


def _const_spec(shape):
    nd = len(shape)
    return pl.BlockSpec(shape, lambda *_: (0,) * nd, pipeline_mode=pl.Buffered(1))


def _rms(x, g):
    ms = jnp.mean(x * x, axis=-1, keepdims=True)
    return x * lax.rsqrt(ms + EPS) * g


def _pre(x, g, mod_ref, k):
    return _rms(x, g) * (1.0 + mod_ref[0, 3 * k + 1:3 * k + 2, :]) + mod_ref[0, 3 * k:3 * k + 1, :]


def _post(x, y, g, mod_ref, k, weight):
    return x + (weight * mod_ref[0, 3 * k + 2:3 * k + 3, :]) * _rms(y, g)


def _sigmoid(x):
    return 1.0 / (1.0 + jnp.exp(-x))


def _gelu_tanh(x):
    return 0.5 * x * (1.0 + jnp.tanh(math.sqrt(2.0 / math.pi) * (x + 0.044715 * (x * x * x))))


def _mod_kernel(cb_ref, w_ref, b_ref, o_ref, *, d_model, tn):
    nb = tn // LANE

    def body(t, accs):
        d0 = pl.multiple_of(t * SUBLANE, SUBLANE)
        s = []
        for r in range(2):
            cv = cb_ref[r, pl.ds(d0, SUBLANE), :]
            s.append(cv * _sigmoid(cv))
        new = list(accs)
        for j in range(nb):
            w = w_ref[0, pl.ds(d0, SUBLANE), j * LANE:(j + 1) * LANE]
            for r in range(2):
                new[r * nb + j] = accs[r * nb + j] + w * s[r]
        return tuple(new)

    init = tuple(jnp.zeros((SUBLANE, LANE), F32) for _ in range(2 * nb))
    accs = lax.fori_loop(0, d_model // SUBLANE, body, init)
    for r in range(2):
        for j in range(nb):
            o_ref[0, r:r + 1, j * LANE:(j + 1) * LANE] = (
                jnp.sum(accs[r * nb + j], axis=0, keepdims=True) + b_ref[0, :, j * LANE:(j + 1) * LANE])


def _modulation(c_ctx, c, w_ada, b_ada):
    depth, d_model, n_out = w_ada.shape
    tn = 9 * LANE
    assert n_out % tn == 0
    cb = jnp.broadcast_to(jnp.stack([c_ctx, c[0]])[:, :, None], (2, d_model, LANE))
    out = pl.pallas_call(
        functools.partial(_mod_kernel, d_model=d_model, tn=tn),
        out_shape=jax.ShapeDtypeStruct((depth, 2, n_out), F32),
        grid=(depth, n_out // tn),
        in_specs=[
            pl.BlockSpec((2, d_model, LANE), lambda l, j: (0, 0, 0)),
            pl.BlockSpec((1, d_model, tn), lambda l, j: (l, 0, j)),
            pl.BlockSpec((1, 1, tn), lambda l, j: (l, 0, j)),
        ],
        out_specs=pl.BlockSpec((1, 2, tn), lambda l, j: (l, 0, j)),
        compiler_params=_params(("arbitrary", "arbitrary")),
        name="adaln_mod",
    )(cb, w_ada, b_ada.reshape(depth, 1, n_out))
    return out.reshape(depth, 2, N_MOD, d_model)


def _ffn_body(x, mod_ref, g_ref, wg_ref, wu_ref, wd_ref, k, gi):
    h = _pre(x, g_ref[gi:gi + 1, :], mod_ref, k).astype(BF16)
    gate = _dot(h, wg_ref[...])
    up = _dot(h, wu_ref[...])
    act = (gate * _sigmoid(gate) * up).astype(BF16)
    y = _dot(act, wd_ref[...])
    return _post(x, y, g_ref[gi + 1:gi + 2, :], mod_ref, k, 0.5)


def _ffn_kernel(x_ref, mod_ref, g_ref, wg_ref, wu_ref, wd_ref, o_ref, *, k, gi):
    o_ref[...] = _ffn_body(x_ref[...], mod_ref, g_ref, wg_ref, wu_ref, wd_ref, k, gi)


def _ffn_first_kernel(x_ref, ctx_ref, renc_ref, cenc_ref, mod_ref, g_ref, wg_ref, wu_ref, wd_ref,
                      o_ref, xs_ref, *, k, gi, n_lat_tiles, tm):
    i = pl.program_id(0)
    half = renc_ref.shape[1]
    rows_per_tile = tm // GRID_W

    @pl.when(i >= n_lat_tiles)
    def _():
        xs_ref[...] = ctx_ref[...]

    @pl.when(i < n_lat_tiles)
    def _():
        r0 = i * rows_per_tile
        xs_ref[:, half:] = x_ref[:, half:] + cenc_ref[...]
        for q in range(rows_per_tile):
            xs_ref[q * GRID_W:(q + 1) * GRID_W, :half] = (
                x_ref[q * GRID_W:(q + 1) * GRID_W, :half] + renc_ref[pl.ds(r0 + q, 1), :])

    o_ref[...] = _ffn_body(xs_ref[...], mod_ref, g_ref, wg_ref, wu_ref, wd_ref, k, gi)


def _mod_spec(d_model, n_lat_tiles):
    return pl.BlockSpec((1, N_MOD, d_model), lambda i: (jnp.where(i >= n_lat_tiles, 0, 1), 0, 0))


def _ffn(xa, mod, g, wg, wu, wd, *, k, gi, n_lat_tiles, n_tiles):
    d = xa.shape[1]
    f = wg.shape[1]
    return pl.pallas_call(
        functools.partial(_ffn_kernel, k=k, gi=gi),
        out_shape=jax.ShapeDtypeStruct((n_tiles * TM, d), F32),
        grid=(n_tiles,),
        in_specs=[
            pl.BlockSpec((TM, d), lambda i: (i, 0)),
            _mod_spec(d, n_lat_tiles),
            _const_spec(g.shape),
            _const_spec((d, f)), _const_spec((d, f)), _const_spec((f, d)),
        ],
        out_specs=pl.BlockSpec((TM, d), lambda i: (i, 0)),
        compiler_params=_params(("arbitrary",)),
        name=f"ffn_k{k}",
    )(xa, mod, g, wg, wu, wd)


def _ffn_first(x2, ctx2, renc, cenc, mod, g, wg, wu, wd, *, k, gi):
    lc, d = ctx2.shape
    l = x2.shape[0]
    f = wg.shape[1]
    n_lat_tiles = l // TM
    nt = (lc + l) // TM
    return pl.pallas_call(
        functools.partial(_ffn_first_kernel, k=k, gi=gi, n_lat_tiles=n_lat_tiles, tm=TM),
        out_shape=jax.ShapeDtypeStruct((lc + l, d), F32),
        grid=(nt,),
        in_specs=[
            pl.BlockSpec((TM, d), lambda i: (jnp.minimum(i, n_lat_tiles - 1), 0)),
            pl.BlockSpec((TM, d), lambda i: (jnp.maximum(i - n_lat_tiles, 0), 0)),
            _const_spec(renc.shape), _const_spec(cenc.shape),
            _mod_spec(d, n_lat_tiles),
            _const_spec(g.shape),
            _const_spec((d, f)), _const_spec((d, f)), _const_spec((f, d)),
        ],
        out_specs=pl.BlockSpec((TM, d), lambda i: (i, 0)),
        scratch_shapes=[pltpu.VMEM((TM, d), F32)],
        compiler_params=_params(("arbitrary",)),
        name="ffn_first",
    )(x2, ctx2, renc, cenc, mod, g, wg, wu, wd)


def _inproj_kernel(x_ref, mod_ref, g_ref, win_ref, cs_ref, u_ref, p_ref, q_ref, *, d_ssm, head):
    h = _pre(x_ref[...], g_ref[2:3, :], mod_ref, 1).astype(BF16)
    hh = _dot(h, win_ref[...])
    u_ref[...] = hh[:, :d_ssm]
    for n in range(N_FNET_HEADS):
        lo = d_ssm + n * head
        pq = _dot(hh[:, lo:lo + head].astype(BF16), cs_ref[...])
        p_ref[:, n * head:(n + 1) * head] = pq[:, :head]
        q_ref[:, n * head:(n + 1) * head] = pq[:, head:]


def _inproj(xa, mod, g, win, cs, *, n_lat_tiles, d_ssm):
    n, d = xa.shape
    d_f = win.shape[1] - d_ssm
    head = d_f // N_FNET_HEADS
    nt = n // TM
    return pl.pallas_call(
        functools.partial(_inproj_kernel, d_ssm=d_ssm, head=head),
        out_shape=(jax.ShapeDtypeStruct((n, d_ssm), F32),
                   jax.ShapeDtypeStruct((n, d_f), F32),
                   jax.ShapeDtypeStruct((n, d_f), F32)),
        grid=(nt,),
        in_specs=[
            pl.BlockSpec((TM, d), lambda i: (i, 0)),
            _mod_spec(d, n_lat_tiles),
            _const_spec(g.shape),
            _const_spec(win.shape), _const_spec(cs.shape),
        ],
        out_specs=(pl.BlockSpec((TM, d_ssm), lambda i: (i, 0)),
                   pl.BlockSpec((TM, d_f), lambda i: (i, 0)),
                   pl.BlockSpec((TM, d_f), lambda i: (i, 0))),
        compiler_params=_params(("arbitrary",)),
        name="in_proj",
    )(xa, mod, g, win, cs)


def _cmul_add(ar, ai, xr, xi, zr, zi):
    return ar * xr - ai * xi + zr, ar * xi + ai * xr + zi


def _s5_state_kernel(u_ref, wz_ref, mu_ref, d_ref, xp_ref, z_ref, xs_ref, *, nseg, ncc):
    w = LANE
    nc = d_ref.shape[1]
    nlat = SUBLANE * nseg

    @pl.when(pl.program_id(1) == 0)
    def _():
        for tt in range(CHUNK):
            d_ref[0, :, tt * w:(tt + 1) * w] = u_ref[pl.ds(tt, nc, stride=CHUNK), :].astype(BF16)

    zz = _dot(d_ref[0], wz_ref[0])
    for c in range(4):
        z_ref[c] = zz[:, c * w:(c + 1) * w]

    mu = [mu_ref[0, 0, 0, c:c + 1, :] for c in range(4)]
    mun = [mu_ref[0, 0, 1, c:c + 1, :] for c in range(4)]

    zc = zz[nlat:nlat + ncc, :]
    sf = (jnp.zeros((1, w), F32), jnp.zeros((1, w), F32))
    sb = (jnp.zeros((1, w), F32), jnp.zeros((1, w), F32))
    for j in range(ncc):
        jb = ncc - 1 - j
        xs_ref[0, nlat + j:nlat + j + 1, :] = sf[0]
        xs_ref[1, nlat + j:nlat + j + 1, :] = sf[1]
        xs_ref[2, nlat + jb:nlat + jb + 1, :] = sb[0]
        xs_ref[3, nlat + jb:nlat + jb + 1, :] = sb[1]
        sf = _cmul_add(mu[0], mu[1], sf[0], sf[1], zc[j:j + 1, 0:w], zc[j:j + 1, w:2 * w])
        sb = _cmul_add(mu[2], mu[3], sb[0], sb[1], zc[jb:jb + 1, 2 * w:3 * w], zc[jb:jb + 1, 3 * w:4 * w])

    mub = [jnp.broadcast_to(m, (SUBLANE, w)) for m in mu]

    def rows(i):
        return pl.ds(i, SUBLANE, stride=nseg)

    def step(i, st):
        ib = nseg - 1 - i
        fr, fi = _cmul_add(mub[0], mub[1], st[0], st[1], z_ref[0, rows(i), :], z_ref[1, rows(i), :])
        br, bi = _cmul_add(mub[2], mub[3], st[2], st[3], z_ref[2, rows(ib), :], z_ref[3, rows(ib), :])
        return fr, fi, br, bi

    zero = jnp.zeros((SUBLANE, w), F32)
    fin = lax.fori_loop(0, nseg, step, (zero, zero, zero, zero))

    cf, cb = sf, sb
    rows_f, rows_b = [None] * SUBLANE, [None] * SUBLANE
    for s in range(SUBLANE):
        sr = SUBLANE - 1 - s
        rows_f[s] = cf
        rows_b[sr] = cb
        cf = _cmul_add(mun[0], mun[1], cf[0], cf[1], fin[0][s:s + 1, :], fin[1][s:s + 1, :])
        cb = _cmul_add(mun[2], mun[3], cb[0], cb[1], fin[2][sr:sr + 1, :], fin[3][sr:sr + 1, :])
    carry = (jnp.concatenate([r[0] for r in rows_f], axis=0), jnp.concatenate([r[1] for r in rows_f], axis=0),
             jnp.concatenate([r[0] for r in rows_b], axis=0), jnp.concatenate([r[1] for r in rows_b], axis=0))

    def step2(i, st):
        ib = nseg - 1 - i
        xs_ref[0, rows(i), :] = st[0]
        xs_ref[1, rows(i), :] = st[1]
        xs_ref[2, rows(ib), :] = st[2]
        xs_ref[3, rows(ib), :] = st[3]
        return step(i, st)

    lax.fori_loop(0, nseg, step2, carry)
    for c in range(4):
        xp_ref[0, :, c * w:(c + 1) * w] = xs_ref[c].astype(BF16)


def _s5_readout_kernel(d_ref, xp_ref, tt_ref, wy_ref, dsk_ref, y_ref, bt_ref):
    nt = CHUNK // 2
    tw = 2 * LANE

    @pl.when(pl.program_id(1) == 0)
    def _():
        for i in range(nt):
            for j in range(nt):
                bt_ref[i * tw:(i + 1) * tw, j * tw:(j + 1) * tw] = tt_ref[0, j - i + nt - 1]

    nrows = d_ref.shape[1]
    cw = 4 * LANE
    for cg in range(CHUNK * LANE // cw):
        cols = slice(cg * cw, (cg + 1) * cw)
        y = (_dot(d_ref[0], bt_ref[:, cols]) + _dot(xp_ref[0], wy_ref[0, :, cols])
             + d_ref[0, :, cols].astype(F32) * dsk_ref[0, :, cols])
        for k in range(cw // LANE):
            tt = cg * (cw // LANE) + k
            y_ref[pl.ds(tt, nrows, stride=CHUNK), :] = y[:, k * LANE:(k + 1) * LANE]


def _s5_operators(lam_re, lam_im, log_dt, b_re, b_im, c_re, c_im, d_skip, nseg):
    hp = lax.Precision.HIGHEST
    t = CHUNK
    ng, ns = lam_re.shape[1], lam_re.shape[2]
    nh = b_re.shape[-1]
    gp = ng * ns
    g8 = LANE // nh
    na = ng // g8
    nb = g8 // 2
    assert 2 * ns == LANE and t % 2 == 0
    dt = jnp.exp(log_dt)[..., None]
    a, b = (lam_re * dt).reshape(2, gp), (lam_im * dt).reshape(2, gp)

    def lam_pow(m):
        mm = jnp.asarray(m, F32).reshape(-1, 1, 1)
        mag = jnp.exp(a[None] * mm)
        return mag * jnp.cos(b[None] * mm), mag * jnp.sin(b[None] * mm)

    pr, pi = lam_pow(np.arange(t + 1))
    lr, li = lam_re.reshape(2, gp), lam_im.reshape(2, gp)
    nr, ni = pr[1] - 1.0, pi[1]
    den = lr * lr + li * li
    qr, qi = (nr * lr + ni * li) / den, (ni * lr - nr * li) / den
    bt_re = b_re.transpose(0, 3, 1, 2).reshape(2, nh, gp)
    bt_im = b_im.transpose(0, 3, 1, 2).reshape(2, nh, gp)
    bb_re = qr[:, None] * bt_re - qi[:, None] * bt_im
    bb_im = qr[:, None] * bt_im + qi[:, None] * bt_re
    e_re = pr[:, :, None] * bb_re[None] - pi[:, :, None] * bb_im[None]
    e_im = pr[:, :, None] * bb_im[None] + pi[:, :, None] * bb_re[None]
    ct_re = c_re.transpose(0, 2, 1, 3).reshape(2, nh, gp)
    ct_im = c_im.transpose(0, 2, 1, 3).reshape(2, nh, gp)

    prod = (ct_re[None, :, None] * e_re[:t, :, :, None] - ct_im[None, :, None] * e_im[:t, :, :, None])
    kk = prod.reshape(t, 2, nh, nh, ng, ns).sum(-1)
    kt = jnp.concatenate([kk[1:, 1][::-1], (kk[0, 0] + kk[0, 1])[None], kk[1:, 0]], axis=0)
    ktr = kt.transpose(0, 3, 1, 2).reshape((2 * t - 1) * ng * nh, nh)
    tile_h = np.tile(np.eye(nh, dtype=np.float32), (1, g8))
    blk = np.kron(np.eye(g8, dtype=np.float32), np.ones((nh, nh), np.float32))
    kr = jnp.dot(ktr, tile_h, precision=hp).reshape(2 * t - 1, na, LANE, LANE) * blk
    nt = t // 2
    idx = np.array([[[2 * dl + j2 - i2 + t - 1 for j2 in (0, 1)] for i2 in (0, 1)] for dl in range(-(nt - 1), nt)])
    tiles = kr.astype(BF16)[idx]
    tiles = tiles.transpose(3, 0, 1, 4, 2, 5).reshape(na, 2 * nt - 1, 2 * LANE, 2 * LANE)

    ez = jnp.stack([jnp.stack([e_re[:t][::-1, 0], e_re[:t, 1]]), jnp.stack([e_im[:t][::-1, 0], e_im[:t, 1]])], axis=1)
    ez = ez.reshape(2, 2, t, nh, na, nb, LANE).transpose(4, 2, 3, 5, 0, 1, 6)
    lane_g2 = np.arange(LANE) // ns
    mask_z = (np.arange(g8)[:, None, None] == 2 * np.arange(nb)[None, :, None] + lane_g2[None, None, :])
    mask_z = jnp.asarray(mask_z[None, None, :, None, :, None, None, :], F32)
    wz = (ez[:, :, None] * mask_z).astype(BF16).reshape(na, t * LANE, nb * 4 * LANE)

    cgp_re = c_re.transpose(0, 1, 3, 2).reshape(2, gp, nh)
    cgp_im = c_im.transpose(0, 1, 3, 2).reshape(2, gp, nh)
    tile_j = np.tile(np.eye(nh, dtype=np.float32), (1, t))
    rep_j = np.kron(np.eye(t, dtype=np.float32), np.ones((1, nh), np.float32))
    ctl_re = jnp.dot(cgp_re, tile_j, precision=hp)
    ctl_im = jnp.dot(cgp_im, tile_j, precision=hp)
    sl = slice(1, t + 1)
    pw_re = jnp.stack([pr[sl, 0], pr[sl, 1][::-1]]).transpose(0, 2, 1)
    pw_im = jnp.stack([pi[sl, 0], pi[sl, 1][::-1]]).transpose(0, 2, 1)
    pw_re = jnp.dot(pw_re, rep_j, precision=hp)
    pw_im = jnp.dot(pw_im, rep_j, precision=hp)
    wy_re = ctl_re * pw_re - ctl_im * pw_im
    wy_im = ctl_re * pw_im + ctl_im * pw_re
    sy = jnp.stack([wy_re, -wy_im], axis=1).reshape(2, 2, na, nb, LANE, t * nh)
    sy = sy.transpose(2, 3, 0, 1, 4, 5).reshape(na, nb * 4 * LANE, t * nh).astype(BF16)
    r_mat = np.kron(np.eye(t, dtype=np.float32), np.tile(np.eye(nh, dtype=np.float32), (1, g8)))
    sy = jnp.dot(sy, jnp.asarray(r_mat, F32).astype(BF16), preferred_element_type=BF16)
    row = jnp.arange(nb * 4 * LANE)
    row_g = 2 * (row // (4 * LANE)) + (row % LANE) // ns
    col_g = (jnp.arange(t * LANE) % LANE) // nh
    wy = jnp.where((row_g[:, None] == col_g[None, :])[None], sy, jnp.zeros((), BF16))

    mr, mi = lam_pow([t, t * nseg])
    mu = jnp.stack([mr, mi], axis=2).reshape(2, 4, na, nb, LANE).transpose(2, 3, 0, 1, 4)
    dsk = jnp.tile(d_skip.reshape(na, 1, LANE), (1, 1, t))
    return tiles, wz, wy, mu, dsk


def _s5(u, l, ops):
    tiles, wz, wy, mu, dsk = ops
    n, d_ssm = u.shape
    na = d_ssm // LANE
    nb = wz.shape[2] // (4 * LANE)
    fw = CHUNK * LANE
    nc = n // CHUNK
    nlat = l // CHUNK
    ncc = nc - nlat
    nseg = nlat // SUBLANE
    assert nseg * SUBLANE == nlat and nc * CHUNK == n
    d, xp = pl.pallas_call(
        functools.partial(_s5_state_kernel, nseg=nseg, ncc=ncc),
        out_shape=(jax.ShapeDtypeStruct((na, nc, fw), BF16), jax.ShapeDtypeStruct((na, nc, nb * 4 * LANE), BF16)),
        grid=(na, nb),
        in_specs=[
            pl.BlockSpec((n, LANE), lambda a, b: (0, a)),
            pl.BlockSpec((1, fw, 4 * LANE), lambda a, b: (a, 0, b)),
            pl.BlockSpec((1, 1, 2, 4, LANE), lambda a, b: (a, b, 0, 0, 0)),
        ],
        out_specs=(pl.BlockSpec((1, nc, fw), lambda a, b: (a, 0, 0)),
                   pl.BlockSpec((1, nc, 4 * LANE), lambda a, b: (a, 0, b))),
        scratch_shapes=[pltpu.VMEM((4, nc, LANE), F32), pltpu.VMEM((4, nc, LANE), F32)],
        compiler_params=_params(("arbitrary", "arbitrary")),
        name="s5_state",
    )(u, wz, mu)
    rbs = max(r for r in range(16, min(nc, 512) + 1, 16) if nc % r == 0)
    nrb = nc // rbs
    return pl.pallas_call(
        _s5_readout_kernel,
        out_shape=jax.ShapeDtypeStruct((n, d_ssm), F32),
        grid=(na, nrb),
        in_specs=[
            pl.BlockSpec((1, rbs, fw), lambda a, r: (a, r, 0)),
            pl.BlockSpec((1, rbs, nb * 4 * LANE), lambda a, r: (a, r, 0)),
            pl.BlockSpec((1,) + tiles.shape[1:], lambda a, r: (a, 0, 0, 0)),
            pl.BlockSpec((1,) + wy.shape[1:], lambda a, r: (a, 0, 0), pipeline_mode=pl.Buffered(1)),
            pl.BlockSpec((1, 1, fw), lambda a, r: (a, 0, 0)),
        ],
        out_specs=pl.BlockSpec((rbs * CHUNK, LANE), lambda a, r: (r, a)),
        scratch_shapes=[pltpu.VMEM((fw, fw), BF16)],
        compiler_params=_params(("arbitrary", "arbitrary")),
        name="s5_readout",
    )(d, xp, tiles, wy, dsk)


def _fft1_kernel(p_ref, q_ref, m_ref, o_ref, s_ref, *, nb):
    for n in range(nb):
        rhs = jnp.concatenate([p_ref[:, n, :], q_ref[:, n, :]], axis=0).astype(BF16)
        s_ref[:, n, :] = _dot(m_ref[...], rhs)
    o_ref[...] = s_ref[...].astype(BF16)


def _fft2_kernel(yr_ref, yi_ref, g_ref, o_ref, *, kb):
    for k in range(kb):
        rhs = jnp.concatenate([yr_ref[k], yi_ref[k]], axis=0)
        o_ref[:, k, :] = _dot(g_ref[k], rhs)


def _fft_ctx_kernel(p_ref, q_ref, m_ref, o_ref):
    rhs = jnp.concatenate([p_ref[...], q_ref[...]], axis=0).astype(BF16)
    o_ref[...] = _dot(m_ref[...], rhs)


def _fnet_tables(l, lc):
    r = math.isqrt(l)
    assert r * r == l
    k = np.arange(r)
    th = 2.0 * np.pi * (np.outer(k, k) % r) / r
    c, s = np.cos(th), np.sin(th)
    m1 = np.block([[c, -s], [-s, -c]])
    ta = 2.0 * np.pi * np.outer(k, k) / l
    ca, sa = jnp.asarray(np.cos(ta), F32)[:, None, :], jnp.asarray(np.sin(ta), F32)[:, None, :]
    cb, sb = jnp.asarray(c, F32)[None, :, :], jnp.asarray(s, F32)[None, :, :]
    scale = 1.0 / math.sqrt(l)
    g2 = jnp.concatenate([(ca * cb - sa * sb) * scale, (sa * cb + ca * sb) * scale], axis=-1)
    kc = np.arange(lc)
    thc = 2.0 * np.pi * (np.outer(kc, kc) % lc) / lc
    mc = np.concatenate([np.cos(thc), -np.sin(thc)], axis=1) / math.sqrt(lc)
    return (jnp.asarray(m1, F32).astype(BF16), g2.astype(BF16), jnp.asarray(mc, F32).astype(BF16))


def _fnet_lat(p, q, l, m1, g2):
    n, c = p.shape
    r = m1.shape[0] // 2
    assert n % r == 0
    p3 = p.reshape(n // r, r, c)
    q3 = q.reshape(n // r, r, c)
    nb = 16
    y1 = pl.pallas_call(
        functools.partial(_fft1_kernel, nb=nb),
        out_shape=jax.ShapeDtypeStruct((2 * r, r, c), BF16),
        grid=(r // nb,),
        in_specs=[pl.BlockSpec((r, nb, c), lambda j: (0, j, 0)),
                  pl.BlockSpec((r, nb, c), lambda j: (0, j, 0)),
                  _const_spec(m1.shape)],
        out_specs=pl.BlockSpec((2 * r, nb, c), lambda j: (0, j, 0)),
        scratch_shapes=[pltpu.VMEM((2 * r, nb, c), F32)],
        compiler_params=_params(("arbitrary",)),
        name="fft_stage1",
    )(p3, q3, m1)
    kb = 8
    out = pl.pallas_call(
        functools.partial(_fft2_kernel, kb=kb),
        out_shape=jax.ShapeDtypeStruct((r, r, c), F32),
        grid=(r // kb,),
        in_specs=[pl.BlockSpec((kb, r, c), lambda j: (j, 0, 0)),
                  pl.BlockSpec((kb, r, c), lambda j: (r // kb + j, 0, 0)),
                  pl.BlockSpec((kb, r, 2 * r), lambda j: (j, 0, 0))],
        out_specs=pl.BlockSpec((r, kb, c), lambda j: (0, j, 0)),
        compiler_params=_params(("arbitrary",)),
        name="fft_stage2",
    )(y1, y1, g2)
    return out.reshape(l, c)


def _fnet_ctx(p, q, l, lc, mc):
    c = p.shape[1]
    assert l % lc == 0
    return pl.pallas_call(
        _fft_ctx_kernel,
        out_shape=jax.ShapeDtypeStruct((lc, c), F32),
        grid=(1,),
        in_specs=[pl.BlockSpec((lc, c), lambda i: (l // lc, 0)),
                  pl.BlockSpec((lc, c), lambda i: (l // lc, 0)),
                  _const_spec(mc.shape)],
        out_specs=pl.BlockSpec((lc, c), lambda i: (0, 0)),
        compiler_params=_params(("arbitrary",)),
        name="fft_ctx",
    )(p, q, mc)


def _outproj_kernel(x_ref, ys_ref, yfl_ref, yfc_ref, mod_ref, g_ref, wglu_ref, wfm_ref, wout_ref,
                    o_ref, *, n_lat_tiles, d_ssm, head):
    is_ctx = pl.program_id(0) >= n_lat_tiles
    yf = jnp.where(is_ctx, yfc_ref[...], yfl_ref[...]).astype(BF16)
    h = _gelu_tanh(ys_ref[...])
    s = h * _sigmoid(_dot(h.astype(BF16), wglu_ref[...]))
    y = _dot(s.astype(BF16), wout_ref[:d_ssm, :])
    for n in range(N_FNET_HEADS):
        fm = _dot(yf[:, n * head:(n + 1) * head], wfm_ref[n])
        y = y + _dot(fm.astype(BF16), wout_ref[d_ssm + n * head:d_ssm + (n + 1) * head, :])
    o_ref[...] = _post(x_ref[...], y, g_ref[3:4, :], mod_ref, 1, 1.0)


def _outproj(xa, ys, yfl, yfc, mod, g, wglu, wfm, wout, *, n_lat_tiles, n_tiles):
    d = xa.shape[1]
    d_ssm = ys.shape[1]
    d_f = yfl.shape[1]
    head = d_f // N_FNET_HEADS
    return pl.pallas_call(
        functools.partial(_outproj_kernel, n_lat_tiles=n_lat_tiles, d_ssm=d_ssm, head=head),
        out_shape=jax.ShapeDtypeStruct((n_tiles * TM, d), F32),
        grid=(n_tiles,),
        in_specs=[
            pl.BlockSpec((TM, d), lambda i: (i, 0)),
            pl.BlockSpec((TM, d_ssm), lambda i: (i, 0)),
            pl.BlockSpec((TM, d_f), lambda i: (jnp.minimum(i, n_lat_tiles - 1), 0)),
            pl.BlockSpec((TM, d_f), lambda i: (jnp.maximum(i - n_lat_tiles, 0), 0)),
            _mod_spec(d, n_lat_tiles),
            _const_spec(g.shape),
            _const_spec(wglu.shape), _const_spec(wfm.shape), _const_spec(wout.shape),
        ],
        out_specs=pl.BlockSpec((TM, d), lambda i: (i, 0)),
        compiler_params=_params(("arbitrary",)),
        name="out_proj",
    )(xa, ys, yfl, yfc, mod, g, wglu, wfm, wout)


def _pos_tables(l, d):
    quarter = d // 4
    omega = 1.0 / (POS_BASE ** (jnp.arange(quarter, dtype=F32) / quarter))

    def enc(pv):
        ang = pv[:, None] * omega[None, :]
        return jnp.concatenate([jnp.sin(ang), jnp.cos(ang)], axis=-1)

    renc = enc(jnp.arange(l // GRID_W, dtype=F32))
    cenc = jnp.tile(enc(jnp.arange(GRID_W, dtype=F32)), (TM // GRID_W, 1))
    return renc, cenc


def kernel(x, c, ctx, c_ctx, w_ada, b_ada, norm_g, ffn1_gu, ffn1_down, ffn2_gu, ffn2_down, w_in, w_out,
           ssm_lam_re, ssm_lam_im, ssm_log_dt, ssm_b_re, ssm_b_im, ssm_c_re, ssm_c_im, ssm_d, w_glu, w_fmix):
    bsz, l, d = x.shape
    lc = ctx.shape[1]
    depth = w_ada.shape[0]
    d_ff = ffn1_down.shape[1]
    d_ssm = w_glu.shape[1]
    head = w_fmix.shape[2]
    assert bsz == 1 and lc == TM and l % TM == 0 and TM % GRID_W == 0
    n_lat_tiles = l // TM
    n_all_tiles = (l + lc) // TM
    nseg = (l // CHUNK) // SUBLANE

    mods = _modulation(c_ctx, c, w_ada, b_ada)
    renc, cenc = _pos_tables(l, d)
    m1, g2, mc = _fnet_tables(l, lc)
    kc = np.arange(head)
    thc = 2.0 * np.pi * (np.outer(kc, kc) % head) / head
    cs = jnp.asarray(np.concatenate([np.cos(thc), np.sin(thc)], axis=1) / math.sqrt(head), F32).astype(BF16)

    xa = None
    for li in range(depth):
        last = li == depth - 1
        mod = mods[li]
        g = norm_g[li]
        wg1, wu1 = ffn1_gu[li, :, :d_ff].astype(BF16), ffn1_gu[li, :, d_ff:].astype(BF16)
        wg2, wu2 = ffn2_gu[li, :, :d_ff].astype(BF16), ffn2_gu[li, :, d_ff:].astype(BF16)
        wd1, wd2 = ffn1_down[li].astype(BF16), ffn2_down[li].astype(BF16)
        ops = _s5_operators(ssm_lam_re[li], ssm_lam_im[li], ssm_log_dt[li], ssm_b_re[li], ssm_b_im[li],
                            ssm_c_re[li], ssm_c_im[li], ssm_d[li], nseg)

        if li == 0:
            xa = _ffn_first(x[0], ctx[0], renc, cenc, mod, g, wg1, wu1, wd1, k=0, gi=0)
        else:
            xa = _ffn(xa, mod, g, wg1, wu1, wd1, k=0, gi=0, n_lat_tiles=n_lat_tiles, n_tiles=n_all_tiles)

        u, p, q = _inproj(xa, mod, g, w_in[li].astype(BF16), cs, n_lat_tiles=n_lat_tiles, d_ssm=d_ssm)
        ys = _s5(u, l, ops)
        yf_l = _fnet_lat(p, q, l, m1, g2)
        yf_c = _fnet_ctx(p, q, l, lc, mc)
        n_tiles = n_lat_tiles if last else n_all_tiles
        xa = _outproj(xa, ys, yf_l, yf_c, mod, g, w_glu[li].astype(BF16), w_fmix[li].astype(BF16),
                      w_out[li].astype(BF16), n_lat_tiles=n_lat_tiles, n_tiles=n_tiles)
        xa = _ffn(xa, mod, g, wg2, wu2, wd2, k=2, gi=4, n_lat_tiles=n_lat_tiles, n_tiles=n_tiles)
    return xa[None]
```

```python
import functools
import math

import numpy as np
import jax
import jax.numpy as jnp
from jax import lax
from jax.experimental import pallas as pl
from jax.experimental.pallas import tpu as pltpu

F32 = jnp.float32
BF16 = jnp.bfloat16

LANE = 128
SUBLANE = 8
VMEM_LIMIT = 56 * 1024 * 1024

EPS = 1e-6
GRID_W = 64
POS_BASE = 10000.0
N_MOD = 9
SSM_GROUP = 16
STATE = 64
N_FNET_HEADS = 4
CHUNK = 16
TM = 256


def _dot(a, b):
    return jnp.dot(a, b, preferred_element_type=F32)


def _params(sem=None):
    return pltpu.CompilerParams(dimension_semantics=sem, vmem_limit_bytes=VMEM_LIMIT)


def _const_spec(shape):
    nd = len(shape)
    return pl.BlockSpec(shape, lambda *_: (0,) * nd, pipeline_mode=pl.Buffered(1))


def _rms(x, g):
    ms = jnp.mean(x * x, axis=-1, keepdims=True)
    return x * lax.rsqrt(ms + EPS) * g


def _pre(x, g, mod_ref, k):
    return _rms(x, g) * (1.0 + mod_ref[0, 3 * k + 1:3 * k + 2, :]) + mod_ref[0, 3 * k:3 * k + 1, :]


def _post(x, y, g, mod_ref, k, weight):
    return x + (weight * mod_ref[0, 3 * k + 2:3 * k + 3, :]) * _rms(y, g)


def _sigmoid(x):
    return 1.0 / (1.0 + jnp.exp(-x))


def _gelu_tanh(x):
    return 0.5 * x * (1.0 + jnp.tanh(math.sqrt(2.0 / math.pi) * (x + 0.044715 * (x * x * x))))


def _mod_kernel(cb_ref, w_ref, b_ref, o_ref, *, d_model, tn):
    nb = tn // LANE

    def body(t, accs):
        d0 = pl.multiple_of(t * SUBLANE, SUBLANE)
        s = []
        for r in range(2):
            cv = cb_ref[r, pl.ds(d0, SUBLANE), :]
            s.append(cv * _sigmoid(cv))
        new = list(accs)
        for j in range(nb):
            w = w_ref[0, pl.ds(d0, SUBLANE), j * LANE:(j + 1) * LANE]
            for r in range(2):
                new[r * nb + j] = accs[r * nb + j] + w * s[r]
        return tuple(new)

    init = tuple(jnp.zeros((SUBLANE, LANE), F32) for _ in range(2 * nb))
    accs = lax.fori_loop(0, d_model // SUBLANE, body, init)
    for r in range(2):
        for j in range(nb):
            o_ref[0, r:r + 1, j * LANE:(j + 1) * LANE] = (
                jnp.sum(accs[r * nb + j], axis=0, keepdims=True) + b_ref[0, :, j * LANE:(j + 1) * LANE])


def _modulation(c_ctx, c, w_ada, b_ada):
    depth, d_model, n_out = w_ada.shape
    tn = 9 * LANE
    assert n_out % tn == 0
    cb = jnp.broadcast_to(jnp.stack([c_ctx, c[0]])[:, :, None], (2, d_model, LANE))
    out = pl.pallas_call(
        functools.partial(_mod_kernel, d_model=d_model, tn=tn),
        out_shape=jax.ShapeDtypeStruct((depth, 2, n_out), F32),
        grid=(depth, n_out // tn),
        in_specs=[
            pl.BlockSpec((2, d_model, LANE), lambda l, j: (0, 0, 0)),
            pl.BlockSpec((1, d_model, tn), lambda l, j: (l, 0, j)),
            pl.BlockSpec((1, 1, tn), lambda l, j: (l, 0, j)),
        ],
        out_specs=pl.BlockSpec((1, 2, tn), lambda l, j: (l, 0, j)),
        compiler_params=_params(("arbitrary", "arbitrary")),
        name="adaln_mod",
    )(cb, w_ada, b_ada.reshape(depth, 1, n_out))
    return out.reshape(depth, 2, N_MOD, d_model)


def _ffn_body(x, mod_ref, g_ref, wgu_ref, wd_ref, k, gi):
    f = wd_ref.shape[0]
    h = _pre(x, g_ref[gi:gi + 1, :], mod_ref, k).astype(BF16)
    gate = _dot(h, wgu_ref[:, :f])
    up = _dot(h, wgu_ref[:, f:])
    act = (gate * _sigmoid(gate) * up).astype(BF16)
    y = _dot(act, wd_ref[...])
    return _post(x, y, g_ref[gi + 1:gi + 2, :], mod_ref, k, 0.5)


def _ffn_kernel(x_ref, mod_ref, g_ref, wgu_ref, wd_ref, o_ref, *, k, gi):
    o_ref[...] = _ffn_body(x_ref[...], mod_ref, g_ref, wgu_ref, wd_ref, k, gi)


def _ffn_first_kernel(x_ref, ctx_ref, renc_ref, cenc_ref, mod_ref, g_ref, wgu_ref, wd_ref,
                      o_ref, xs_ref, *, k, gi, n_lat_tiles, tm):
    i = pl.program_id(0)
    half = renc_ref.shape[1]
    rows_per_tile = tm // GRID_W

    @pl.when(i >= n_lat_tiles)
    def _():
        xs_ref[...] = ctx_ref[...]

    @pl.when(i < n_lat_tiles)
    def _():
        r0 = i * rows_per_tile
        xs_ref[:, half:] = x_ref[:, half:] + cenc_ref[...]
        for q in range(rows_per_tile):
            xs_ref[q * GRID_W:(q + 1) * GRID_W, :half] = (
                x_ref[q * GRID_W:(q + 1) * GRID_W, :half] + renc_ref[pl.ds(r0 + q, 1), :])

    o_ref[...] = _ffn_body(xs_ref[...], mod_ref, g_ref, wgu_ref, wd_ref, k, gi)


def _mod_spec(d_model, n_lat_tiles):
    return pl.BlockSpec((1, N_MOD, d_model), lambda i: (jnp.where(i >= n_lat_tiles, 0, 1), 0, 0))


def _ffn(xa, mod, g, wgu, wd, *, k, gi, n_lat_tiles, n_tiles):
    d = xa.shape[1]
    f = wd.shape[0]
    return pl.pallas_call(
        functools.partial(_ffn_kernel, k=k, gi=gi),
        out_shape=jax.ShapeDtypeStruct((n_tiles * TM, d), F32),
        grid=(n_tiles,),
        in_specs=[
            pl.BlockSpec((TM, d), lambda i: (i, 0)),
            _mod_spec(d, n_lat_tiles),
            _const_spec(g.shape),
            _const_spec((d, 2 * f)), _const_spec((f, d)),
        ],
        out_specs=pl.BlockSpec((TM, d), lambda i: (i, 0)),
        compiler_params=_params(("arbitrary",)),
        name=f"ffn_k{k}",
    )(xa, mod, g, wgu, wd)


def _ffn_first(x2, ctx2, renc, cenc, mod, g, wgu, wd, *, k, gi):
    lc, d = ctx2.shape
    l = x2.shape[0]
    f = wd.shape[0]
    n_lat_tiles = l // TM
    nt = (lc + l) // TM
    return pl.pallas_call(
        functools.partial(_ffn_first_kernel, k=k, gi=gi, n_lat_tiles=n_lat_tiles, tm=TM),
        out_shape=jax.ShapeDtypeStruct((lc + l, d), F32),
        grid=(nt,),
        in_specs=[
            pl.BlockSpec((TM, d), lambda i: (jnp.minimum(i, n_lat_tiles - 1), 0)),
            pl.BlockSpec((TM, d), lambda i: (jnp.maximum(i - n_lat_tiles, 0), 0)),
            _const_spec(renc.shape), _const_spec(cenc.shape),
            _mod_spec(d, n_lat_tiles),
            _const_spec(g.shape),
            _const_spec((d, 2 * f)), _const_spec((f, d)),
        ],
        out_specs=pl.BlockSpec((TM, d), lambda i: (i, 0)),
        scratch_shapes=[pltpu.VMEM((TM, d), F32)],
        compiler_params=_params(("arbitrary",)),
        name="ffn_first",
    )(x2, ctx2, renc, cenc, mod, g, wgu, wd)


def _inproj_kernel(x_ref, mod_ref, g_ref, win_ref, cs_ref, u_ref, p_ref, q_ref, *, d_ssm, head):
    h = _pre(x_ref[...], g_ref[2:3, :], mod_ref, 1).astype(BF16)
    hh = _dot(h, win_ref[...])
    u_ref[...] = hh[:, :d_ssm]
    for n in range(N_FNET_HEADS):
        lo = d_ssm + n * head
        pq = _dot(hh[:, lo:lo + head].astype(BF16), cs_ref[...])
        p_ref[:, n * head:(n + 1) * head] = pq[:, :head]
        q_ref[:, n * head:(n + 1) * head] = pq[:, head:]


def _inproj(xa, mod, g, win, cs, *, n_lat_tiles, d_ssm):
    n, d = xa.shape
    d_f = win.shape[1] - d_ssm
    head = d_f // N_FNET_HEADS
    nt = n // TM
    return pl.pallas_call(
        functools.partial(_inproj_kernel, d_ssm=d_ssm, head=head),
        out_shape=(jax.ShapeDtypeStruct((n, d_ssm), F32),
                   jax.ShapeDtypeStruct((n, d_f), F32),
                   jax.ShapeDtypeStruct((n, d_f), F32)),
        grid=(nt,),
        in_specs=[
            pl.BlockSpec((TM, d), lambda i: (i, 0)),
            _mod_spec(d, n_lat_tiles),
            _const_spec(g.shape),
            _const_spec(win.shape), _const_spec(cs.shape),
        ],
        out_specs=(pl.BlockSpec((TM, d_ssm), lambda i: (i, 0)),
                   pl.BlockSpec((TM, d_f), lambda i: (i, 0)),
                   pl.BlockSpec((TM, d_f), lambda i: (i, 0))),
        compiler_params=_params(("arbitrary",)),
        name="in_proj",
    )(xa, mod, g, win, cs)


NSEGS = 16
PAIRS = 2


def _cmul_add(ar, ai, xr, xi, zr, zi):
    return ar * xr - ai * xi + zr, ar * xi + ai * xr + zi


def _dot_nt(a, b):
    return lax.dot_general(a, b, (((1,), (1,)), ((), ())), precision=lax.Precision.HIGHEST,
                           preferred_element_type=F32)


def _s5_taps_kernel(e_re_ref, e_im_ref, ct_re_ref, ct_im_ref, kr_ref):
    t = CHUNK
    nh = ct_re_ref.shape[1]
    g8 = LANE // nh
    gw = e_re_ref.shape[2] // g8
    same = (lax.broadcasted_iota(jnp.int32, (LANE, g8 * gw), 0) // nh
            == lax.broadcasted_iota(jnp.int32, (LANE, g8 * gw), 1) // gw)
    taps = []
    for d in range(2):
        cre = jnp.where(same, jnp.concatenate([ct_re_ref[d]] * g8, axis=0), 0.0)
        cim = jnp.where(same, jnp.concatenate([ct_im_ref[d]] * g8, axis=0), 0.0)
        taps.append(_dot_nt(e_re_ref[d], cre) - _dot_nt(e_im_ref[d], cim))
    blk = (lax.broadcasted_iota(jnp.int32, (LANE, LANE), 0) // nh
           == lax.broadcasted_iota(jnp.int32, (LANE, LANE), 1) // nh)
    for lag in range(-(t - 1), t):
        if lag > 0:
            src = taps[0][lag * nh:(lag + 1) * nh]
        elif lag < 0:
            src = taps[1][-lag * nh:(1 - lag) * nh]
        else:
            src = taps[0][:nh] + taps[1][:nh]
        kr_ref[0, lag + t - 1] = jnp.where(blk, jnp.concatenate([src] * g8, axis=0), 0.0).astype(BF16)


def _s5_state_kernel(u_ref, ez_ref, mu_ref, d_ref, xp_ref, wz_ref, z_ref, xs_ref, *, seglen, ncc):
    w = LANE
    nc = d_ref.shape[1]
    nlat = NSEGS * seglen
    nv = NSEGS // SUBLANE
    nh = ez_ref.shape[2] // CHUNK
    ns = w // 2

    @pl.when(pl.program_id(1) == 0)
    def _():
        def fold(i, carry):
            r0 = pl.multiple_of(i * NSEGS, NSEGS)
            for tt in range(CHUNK):
                piece = u_ref[pl.ds(i * CHUNK + tt, NSEGS, stride=CHUNK * seglen), :]
                d_ref[0, pl.ds(r0, NSEGS), tt * w:(tt + 1) * w] = piece.astype(BF16)
            return carry

        lax.fori_loop(0, seglen, fold, 0)
        for tt in range(CHUNK):
            d_ref[0, nlat:nlat + ncc, tt * w:(tt + 1) * w] = (
                u_ref[pl.ds(nlat * CHUNK + tt, ncc, stride=CHUNK), :].astype(BF16))

    wz_ref[...] = jnp.zeros(wz_ref.shape, BF16)
    first = (lax.broadcasted_iota(jnp.int32, (nh, 4 * w), 1) % w) < ns
    for pb in range(PAIRS):
        pair = pl.program_id(1) * PAIRS + pb
        for i in range(CHUNK):
            ez = ez_ref[0, pb, i * nh:(i + 1) * nh, :]
            both = jnp.concatenate([jnp.where(first, ez, jnp.zeros_like(ez)),
                                    jnp.where(first, jnp.zeros_like(ez), ez)], axis=0)
            r0 = pl.multiple_of(i * w + pair * 2 * nh, 2 * nh)
            wz_ref[pl.ds(r0, 2 * nh), pb * 4 * w:(pb + 1) * 4 * w] = both

    for pb in range(PAIRS):
        zz = _dot(d_ref[0], wz_ref[:, pb * 4 * w:(pb + 1) * 4 * w])
        for c in range(4):
            z_ref[pb * 4 + c] = zz[:, c * w:(c + 1) * w]

    mu = [[mu_ref[0, pb, 0, c:c + 1, :] for c in range(4)] for pb in range(PAIRS)]
    mun = [[mu_ref[0, pb, 1, c:c + 1, :] for c in range(4)] for pb in range(PAIRS)]

    pre = []
    for pb in range(PAIRS):
        zc = [z_ref[pb * 4 + c, nlat:nlat + ncc, :] for c in range(4)]
        sf = (jnp.zeros((1, w), F32), jnp.zeros((1, w), F32))
        sb = (jnp.zeros((1, w), F32), jnp.zeros((1, w), F32))
        for j in range(ncc):
            jb = ncc - 1 - j
            xs_ref[pb * 4 + 0, nlat + j:nlat + j + 1, :] = sf[0]
            xs_ref[pb * 4 + 1, nlat + j:nlat + j + 1, :] = sf[1]
            xs_ref[pb * 4 + 2, nlat + jb:nlat + jb + 1, :] = sb[0]
            xs_ref[pb * 4 + 3, nlat + jb:nlat + jb + 1, :] = sb[1]
            sf = _cmul_add(mu[pb][0], mu[pb][1], sf[0], sf[1], zc[0][j:j + 1], zc[1][j:j + 1])
            sb = _cmul_add(mu[pb][2], mu[pb][3], sb[0], sb[1], zc[2][jb:jb + 1], zc[3][jb:jb + 1])
        pre.append((sf, sb))

    mub = [[jnp.broadcast_to(m, (SUBLANE, w)) for m in mu[pb]] for pb in range(PAIRS)]

    def rows(i, v):
        return pl.ds(pl.multiple_of(i * NSEGS + v * SUBLANE, SUBLANE), SUBLANE)

    def step(i, st):
        ib = seglen - 1 - i
        new = []
        for pb in range(PAIRS):
            for v in range(nv):
                k = (pb * nv + v) * 4
                m = mub[pb]
                fr, fi = _cmul_add(m[0], m[1], st[k], st[k + 1],
                                   z_ref[pb * 4 + 0, rows(i, v), :], z_ref[pb * 4 + 1, rows(i, v), :])
                br, bi = _cmul_add(m[2], m[3], st[k + 2], st[k + 3],
                                   z_ref[pb * 4 + 2, rows(ib, v), :], z_ref[pb * 4 + 3, rows(ib, v), :])
                new += [fr, fi, br, bi]
        return tuple(new)

    zero = jnp.zeros((SUBLANE, w), F32)
    fin = lax.fori_loop(0, seglen, step, (zero,) * (PAIRS * nv * 4))

    carry = []
    for pb in range(PAIRS):
        cf, cb = pre[pb]
        rows_f, rows_b = [None] * NSEGS, [None] * NSEGS
        for s in range(NSEGS):
            sr = NSEGS - 1 - s
            rows_f[s] = cf
            rows_b[sr] = cb
            kf = (pb * nv + s // SUBLANE) * 4
            kb = (pb * nv + sr // SUBLANE) * 4
            sl, srl = s % SUBLANE, sr % SUBLANE
            cf = _cmul_add(mun[pb][0], mun[pb][1], cf[0], cf[1], fin[kf][sl:sl + 1, :], fin[kf + 1][sl:sl + 1, :])
            cb = _cmul_add(mun[pb][2], mun[pb][3], cb[0], cb[1],
                           fin[kb + 2][srl:srl + 1, :], fin[kb + 3][srl:srl + 1, :])
        for v in range(nv):
            seg = slice(v * SUBLANE, (v + 1) * SUBLANE)
            carry += [jnp.concatenate([r[0] for r in rows_f[seg]], axis=0),
                      jnp.concatenate([r[1] for r in rows_f[seg]], axis=0),
                      jnp.concatenate([r[0] for r in rows_b[seg]], axis=0),
                      jnp.concatenate([r[1] for r in rows_b[seg]], axis=0)]

    def step2(i, st):
        ib = seglen - 1 - i
        for pb in range(PAIRS):
            for v in range(nv):
                k = (pb * nv + v) * 4
                xs_ref[pb * 4 + 0, rows(i, v), :] = st[k]
                xs_ref[pb * 4 + 1, rows(i, v), :] = st[k + 1]
                xs_ref[pb * 4 + 2, rows(ib, v), :] = st[k + 2]
                xs_ref[pb * 4 + 3, rows(ib, v), :] = st[k + 3]
        return step(i, st)

    lax.fori_loop(0, seglen, step2, tuple(carry))
    for k in range(PAIRS * 4):
        xp_ref[0, :, k * w:(k + 1) * w] = xs_ref[k].astype(BF16)


def _s5_readout_kernel(d_ref, xp_ref, kr_ref, sy_ref, rm_ref, dsk_ref, y_ref, bt_ref, wy_ref, yb_ref,
                       *, seglen, n_lat):
    w = LANE
    cw = 4 * w
    fw = CHUNK * w
    ncg = fw // cw

    @pl.when(pl.program_id(1) == 0)
    def _():
        for i in range(CHUNK):
            for j in range(CHUNK):
                bt_ref[i * w:(i + 1) * w, j * w:(j + 1) * w] = kr_ref[0, j - i + CHUNK - 1]
        nrow = sy_ref.shape[1]
        nh = rm_ref.shape[0] // CHUNK
        row = lax.broadcasted_iota(jnp.int32, (nrow, cw), 0)
        row_g = 2 * (row // cw) + (row % w) // (w // 2)
        col_g = (lax.broadcasted_iota(jnp.int32, (nrow, cw), 1) % w) // nh
        same = row_g == col_g
        for cg in range(ncg):
            cols = slice(cg * cw, (cg + 1) * cw)
            wy_ref[:, cols] = jnp.where(same, _dot(sy_ref[0], rm_ref[:, cols]), 0.0).astype(BF16)

    nrows = d_ref.shape[1]
    for cg in range(ncg):
        cols = slice(cg * cw, (cg + 1) * cw)
        yb_ref[:, cols] = (_dot(d_ref[0], bt_ref[:, cols]) + _dot(xp_ref[0], wy_ref[:, cols])
                           + d_ref[0, :, cols].astype(F32) * dsk_ref[0, :, cols])

    ngrp = nrows // NSEGS

    def unfold(j, carry):
        gi = pl.program_id(1) * ngrp + j
        r0 = pl.multiple_of(j * NSEGS, NSEGS)

        @pl.when(gi < seglen)
        def _():
            for tt in range(CHUNK):
                y_ref[pl.ds(gi * CHUNK + tt, NSEGS, stride=CHUNK * seglen), :] = (
                    yb_ref[pl.ds(r0, NSEGS), tt * w:(tt + 1) * w])

        @pl.when(gi >= seglen)
        def _():
            base = n_lat + (gi - seglen) * NSEGS * CHUNK
            for tt in range(CHUNK):
                y_ref[pl.ds(base + tt, NSEGS, stride=CHUNK), :] = yb_ref[pl.ds(r0, NSEGS), tt * w:(tt + 1) * w]

        return carry

    lax.fori_loop(0, ngrp, unfold, 0)


def _s5_operators(lam_re, lam_im, log_dt, b_re, b_im, c_re, c_im, d_skip, seglen):
    hp = lax.Precision.HIGHEST
    t = CHUNK
    ng, ns = lam_re.shape[1], lam_re.shape[2]
    nh = b_re.shape[-1]
    gp = ng * ns
    g8 = LANE // nh
    na = ng // g8
    nb = g8 // 2
    assert 2 * ns == LANE and nb % PAIRS == 0
    dt = jnp.exp(log_dt)[..., None]
    a, b = (lam_re * dt).reshape(2, gp), (lam_im * dt).reshape(2, gp)

    def lam_pow(m):
        mm = jnp.asarray(m, F32).reshape(-1, 1, 1)
        mag = jnp.exp(a[None] * mm)
        return mag * jnp.cos(b[None] * mm), mag * jnp.sin(b[None] * mm)

    pr, pi = lam_pow(np.arange(t + 1))
    lr, li = lam_re.reshape(2, gp), lam_im.reshape(2, gp)
    nr, ni = pr[1] - 1.0, pi[1]
    den = lr * lr + li * li
    qr, qi = (nr * lr + ni * li) / den, (ni * lr - nr * li) / den
    bt_re = b_re.transpose(0, 3, 1, 2).reshape(2, nh, gp)
    bt_im = b_im.transpose(0, 3, 1, 2).reshape(2, nh, gp)
    bb_re = qr[:, None] * bt_re - qi[:, None] * bt_im
    bb_im = qr[:, None] * bt_im + qi[:, None] * bt_re
    e_re = pr[:, :, None] * bb_re[None] - pi[:, :, None] * bb_im[None]
    e_im = pr[:, :, None] * bb_im[None] + pi[:, :, None] * bb_re[None]
    ct_re = c_re.transpose(0, 2, 1, 3).reshape(2, nh, gp)
    ct_im = c_im.transpose(0, 2, 1, 3).reshape(2, nh, gp)

    em_re = e_re[:t].transpose(1, 0, 2, 3).reshape(2, t * nh, gp)
    em_im = e_im[:t].transpose(1, 0, 2, 3).reshape(2, t * nh, gp)
    tw = gp // na
    kr = pl.pallas_call(
        _s5_taps_kernel,
        out_shape=jax.ShapeDtypeStruct((na, 2 * t - 1, LANE, LANE), BF16),
        grid=(na,),
        in_specs=[pl.BlockSpec((2, t * nh, tw), lambda i: (0, 0, i)),
                  pl.BlockSpec((2, t * nh, tw), lambda i: (0, 0, i)),
                  pl.BlockSpec((2, nh, tw), lambda i: (0, 0, i)),
                  pl.BlockSpec((2, nh, tw), lambda i: (0, 0, i))],
        out_specs=pl.BlockSpec((1, 2 * t - 1, LANE, LANE), lambda i: (i, 0, 0, 0)),
        compiler_params=_params(("arbitrary",)),
        name="s5_taps",
    )(em_re, em_im, ct_re, ct_im)

    ez = jnp.stack([jnp.stack([e_re[:t][::-1, 0], e_re[:t, 1]]), jnp.stack([e_im[:t][::-1, 0], e_im[:t, 1]])], axis=1)
    ez = ez.reshape(2, 2, t, nh, na, nb, LANE).transpose(4, 5, 2, 3, 0, 1, 6)
    ez = ez.reshape(na, nb, t * nh, 4 * LANE).astype(BF16)

    cgp_re = c_re.transpose(0, 1, 3, 2).reshape(2, gp, nh)
    cgp_im = c_im.transpose(0, 1, 3, 2).reshape(2, gp, nh)
    tile_j = np.tile(np.eye(nh, dtype=np.float32), (1, t))
    rep_j = np.kron(np.eye(t, dtype=np.float32), np.ones((1, nh), np.float32))
    ctl_re = jnp.dot(cgp_re, tile_j, precision=hp)
    ctl_im = jnp.dot(cgp_im, tile_j, precision=hp)
    sl = slice(1, t + 1)
    pw_re = jnp.stack([pr[sl, 0], pr[sl, 1][::-1]]).transpose(0, 2, 1)
    pw_im = jnp.stack([pi[sl, 0], pi[sl, 1][::-1]]).transpose(0, 2, 1)
    pw_re = jnp.dot(pw_re, rep_j, precision=hp)
    pw_im = jnp.dot(pw_im, rep_j, precision=hp)
    wy_re = ctl_re * pw_re - ctl_im * pw_im
    wy_im = ctl_re * pw_im + ctl_im * pw_re
    sy = jnp.stack([wy_re, -wy_im], axis=1).reshape(2, 2, na, nb, LANE, t * nh)
    sy = sy.transpose(2, 3, 0, 1, 4, 5).reshape(na, nb * 4 * LANE, t * nh).astype(BF16)
    r_mat = np.kron(np.eye(t, dtype=np.float32), np.tile(np.eye(nh, dtype=np.float32), (1, g8)))
    r_mat = jnp.asarray(r_mat, F32).astype(BF16)

    mr, mi = lam_pow([t, t * seglen])
    mu = jnp.stack([mr, mi], axis=2).reshape(2, 4, na, nb, LANE).transpose(2, 3, 0, 1, 4)
    dsk = jnp.tile(d_skip.reshape(na, 1, LANE), (1, 1, t))
    return kr, ez, sy, r_mat, mu, dsk


def _s5(u, l, ops):
    kr, ez, sy, r_mat, mu, dsk = ops
    n, d_ssm = u.shape
    na, nb = ez.shape[0], ez.shape[1]
    fw = CHUNK * LANE
    nc = n // CHUNK
    nlat = l // CHUNK
    ncc = nc - nlat
    seglen = nlat // NSEGS
    sw = PAIRS * 4 * LANE
    assert seglen * NSEGS == nlat and nc * CHUNK == n
    d, xp = pl.pallas_call(
        functools.partial(_s5_state_kernel, seglen=seglen, ncc=ncc),
        out_shape=(jax.ShapeDtypeStruct((na, nc, fw), BF16), jax.ShapeDtypeStruct((na, nc, nb * 4 * LANE), BF16)),
        grid=(na, nb // PAIRS),
        in_specs=[
            pl.BlockSpec((n, LANE), lambda a, b: (0, a)),
            pl.BlockSpec((1, PAIRS) + ez.shape[2:], lambda a, b: (a, b, 0, 0)),
            pl.BlockSpec((1, PAIRS, 2, 4, LANE), lambda a, b: (a, b, 0, 0, 0)),
        ],
        out_specs=(pl.BlockSpec((1, nc, fw), lambda a, b: (a, 0, 0)),
                   pl.BlockSpec((1, nc, sw), lambda a, b: (a, 0, b))),
        scratch_shapes=[pltpu.VMEM((fw, sw), BF16),
                        pltpu.VMEM((PAIRS * 4, nc, LANE), F32), pltpu.VMEM((PAIRS * 4, nc, LANE), F32)],
        compiler_params=_params(("arbitrary", "arbitrary")),
        name="s5_state",
    )(u, ez, mu)
    rbs = max(r for r in range(NSEGS, min(nc, 512) + 1, NSEGS) if nc % r == 0)
    nrb = nc // rbs
    assert NSEGS % 16 == 0 and ncc % NSEGS == 0
    return pl.pallas_call(
        functools.partial(_s5_readout_kernel, seglen=seglen, n_lat=l),
        out_shape=jax.ShapeDtypeStruct((n, d_ssm), F32),
        grid=(na, nrb),
        in_specs=[
            pl.BlockSpec((1, rbs, fw), lambda a, r: (a, r, 0)),
            pl.BlockSpec((1, rbs, nb * 4 * LANE), lambda a, r: (a, r, 0)),
            pl.BlockSpec((1,) + kr.shape[1:], lambda a, r: (a, 0, 0, 0)),
            pl.BlockSpec((1,) + sy.shape[1:], lambda a, r: (a, 0, 0)),
            _const_spec(r_mat.shape),
            pl.BlockSpec((1, 1, fw), lambda a, r: (a, 0, 0)),
        ],
        out_specs=pl.BlockSpec((n, LANE), lambda a, r: (0, a)),
        scratch_shapes=[pltpu.VMEM((fw, fw), BF16), pltpu.VMEM((nb * 4 * LANE, fw), BF16),
                        pltpu.VMEM((rbs, fw), F32)],
        compiler_params=_params(("arbitrary", "arbitrary")),
        name="s5_readout",
    )(d, xp, kr, sy, r_mat, dsk)


def _fft1_kernel(p_ref, q_ref, m_ref, o_ref, s_ref, *, nb):
    for n in range(nb):
        rhs = jnp.concatenate([p_ref[:, n, :], q_ref[:, n, :]], axis=0).astype(BF16)
        s_ref[:, n, :] = _dot(m_ref[...], rhs)
    o_ref[...] = s_ref[...].astype(BF16)


def _fft2_kernel(yr_ref, yi_ref, g_ref, o_ref, *, kb):
    for k in range(kb):
        rhs = jnp.concatenate([yr_ref[k], yi_ref[k]], axis=0)
        o_ref[:, k, :] = _dot(g_ref[k], rhs)


def _fft_ctx_kernel(p_ref, q_ref, m_ref, o_ref):
    rhs = jnp.concatenate([p_ref[...], q_ref[...]], axis=0).astype(BF16)
    o_ref[...] = _dot(m_ref[...], rhs)


def _fnet_tables(l, lc):
    r = math.isqrt(l)
    assert r * r == l
    k = np.arange(r)
    th = 2.0 * np.pi * (np.outer(k, k) % r) / r
    c, s = np.cos(th), np.sin(th)
    m1 = np.block([[c, -s], [-s, -c]])
    ta = 2.0 * np.pi * np.outer(k, k) / l
    ca, sa = jnp.asarray(np.cos(ta), F32)[:, None, :], jnp.asarray(np.sin(ta), F32)[:, None, :]
    cb, sb = jnp.asarray(c, F32)[None, :, :], jnp.asarray(s, F32)[None, :, :]
    scale = 1.0 / math.sqrt(l)
    g2 = jnp.concatenate([(ca * cb - sa * sb) * scale, (sa * cb + ca * sb) * scale], axis=-1)
    kc = np.arange(lc)
    thc = 2.0 * np.pi * (np.outer(kc, kc) % lc) / lc
    mc = np.concatenate([np.cos(thc), -np.sin(thc)], axis=1) / math.sqrt(lc)
    return (jnp.asarray(m1, F32).astype(BF16), g2.astype(BF16), jnp.asarray(mc, F32).astype(BF16))


def _fnet_lat(p, q, l, m1, g2):
    n, c = p.shape
    r = m1.shape[0] // 2
    assert n % r == 0
    p3 = p.reshape(n // r, r, c)
    q3 = q.reshape(n // r, r, c)
    nb = 16
    y1 = pl.pallas_call(
        functools.partial(_fft1_kernel, nb=nb),
        out_shape=jax.ShapeDtypeStruct((2 * r, r, c), BF16),
        grid=(r // nb,),
        in_specs=[pl.BlockSpec((r, nb, c), lambda j: (0, j, 0)),
                  pl.BlockSpec((r, nb, c), lambda j: (0, j, 0)),
                  _const_spec(m1.shape)],
        out_specs=pl.BlockSpec((2 * r, nb, c), lambda j: (0, j, 0)),
        scratch_shapes=[pltpu.VMEM((2 * r, nb, c), F32)],
        compiler_params=_params(("arbitrary",)),
        name="fft_stage1",
    )(p3, q3, m1)
    kb = 8
    out = pl.pallas_call(
        functools.partial(_fft2_kernel, kb=kb),
        out_shape=jax.ShapeDtypeStruct((r, r, c), F32),
        grid=(r // kb,),
        in_specs=[pl.BlockSpec((kb, r, c), lambda j: (j, 0, 0)),
                  pl.BlockSpec((kb, r, c), lambda j: (r // kb + j, 0, 0)),
                  pl.BlockSpec((kb, r, 2 * r), lambda j: (j, 0, 0))],
        out_specs=pl.BlockSpec((r, kb, c), lambda j: (0, j, 0)),
        compiler_params=_params(("arbitrary",)),
        name="fft_stage2",
    )(y1, y1, g2)
    return out.reshape(l, c)


def _fnet_ctx(p, q, l, lc, mc):
    c = p.shape[1]
    assert l % lc == 0
    return pl.pallas_call(
        _fft_ctx_kernel,
        out_shape=jax.ShapeDtypeStruct((lc, c), F32),
        grid=(1,),
        in_specs=[pl.BlockSpec((lc, c), lambda i: (l // lc, 0)),
                  pl.BlockSpec((lc, c), lambda i: (l // lc, 0)),
                  _const_spec(mc.shape)],
        out_specs=pl.BlockSpec((lc, c), lambda i: (0, 0)),
        compiler_params=_params(("arbitrary",)),
        name="fft_ctx",
    )(p, q, mc)


def _outproj_kernel(x_ref, ys_ref, yfl_ref, yfc_ref, mod_ref, g_ref, wglu_ref, wfm_ref, wout_ref,
                    o_ref, *, n_lat_tiles, d_ssm, head):
    is_ctx = pl.program_id(0) >= n_lat_tiles
    yf = jnp.where(is_ctx, yfc_ref[...], yfl_ref[...]).astype(BF16)
    h = _gelu_tanh(ys_ref[...])
    s = h * _sigmoid(_dot(h.astype(BF16), wglu_ref[...]))
    y = _dot(s.astype(BF16), wout_ref[:d_ssm, :])
    for n in range(N_FNET_HEADS):
        fm = _dot(yf[:, n * head:(n + 1) * head], wfm_ref[n])
        y = y + _dot(fm.astype(BF16), wout_ref[d_ssm + n * head:d_ssm + (n + 1) * head, :])
    o_ref[...] = _post(x_ref[...], y, g_ref[3:4, :], mod_ref, 1, 1.0)


def _outproj(xa, ys, yfl, yfc, mod, g, wglu, wfm, wout, *, n_lat_tiles, n_tiles):
    d = xa.shape[1]
    d_ssm = ys.shape[1]
    d_f = yfl.shape[1]
    head = d_f // N_FNET_HEADS
    return pl.pallas_call(
        functools.partial(_outproj_kernel, n_lat_tiles=n_lat_tiles, d_ssm=d_ssm, head=head),
        out_shape=jax.ShapeDtypeStruct((n_tiles * TM, d), F32),
        grid=(n_tiles,),
        in_specs=[
            pl.BlockSpec((TM, d), lambda i: (i, 0)),
            pl.BlockSpec((TM, d_ssm), lambda i: (i, 0)),
            pl.BlockSpec((TM, d_f), lambda i: (jnp.minimum(i, n_lat_tiles - 1), 0)),
            pl.BlockSpec((TM, d_f), lambda i: (jnp.maximum(i - n_lat_tiles, 0), 0)),
            _mod_spec(d, n_lat_tiles),
            _const_spec(g.shape),
            _const_spec(wglu.shape), _const_spec(wfm.shape), _const_spec(wout.shape),
        ],
        out_specs=pl.BlockSpec((TM, d), lambda i: (i, 0)),
        compiler_params=_params(("arbitrary",)),
        name="out_proj",
    )(xa, ys, yfl, yfc, mod, g, wglu, wfm, wout)


def _pos_tables(l, d):
    quarter = d // 4
    omega = 1.0 / (POS_BASE ** (jnp.arange(quarter, dtype=F32) / quarter))

    def enc(pv):
        ang = pv[:, None] * omega[None, :]
        return jnp.concatenate([jnp.sin(ang), jnp.cos(ang)], axis=-1)

    renc = enc(jnp.arange(l // GRID_W, dtype=F32))
    cenc = jnp.tile(enc(jnp.arange(GRID_W, dtype=F32)), (TM // GRID_W, 1))
    return renc, cenc


def kernel(x, c, ctx, c_ctx, w_ada, b_ada, norm_g, ffn1_gu, ffn1_down, ffn2_gu, ffn2_down, w_in, w_out,
           ssm_lam_re, ssm_lam_im, ssm_log_dt, ssm_b_re, ssm_b_im, ssm_c_re, ssm_c_im, ssm_d, w_glu, w_fmix):
    bsz, l, d = x.shape
    lc = ctx.shape[1]
    depth = w_ada.shape[0]
    d_ssm = w_glu.shape[1]
    head = w_fmix.shape[2]
    assert bsz == 1 and lc == TM and l % TM == 0 and TM % GRID_W == 0
    n_lat_tiles = l // TM
    n_all_tiles = (l + lc) // TM
    seglen = (l // CHUNK) // NSEGS

    mods = _modulation(c_ctx, c, w_ada, b_ada)
    renc, cenc = _pos_tables(l, d)
    m1, g2, mc = _fnet_tables(l, lc)
    kc = np.arange(head)
    thc = 2.0 * np.pi * (np.outer(kc, kc) % head) / head
    cs = jnp.asarray(np.concatenate([np.cos(thc), np.sin(thc)], axis=1) / math.sqrt(head), F32).astype(BF16)

    xa = None
    for li in range(depth):
        last = li == depth - 1
        mod = mods[li]
        g = norm_g[li]
        wgu1, wgu2 = ffn1_gu[li].astype(BF16), ffn2_gu[li].astype(BF16)
        wd1, wd2 = ffn1_down[li].astype(BF16), ffn2_down[li].astype(BF16)
        ops = _s5_operators(ssm_lam_re[li], ssm_lam_im[li], ssm_log_dt[li], ssm_b_re[li], ssm_b_im[li],
                            ssm_c_re[li], ssm_c_im[li], ssm_d[li], seglen)

        if li == 0:
            xa = _ffn_first(x[0], ctx[0], renc, cenc, mod, g, wgu1, wd1, k=0, gi=0)
        else:
            xa = _ffn(xa, mod, g, wgu1, wd1, k=0, gi=0, n_lat_tiles=n_lat_tiles, n_tiles=n_all_tiles)

        u, p, q = _inproj(xa, mod, g, w_in[li].astype(BF16), cs, n_lat_tiles=n_lat_tiles, d_ssm=d_ssm)
        ys = _s5(u, l, ops)
        yf_l = _fnet_lat(p, q, l, m1, g2)
        yf_c = _fnet_ctx(p, q, l, lc, mc)
        n_tiles = n_lat_tiles if last else n_all_tiles
        xa = _outproj(xa, ys, yf_l, yf_c, mod, g, w_glu[li].astype(BF16), w_fmix[li].astype(BF16),
                      w_out[li].astype(BF16), n_lat_tiles=n_lat_tiles, n_tiles=n_tiles)
        xa = _ffn(xa, mod, g, wgu2, wd2, k=2, gi=4, n_lat_tiles=n_lat_tiles, n_tiles=n_tiles)
    return xa[None]
```

```python
import functools
import math

import numpy as np
import jax
import jax.numpy as jnp
from jax import lax
from jax.experimental import pallas as pl
from jax.experimental.pallas import tpu as pltpu

F32 = jnp.float32
BF16 = jnp.bfloat16

LANE = 128
SUBLANE = 8
VMEM_LIMIT = 56 * 1024 * 1024

EPS = 1e-6
GRID_W = 64
POS_BASE = 10000.0
N_MOD = 9
SSM_GROUP = 16
STATE = 64
N_FNET_HEADS = 4
CHUNK = 16
TM = 1024
TM_FIRST = 512
CTX, LAT = 0, 1


def _dot(a, b):
    return jnp.dot(a, b, preferred_element_type=F32)


def _params(sem=None):
    return pltpu.CompilerParams(dimension_semantics=sem, vmem_limit_bytes=VMEM_LIMIT)


def _const_spec(shape):
    nd = len(shape)
    return pl.BlockSpec(shape, lambda *_: (0,) * nd, pipeline_mode=pl.Buffered(1))


def _rms(x, g):
    ms = jnp.mean(x * x, axis=-1, keepdims=True)
    return x * lax.rsqrt(ms + EPS) * g


def _pre(x, g, mod_ref, who, k):
    return _rms(x, g) * (1.0 + mod_ref[who, 3 * k + 1:3 * k + 2, :]) + mod_ref[who, 3 * k:3 * k + 1, :]


def _post(x, y, g, mod_ref, who, k, weight):
    return x + (weight * mod_ref[who, 3 * k + 2:3 * k + 3, :]) * _rms(y, g)


def _on_last_step(fn):
    pl.when(pl.program_id(0) == pl.num_programs(0) - 1)(fn)


def _sigmoid(x):
    return 1.0 / (1.0 + jnp.exp(-x))


def _gelu_tanh(x):
    return 0.5 * x * (1.0 + jnp.tanh(math.sqrt(2.0 / math.pi) * (x + 0.044715 * (x * x * x))))


def _mod_kernel(cb_ref, w_ref, b_ref, o_ref, *, d_model, tn):
    nb = tn // LANE

    def body(t, accs):
        d0 = pl.multiple_of(t * SUBLANE, SUBLANE)
        s = []
        for r in range(2):
            cv = cb_ref[r, pl.ds(d0, SUBLANE), :]
            s.append(cv * _sigmoid(cv))
        new = list(accs)
        for j in range(nb):
            w = w_ref[0, pl.ds(d0, SUBLANE), j * LANE:(j + 1) * LANE]
            for r in range(2):
                new[r * nb + j] = accs[r * nb + j] + w * s[r]
        return tuple(new)

    init = tuple(jnp.zeros((SUBLANE, LANE), F32) for _ in range(2 * nb))
    accs = lax.fori_loop(0, d_model // SUBLANE, body, init)
    for r in range(2):
        for j in range(nb):
            o_ref[0, r:r + 1, j * LANE:(j + 1) * LANE] = (
                jnp.sum(accs[r * nb + j], axis=0, keepdims=True) + b_ref[0, :, j * LANE:(j + 1) * LANE])


def _modulation(c_ctx, c, w_ada, b_ada):
    depth, d_model, n_out = w_ada.shape
    tn = 9 * LANE
    assert n_out % tn == 0
    cb = jnp.broadcast_to(jnp.stack([c_ctx, c[0]])[:, :, None], (2, d_model, LANE))
    out = pl.pallas_call(
        functools.partial(_mod_kernel, d_model=d_model, tn=tn),
        out_shape=jax.ShapeDtypeStruct((depth, 2, n_out), F32),
        grid=(depth, n_out // tn),
        in_specs=[
            pl.BlockSpec((2, d_model, LANE), lambda l, j: (0, 0, 0)),
            pl.BlockSpec((1, d_model, tn), lambda l, j: (l, 0, j)),
            pl.BlockSpec((1, 1, tn), lambda l, j: (l, 0, j)),
        ],
        out_specs=pl.BlockSpec((1, 2, tn), lambda l, j: (l, 0, j)),
        compiler_params=_params(("arbitrary", "arbitrary")),
        name="adaln_mod",
    )(cb, w_ada, b_ada.reshape(depth, 1, n_out))
    return out.reshape(depth, 2, N_MOD, d_model)


def _ffn_body(x, who, mod_ref, g_ref, wgu_ref, wd_ref, k, gi):
    f = wd_ref.shape[0]
    h = _pre(x, g_ref[gi:gi + 1, :], mod_ref, who, k).astype(BF16)
    gate = _dot(h, wgu_ref[:, :f])
    up = _dot(h, wgu_ref[:, f:])
    act = (gate * _sigmoid(gate) * up).astype(BF16)
    y = _dot(act, wd_ref[...])
    return _post(x, y, g_ref[gi + 1:gi + 2, :], mod_ref, who, k, 0.5)


def _ffn_kernel(*refs, k, gi, with_ctx):
    if with_ctx:
        x_ref, xc_ref, mod_ref, g_ref, wgu_ref, wd_ref, o_ref, oc_ref = refs
    else:
        x_ref, mod_ref, g_ref, wgu_ref, wd_ref, o_ref = refs
    o_ref[...] = _ffn_body(x_ref[...], LAT, mod_ref, g_ref, wgu_ref, wd_ref, k, gi)
    if with_ctx:
        def _():
            oc_ref[...] = _ffn_body(xc_ref[...], CTX, mod_ref, g_ref, wgu_ref, wd_ref, k, gi)
        _on_last_step(_)


def _ffn_first_kernel(x_ref, xc_ref, renc_ref, cenc_ref, mod_ref, g_ref, wgu_ref, wd_ref,
                      o_ref, oc_ref, xs_ref, *, k, gi):
    tm = x_ref.shape[0]
    half = renc_ref.shape[1]
    r0 = pl.program_id(0) * (tm // GRID_W)
    xs_ref[:, half:] = x_ref[:, half:] + cenc_ref[...]
    for q in range(tm // GRID_W):
        xs_ref[q * GRID_W:(q + 1) * GRID_W, :half] = (
            x_ref[q * GRID_W:(q + 1) * GRID_W, :half] + renc_ref[pl.ds(r0 + q, 1), :])
    o_ref[...] = _ffn_body(xs_ref[...], LAT, mod_ref, g_ref, wgu_ref, wd_ref, k, gi)

    def _():
        oc_ref[...] = _ffn_body(xc_ref[...], CTX, mod_ref, g_ref, wgu_ref, wd_ref, k, gi)
    _on_last_step(_)


def _tile_spec(cols, tm=TM):
    return pl.BlockSpec((tm, cols), lambda i: (i, 0))


def _whole_spec(shape):
    nd = len(shape)
    return pl.BlockSpec(shape, lambda *_: (0,) * nd)


def _ffn(xl, xc, mod, g, wgu, wd, *, k, gi):
    l, d = xl.shape
    with_ctx = xc is not None
    consts = [mod, g, wgu, wd]
    outs = pl.pallas_call(
        functools.partial(_ffn_kernel, k=k, gi=gi, with_ctx=with_ctx),
        out_shape=[jax.ShapeDtypeStruct(xl.shape, F32)] + ([jax.ShapeDtypeStruct(xc.shape, F32)] if with_ctx else []),
        grid=(l // TM,),
        in_specs=([_tile_spec(d)] + ([_whole_spec(xc.shape)] if with_ctx else [])
                  + [_const_spec(a.shape) for a in consts]),
        out_specs=[_tile_spec(d)] + ([_whole_spec(xc.shape)] if with_ctx else []),
        compiler_params=_params(("arbitrary",)),
        name=f"ffn_k{k}",
    )(*([xl] + ([xc] if with_ctx else []) + consts))
    return (outs[0], outs[1]) if with_ctx else (outs[0], None)


def _ffn_first(x2, ctx2, renc, cenc, mod, g, wgu, wd, *, k, gi):
    l, d = x2.shape
    consts = [renc, cenc, mod, g, wgu, wd]
    return pl.pallas_call(
        functools.partial(_ffn_first_kernel, k=k, gi=gi),
        out_shape=[jax.ShapeDtypeStruct(x2.shape, F32), jax.ShapeDtypeStruct(ctx2.shape, F32)],
        grid=(l // TM_FIRST,),
        in_specs=[_tile_spec(d, TM_FIRST), _whole_spec(ctx2.shape)] + [_const_spec(a.shape) for a in consts],
        out_specs=[_tile_spec(d, TM_FIRST), _whole_spec(ctx2.shape)],
        scratch_shapes=[pltpu.VMEM((TM_FIRST, d), F32)],
        compiler_params=_params(("arbitrary",)),
        name="ffn_first",
    )(x2, ctx2, *consts)


def _inproj_body(x, who, mod_ref, g_ref, win_ref, cs_ref, u_ref, p_ref, q_ref, d_ssm, head):
    h = _pre(x, g_ref[2:3, :], mod_ref, who, 1).astype(BF16)
    hh = _dot(h, win_ref[...])
    u_ref[...] = hh[:, :d_ssm]
    for n in range(N_FNET_HEADS):
        lo = d_ssm + n * head
        pq = _dot(hh[:, lo:lo + head].astype(BF16), cs_ref[...])
        p_ref[:, n * head:(n + 1) * head] = pq[:, :head]
        q_ref[:, n * head:(n + 1) * head] = pq[:, head:]


def _inproj_kernel(x_ref, xc_ref, mod_ref, g_ref, win_ref, cs_ref, u_ref, p_ref, q_ref, uc_ref, pc_ref, qc_ref,
                   *, d_ssm, head):
    _inproj_body(x_ref[...], LAT, mod_ref, g_ref, win_ref, cs_ref, u_ref, p_ref, q_ref, d_ssm, head)

    def _():
        _inproj_body(xc_ref[...], CTX, mod_ref, g_ref, win_ref, cs_ref, uc_ref, pc_ref, qc_ref, d_ssm, head)
    _on_last_step(_)


def _inproj(xl, xc, mod, g, win, cs, *, d_ssm):
    l, d = xl.shape
    lc = xc.shape[0]
    d_f = win.shape[1] - d_ssm
    head = d_f // N_FNET_HEADS
    consts = [mod, g, win, cs]
    widths = (d_ssm, d_f, d_f)
    return pl.pallas_call(
        functools.partial(_inproj_kernel, d_ssm=d_ssm, head=head),
        out_shape=[jax.ShapeDtypeStruct((l, wd_), F32) for wd_ in widths]
        + [jax.ShapeDtypeStruct((lc, wd_), F32) for wd_ in widths],
        grid=(l // TM,),
        in_specs=[_tile_spec(d), _whole_spec(xc.shape)] + [_const_spec(a.shape) for a in consts],
        out_specs=[_tile_spec(wd_) for wd_ in widths] + [_whole_spec((lc, wd_)) for wd_ in widths],
        compiler_params=_params(("arbitrary",)),
        name="in_proj",
    )(xl, xc, *consts)


NSEGS = 16
PAIRS = 2


def _cmul_add(ar, ai, xr, xi, zr, zi):
    return ar * xr - ai * xi + zr, ar * xi + ai * xr + zi


def _dot_nt(a, b):
    return lax.dot_general(a, b, (((1,), (1,)), ((), ())), precision=lax.Precision.HIGHEST,
                           preferred_element_type=F32)


def _s5_taps_kernel(e_re_ref, e_im_ref, ct_re_ref, ct_im_ref, kr_ref):
    t = CHUNK
    nh = ct_re_ref.shape[1]
    g8 = LANE // nh
    gw = e_re_ref.shape[2] // g8
    same = (lax.broadcasted_iota(jnp.int32, (LANE, g8 * gw), 0) // nh
            == lax.broadcasted_iota(jnp.int32, (LANE, g8 * gw), 1) // gw)
    taps = []
    for d in range(2):
        cre = jnp.where(same, jnp.concatenate([ct_re_ref[d]] * g8, axis=0), 0.0)
        cim = jnp.where(same, jnp.concatenate([ct_im_ref[d]] * g8, axis=0), 0.0)
        taps.append(_dot_nt(e_re_ref[d], cre) - _dot_nt(e_im_ref[d], cim))
    blk = (lax.broadcasted_iota(jnp.int32, (LANE, LANE), 0) // nh
           == lax.broadcasted_iota(jnp.int32, (LANE, LANE), 1) // nh)
    for lag in range(-(t - 1), t):
        if lag > 0:
            src = taps[0][lag * nh:(lag + 1) * nh]
        elif lag < 0:
            src = taps[1][-lag * nh:(1 - lag) * nh]
        else:
            src = taps[0][:nh] + taps[1][:nh]
        kr_ref[0, lag + t - 1] = jnp.where(blk, jnp.concatenate([src] * g8, axis=0), 0.0).astype(BF16)


def _s5_state_kernel(u_ref, uc_ref, ez_ref, mu_ref, d_ref, xp_ref, wz_ref, z_ref, xs_ref, *, seglen, ncc):
    w = LANE
    nc = d_ref.shape[1]
    nlat = NSEGS * seglen
    nv = NSEGS // SUBLANE
    nh = ez_ref.shape[2] // CHUNK
    ns = w // 2

    @pl.when(pl.program_id(1) == 0)
    def _():
        def fold(i, carry):
            r0 = pl.multiple_of(i * NSEGS, NSEGS)
            for tt in range(CHUNK):
                piece = u_ref[pl.ds(i * CHUNK + tt, NSEGS, stride=CHUNK * seglen), :]
                d_ref[0, pl.ds(r0, NSEGS), tt * w:(tt + 1) * w] = piece.astype(BF16)
            return carry

        lax.fori_loop(0, seglen, fold, 0)
        for tt in range(CHUNK):
            d_ref[0, nlat:nlat + ncc, tt * w:(tt + 1) * w] = (
                uc_ref[pl.ds(tt, ncc, stride=CHUNK), :].astype(BF16))

    wz_ref[...] = jnp.zeros(wz_ref.shape, BF16)
    first = (lax.broadcasted_iota(jnp.int32, (nh, 4 * w), 1) % w) < ns
    for pb in range(PAIRS):
        pair = pl.program_id(1) * PAIRS + pb
        for i in range(CHUNK):
            ez = ez_ref[0, pb, i * nh:(i + 1) * nh, :]
            both = jnp.concatenate([jnp.where(first, ez, jnp.zeros_like(ez)),
                                    jnp.where(first, jnp.zeros_like(ez), ez)], axis=0)
            r0 = pl.multiple_of(i * w + pair * 2 * nh, 2 * nh)
            wz_ref[pl.ds(r0, 2 * nh), pb * 4 * w:(pb + 1) * 4 * w] = both

    for pb in range(PAIRS):
        zz = _dot(d_ref[0], wz_ref[:, pb * 4 * w:(pb + 1) * 4 * w])
        for c in range(4):
            z_ref[pb * 4 + c] = zz[:, c * w:(c + 1) * w]

    mu = [[mu_ref[0, pb, 0, c:c + 1, :] for c in range(4)] for pb in range(PAIRS)]
    mun = [[mu_ref[0, pb, 1, c:c + 1, :] for c in range(4)] for pb in range(PAIRS)]

    pre = []
    for pb in range(PAIRS):
        zc = [z_ref[pb * 4 + c, nlat:nlat + ncc, :] for c in range(4)]
        sf = (jnp.zeros((1, w), F32), jnp.zeros((1, w), F32))
        sb = (jnp.zeros((1, w), F32), jnp.zeros((1, w), F32))
        for j in range(ncc):
            jb = ncc - 1 - j
            xs_ref[pb * 4 + 0, nlat + j:nlat + j + 1, :] = sf[0]
            xs_ref[pb * 4 + 1, nlat + j:nlat + j + 1, :] = sf[1]
            xs_ref[pb * 4 + 2, nlat + jb:nlat + jb + 1, :] = sb[0]
            xs_ref[pb * 4 + 3, nlat + jb:nlat + jb + 1, :] = sb[1]
            sf = _cmul_add(mu[pb][0], mu[pb][1], sf[0], sf[1], zc[0][j:j + 1], zc[1][j:j + 1])
            sb = _cmul_add(mu[pb][2], mu[pb][3], sb[0], sb[1], zc[2][jb:jb + 1], zc[3][jb:jb + 1])
        pre.append((sf, sb))

    mub = [[jnp.broadcast_to(m, (SUBLANE, w)) for m in mu[pb]] for pb in range(PAIRS)]

    def rows(i, v):
        return pl.ds(pl.multiple_of(i * NSEGS + v * SUBLANE, SUBLANE), SUBLANE)

    def step(i, st):
        ib = seglen - 1 - i
        new = []
        for pb in range(PAIRS):
            for v in range(nv):
                k = (pb * nv + v) * 4
                m = mub[pb]
                fr, fi = _cmul_add(m[0], m[1], st[k], st[k + 1],
                                   z_ref[pb * 4 + 0, rows(i, v), :], z_ref[pb * 4 + 1, rows(i, v), :])
                br, bi = _cmul_add(m[2], m[3], st[k + 2], st[k + 3],
                                   z_ref[pb * 4 + 2, rows(ib, v), :], z_ref[pb * 4 + 3, rows(ib, v), :])
                new += [fr, fi, br, bi]
        return tuple(new)

    zero = jnp.zeros((SUBLANE, w), F32)
    fin = lax.fori_loop(0, seglen, step, (zero,) * (PAIRS * nv * 4))

    carry = []
    for pb in range(PAIRS):
        cf, cb = pre[pb]
        rows_f, rows_b = [None] * NSEGS, [None] * NSEGS
        for s in range(NSEGS):
            sr = NSEGS - 1 - s
            rows_f[s] = cf
            rows_b[sr] = cb
            kf = (pb * nv + s // SUBLANE) * 4
            kb = (pb * nv + sr // SUBLANE) * 4
            sl, srl = s % SUBLANE, sr % SUBLANE
            cf = _cmul_add(mun[pb][0], mun[pb][1], cf[0], cf[1], fin[kf][sl:sl + 1, :], fin[kf + 1][sl:sl + 1, :])
            cb = _cmul_add(mun[pb][2], mun[pb][3], cb[0], cb[1],
                           fin[kb + 2][srl:srl + 1, :], fin[kb + 3][srl:srl + 1, :])
        for v in range(nv):
            seg = slice(v * SUBLANE, (v + 1) * SUBLANE)
            carry += [jnp.concatenate([r[0] for r in rows_f[seg]], axis=0),
                      jnp.concatenate([r[1] for r in rows_f[seg]], axis=0),
                      jnp.concatenate([r[0] for r in rows_b[seg]], axis=0),
                      jnp.concatenate([r[1] for r in rows_b[seg]], axis=0)]

    def step2(i, st):
        ib = seglen - 1 - i
        for pb in range(PAIRS):
            for v in range(nv):
                k = (pb * nv + v) * 4
                xs_ref[pb * 4 + 0, rows(i, v), :] = st[k]
                xs_ref[pb * 4 + 1, rows(i, v), :] = st[k + 1]
                xs_ref[pb * 4 + 2, rows(ib, v), :] = st[k + 2]
                xs_ref[pb * 4 + 3, rows(ib, v), :] = st[k + 3]
        return step(i, st)

    lax.fori_loop(0, seglen, step2, tuple(carry))
    for k in range(PAIRS * 4):
        xp_ref[0, :, k * w:(k + 1) * w] = xs_ref[k].astype(BF16)


def _s5_readout_kernel(d_ref, xp_ref, kr_ref, sy_ref, rm_ref, dsk_ref, y_ref, yc_ref, bt_ref, wy_ref, yb_ref,
                       *, seglen):
    w = LANE
    cw = 4 * w
    fw = CHUNK * w
    ncg = fw // cw

    @pl.when(pl.program_id(1) == 0)
    def _():
        for i in range(CHUNK):
            for j in range(CHUNK):
                bt_ref[i * w:(i + 1) * w, j * w:(j + 1) * w] = kr_ref[0, j - i + CHUNK - 1]
        nrow = sy_ref.shape[1]
        nh = rm_ref.shape[0] // CHUNK
        row = lax.broadcasted_iota(jnp.int32, (nrow, cw), 0)
        row_g = 2 * (row // cw) + (row % w) // (w // 2)
        col_g = (lax.broadcasted_iota(jnp.int32, (nrow, cw), 1) % w) // nh
        same = row_g == col_g
        for cg in range(ncg):
            cols = slice(cg * cw, (cg + 1) * cw)
            wy_ref[:, cols] = jnp.where(same, _dot(sy_ref[0], rm_ref[:, cols]), 0.0).astype(BF16)

    nrows = d_ref.shape[1]
    for cg in range(ncg):
        cols = slice(cg * cw, (cg + 1) * cw)
        yb_ref[:, cols] = (_dot(d_ref[0], bt_ref[:, cols]) + _dot(xp_ref[0], wy_ref[:, cols])
                           + d_ref[0, :, cols].astype(F32) * dsk_ref[0, :, cols])

    ngrp = nrows // NSEGS

    def unfold(j, carry):
        gi = pl.program_id(1) * ngrp + j
        r0 = pl.multiple_of(j * NSEGS, NSEGS)

        @pl.when(gi < seglen)
        def _():
            for tt in range(CHUNK):
                y_ref[pl.ds(gi * CHUNK + tt, NSEGS, stride=CHUNK * seglen), :] = (
                    yb_ref[pl.ds(r0, NSEGS), tt * w:(tt + 1) * w])

        @pl.when(gi >= seglen)
        def _():
            base = (gi - seglen) * NSEGS * CHUNK
            for tt in range(CHUNK):
                yc_ref[pl.ds(base + tt, NSEGS, stride=CHUNK), :] = yb_ref[pl.ds(r0, NSEGS), tt * w:(tt + 1) * w]

        return carry

    lax.fori_loop(0, ngrp, unfold, 0)


def _s5_operators(lam_re, lam_im, log_dt, b_re, b_im, c_re, c_im, d_skip, seglen):
    hp = lax.Precision.HIGHEST
    t = CHUNK
    ng, ns = lam_re.shape[1], lam_re.shape[2]
    nh = b_re.shape[-1]
    gp = ng * ns
    g8 = LANE // nh
    na = ng // g8
    nb = g8 // 2
    assert 2 * ns == LANE and nb % PAIRS == 0
    dt = jnp.exp(log_dt)[..., None]
    a, b = (lam_re * dt).reshape(2, gp), (lam_im * dt).reshape(2, gp)

    def lam_pow(m):
        mm = jnp.asarray(m, F32).reshape(-1, 1, 1)
        mag = jnp.exp(a[None] * mm)
        return mag * jnp.cos(b[None] * mm), mag * jnp.sin(b[None] * mm)

    pr, pi = lam_pow(np.arange(t + 1))
    lr, li = lam_re.reshape(2, gp), lam_im.reshape(2, gp)
    nr, ni = pr[1] - 1.0, pi[1]
    den = lr * lr + li * li
    qr, qi = (nr * lr + ni * li) / den, (ni * lr - nr * li) / den
    bt_re = b_re.transpose(0, 3, 1, 2).reshape(2, nh, gp)
    bt_im = b_im.transpose(0, 3, 1, 2).reshape(2, nh, gp)
    bb_re = qr[:, None] * bt_re - qi[:, None] * bt_im
    bb_im = qr[:, None] * bt_im + qi[:, None] * bt_re
    e_re = pr[:, :, None] * bb_re[None] - pi[:, :, None] * bb_im[None]
    e_im = pr[:, :, None] * bb_im[None] + pi[:, :, None] * bb_re[None]
    ct_re = c_re.transpose(0, 2, 1, 3).reshape(2, nh, gp)
    ct_im = c_im.transpose(0, 2, 1, 3).reshape(2, nh, gp)

    em_re = e_re[:t].transpose(1, 0, 2, 3).reshape(2, t * nh, gp)
    em_im = e_im[:t].transpose(1, 0, 2, 3).reshape(2, t * nh, gp)
    tw = gp // na
    kr = pl.pallas_call(
        _s5_taps_kernel,
        out_shape=jax.ShapeDtypeStruct((na, 2 * t - 1, LANE, LANE), BF16),
        grid=(na,),
        in_specs=[pl.BlockSpec((2, t * nh, tw), lambda i: (0, 0, i)),
                  pl.BlockSpec((2, t * nh, tw), lambda i: (0, 0, i)),
                  pl.BlockSpec((2, nh, tw), lambda i: (0, 0, i)),
                  pl.BlockSpec((2, nh, tw), lambda i: (0, 0, i))],
        out_specs=pl.BlockSpec((1, 2 * t - 1, LANE, LANE), lambda i: (i, 0, 0, 0)),
        compiler_params=_params(("arbitrary",)),
        name="s5_taps",
    )(em_re, em_im, ct_re, ct_im)

    ez = jnp.stack([jnp.stack([e_re[:t][::-1, 0], e_re[:t, 1]]), jnp.stack([e_im[:t][::-1, 0], e_im[:t, 1]])], axis=1)
    ez = ez.reshape(2, 2, t, nh, na, nb, LANE).transpose(4, 5, 2, 3, 0, 1, 6)
    ez = ez.reshape(na, nb, t * nh, 4 * LANE).astype(BF16)

    cgp_re = c_re.transpose(0, 1, 3, 2).reshape(2, gp, nh)
    cgp_im = c_im.transpose(0, 1, 3, 2).reshape(2, gp, nh)
    tile_j = np.tile(np.eye(nh, dtype=np.float32), (1, t))
    rep_j = np.kron(np.eye(t, dtype=np.float32), np.ones((1, nh), np.float32))
    ctl_re = jnp.dot(cgp_re, tile_j, precision=hp)
    ctl_im = jnp.dot(cgp_im, tile_j, precision=hp)
    sl = slice(1, t + 1)
    pw_re = jnp.stack([pr[sl, 0], pr[sl, 1][::-1]]).transpose(0, 2, 1)
    pw_im = jnp.stack([pi[sl, 0], pi[sl, 1][::-1]]).transpose(0, 2, 1)
    pw_re = jnp.dot(pw_re, rep_j, precision=hp)
    pw_im = jnp.dot(pw_im, rep_j, precision=hp)
    wy_re = ctl_re * pw_re - ctl_im * pw_im
    wy_im = ctl_re * pw_im + ctl_im * pw_re
    sy = jnp.stack([wy_re, -wy_im], axis=1).reshape(2, 2, na, nb, LANE, t * nh)
    sy = sy.transpose(2, 3, 0, 1, 4, 5).reshape(na, nb * 4 * LANE, t * nh).astype(BF16)
    r_mat = np.kron(np.eye(t, dtype=np.float32), np.tile(np.eye(nh, dtype=np.float32), (1, g8)))
    r_mat = jnp.asarray(r_mat, F32).astype(BF16)

    mr, mi = lam_pow([t, t * seglen])
    mu = jnp.stack([mr, mi], axis=2).reshape(2, 4, na, nb, LANE).transpose(2, 3, 0, 1, 4)
    dsk = jnp.tile(d_skip.reshape(na, 1, LANE), (1, 1, t))
    return kr, ez, sy, r_mat, mu, dsk


def _s5(u, uc, ops):
    kr, ez, sy, r_mat, mu, dsk = ops
    l, d_ssm = u.shape
    lc = uc.shape[0]
    na, nb = ez.shape[0], ez.shape[1]
    fw = CHUNK * LANE
    nlat, ncc = l // CHUNK, lc // CHUNK
    nc = nlat + ncc
    seglen = nlat // NSEGS
    sw = PAIRS * 4 * LANE
    assert seglen * NSEGS == nlat and ncc * CHUNK == lc
    d, xp = pl.pallas_call(
        functools.partial(_s5_state_kernel, seglen=seglen, ncc=ncc),
        out_shape=(jax.ShapeDtypeStruct((na, nc, fw), BF16), jax.ShapeDtypeStruct((na, nc, nb * 4 * LANE), BF16)),
        grid=(na, nb // PAIRS),
        in_specs=[
            pl.BlockSpec((l, LANE), lambda a, b: (0, a)),
            pl.BlockSpec((lc, LANE), lambda a, b: (0, a)),
            pl.BlockSpec((1, PAIRS) + ez.shape[2:], lambda a, b: (a, b, 0, 0)),
            pl.BlockSpec((1, PAIRS, 2, 4, LANE), lambda a, b: (a, b, 0, 0, 0)),
        ],
        out_specs=(pl.BlockSpec((1, nc, fw), lambda a, b: (a, 0, 0)),
                   pl.BlockSpec((1, nc, sw), lambda a, b: (a, 0, b))),
        scratch_shapes=[pltpu.VMEM((fw, sw), BF16),
                        pltpu.VMEM((PAIRS * 4, nc, LANE), F32), pltpu.VMEM((PAIRS * 4, nc, LANE), F32)],
        compiler_params=_params(("arbitrary", "arbitrary")),
        name="s5_state",
    )(u, uc, ez, mu)
    rbs = max(r for r in range(NSEGS, min(nc, 512) + 1, NSEGS) if nc % r == 0)
    nrb = nc // rbs
    assert NSEGS % 16 == 0 and ncc % NSEGS == 0
    return pl.pallas_call(
        functools.partial(_s5_readout_kernel, seglen=seglen),
        out_shape=(jax.ShapeDtypeStruct((l, d_ssm), F32), jax.ShapeDtypeStruct((lc, d_ssm), F32)),
        grid=(na, nrb),
        in_specs=[
            pl.BlockSpec((1, rbs, fw), lambda a, r: (a, r, 0)),
            pl.BlockSpec((1, rbs, nb * 4 * LANE), lambda a, r: (a, r, 0)),
            pl.BlockSpec((1,) + kr.shape[1:], lambda a, r: (a, 0, 0, 0)),
            pl.BlockSpec((1,) + sy.shape[1:], lambda a, r: (a, 0, 0)),
            _const_spec(r_mat.shape),
            pl.BlockSpec((1, 1, fw), lambda a, r: (a, 0, 0)),
        ],
        out_specs=(pl.BlockSpec((l, LANE), lambda a, r: (0, a)), pl.BlockSpec((lc, LANE), lambda a, r: (0, a))),
        scratch_shapes=[pltpu.VMEM((fw, fw), BF16), pltpu.VMEM((nb * 4 * LANE, fw), BF16),
                        pltpu.VMEM((rbs, fw), F32)],
        compiler_params=_params(("arbitrary", "arbitrary")),
        name="s5_readout",
    )(d, xp, kr, sy, r_mat, dsk)


def _fft1_kernel(p_ref, q_ref, m_ref, o_ref, s_ref, *, nb):
    for n in range(nb):
        rhs = jnp.concatenate([p_ref[:, n, :], q_ref[:, n, :]], axis=0).astype(BF16)
        s_ref[:, n, :] = _dot(m_ref[...], rhs)
    o_ref[...] = s_ref[...].astype(BF16)


def _fft2_kernel(yr_ref, yi_ref, g_ref, o_ref, *, kb):
    for k in range(kb):
        rhs = jnp.concatenate([yr_ref[k], yi_ref[k]], axis=0)
        o_ref[:, k, :] = _dot(g_ref[k], rhs)


def _fft_ctx_kernel(p_ref, q_ref, m_ref, o_ref):
    rhs = jnp.concatenate([p_ref[...], q_ref[...]], axis=0).astype(BF16)
    o_ref[...] = _dot(m_ref[...], rhs)


def _fnet_tables(l, lc):
    r = math.isqrt(l)
    assert r * r == l
    k = np.arange(r)
    th = 2.0 * np.pi * (np.outer(k, k) % r) / r
    c, s = np.cos(th), np.sin(th)
    m1 = np.block([[c, -s], [-s, -c]])
    ta = 2.0 * np.pi * np.outer(k, k) / l
    ca, sa = jnp.asarray(np.cos(ta), F32)[:, None, :], jnp.asarray(np.sin(ta), F32)[:, None, :]
    cb, sb = jnp.asarray(c, F32)[None, :, :], jnp.asarray(s, F32)[None, :, :]
    scale = 1.0 / math.sqrt(l)
    g2 = jnp.concatenate([(ca * cb - sa * sb) * scale, (sa * cb + ca * sb) * scale], axis=-1)
    kc = np.arange(lc)
    thc = 2.0 * np.pi * (np.outer(kc, kc) % lc) / lc
    mc = np.concatenate([np.cos(thc), -np.sin(thc)], axis=1) / math.sqrt(lc)
    return (jnp.asarray(m1, F32).astype(BF16), g2.astype(BF16), jnp.asarray(mc, F32).astype(BF16))


def _fnet_lat(p, q, m1, g2):
    l, c = p.shape
    r = m1.shape[0] // 2
    assert r * r == l
    p3 = p.reshape(r, r, c)
    q3 = q.reshape(r, r, c)
    nb = 16
    y1 = pl.pallas_call(
        functools.partial(_fft1_kernel, nb=nb),
        out_shape=jax.ShapeDtypeStruct((2 * r, r, c), BF16),
        grid=(r // nb,),
        in_specs=[pl.BlockSpec((r, nb, c), lambda j: (0, j, 0)),
                  pl.BlockSpec((r, nb, c), lambda j: (0, j, 0)),
                  _const_spec(m1.shape)],
        out_specs=pl.BlockSpec((2 * r, nb, c), lambda j: (0, j, 0)),
        scratch_shapes=[pltpu.VMEM((2 * r, nb, c), F32)],
        compiler_params=_params(("arbitrary",)),
        name="fft_stage1",
    )(p3, q3, m1)
    kb = 8
    out = pl.pallas_call(
        functools.partial(_fft2_kernel, kb=kb),
        out_shape=jax.ShapeDtypeStruct((r, r, c), F32),
        grid=(r // kb,),
        in_specs=[pl.BlockSpec((kb, r, c), lambda j: (j, 0, 0)),
                  pl.BlockSpec((kb, r, c), lambda j: (r // kb + j, 0, 0)),
                  pl.BlockSpec((kb, r, 2 * r), lambda j: (j, 0, 0))],
        out_specs=pl.BlockSpec((r, kb, c), lambda j: (0, j, 0)),
        compiler_params=_params(("arbitrary",)),
        name="fft_stage2",
    )(y1, y1, g2)
    return out.reshape(l, c)


def _fnet_ctx(pc, qc, mc):
    return pl.pallas_call(
        _fft_ctx_kernel,
        out_shape=jax.ShapeDtypeStruct(pc.shape, F32),
        grid=(1,),
        in_specs=[_whole_spec(pc.shape), _whole_spec(qc.shape), _const_spec(mc.shape)],
        out_specs=_whole_spec(pc.shape),
        compiler_params=_params(("arbitrary",)),
        name="fft_ctx",
    )(pc, qc, mc)


def _outproj_body(x, ys, yf, who, mod_ref, g_ref, wglu_ref, wfm_ref, wout_ref, d_ssm, head):
    yf = yf.astype(BF16)
    h = _gelu_tanh(ys)
    s = h * _sigmoid(_dot(h.astype(BF16), wglu_ref[...]))
    y = _dot(s.astype(BF16), wout_ref[:d_ssm, :])
    for n in range(N_FNET_HEADS):
        fm = _dot(yf[:, n * head:(n + 1) * head], wfm_ref[n])
        y = y + _dot(fm.astype(BF16), wout_ref[d_ssm + n * head:d_ssm + (n + 1) * head, :])
    return _post(x, y, g_ref[3:4, :], mod_ref, who, 1, 1.0)


def _outproj_kernel(*refs, d_ssm, head, with_ctx):
    if with_ctx:
        x_ref, ys_ref, yf_ref, xc_ref, ysc_ref, yfc_ref, mod_ref, g_ref, wglu_ref, wfm_ref, wout_ref, o_ref, oc_ref = refs
    else:
        x_ref, ys_ref, yf_ref, mod_ref, g_ref, wglu_ref, wfm_ref, wout_ref, o_ref = refs
    o_ref[...] = _outproj_body(x_ref[...], ys_ref[...], yf_ref[...], LAT, mod_ref, g_ref, wglu_ref, wfm_ref,
                               wout_ref, d_ssm, head)
    if with_ctx:
        def _():
            oc_ref[...] = _outproj_body(xc_ref[...], ysc_ref[...], yfc_ref[...], CTX, mod_ref, g_ref, wglu_ref,
                                        wfm_ref, wout_ref, d_ssm, head)
        _on_last_step(_)


def _outproj(lat, ctx, mod, g, wglu, wfm, wout):
    xl, ys, yf = lat
    l, d = xl.shape
    d_ssm, d_f = ys.shape[1], yf.shape[1]
    with_ctx = ctx is not None
    ctx = list(ctx) if with_ctx else []
    consts = [mod, g, wglu, wfm, wout]
    outs = pl.pallas_call(
        functools.partial(_outproj_kernel, d_ssm=d_ssm, head=d_f // N_FNET_HEADS, with_ctx=with_ctx),
        out_shape=[jax.ShapeDtypeStruct(xl.shape, F32)] + ([jax.ShapeDtypeStruct(ctx[0].shape, F32)] if with_ctx else []),
        grid=(l // TM,),
        in_specs=([_tile_spec(d), _tile_spec(d_ssm), _tile_spec(d_f)] + [_whole_spec(a.shape) for a in ctx]
                  + [_const_spec(a.shape) for a in consts]),
        out_specs=[_tile_spec(d)] + ([_whole_spec(ctx[0].shape)] if with_ctx else []),
        compiler_params=_params(("arbitrary",)),
        name="out_proj",
    )(xl, ys, yf, *ctx, *consts)
    return (outs[0], outs[1]) if with_ctx else (outs[0], None)


def _pos_tables(l, d):
    quarter = d // 4
    omega = 1.0 / (POS_BASE ** (jnp.arange(quarter, dtype=F32) / quarter))

    def enc(pv):
        ang = pv[:, None] * omega[None, :]
        return jnp.concatenate([jnp.sin(ang), jnp.cos(ang)], axis=-1)

    renc = enc(jnp.arange(l // GRID_W, dtype=F32))
    cenc = jnp.tile(enc(jnp.arange(GRID_W, dtype=F32)), (TM_FIRST // GRID_W, 1))
    return renc, cenc


def kernel(x, c, ctx, c_ctx, w_ada, b_ada, norm_g, ffn1_gu, ffn1_down, ffn2_gu, ffn2_down, w_in, w_out,
           ssm_lam_re, ssm_lam_im, ssm_log_dt, ssm_b_re, ssm_b_im, ssm_c_re, ssm_c_im, ssm_d, w_glu, w_fmix):
    bsz, l, d = x.shape
    lc = ctx.shape[1]
    depth = w_ada.shape[0]
    d_ssm = w_glu.shape[1]
    head = w_fmix.shape[2]
    assert bsz == 1 and l % TM == 0 and l % TM_FIRST == 0 and TM_FIRST % GRID_W == 0
    seglen = (l // CHUNK) // NSEGS

    mods = _modulation(c_ctx, c, w_ada, b_ada)
    renc, cenc = _pos_tables(l, d)
    m1, g2, mc = _fnet_tables(l, lc)
    kc = np.arange(head)
    thc = 2.0 * np.pi * (np.outer(kc, kc) % head) / head
    cs = jnp.asarray(np.concatenate([np.cos(thc), np.sin(thc)], axis=1) / math.sqrt(head), F32).astype(BF16)

    xl, xc = x[0], ctx[0]
    for li in range(depth):
        last = li == depth - 1
        mod = mods[li]
        g = norm_g[li]
        wgu1, wgu2 = ffn1_gu[li].astype(BF16), ffn2_gu[li].astype(BF16)
        wd1, wd2 = ffn1_down[li].astype(BF16), ffn2_down[li].astype(BF16)
        ops = _s5_operators(ssm_lam_re[li], ssm_lam_im[li], ssm_log_dt[li], ssm_b_re[li], ssm_b_im[li],
                            ssm_c_re[li], ssm_c_im[li], ssm_d[li], seglen)

        if li == 0:
            xl, xc = _ffn_first(xl, xc, renc, cenc, mod, g, wgu1, wd1, k=0, gi=0)
        else:
            xl, xc = _ffn(xl, xc, mod, g, wgu1, wd1, k=0, gi=0)

        u, p, q, uc, pc, qc = _inproj(xl, xc, mod, g, w_in[li].astype(BF16), cs, d_ssm=d_ssm)
        ys, ysc = _s5(u, uc, ops)
        yf = _fnet_lat(p, q, m1, g2)
        ctx_in = None if last else (xc, ysc, _fnet_ctx(pc, qc, mc))
        xl, xc = _outproj((xl, ys, yf), ctx_in, mod, g, w_glu[li].astype(BF16), w_fmix[li].astype(BF16),
                          w_out[li].astype(BF16))
        xl, xc = _ffn(xl, xc, mod, g, wgu2, wd2, k=2, gi=4)
    return xl[None]
```

```python
import functools
import math

import numpy as np
import jax
import jax.numpy as jnp
from jax import lax
from jax.experimental import pallas as pl
from jax.experimental.pallas import tpu as pltpu

F32 = jnp.float32
BF16 = jnp.bfloat16

LANE = 128
SUBLANE = 8
VMEM_LIMIT = 56 * 1024 * 1024

EPS = 1e-6
GRID_W = 64
POS_BASE = 10000.0
N_MOD = 9
SSM_GROUP = 16
STATE = 64
N_FNET_HEADS = 4
CHUNK = 16
TM = 512
CTX, LAT = 0, 1


def _dot(a, b):
    return jnp.dot(a, b, preferred_element_type=F32)


def _params(sem=None):
    return pltpu.CompilerParams(dimension_semantics=sem, vmem_limit_bytes=VMEM_LIMIT)


def _const_spec(shape):
    nd = len(shape)
    return pl.BlockSpec(shape, lambda *_: (0,) * nd, pipeline_mode=pl.Buffered(1))


def _rms(x, g):
    ms = jnp.mean(x * x, axis=-1, keepdims=True)
    return x * lax.rsqrt(ms + EPS) * g


def _pre(x, g, mod_ref, who, k):
    return _rms(x, g) * (1.0 + mod_ref[who, 3 * k + 1:3 * k + 2, :]) + mod_ref[who, 3 * k:3 * k + 1, :]


def _post(x, y, g, mod_ref, who, k, weight):
    return x + (weight * mod_ref[who, 3 * k + 2:3 * k + 3, :]) * _rms(y, g)


def _on_last_step(fn):
    pl.when(pl.program_id(0) == pl.num_programs(0) - 1)(fn)


def _sigmoid(x):
    return 1.0 / (1.0 + jnp.exp(-x))


def _gelu_tanh(x):
    return 0.5 * x * (1.0 + jnp.tanh(math.sqrt(2.0 / math.pi) * (x + 0.044715 * (x * x * x))))


def _mod_kernel(cb_ref, w_ref, b_ref, o_ref, *, d_model, tn):
    nb = tn // LANE

    def body(t, accs):
        d0 = pl.multiple_of(t * SUBLANE, SUBLANE)
        s = []
        for r in range(2):
            cv = cb_ref[r, pl.ds(d0, SUBLANE), :]
            s.append(cv * _sigmoid(cv))
        new = list(accs)
        for j in range(nb):
            w = w_ref[0, pl.ds(d0, SUBLANE), j * LANE:(j + 1) * LANE]
            for r in range(2):
                new[r * nb + j] = accs[r * nb + j] + w * s[r]
        return tuple(new)

    init = tuple(jnp.zeros((SUBLANE, LANE), F32) for _ in range(2 * nb))
    accs = lax.fori_loop(0, d_model // SUBLANE, body, init, unroll=4)
    for r in range(2):
        for j in range(nb):
            o_ref[0, r:r + 1, j * LANE:(j + 1) * LANE] = (
                jnp.sum(accs[r * nb + j], axis=0, keepdims=True) + b_ref[0, :, j * LANE:(j + 1) * LANE])


def _modulation(c_ctx, c, w_ada, b_ada):
    depth, d_model, n_out = w_ada.shape
    tn = 9 * LANE
    assert n_out % tn == 0
    cb = jnp.broadcast_to(jnp.stack([c_ctx, c[0]])[:, :, None], (2, d_model, LANE))
    out = pl.pallas_call(
        functools.partial(_mod_kernel, d_model=d_model, tn=tn),
        out_shape=jax.ShapeDtypeStruct((depth, 2, n_out), F32),
        grid=(depth, n_out // tn),
        in_specs=[
            pl.BlockSpec((2, d_model, LANE), lambda l, j: (0, 0, 0)),
            pl.BlockSpec((1, d_model, tn), lambda l, j: (l, 0, j)),
            pl.BlockSpec((1, 1, tn), lambda l, j: (l, 0, j)),
        ],
        out_specs=pl.BlockSpec((1, 2, tn), lambda l, j: (l, 0, j)),
        compiler_params=_params(("arbitrary", "arbitrary")),
        name="adaln_mod",
    )(cb, w_ada, b_ada.reshape(depth, 1, n_out))
    return out.reshape(depth, 2, N_MOD, d_model)


def _ffn_body(x, who, mod_ref, g_ref, wgu_ref, wd_ref, k, gi):
    f = wd_ref.shape[0]
    h = _pre(x, g_ref[gi:gi + 1, :], mod_ref, who, k).astype(BF16)
    gate = _dot(h, wgu_ref[:, :f])
    up = _dot(h, wgu_ref[:, f:])
    act = (gate * _sigmoid(gate) * up).astype(BF16)
    y = _dot(act, wd_ref[...])
    return _post(x, y, g_ref[gi + 1:gi + 2, :], mod_ref, who, k, 0.5)


def _inproj_body(x, who, mod_ref, g_ref, win_ref, cs_ref, u_ref, p_ref, q_ref):
    d_ssm = u_ref.shape[1]
    head = cs_ref.shape[0]
    h = _pre(x, g_ref[2:3, :], mod_ref, who, 1).astype(BF16)
    hh = _dot(h, win_ref[...])
    u_ref[...] = hh[:, :d_ssm]
    for n in range(N_FNET_HEADS):
        lo = d_ssm + n * head
        pq = _dot(hh[:, lo:lo + head].astype(BF16), cs_ref[...])
        p_ref[:, n * head:(n + 1) * head] = pq[:, :head]
        q_ref[:, n * head:(n + 1) * head] = pq[:, head:]


def _outproj_body(x, ys, yf, who, mod_ref, g_ref, wglu_ref, wfm_ref, wout_ref):
    d_ssm = wglu_ref.shape[0]
    head = wfm_ref.shape[1]
    yf = yf.astype(BF16)
    h = _gelu_tanh(ys)
    s = h * _sigmoid(_dot(h.astype(BF16), wglu_ref[...]))
    y = _dot(s.astype(BF16), wout_ref[:d_ssm, :])
    for n in range(N_FNET_HEADS):
        fm = _dot(yf[:, n * head:(n + 1) * head], wfm_ref[n])
        y = y + _dot(fm.astype(BF16), wout_ref[d_ssm + n * head:d_ssm + (n + 1) * head, :])
    return _post(x, y, g_ref[3:4, :], mod_ref, who, 1, 1.0)


def _layer_in_kernel(*refs, first):
    if first:
        (x_ref, xc_ref, renc_ref, cenc_ref, mod_ref, g_ref, wgu_ref, wd_ref, win_ref, cs_ref,
         o_ref, u_ref, p_ref, q_ref, oc_ref, uc_ref, pc_ref, qc_ref, xs_ref) = refs
        tm = x_ref.shape[0]
        half = renc_ref.shape[1]
        r0 = pl.program_id(0) * (tm // GRID_W)
        xs_ref[:, half:] = x_ref[:, half:] + cenc_ref[...]
        for q in range(tm // GRID_W):
            xs_ref[q * GRID_W:(q + 1) * GRID_W, :half] = (
                x_ref[q * GRID_W:(q + 1) * GRID_W, :half] + renc_ref[pl.ds(r0 + q, 1), :])
        x = xs_ref[...]
    else:
        (x_ref, xc_ref, mod_ref, g_ref, wgu_ref, wd_ref, win_ref, cs_ref,
         o_ref, u_ref, p_ref, q_ref, oc_ref, uc_ref, pc_ref, qc_ref) = refs
        x = x_ref[...]
    x1 = _ffn_body(x, LAT, mod_ref, g_ref, wgu_ref, wd_ref, 0, 0)
    o_ref[...] = x1
    _inproj_body(x1, LAT, mod_ref, g_ref, win_ref, cs_ref, u_ref, p_ref, q_ref)

    def _():
        xc1 = _ffn_body(xc_ref[...], CTX, mod_ref, g_ref, wgu_ref, wd_ref, 0, 0)
        oc_ref[...] = xc1
        _inproj_body(xc1, CTX, mod_ref, g_ref, win_ref, cs_ref, uc_ref, pc_ref, qc_ref)
    _on_last_step(_)


def _layer_out_kernel(*refs, with_ctx):
    if with_ctx:
        (x_ref, ys_ref, yf_ref, xc_ref, ysc_ref, yfc_ref, mod_ref, g_ref, wglu_ref, wfm_ref, wout_ref,
         wgu_ref, wd_ref, o_ref, oc_ref) = refs
    else:
        x_ref, ys_ref, yf_ref, mod_ref, g_ref, wglu_ref, wfm_ref, wout_ref, wgu_ref, wd_ref, o_ref = refs
    x2 = _outproj_body(x_ref[...], ys_ref[...], yf_ref[...], LAT, mod_ref, g_ref, wglu_ref, wfm_ref, wout_ref)
    o_ref[...] = _ffn_body(x2, LAT, mod_ref, g_ref, wgu_ref, wd_ref, 2, 4)
    if with_ctx:
        def _():
            xc2 = _outproj_body(xc_ref[...], ysc_ref[...], yfc_ref[...], CTX, mod_ref, g_ref, wglu_ref, wfm_ref,
                                wout_ref)
            oc_ref[...] = _ffn_body(xc2, CTX, mod_ref, g_ref, wgu_ref, wd_ref, 2, 4)
        _on_last_step(_)


def _tile_spec(cols):
    return pl.BlockSpec((TM, cols), lambda i: (i, 0))


def _whole_spec(shape):
    nd = len(shape)
    return pl.BlockSpec(shape, lambda *_: (0,) * nd)


def _layer_spec(arr, li):
    nd = arr.ndim - 1
    return pl.BlockSpec((None,) + arr.shape[1:], lambda *_: (li,) + (0,) * nd, pipeline_mode=pl.Buffered(1))


def _layer_in(xl, xc, pos, li, mods, gains, wgu, wd, win, cs, *, d_ssm):
    l, d = xl.shape
    lc = xc.shape[0]
    first = pos is not None
    d_f = win.shape[2] - d_ssm
    widths = (d, d_ssm, d_f, d_f)
    stacked = [mods, gains, wgu, wd, win]
    return pl.pallas_call(
        functools.partial(_layer_in_kernel, first=first),
        out_shape=[jax.ShapeDtypeStruct((l, wd_), F32) for wd_ in widths]
        + [jax.ShapeDtypeStruct((lc, wd_), F32) for wd_ in widths],
        grid=(l // TM,),
        in_specs=([_tile_spec(d), _whole_spec(xc.shape)] + ([_const_spec(a.shape) for a in pos] if first else [])
                  + [_layer_spec(a, li) for a in stacked] + [_const_spec(cs.shape)]),
        out_specs=[_tile_spec(wd_) for wd_ in widths] + [_whole_spec((lc, wd_)) for wd_ in widths],
        scratch_shapes=[pltpu.VMEM((TM, d), F32)] if first else [],
        compiler_params=_params(("arbitrary",)),
        name="layer_in",
    )(xl, xc, *(pos if first else ()), *stacked, cs)


def _layer_out(lat, ctx, li, mods, gains, wglu, wfm, wout, wgu, wd):
    xl, ys, yf = lat
    l, d = xl.shape
    with_ctx = ctx is not None
    ctx = list(ctx) if with_ctx else []
    stacked = [mods, gains, wglu, wfm, wout, wgu, wd]
    outs = pl.pallas_call(
        functools.partial(_layer_out_kernel, with_ctx=with_ctx),
        out_shape=[jax.ShapeDtypeStruct(xl.shape, F32)] + ([jax.ShapeDtypeStruct(ctx[0].shape, F32)] if with_ctx else []),
        grid=(l // TM,),
        in_specs=([_tile_spec(d), _tile_spec(ys.shape[1]), _tile_spec(yf.shape[1])]
                  + [_whole_spec(a.shape) for a in ctx] + [_layer_spec(a, li) for a in stacked]),
        out_specs=[_tile_spec(d)] + ([_whole_spec(ctx[0].shape)] if with_ctx else []),
        compiler_params=_params(("arbitrary",)),
        name="layer_out",
    )(xl, ys, yf, *ctx, *stacked)
    return (outs[0], outs[1]) if with_ctx else (outs[0], None)


NSEGS = 16
PAIRS = 2


def _cmul_add(ar, ai, xr, xi, zr, zi):
    return ar * xr - ai * xi + zr, ar * xi + ai * xr + zi


def _dot_nt(a, b):
    return lax.dot_general(a, b, (((1,), (1,)), ((), ())), precision=lax.Precision.HIGHEST,
                           preferred_element_type=F32)


def _s5_taps_kernel(e_re_ref, e_im_ref, ct_re_ref, ct_im_ref, kr_ref):
    t = CHUNK
    nh = ct_re_ref.shape[1]
    g8 = LANE // nh
    gw = e_re_ref.shape[2] // g8
    same = (lax.broadcasted_iota(jnp.int32, (LANE, g8 * gw), 0) // nh
            == lax.broadcasted_iota(jnp.int32, (LANE, g8 * gw), 1) // gw)
    taps = []
    for d in range(2):
        cre = jnp.where(same, jnp.concatenate([ct_re_ref[d]] * g8, axis=0), 0.0)
        cim = jnp.where(same, jnp.concatenate([ct_im_ref[d]] * g8, axis=0), 0.0)
        taps.append(_dot_nt(e_re_ref[d], cre) - _dot_nt(e_im_ref[d], cim))
    blk = (lax.broadcasted_iota(jnp.int32, (LANE, LANE), 0) // nh
           == lax.broadcasted_iota(jnp.int32, (LANE, LANE), 1) // nh)
    for lag in range(-(t - 1), t):
        if lag > 0:
            src = taps[0][lag * nh:(lag + 1) * nh]
        elif lag < 0:
            src = taps[1][-lag * nh:(1 - lag) * nh]
        else:
            src = taps[0][:nh] + taps[1][:nh]
        kr_ref[0, lag + t - 1] = jnp.where(blk, jnp.concatenate([src] * g8, axis=0), 0.0).astype(BF16)


def _s5_state_kernel(u_ref, uc_ref, ez_ref, mu_ref, d_ref, xp_ref, wz_ref, z_ref, xs_ref, *, seglen, ncc):
    w = LANE
    nc = d_ref.shape[1]
    nlat = NSEGS * seglen
    nv = NSEGS // SUBLANE
    nh = ez_ref.shape[2] // CHUNK
    ns = w // 2

    @pl.when(pl.program_id(1) == 0)
    def _():
        def fold(i, carry):
            r0 = pl.multiple_of(i * NSEGS, NSEGS)
            for tt in range(CHUNK):
                piece = u_ref[pl.ds(i * CHUNK + tt, NSEGS, stride=CHUNK * seglen), :]
                d_ref[0, pl.ds(r0, NSEGS), tt * w:(tt + 1) * w] = piece.astype(BF16)
            return carry

        lax.fori_loop(0, seglen, fold, 0)
        for tt in range(CHUNK):
            d_ref[0, nlat:nlat + ncc, tt * w:(tt + 1) * w] = (
                uc_ref[pl.ds(tt, ncc, stride=CHUNK), :].astype(BF16))

    wz_ref[...] = jnp.zeros(wz_ref.shape, BF16)
    first = (lax.broadcasted_iota(jnp.int32, (nh, 4 * w), 1) % w) < ns
    for pb in range(PAIRS):
        pair = pl.program_id(1) * PAIRS + pb
        for i in range(CHUNK):
            ez = ez_ref[0, pb, i * nh:(i + 1) * nh, :]
            both = jnp.concatenate([jnp.where(first, ez, jnp.zeros_like(ez)),
                                    jnp.where(first, jnp.zeros_like(ez), ez)], axis=0)
            r0 = pl.multiple_of(i * w + pair * 2 * nh, 2 * nh)
            wz_ref[pl.ds(r0, 2 * nh), pb * 4 * w:(pb + 1) * 4 * w] = both

    for pb in range(PAIRS):
        zz = _dot(d_ref[0], wz_ref[:, pb * 4 * w:(pb + 1) * 4 * w])
        for c in range(4):
            z_ref[pb * 4 + c] = zz[:, c * w:(c + 1) * w]

    mu = [[mu_ref[0, pb, 0, c:c + 1, :] for c in range(4)] for pb in range(PAIRS)]
    mun = [[mu_ref[0, pb, 1, c:c + 1, :] for c in range(4)] for pb in range(PAIRS)]

    pre = []
    for pb in range(PAIRS):
        zc = [z_ref[pb * 4 + c, nlat:nlat + ncc, :] for c in range(4)]
        sf = (jnp.zeros((1, w), F32), jnp.zeros((1, w), F32))
        sb = (jnp.zeros((1, w), F32), jnp.zeros((1, w), F32))
        for j in range(ncc):
            jb = ncc - 1 - j
            xs_ref[pb * 4 + 0, nlat + j:nlat + j + 1, :] = sf[0]
            xs_ref[pb * 4 + 1, nlat + j:nlat + j + 1, :] = sf[1]
            xs_ref[pb * 4 + 2, nlat + jb:nlat + jb + 1, :] = sb[0]
            xs_ref[pb * 4 + 3, nlat + jb:nlat + jb + 1, :] = sb[1]
            sf = _cmul_add(mu[pb][0], mu[pb][1], sf[0], sf[1], zc[0][j:j + 1], zc[1][j:j + 1])
            sb = _cmul_add(mu[pb][2], mu[pb][3], sb[0], sb[1], zc[2][jb:jb + 1], zc[3][jb:jb + 1])
        pre.append((sf, sb))

    mub = [[jnp.broadcast_to(m, (SUBLANE, w)) for m in mu[pb]] for pb in range(PAIRS)]

    def rows(i, v):
        return pl.ds(pl.multiple_of(i * NSEGS + v * SUBLANE, SUBLANE), SUBLANE)

    def step(i, st):
        ib = seglen - 1 - i
        new = []
        for pb in range(PAIRS):
            for v in range(nv):
                k = (pb * nv + v) * 4
                m = mub[pb]
                fr, fi = _cmul_add(m[0], m[1], st[k], st[k + 1],
                                   z_ref[pb * 4 + 0, rows(i, v), :], z_ref[pb * 4 + 1, rows(i, v), :])
                br, bi = _cmul_add(m[2], m[3], st[k + 2], st[k + 3],
                                   z_ref[pb * 4 + 2, rows(ib, v), :], z_ref[pb * 4 + 3, rows(ib, v), :])
                new += [fr, fi, br, bi]
        return tuple(new)

    zero = jnp.zeros((SUBLANE, w), F32)
    fin = lax.fori_loop(0, seglen, step, (zero,) * (PAIRS * nv * 4))

    carry = []
    for pb in range(PAIRS):
        cf, cb = pre[pb]
        rows_f, rows_b = [None] * NSEGS, [None] * NSEGS
        for s in range(NSEGS):
            sr = NSEGS - 1 - s
            rows_f[s] = cf
            rows_b[sr] = cb
            kf = (pb * nv + s // SUBLANE) * 4
            kb = (pb * nv + sr // SUBLANE) * 4
            sl, srl = s % SUBLANE, sr % SUBLANE
            cf = _cmul_add(mun[pb][0], mun[pb][1], cf[0], cf[1], fin[kf][sl:sl + 1, :], fin[kf + 1][sl:sl + 1, :])
            cb = _cmul_add(mun[pb][2], mun[pb][3], cb[0], cb[1],
                           fin[kb + 2][srl:srl + 1, :], fin[kb + 3][srl:srl + 1, :])
        for v in range(nv):
            seg = slice(v * SUBLANE, (v + 1) * SUBLANE)
            carry += [jnp.concatenate([r[0] for r in rows_f[seg]], axis=0),
                      jnp.concatenate([r[1] for r in rows_f[seg]], axis=0),
                      jnp.concatenate([r[0] for r in rows_b[seg]], axis=0),
                      jnp.concatenate([r[1] for r in rows_b[seg]], axis=0)]

    def step2(i, st):
        ib = seglen - 1 - i
        for pb in range(PAIRS):
            for v in range(nv):
                k = (pb * nv + v) * 4
                xs_ref[pb * 4 + 0, rows(i, v), :] = st[k]
                xs_ref[pb * 4 + 1, rows(i, v), :] = st[k + 1]
                xs_ref[pb * 4 + 2, rows(ib, v), :] = st[k + 2]
                xs_ref[pb * 4 + 3, rows(ib, v), :] = st[k + 3]
        return step(i, st)

    lax.fori_loop(0, seglen, step2, tuple(carry))
    for k in range(PAIRS * 4):
        xp_ref[0, :, k * w:(k + 1) * w] = xs_ref[k].astype(BF16)


def _s5_readout_kernel(d_ref, xp_ref, kr_ref, sy_ref, rm_ref, dsk_ref, y_ref, yc_ref, bt_ref, wy_ref, yb_ref,
                       *, seglen):
    w = LANE
    cw = 4 * w
    fw = CHUNK * w
    ncg = fw // cw

    @pl.when(pl.program_id(1) == 0)
    def _():
        for i in range(CHUNK):
            for j in range(CHUNK):
                bt_ref[i * w:(i + 1) * w, j * w:(j + 1) * w] = kr_ref[0, j - i + CHUNK - 1]
        nrow = sy_ref.shape[1]
        nh = rm_ref.shape[0] // CHUNK
        row = lax.broadcasted_iota(jnp.int32, (nrow, cw), 0)
        row_g = 2 * (row // cw) + (row % w) // (w // 2)
        col_g = (lax.broadcasted_iota(jnp.int32, (nrow, cw), 1) % w) // nh
        same = row_g == col_g
        for cg in range(ncg):
            cols = slice(cg * cw, (cg + 1) * cw)
            wy_ref[:, cols] = jnp.where(same, _dot(sy_ref[0], rm_ref[:, cols]), 0.0).astype(BF16)

    nrows = d_ref.shape[1]
    for cg in range(ncg):
        cols = slice(cg * cw, (cg + 1) * cw)
        yb_ref[:, cols] = (_dot(d_ref[0], bt_ref[:, cols]) + _dot(xp_ref[0], wy_ref[:, cols])
                           + d_ref[0, :, cols].astype(F32) * dsk_ref[0, :, cols])

    ngrp = nrows // NSEGS

    def unfold(j, carry):
        gi = pl.program_id(1) * ngrp + j
        r0 = pl.multiple_of(j * NSEGS, NSEGS)

        @pl.when(gi < seglen)
        def _():
            for tt in range(CHUNK):
                y_ref[pl.ds(gi * CHUNK + tt, NSEGS, stride=CHUNK * seglen), :] = (
                    yb_ref[pl.ds(r0, NSEGS), tt * w:(tt + 1) * w])

        @pl.when(gi >= seglen)
        def _():
            base = (gi - seglen) * NSEGS * CHUNK
            for tt in range(CHUNK):
                yc_ref[pl.ds(base + tt, NSEGS, stride=CHUNK), :] = yb_ref[pl.ds(r0, NSEGS), tt * w:(tt + 1) * w]

        return carry

    lax.fori_loop(0, ngrp, unfold, 0)


def _s5_operators(lam_re, lam_im, log_dt, b_re, b_im, c_re, c_im, d_skip, seglen):
    hp = lax.Precision.HIGHEST
    t = CHUNK
    ng, ns = lam_re.shape[1], lam_re.shape[2]
    nh = b_re.shape[-1]
    gp = ng * ns
    g8 = LANE // nh
    na = ng // g8
    nb = g8 // 2
    assert 2 * ns == LANE and nb % PAIRS == 0
    dt = jnp.exp(log_dt)[..., None]
    a, b = (lam_re * dt).reshape(2, gp), (lam_im * dt).reshape(2, gp)

    def lam_pow(m):
        mm = jnp.asarray(m, F32).reshape(-1, 1, 1)
        mag = jnp.exp(a[None] * mm)
        return mag * jnp.cos(b[None] * mm), mag * jnp.sin(b[None] * mm)

    pr, pi = lam_pow(np.arange(t + 1))
    lr, li = lam_re.reshape(2, gp), lam_im.reshape(2, gp)
    nr, ni = pr[1] - 1.0, pi[1]
    den = lr * lr + li * li
    qr, qi = (nr * lr + ni * li) / den, (ni * lr - nr * li) / den
    bt_re = b_re.transpose(0, 3, 1, 2).reshape(2, nh, gp)
    bt_im = b_im.transpose(0, 3, 1, 2).reshape(2, nh, gp)
    bb_re = qr[:, None] * bt_re - qi[:, None] * bt_im
    bb_im = qr[:, None] * bt_im + qi[:, None] * bt_re
    e_re = pr[:, :, None] * bb_re[None] - pi[:, :, None] * bb_im[None]
    e_im = pr[:, :, None] * bb_im[None] + pi[:, :, None] * bb_re[None]
    ct_re = c_re.transpose(0, 2, 1, 3).reshape(2, nh, gp)
    ct_im = c_im.transpose(0, 2, 1, 3).reshape(2, nh, gp)

    em_re = e_re[:t].transpose(1, 0, 2, 3).reshape(2, t * nh, gp)
    em_im = e_im[:t].transpose(1, 0, 2, 3).reshape(2, t * nh, gp)
    tw = gp // na
    kr = pl.pallas_call(
        _s5_taps_kernel,
        out_shape=jax.ShapeDtypeStruct((na, 2 * t - 1, LANE, LANE), BF16),
        grid=(na,),
        in_specs=[pl.BlockSpec((2, t * nh, tw), lambda i: (0, 0, i)),
                  pl.BlockSpec((2, t * nh, tw), lambda i: (0, 0, i)),
                  pl.BlockSpec((2, nh, tw), lambda i: (0, 0, i)),
                  pl.BlockSpec((2, nh, tw), lambda i: (0, 0, i))],
        out_specs=pl.BlockSpec((1, 2 * t - 1, LANE, LANE), lambda i: (i, 0, 0, 0)),
        compiler_params=_params(("arbitrary",)),
        name="s5_taps",
    )(em_re, em_im, ct_re, ct_im)

    def ez_piece(e):
        return e.reshape(t * nh, na, nb, LANE).transpose(1, 2, 0, 3)
    ez = jnp.concatenate([ez_piece(e_re[:t][::-1, 0]), ez_piece(e_im[:t][::-1, 0]),
                          ez_piece(e_re[:t, 1]), ez_piece(e_im[:t, 1])], axis=-1).astype(BF16)

    cgp_re = c_re.transpose(0, 1, 3, 2).reshape(2, gp, nh)
    cgp_im = c_im.transpose(0, 1, 3, 2).reshape(2, gp, nh)
    tile_j = np.tile(np.eye(nh, dtype=np.float32), (1, t))
    rep_j = np.kron(np.eye(t, dtype=np.float32), np.ones((1, nh), np.float32))
    ctl_re = jnp.dot(cgp_re, tile_j, precision=hp)
    ctl_im = jnp.dot(cgp_im, tile_j, precision=hp)
    sl = slice(1, t + 1)
    pw_re = jnp.stack([pr[sl, 0], pr[sl, 1][::-1]]).transpose(0, 2, 1)
    pw_im = jnp.stack([pi[sl, 0], pi[sl, 1][::-1]]).transpose(0, 2, 1)
    pw_re = jnp.dot(pw_re, rep_j, precision=hp)
    pw_im = jnp.dot(pw_im, rep_j, precision=hp)
    wy_re = ctl_re * pw_re - ctl_im * pw_im
    wy_im = ctl_re * pw_im + ctl_im * pw_re
    sy = jnp.stack([wy_re, -wy_im], axis=1).reshape(2, 2, na, nb, LANE, t * nh)
    sy = sy.transpose(2, 3, 0, 1, 4, 5).reshape(na, nb * 4 * LANE, t * nh).astype(BF16)
    r_mat = np.kron(np.eye(t, dtype=np.float32), np.tile(np.eye(nh, dtype=np.float32), (1, g8)))
    r_mat = jnp.asarray(r_mat, F32).astype(BF16)

    mr, mi = lam_pow([t, t * seglen])
    mu = jnp.stack([mr, mi], axis=2).reshape(2, 4, na, nb, LANE).transpose(2, 3, 0, 1, 4)
    dsk = jnp.tile(d_skip.reshape(na, 1, LANE), (1, 1, t))
    return kr, ez, sy, r_mat, mu, dsk


def _s5(u, uc, ops):
    kr, ez, sy, r_mat, mu, dsk = ops
    l, d_ssm = u.shape
    lc = uc.shape[0]
    na, nb = ez.shape[0], ez.shape[1]
    fw = CHUNK * LANE
    nlat, ncc = l // CHUNK, lc // CHUNK
    nc = nlat + ncc
    seglen = nlat // NSEGS
    sw = PAIRS * 4 * LANE
    assert seglen * NSEGS == nlat and ncc * CHUNK == lc
    d, xp = pl.pallas_call(
        functools.partial(_s5_state_kernel, seglen=seglen, ncc=ncc),
        out_shape=(jax.ShapeDtypeStruct((na, nc, fw), BF16), jax.ShapeDtypeStruct((na, nc, nb * 4 * LANE), BF16)),
        grid=(na, nb // PAIRS),
        in_specs=[
            pl.BlockSpec((l, LANE), lambda a, b: (0, a)),
            pl.BlockSpec((lc, LANE), lambda a, b: (0, a)),
            pl.BlockSpec((1, PAIRS) + ez.shape[2:], lambda a, b: (a, b, 0, 0)),
            pl.BlockSpec((1, PAIRS, 2, 4, LANE), lambda a, b: (a, b, 0, 0, 0)),
        ],
        out_specs=(pl.BlockSpec((1, nc, fw), lambda a, b: (a, 0, 0)),
                   pl.BlockSpec((1, nc, sw), lambda a, b: (a, 0, b))),
        scratch_shapes=[pltpu.VMEM((fw, sw), BF16),
                        pltpu.VMEM((PAIRS * 4, nc, LANE), F32), pltpu.VMEM((PAIRS * 4, nc, LANE), F32)],
        compiler_params=_params(("arbitrary", "arbitrary")),
        name="s5_state",
    )(u, uc, ez, mu)
    rbs = max(r for r in range(NSEGS, min(nc, 512) + 1, NSEGS) if nc % r == 0)
    nrb = nc // rbs
    assert NSEGS % 16 == 0 and ncc % NSEGS == 0
    return pl.pallas_call(
        functools.partial(_s5_readout_kernel, seglen=seglen),
        out_shape=(jax.ShapeDtypeStruct((l, d_ssm), F32), jax.ShapeDtypeStruct((lc, d_ssm), F32)),
        grid=(na, nrb),
        in_specs=[
            pl.BlockSpec((1, rbs, fw), lambda a, r: (a, r, 0)),
            pl.BlockSpec((1, rbs, nb * 4 * LANE), lambda a, r: (a, r, 0)),
            pl.BlockSpec((1,) + kr.shape[1:], lambda a, r: (a, 0, 0, 0)),
            pl.BlockSpec((1,) + sy.shape[1:], lambda a, r: (a, 0, 0)),
            _const_spec(r_mat.shape),
            pl.BlockSpec((1, 1, fw), lambda a, r: (a, 0, 0)),
        ],
        out_specs=(pl.BlockSpec((l, LANE), lambda a, r: (0, a)), pl.BlockSpec((lc, LANE), lambda a, r: (0, a))),
        scratch_shapes=[pltpu.VMEM((fw, fw), BF16), pltpu.VMEM((nb * 4 * LANE, fw), BF16),
                        pltpu.VMEM((rbs, fw), F32)],
        compiler_params=_params(("arbitrary", "arbitrary")),
        name="s5_readout",
    )(d, xp, kr, sy, r_mat, dsk)


def _fft1_kernel(p_ref, q_ref, m_ref, o_ref, s_ref, *, nb):
    for n in range(nb):
        rhs = jnp.concatenate([p_ref[:, n, :], q_ref[:, n, :]], axis=0).astype(BF16)
        s_ref[:, n, :] = _dot(m_ref[...], rhs)
    o_ref[...] = s_ref[...].astype(BF16)


def _fft2_kernel(yr_ref, yi_ref, g_ref, o_ref, *, kb):
    for k in range(kb):
        rhs = jnp.concatenate([yr_ref[k], yi_ref[k]], axis=0)
        o_ref[:, k, :] = _dot(g_ref[k], rhs)


def _fft_ctx_kernel(p_ref, q_ref, m_ref, o_ref):
    rhs = jnp.concatenate([p_ref[...], q_ref[...]], axis=0).astype(BF16)
    o_ref[...] = _dot(m_ref[...], rhs)


def _fnet_tables(l, lc):
    r = math.isqrt(l)
    assert r * r == l
    k = np.arange(r)
    th = 2.0 * np.pi * (np.outer(k, k) % r) / r
    c, s = np.cos(th), np.sin(th)
    m1 = np.block([[c, -s], [-s, -c]])
    ta = 2.0 * np.pi * np.outer(k, k) / l
    ca, sa = jnp.asarray(np.cos(ta), F32)[:, None, :], jnp.asarray(np.sin(ta), F32)[:, None, :]
    cb, sb = jnp.asarray(c, F32)[None, :, :], jnp.asarray(s, F32)[None, :, :]
    scale = 1.0 / math.sqrt(l)
    g2 = jnp.concatenate([(ca * cb - sa * sb) * scale, (sa * cb + ca * sb) * scale], axis=-1)
    kc = np.arange(lc)
    thc = 2.0 * np.pi * (np.outer(kc, kc) % lc) / lc
    mc = np.concatenate([np.cos(thc), -np.sin(thc)], axis=1) / math.sqrt(lc)
    return (jnp.asarray(m1, F32).astype(BF16), g2.astype(BF16), jnp.asarray(mc, F32).astype(BF16))


def _fnet_lat(p, q, m1, g2):
    l, c = p.shape
    r = m1.shape[0] // 2
    assert r * r == l
    p3 = p.reshape(r, r, c)
    q3 = q.reshape(r, r, c)
    nb = 16
    y1 = pl.pallas_call(
        functools.partial(_fft1_kernel, nb=nb),
        out_shape=jax.ShapeDtypeStruct((2 * r, r, c), BF16),
        grid=(r // nb,),
        in_specs=[pl.BlockSpec((r, nb, c), lambda j: (0, j, 0)),
                  pl.BlockSpec((r, nb, c), lambda j: (0, j, 0)),
                  _const_spec(m1.shape)],
        out_specs=pl.BlockSpec((2 * r, nb, c), lambda j: (0, j, 0)),
        scratch_shapes=[pltpu.VMEM((2 * r, nb, c), F32)],
        compiler_params=_params(("arbitrary",)),
        name="fft_stage1",
    )(p3, q3, m1)
    kb = 8
    out = pl.pallas_call(
        functools.partial(_fft2_kernel, kb=kb),
        out_shape=jax.ShapeDtypeStruct((r, r, c), F32),
        grid=(r // kb,),
        in_specs=[pl.BlockSpec((kb, r, c), lambda j: (j, 0, 0)),
                  pl.BlockSpec((kb, r, c), lambda j: (r // kb + j, 0, 0)),
                  pl.BlockSpec((kb, r, 2 * r), lambda j: (j, 0, 0))],
        out_specs=pl.BlockSpec((r, kb, c), lambda j: (0, j, 0)),
        compiler_params=_params(("arbitrary",)),
        name="fft_stage2",
    )(y1, y1, g2)
    return out.reshape(l, c)


def _fnet_ctx(pc, qc, mc):
    return pl.pallas_call(
        _fft_ctx_kernel,
        out_shape=jax.ShapeDtypeStruct(pc.shape, F32),
        grid=(1,),
        in_specs=[_whole_spec(pc.shape), _whole_spec(qc.shape), _const_spec(mc.shape)],
        out_specs=_whole_spec(pc.shape),
        compiler_params=_params(("arbitrary",)),
        name="fft_ctx",
    )(pc, qc, mc)


def _pos_tables(l, d):
    quarter = d // 4
    omega = 1.0 / (POS_BASE ** (jnp.arange(quarter, dtype=F32) / quarter))

    def enc(pv):
        ang = pv[:, None] * omega[None, :]
        return jnp.concatenate([jnp.sin(ang), jnp.cos(ang)], axis=-1)

    renc = enc(jnp.arange(l // GRID_W, dtype=F32))
    cenc = jnp.tile(enc(jnp.arange(GRID_W, dtype=F32)), (TM // GRID_W, 1))
    return renc, cenc


def kernel(x, c, ctx, c_ctx, w_ada, b_ada, norm_g, ffn1_gu, ffn1_down, ffn2_gu, ffn2_down, w_in, w_out,
           ssm_lam_re, ssm_lam_im, ssm_log_dt, ssm_b_re, ssm_b_im, ssm_c_re, ssm_c_im, ssm_d, w_glu, w_fmix):
    bsz, l, d = x.shape
    lc = ctx.shape[1]
    depth = w_ada.shape[0]
    d_ssm = w_glu.shape[1]
    head = w_fmix.shape[2]
    assert bsz == 1 and l % TM == 0 and TM % GRID_W == 0
    seglen = (l // CHUNK) // NSEGS

    mods = _modulation(c_ctx, c, w_ada, b_ada)
    pos = _pos_tables(l, d)
    m1, g2, mc = _fnet_tables(l, lc)
    kc = np.arange(head)
    thc = 2.0 * np.pi * (np.outer(kc, kc) % head) / head
    cs = jnp.asarray(np.concatenate([np.cos(thc), np.sin(thc)], axis=1) / math.sqrt(head), F32).astype(BF16)
    wgu1, wd1, wgu2, wd2 = (a.astype(BF16) for a in (ffn1_gu, ffn1_down, ffn2_gu, ffn2_down))
    win, wout, wglu, wfm = (a.astype(BF16) for a in (w_in, w_out, w_glu, w_fmix))

    xl, xc = x[0], ctx[0]
    for li in range(depth):
        last = li == depth - 1
        ops = _s5_operators(ssm_lam_re[li], ssm_lam_im[li], ssm_log_dt[li], ssm_b_re[li], ssm_b_im[li],
                            ssm_c_re[li], ssm_c_im[li], ssm_d[li], seglen)
        xl, u, p, q, xc, uc, pc, qc = _layer_in(xl, xc, pos if li == 0 else None, li, mods, norm_g, wgu1, wd1,
                                                win, cs, d_ssm=d_ssm)
        ys, ysc = _s5(u, uc, ops)
        yf = _fnet_lat(p, q, m1, g2)
        ctx_in = None if last else (xc, ysc, _fnet_ctx(pc, qc, mc))
        xl, xc = _layer_out((xl, ys, yf), ctx_in, li, mods, norm_g, wglu, wfm, wout, wgu2, wd2)
    return xl[None]
```

```python
import functools
import math

import numpy as np
import jax
import jax.numpy as jnp
from jax import lax
from jax.experimental import pallas as pl
from jax.experimental.pallas import tpu as pltpu

F32 = jnp.float32
BF16 = jnp.bfloat16

LANE = 128
SUBLANE = 8
VMEM_LIMIT = 56 * 1024 * 1024

EPS = 1e-6
GRID_W = 64
POS_BASE = 10000.0
N_MOD = 9
SSM_GROUP = 16
STATE = 64
N_FNET_HEADS = 4
CHUNK = 16
TM = 512
CTX, LAT = 0, 1


def _dot(a, b):
    return jnp.dot(a, b, preferred_element_type=F32)


def _params(sem=None):
    return pltpu.CompilerParams(dimension_semantics=sem, vmem_limit_bytes=VMEM_LIMIT)


def _const_spec(shape):
    nd = len(shape)
    return pl.BlockSpec(shape, lambda *_: (0,) * nd, pipeline_mode=pl.Buffered(1))


def _rms(x, g):
    ms = jnp.mean(x * x, axis=-1, keepdims=True)
    return x * lax.rsqrt(ms + EPS) * g


def _pre(x, g, mod_ref, who, k):
    return _rms(x, g) * (1.0 + mod_ref[who, 3 * k + 1:3 * k + 2, :]) + mod_ref[who, 3 * k:3 * k + 1, :]


def _post(x, y, g, mod_ref, who, k, weight):
    return x + (weight * mod_ref[who, 3 * k + 2:3 * k + 3, :]) * _rms(y, g)


def _on_last_step(fn):
    pl.when(pl.program_id(0) == pl.num_programs(0) - 1)(fn)


def _sigmoid(x):
    return 1.0 / (1.0 + jnp.exp(-x))


def _gelu_tanh(x):
    return 0.5 * x * (1.0 + jnp.tanh(math.sqrt(2.0 / math.pi) * (x + 0.044715 * (x * x * x))))


def _mod_kernel(cb_ref, w_ref, b_ref, o_ref, *, d_model, tn):
    nb = tn // LANE

    def body(t, accs):
        d0 = pl.multiple_of(t * SUBLANE, SUBLANE)
        s = []
        for r in range(2):
            cv = cb_ref[r, pl.ds(d0, SUBLANE), :]
            s.append(cv * _sigmoid(cv))
        new = list(accs)
        for j in range(nb):
            w = w_ref[0, pl.ds(d0, SUBLANE), j * LANE:(j + 1) * LANE]
            for r in range(2):
                new[r * nb + j] = accs[r * nb + j] + w * s[r]
        return tuple(new)

    init = tuple(jnp.zeros((SUBLANE, LANE), F32) for _ in range(2 * nb))
    accs = lax.fori_loop(0, d_model // SUBLANE, body, init, unroll=4)
    for r in range(2):
        for j in range(nb):
            o_ref[0, r:r + 1, j * LANE:(j + 1) * LANE] = (
                jnp.sum(accs[r * nb + j], axis=0, keepdims=True) + b_ref[0, :, j * LANE:(j + 1) * LANE])


def _modulation(c_ctx, c, w_ada, b_ada):
    depth, d_model, n_out = w_ada.shape
    tn = 9 * LANE
    assert n_out % tn == 0
    cb = jnp.broadcast_to(jnp.stack([c_ctx, c[0]])[:, :, None], (2, d_model, LANE))
    out = pl.pallas_call(
        functools.partial(_mod_kernel, d_model=d_model, tn=tn),
        out_shape=jax.ShapeDtypeStruct((depth, 2, n_out), F32),
        grid=(depth, n_out // tn),
        in_specs=[
            pl.BlockSpec((2, d_model, LANE), lambda l, j: (0, 0, 0)),
            pl.BlockSpec((1, d_model, tn), lambda l, j: (l, 0, j)),
            pl.BlockSpec((1, 1, tn), lambda l, j: (l, 0, j)),
        ],
        out_specs=pl.BlockSpec((1, 2, tn), lambda l, j: (l, 0, j)),
        compiler_params=_params(("arbitrary", "arbitrary")),
        name="adaln_mod",
    )(cb, w_ada, b_ada.reshape(depth, 1, n_out))
    return out.reshape(depth, 2, N_MOD, d_model)


def _ffn_body(x, who, mod_ref, g_ref, wgu_ref, wd_ref, k, gi):
    f = wd_ref.shape[0]
    h = _pre(x, g_ref[gi:gi + 1, :], mod_ref, who, k).astype(BF16)
    gate = _dot(h, wgu_ref[:, :f])
    up = _dot(h, wgu_ref[:, f:])
    act = (gate * _sigmoid(gate) * up).astype(BF16)
    y = _dot(act, wd_ref[...])
    return _post(x, y, g_ref[gi + 1:gi + 2, :], mod_ref, who, k, 0.5)


def _inproj_body(x, who, mod_ref, g_ref, win_ref, cs_ref, u_ref, p_ref, q_ref):
    d_ssm = u_ref.shape[1]
    head = cs_ref.shape[0]
    h = _pre(x, g_ref[2:3, :], mod_ref, who, 1).astype(BF16)
    hh = _dot(h, win_ref[...])
    u_ref[...] = hh[:, :d_ssm]
    for n in range(N_FNET_HEADS):
        lo = d_ssm + n * head
        pq = _dot(hh[:, lo:lo + head].astype(BF16), cs_ref[...])
        p_ref[:, n * head:(n + 1) * head] = pq[:, :head]
        q_ref[:, n * head:(n + 1) * head] = pq[:, head:]


def _outproj_body(x, ys, yf, who, mod_ref, g_ref, wglu_ref, wfm_ref, wout_ref):
    d_ssm = wglu_ref.shape[0]
    head = wfm_ref.shape[1]
    yf = yf.astype(BF16)
    h = _gelu_tanh(ys)
    s = h * _sigmoid(_dot(h.astype(BF16), wglu_ref[...]))
    y = _dot(s.astype(BF16), wout_ref[:d_ssm, :])
    for n in range(N_FNET_HEADS):
        fm = _dot(yf[:, n * head:(n + 1) * head], wfm_ref[n])
        y = y + _dot(fm.astype(BF16), wout_ref[d_ssm + n * head:d_ssm + (n + 1) * head, :])
    return _post(x, y, g_ref[3:4, :], mod_ref, who, 1, 1.0)


def _layer_in_kernel(*refs, first):
    if first:
        (x_ref, xc_ref, renc_ref, cenc_ref, mod_ref, g_ref, wgu_ref, wd_ref, win_ref, cs_ref,
         o_ref, u_ref, p_ref, q_ref, oc_ref, uc_ref, pc_ref, qc_ref, xs_ref) = refs
        tm = x_ref.shape[0]
        half = renc_ref.shape[1]
        r0 = pl.program_id(0) * (tm // GRID_W)
        xs_ref[:, half:] = x_ref[:, half:] + cenc_ref[...]
        for q in range(tm // GRID_W):
            xs_ref[q * GRID_W:(q + 1) * GRID_W, :half] = (
                x_ref[q * GRID_W:(q + 1) * GRID_W, :half] + renc_ref[pl.ds(r0 + q, 1), :])
        x = xs_ref[...]
    else:
        (x_ref, xc_ref, mod_ref, g_ref, wgu_ref, wd_ref, win_ref, cs_ref,
         o_ref, u_ref, p_ref, q_ref, oc_ref, uc_ref, pc_ref, qc_ref) = refs
        x = x_ref[...]
    x1 = _ffn_body(x, LAT, mod_ref, g_ref, wgu_ref, wd_ref, 0, 0)
    o_ref[...] = x1
    _inproj_body(x1, LAT, mod_ref, g_ref, win_ref, cs_ref, u_ref, p_ref, q_ref)

    def _():
        xc1 = _ffn_body(xc_ref[...], CTX, mod_ref, g_ref, wgu_ref, wd_ref, 0, 0)
        oc_ref[...] = xc1
        _inproj_body(xc1, CTX, mod_ref, g_ref, win_ref, cs_ref, uc_ref, pc_ref, qc_ref)
    _on_last_step(_)


def _layer_out_kernel(*refs, with_ctx):
    if with_ctx:
        (x_ref, ys_ref, yf_ref, xc_ref, ysc_ref, yfc_ref, mod_ref, g_ref, wglu_ref, wfm_ref, wout_ref,
         wgu_ref, wd_ref, o_ref, oc_ref) = refs
    else:
        x_ref, ys_ref, yf_ref, mod_ref, g_ref, wglu_ref, wfm_ref, wout_ref, wgu_ref, wd_ref, o_ref = refs
    x2 = _outproj_body(x_ref[...], ys_ref[...], yf_ref[...], LAT, mod_ref, g_ref, wglu_ref, wfm_ref, wout_ref)
    o_ref[...] = _ffn_body(x2, LAT, mod_ref, g_ref, wgu_ref, wd_ref, 2, 4)
    if with_ctx:
        def _():
            xc2 = _outproj_body(xc_ref[...], ysc_ref[...], yfc_ref[...], CTX, mod_ref, g_ref, wglu_ref, wfm_ref,
                                wout_ref)
            oc_ref[...] = _ffn_body(xc2, CTX, mod_ref, g_ref, wgu_ref, wd_ref, 2, 4)
        _on_last_step(_)


def _tile_spec(cols):
    return pl.BlockSpec((TM, cols), lambda i: (i, 0))


def _whole_spec(shape):
    nd = len(shape)
    return pl.BlockSpec(shape, lambda *_: (0,) * nd)


def _layer_spec(arr, li):
    nd = arr.ndim - 1
    return pl.BlockSpec((None,) + arr.shape[1:], lambda *_: (li,) + (0,) * nd, pipeline_mode=pl.Buffered(1))


def _layer_in(xl, xc, pos, li, mods, gains, wgu, wd, win, cs, *, d_ssm):
    l, d = xl.shape
    lc = xc.shape[0]
    first = pos is not None
    d_f = win.shape[2] - d_ssm
    widths = (d, d_ssm, d_f, d_f)
    stacked = [mods, gains, wgu, wd, win]
    return pl.pallas_call(
        functools.partial(_layer_in_kernel, first=first),
        out_shape=[jax.ShapeDtypeStruct((l, wd_), F32) for wd_ in widths]
        + [jax.ShapeDtypeStruct((lc, wd_), F32) for wd_ in widths],
        grid=(l // TM,),
        in_specs=([_tile_spec(d), _whole_spec(xc.shape)] + ([_const_spec(a.shape) for a in pos] if first else [])
                  + [_layer_spec(a, li) for a in stacked] + [_const_spec(cs.shape)]),
        out_specs=[_tile_spec(wd_) for wd_ in widths] + [_whole_spec((lc, wd_)) for wd_ in widths],
        scratch_shapes=[pltpu.VMEM((TM, d), F32)] if first else [],
        compiler_params=_params(("arbitrary",)),
        name="layer_in",
    )(xl, xc, *(pos if first else ()), *stacked, cs)


def _layer_out(lat, ctx, li, mods, gains, wglu, wfm, wout, wgu, wd):
    xl, ys, yf = lat
    l, d = xl.shape
    with_ctx = ctx is not None
    ctx = list(ctx) if with_ctx else []
    stacked = [mods, gains, wglu, wfm, wout, wgu, wd]
    outs = pl.pallas_call(
        functools.partial(_layer_out_kernel, with_ctx=with_ctx),
        out_shape=[jax.ShapeDtypeStruct(xl.shape, F32)] + ([jax.ShapeDtypeStruct(ctx[0].shape, F32)] if with_ctx else []),
        grid=(l // TM,),
        in_specs=([_tile_spec(d), _tile_spec(ys.shape[1]), _tile_spec(yf.shape[1])]
                  + [_whole_spec(a.shape) for a in ctx] + [_layer_spec(a, li) for a in stacked]),
        out_specs=[_tile_spec(d)] + ([_whole_spec(ctx[0].shape)] if with_ctx else []),
        compiler_params=_params(("arbitrary",)),
        name="layer_out",
    )(xl, ys, yf, *ctx, *stacked)
    return (outs[0], outs[1]) if with_ctx else (outs[0], None)


NSEGS = 16
PAIRS = 2


def _cmul_add(ar, ai, xr, xi, zr, zi):
    return ar * xr - ai * xi + zr, ar * xi + ai * xr + zi


def _dot_nt(a, b):
    return lax.dot_general(a, b, (((1,), (1,)), ((), ())), precision=lax.Precision.HIGHEST,
                           preferred_element_type=F32)


def _s5_taps_kernel(e_re_ref, e_im_ref, ct_re_ref, ct_im_ref, kr_ref):
    t = CHUNK
    nh = ct_re_ref.shape[1]
    g8 = LANE // nh
    gw = e_re_ref.shape[2] // g8
    same = (lax.broadcasted_iota(jnp.int32, (LANE, g8 * gw), 0) // nh
            == lax.broadcasted_iota(jnp.int32, (LANE, g8 * gw), 1) // gw)
    taps = []
    for d in range(2):
        cre = jnp.where(same, jnp.concatenate([ct_re_ref[d]] * g8, axis=0), 0.0)
        cim = jnp.where(same, jnp.concatenate([ct_im_ref[d]] * g8, axis=0), 0.0)
        taps.append(_dot_nt(e_re_ref[d], cre) - _dot_nt(e_im_ref[d], cim))
    blk = (lax.broadcasted_iota(jnp.int32, (LANE, LANE), 0) // nh
           == lax.broadcasted_iota(jnp.int32, (LANE, LANE), 1) // nh)
    for lag in range(-(t - 1), t):
        if lag > 0:
            src = taps[0][lag * nh:(lag + 1) * nh]
        elif lag < 0:
            src = taps[1][-lag * nh:(1 - lag) * nh]
        else:
            src = taps[0][:nh] + taps[1][:nh]
        kr_ref[0, lag + t - 1] = jnp.where(blk, jnp.concatenate([src] * g8, axis=0), 0.0).astype(BF16)


def _s5_state_kernel(u_ref, uc_ref, ez_ref, mu_ref, d_ref, xp_ref, wz_ref, z_ref, xs_ref, *, seglen, ncc):
    w = LANE
    nc = d_ref.shape[1]
    nlat = NSEGS * seglen
    nv = NSEGS // SUBLANE
    nh = ez_ref.shape[2] // CHUNK
    ns = w // 2

    @pl.when(pl.program_id(1) == 0)
    def _():
        def fold(i, carry):
            r0 = pl.multiple_of(i * NSEGS, NSEGS)
            tiles = [u_ref[pl.ds(pl.multiple_of((s * seglen + i) * CHUNK, CHUNK), CHUNK), :] for s in range(NSEGS)]
            d_ref[0, pl.ds(r0, NSEGS), :] = jnp.concatenate(tiles, axis=0).reshape(NSEGS, CHUNK * w).astype(BF16)
            return carry

        lax.fori_loop(0, seglen, fold, 0)
        d_ref[0, nlat:nlat + ncc, :] = uc_ref[...].reshape(ncc, CHUNK * w).astype(BF16)

    wz_ref[...] = jnp.zeros(wz_ref.shape, BF16)
    first = (lax.broadcasted_iota(jnp.int32, (nh, 4 * w), 1) % w) < ns
    for pb in range(PAIRS):
        pair = pl.program_id(1) * PAIRS + pb
        for i in range(CHUNK):
            ez = ez_ref[0, pb, i * nh:(i + 1) * nh, :]
            both = jnp.concatenate([jnp.where(first, ez, jnp.zeros_like(ez)),
                                    jnp.where(first, jnp.zeros_like(ez), ez)], axis=0)
            r0 = pl.multiple_of(i * w + pair * 2 * nh, 2 * nh)
            wz_ref[pl.ds(r0, 2 * nh), pb * 4 * w:(pb + 1) * 4 * w] = both

    for pb in range(PAIRS):
        zz = _dot(d_ref[0], wz_ref[:, pb * 4 * w:(pb + 1) * 4 * w])
        for c in range(4):
            z_ref[pb * 4 + c] = zz[:, c * w:(c + 1) * w]

    mu = [[mu_ref[0, pb, 0, c:c + 1, :] for c in range(4)] for pb in range(PAIRS)]
    mun = [[mu_ref[0, pb, 1, c:c + 1, :] for c in range(4)] for pb in range(PAIRS)]

    pre = []
    for pb in range(PAIRS):
        zc = [z_ref[pb * 4 + c, nlat:nlat + ncc, :] for c in range(4)]
        sf = (jnp.zeros((1, w), F32), jnp.zeros((1, w), F32))
        sb = (jnp.zeros((1, w), F32), jnp.zeros((1, w), F32))
        for j in range(ncc):
            jb = ncc - 1 - j
            xs_ref[pb * 4 + 0, nlat + j:nlat + j + 1, :] = sf[0]
            xs_ref[pb * 4 + 1, nlat + j:nlat + j + 1, :] = sf[1]
            xs_ref[pb * 4 + 2, nlat + jb:nlat + jb + 1, :] = sb[0]
            xs_ref[pb * 4 + 3, nlat + jb:nlat + jb + 1, :] = sb[1]
            sf = _cmul_add(mu[pb][0], mu[pb][1], sf[0], sf[1], zc[0][j:j + 1], zc[1][j:j + 1])
            sb = _cmul_add(mu[pb][2], mu[pb][3], sb[0], sb[1], zc[2][jb:jb + 1], zc[3][jb:jb + 1])
        pre.append((sf, sb))

    mub = [[jnp.broadcast_to(m, (SUBLANE, w)) for m in mu[pb]] for pb in range(PAIRS)]

    def rows(i, v):
        return pl.ds(pl.multiple_of(i * NSEGS + v * SUBLANE, SUBLANE), SUBLANE)

    def step(i, st):
        ib = seglen - 1 - i
        new = []
        for pb in range(PAIRS):
            for v in range(nv):
                k = (pb * nv + v) * 4
                m = mub[pb]
                fr, fi = _cmul_add(m[0], m[1], st[k], st[k + 1],
                                   z_ref[pb * 4 + 0, rows(i, v), :], z_ref[pb * 4 + 1, rows(i, v), :])
                br, bi = _cmul_add(m[2], m[3], st[k + 2], st[k + 3],
                                   z_ref[pb * 4 + 2, rows(ib, v), :], z_ref[pb * 4 + 3, rows(ib, v), :])
                new += [fr, fi, br, bi]
        return tuple(new)

    zero = jnp.zeros((SUBLANE, w), F32)
    fin = lax.fori_loop(0, seglen, step, (zero,) * (PAIRS * nv * 4))

    carry = []
    for pb in range(PAIRS):
        cf, cb = pre[pb]
        rows_f, rows_b = [None] * NSEGS, [None] * NSEGS
        for s in range(NSEGS):
            sr = NSEGS - 1 - s
            rows_f[s] = cf
            rows_b[sr] = cb
            kf = (pb * nv + s // SUBLANE) * 4
            kb = (pb * nv + sr // SUBLANE) * 4
            sl, srl = s % SUBLANE, sr % SUBLANE
            cf = _cmul_add(mun[pb][0], mun[pb][1], cf[0], cf[1], fin[kf][sl:sl + 1, :], fin[kf + 1][sl:sl + 1, :])
            cb = _cmul_add(mun[pb][2], mun[pb][3], cb[0], cb[1],
                           fin[kb + 2][srl:srl + 1, :], fin[kb + 3][srl:srl + 1, :])
        for v in range(nv):
            seg = slice(v * SUBLANE, (v + 1) * SUBLANE)
            carry += [jnp.concatenate([r[0] for r in rows_f[seg]], axis=0),
                      jnp.concatenate([r[1] for r in rows_f[seg]], axis=0),
                      jnp.concatenate([r[0] for r in rows_b[seg]], axis=0),
                      jnp.concatenate([r[1] for r in rows_b[seg]], axis=0)]

    def step2(i, st):
        ib = seglen - 1 - i
        for pb in range(PAIRS):
            for v in range(nv):
                k = (pb * nv + v) * 4
                xs_ref[pb * 4 + 0, rows(i, v), :] = st[k]
                xs_ref[pb * 4 + 1, rows(i, v), :] = st[k + 1]
                xs_ref[pb * 4 + 2, rows(ib, v), :] = st[k + 2]
                xs_ref[pb * 4 + 3, rows(ib, v), :] = st[k + 3]
        return step(i, st)

    lax.fori_loop(0, seglen, step2, tuple(carry))
    for k in range(PAIRS * 4):
        xp_ref[0, :, k * w:(k + 1) * w] = xs_ref[k].astype(BF16)


def _s5_readout_kernel(d_ref, xp_ref, kr_ref, sy_ref, rm_ref, dsk_ref, y_ref, yc_ref, bt_ref, wy_ref, yb_ref,
                       *, seglen):
    w = LANE
    cw = 4 * w
    fw = CHUNK * w
    ncg = fw // cw

    @pl.when(pl.program_id(1) == 0)
    def _():
        for i in range(CHUNK):
            for j in range(CHUNK):
                bt_ref[i * w:(i + 1) * w, j * w:(j + 1) * w] = kr_ref[0, j - i + CHUNK - 1]
        nrow = sy_ref.shape[1]
        nh = rm_ref.shape[0] // CHUNK
        row = lax.broadcasted_iota(jnp.int32, (nrow, cw), 0)
        row_g = 2 * (row // cw) + (row % w) // (w // 2)
        col_g = (lax.broadcasted_iota(jnp.int32, (nrow, cw), 1) % w) // nh
        same = row_g == col_g
        for cg in range(ncg):
            cols = slice(cg * cw, (cg + 1) * cw)
            wy_ref[:, cols] = jnp.where(same, _dot(sy_ref[0], rm_ref[:, cols]), 0.0).astype(BF16)

    nrows = d_ref.shape[1]
    for cg in range(ncg):
        cols = slice(cg * cw, (cg + 1) * cw)
        yb_ref[:, cols] = (_dot(d_ref[0], bt_ref[:, cols]) + _dot(xp_ref[0], wy_ref[:, cols])
                           + d_ref[0, :, cols].astype(F32) * dsk_ref[0, :, cols])

    ngrp = nrows // NSEGS

    def unfold(j, carry):
        gi = pl.program_id(1) * ngrp + j
        r0 = pl.multiple_of(j * NSEGS, NSEGS)
        tok = yb_ref[pl.ds(r0, NSEGS), :].reshape(NSEGS * CHUNK, w)

        @pl.when(gi < seglen)
        def _():
            for s in range(NSEGS):
                t0 = pl.multiple_of((s * seglen + gi) * CHUNK, CHUNK)
                y_ref[pl.ds(t0, CHUNK), :] = tok[s * CHUNK:(s + 1) * CHUNK]

        @pl.when(gi >= seglen)
        def _():
            t0 = pl.multiple_of((gi - seglen) * NSEGS * CHUNK, NSEGS * CHUNK)
            yc_ref[pl.ds(t0, NSEGS * CHUNK), :] = tok

        return carry

    lax.fori_loop(0, ngrp, unfold, 0)


def _s5_tables(lam_re, lam_im, log_dt, b_re, b_im, c_re, c_im, d_skip, *, seglen):
    hp = lax.Precision.HIGHEST
    t = CHUNK
    ng, ns = lam_re.shape[1], lam_re.shape[2]
    nh = b_re.shape[-1]
    gp = ng * ns
    g8 = LANE // nh
    na = ng // g8
    nb = g8 // 2
    assert 2 * ns == LANE and nb % PAIRS == 0
    dt = jnp.exp(log_dt)[..., None]
    a, b = (lam_re * dt).reshape(2, gp), (lam_im * dt).reshape(2, gp)

    def lam_pow(m):
        mm = jnp.asarray(m, F32).reshape(-1, 1, 1)
        mag = jnp.exp(a[None] * mm)
        return mag * jnp.cos(b[None] * mm), mag * jnp.sin(b[None] * mm)

    pr, pi = lam_pow(np.arange(t + 1))
    lr, li = lam_re.reshape(2, gp), lam_im.reshape(2, gp)
    nr, ni = pr[1] - 1.0, pi[1]
    den = lr * lr + li * li
    qr, qi = (nr * lr + ni * li) / den, (ni * lr - nr * li) / den
    bt_re = b_re.transpose(0, 3, 1, 2).reshape(2, nh, gp)
    bt_im = b_im.transpose(0, 3, 1, 2).reshape(2, nh, gp)
    bb_re = qr[:, None] * bt_re - qi[:, None] * bt_im
    bb_im = qr[:, None] * bt_im + qi[:, None] * bt_re
    e_re = pr[:, :, None] * bb_re[None] - pi[:, :, None] * bb_im[None]
    e_im = pr[:, :, None] * bb_im[None] + pi[:, :, None] * bb_re[None]
    ct_re = c_re.transpose(0, 2, 1, 3).reshape(2, nh, gp)
    ct_im = c_im.transpose(0, 2, 1, 3).reshape(2, nh, gp)

    em_re = e_re[:t].transpose(1, 0, 2, 3).reshape(2, t * nh, gp)
    em_im = e_im[:t].transpose(1, 0, 2, 3).reshape(2, t * nh, gp)

    def ez_piece(e):
        return e.reshape(t * nh, na, nb, LANE).transpose(1, 2, 0, 3)
    ez = jnp.concatenate([ez_piece(e_re[:t][::-1, 0]), ez_piece(e_im[:t][::-1, 0]),
                          ez_piece(e_re[:t, 1]), ez_piece(e_im[:t, 1])], axis=-1).astype(BF16)

    cgp_re = c_re.transpose(0, 1, 3, 2).reshape(2, gp, nh)
    cgp_im = c_im.transpose(0, 1, 3, 2).reshape(2, gp, nh)
    tile_j = np.tile(np.eye(nh, dtype=np.float32), (1, t))
    rep_j = np.kron(np.eye(t, dtype=np.float32), np.ones((1, nh), np.float32))
    ctl_re = jnp.dot(cgp_re, tile_j, precision=hp)
    ctl_im = jnp.dot(cgp_im, tile_j, precision=hp)
    sl = slice(1, t + 1)
    pw_re = jnp.stack([pr[sl, 0], pr[sl, 1][::-1]]).transpose(0, 2, 1)
    pw_im = jnp.stack([pi[sl, 0], pi[sl, 1][::-1]]).transpose(0, 2, 1)
    pw_re = jnp.dot(pw_re, rep_j, precision=hp)
    pw_im = jnp.dot(pw_im, rep_j, precision=hp)
    wy_re = ctl_re * pw_re - ctl_im * pw_im
    wy_im = ctl_re * pw_im + ctl_im * pw_re
    sy = jnp.stack([wy_re, -wy_im], axis=1).reshape(2, 2, na, nb, LANE, t * nh)
    sy = sy.transpose(2, 3, 0, 1, 4, 5).reshape(na, nb * 4 * LANE, t * nh).astype(BF16)
    mr, mi = lam_pow([t, t * seglen])
    mu = jnp.stack([mr, mi], axis=2).reshape(2, 4, na, nb, LANE).transpose(2, 3, 0, 1, 4)
    dsk = jnp.tile(d_skip.reshape(na, 1, LANE), (1, 1, t))
    return em_re, em_im, ct_re, ct_im, ez, sy, mu, dsk


def _s5_operators(params, seglen):
    em_re, em_im, ct_re, ct_im, ez, sy, mu, dsk = jax.vmap(functools.partial(_s5_tables, seglen=seglen))(*params)
    depth, _, rows, gp = em_re.shape
    nh = ct_re.shape[2]
    na = ez.shape[1]
    tw = gp // na
    nlag = 2 * CHUNK - 1
    kr = pl.pallas_call(
        _s5_taps_kernel,
        out_shape=jax.ShapeDtypeStruct((depth, na, nlag, LANE, LANE), BF16),
        grid=(depth, na),
        in_specs=[pl.BlockSpec((None, 2, rows, tw), lambda l, i: (l, 0, 0, i)),
                  pl.BlockSpec((None, 2, rows, tw), lambda l, i: (l, 0, 0, i)),
                  pl.BlockSpec((None, 2, nh, tw), lambda l, i: (l, 0, 0, i)),
                  pl.BlockSpec((None, 2, nh, tw), lambda l, i: (l, 0, 0, i))],
        out_specs=pl.BlockSpec((None, 1, nlag, LANE, LANE), lambda l, i: (l, i, 0, 0, 0)),
        compiler_params=_params(("arbitrary", "arbitrary")),
        name="s5_taps",
    )(em_re, em_im, ct_re, ct_im)
    r_mat = np.kron(np.eye(CHUNK, dtype=np.float32), np.tile(np.eye(nh, dtype=np.float32), (1, LANE // nh)))
    return kr, ez, sy, jnp.asarray(r_mat, F32).astype(BF16), mu, dsk


def _s5(u, uc, li, ops):
    kr, ez, sy, r_mat, mu, dsk = ops
    l, d_ssm = u.shape
    lc = uc.shape[0]
    na, nb = ez.shape[1], ez.shape[2]
    fw = CHUNK * LANE
    nlat, ncc = l // CHUNK, lc // CHUNK
    nc = nlat + ncc
    seglen = nlat // NSEGS
    sw = PAIRS * 4 * LANE
    assert seglen * NSEGS == nlat and ncc * CHUNK == lc
    d, xp = pl.pallas_call(
        functools.partial(_s5_state_kernel, seglen=seglen, ncc=ncc),
        out_shape=(jax.ShapeDtypeStruct((na, nc, fw), BF16), jax.ShapeDtypeStruct((na, nc, nb * 4 * LANE), BF16)),
        grid=(na, nb // PAIRS),
        in_specs=[
            pl.BlockSpec((l, LANE), lambda a, b: (0, a)),
            pl.BlockSpec((lc, LANE), lambda a, b: (0, a)),
            pl.BlockSpec((None, 1, PAIRS) + ez.shape[3:], lambda a, b: (li, a, b, 0, 0)),
            pl.BlockSpec((None, 1, PAIRS, 2, 4, LANE), lambda a, b: (li, a, b, 0, 0, 0)),
        ],
        out_specs=(pl.BlockSpec((1, nc, fw), lambda a, b: (a, 0, 0)),
                   pl.BlockSpec((1, nc, sw), lambda a, b: (a, 0, b))),
        scratch_shapes=[pltpu.VMEM((fw, sw), BF16),
                        pltpu.VMEM((PAIRS * 4, nc, LANE), F32), pltpu.VMEM((PAIRS * 4, nc, LANE), F32)],
        compiler_params=_params(("arbitrary", "arbitrary")),
        name="s5_state",
    )(u, uc, ez, mu)
    rbs = max(r for r in range(NSEGS, min(nc, 512) + 1, NSEGS) if nc % r == 0)
    nrb = nc // rbs
    assert NSEGS % 16 == 0 and ncc % NSEGS == 0
    return pl.pallas_call(
        functools.partial(_s5_readout_kernel, seglen=seglen),
        out_shape=(jax.ShapeDtypeStruct((l, d_ssm), F32), jax.ShapeDtypeStruct((lc, d_ssm), F32)),
        grid=(na, nrb),
        in_specs=[
            pl.BlockSpec((1, rbs, fw), lambda a, r: (a, r, 0)),
            pl.BlockSpec((1, rbs, nb * 4 * LANE), lambda a, r: (a, r, 0)),
            pl.BlockSpec((None, 1) + kr.shape[2:], lambda a, r: (li, a, 0, 0, 0)),
            pl.BlockSpec((None, 1) + sy.shape[2:], lambda a, r: (li, a, 0, 0)),
            _const_spec(r_mat.shape),
            pl.BlockSpec((None, 1, 1, fw), lambda a, r: (li, a, 0, 0)),
        ],
        out_specs=(pl.BlockSpec((l, LANE), lambda a, r: (0, a)), pl.BlockSpec((lc, LANE), lambda a, r: (0, a))),
        scratch_shapes=[pltpu.VMEM((fw, fw), BF16), pltpu.VMEM((nb * 4 * LANE, fw), BF16),
                        pltpu.VMEM((rbs, fw), F32)],
        compiler_params=_params(("arbitrary", "arbitrary")),
        name="s5_readout",
    )(d, xp, kr, sy, r_mat, dsk)


def _fft1_kernel(p_ref, q_ref, m_ref, o_ref, s_ref, *, nb):
    for n in range(nb):
        rhs = jnp.concatenate([p_ref[:, n, :], q_ref[:, n, :]], axis=0).astype(BF16)
        s_ref[:, n, :] = _dot(m_ref[...], rhs)
    o_ref[...] = s_ref[...].astype(BF16)


def _fft2_kernel(yr_ref, yi_ref, g_ref, o_ref, *, kb):
    for k in range(kb):
        rhs = jnp.concatenate([yr_ref[k], yi_ref[k]], axis=0)
        o_ref[:, k, :] = _dot(g_ref[k], rhs)


def _fft_ctx_kernel(p_ref, q_ref, m_ref, o_ref):
    rhs = jnp.concatenate([p_ref[...], q_ref[...]], axis=0).astype(BF16)
    o_ref[...] = _dot(m_ref[...], rhs)


def _fnet_tables(l, lc):
    r = math.isqrt(l)
    assert r * r == l
    k = np.arange(r)
    th = 2.0 * np.pi * (np.outer(k, k) % r) / r
    c, s = np.cos(th), np.sin(th)
    m1 = np.block([[c, -s], [-s, -c]])
    ta = 2.0 * np.pi * np.outer(k, k) / l
    ca, sa = jnp.asarray(np.cos(ta), F32)[:, None, :], jnp.asarray(np.sin(ta), F32)[:, None, :]
    cb, sb = jnp.asarray(c, F32)[None, :, :], jnp.asarray(s, F32)[None, :, :]
    scale = 1.0 / math.sqrt(l)
    g2 = jnp.concatenate([(ca * cb - sa * sb) * scale, (sa * cb + ca * sb) * scale], axis=-1)
    kc = np.arange(lc)
    thc = 2.0 * np.pi * (np.outer(kc, kc) % lc) / lc
    mc = np.concatenate([np.cos(thc), -np.sin(thc)], axis=1) / math.sqrt(lc)
    return (jnp.asarray(m1, F32).astype(BF16), g2.astype(BF16), jnp.asarray(mc, F32).astype(BF16))


def _fnet_lat(p, q, m1, g2):
    l, c = p.shape
    r = m1.shape[0] // 2
    assert r * r == l
    p3 = p.reshape(r, r, c)
    q3 = q.reshape(r, r, c)
    nb = 16
    y1 = pl.pallas_call(
        functools.partial(_fft1_kernel, nb=nb),
        out_shape=jax.ShapeDtypeStruct((2 * r, r, c), BF16),
        grid=(r // nb,),
        in_specs=[pl.BlockSpec((r, nb, c), lambda j: (0, j, 0)),
                  pl.BlockSpec((r, nb, c), lambda j: (0, j, 0)),
                  _const_spec(m1.shape)],
        out_specs=pl.BlockSpec((2 * r, nb, c), lambda j: (0, j, 0)),
        scratch_shapes=[pltpu.VMEM((2 * r, nb, c), F32)],
        compiler_params=_params(("arbitrary",)),
        name="fft_stage1",
    )(p3, q3, m1)
    kb = 8
    out = pl.pallas_call(
        functools.partial(_fft2_kernel, kb=kb),
        out_shape=jax.ShapeDtypeStruct((r, r, c), F32),
        grid=(r // kb,),
        in_specs=[pl.BlockSpec((kb, r, c), lambda j: (j, 0, 0)),
                  pl.BlockSpec((kb, r, c), lambda j: (r // kb + j, 0, 0)),
                  pl.BlockSpec((kb, r, 2 * r), lambda j: (j, 0, 0))],
        out_specs=pl.BlockSpec((r, kb, c), lambda j: (0, j, 0)),
        compiler_params=_params(("arbitrary",)),
        name="fft_stage2",
    )(y1, y1, g2)
    return out.reshape(l, c)


def _fnet_ctx(pc, qc, mc):
    return pl.pallas_call(
        _fft_ctx_kernel,
        out_shape=jax.ShapeDtypeStruct(pc.shape, F32),
        grid=(1,),
        in_specs=[_whole_spec(pc.shape), _whole_spec(qc.shape), _const_spec(mc.shape)],
        out_specs=_whole_spec(pc.shape),
        compiler_params=_params(("arbitrary",)),
        name="fft_ctx",
    )(pc, qc, mc)


def _pos_tables(l, d):
    quarter = d // 4
    omega = 1.0 / (POS_BASE ** (jnp.arange(quarter, dtype=F32) / quarter))

    def enc(pv):
        ang = pv[:, None] * omega[None, :]
        return jnp.concatenate([jnp.sin(ang), jnp.cos(ang)], axis=-1)

    renc = enc(jnp.arange(l // GRID_W, dtype=F32))
    cenc = jnp.tile(enc(jnp.arange(GRID_W, dtype=F32)), (TM // GRID_W, 1))
    return renc, cenc


def kernel(x, c, ctx, c_ctx, w_ada, b_ada, norm_g, ffn1_gu, ffn1_down, ffn2_gu, ffn2_down, w_in, w_out,
           ssm_lam_re, ssm_lam_im, ssm_log_dt, ssm_b_re, ssm_b_im, ssm_c_re, ssm_c_im, ssm_d, w_glu, w_fmix):
    bsz, l, d = x.shape
    lc = ctx.shape[1]
    depth = w_ada.shape[0]
    d_ssm = w_glu.shape[1]
    head = w_fmix.shape[2]
    assert bsz == 1 and l % TM == 0 and TM % GRID_W == 0
    seglen = (l // CHUNK) // NSEGS

    mods = _modulation(c_ctx, c, w_ada, b_ada)
    pos = _pos_tables(l, d)
    m1, g2, mc = _fnet_tables(l, lc)
    kc = np.arange(head)
    thc = 2.0 * np.pi * (np.outer(kc, kc) % head) / head
    cs = jnp.asarray(np.concatenate([np.cos(thc), np.sin(thc)], axis=1) / math.sqrt(head), F32).astype(BF16)
    wgu1, wd1, wgu2, wd2 = (a.astype(BF16) for a in (ffn1_gu, ffn1_down, ffn2_gu, ffn2_down))
    win, wout, wglu, wfm = (a.astype(BF16) for a in (w_in, w_out, w_glu, w_fmix))

    ops = _s5_operators((ssm_lam_re, ssm_lam_im, ssm_log_dt, ssm_b_re, ssm_b_im, ssm_c_re, ssm_c_im, ssm_d), seglen)

    xl, xc = x[0], ctx[0]
    for li in range(depth):
        last = li == depth - 1
        xl, u, p, q, xc, uc, pc, qc = _layer_in(xl, xc, pos if li == 0 else None, li, mods, norm_g, wgu1, wd1,
                                                win, cs, d_ssm=d_ssm)
        ys, ysc = _s5(u, uc, li, ops)
        yf = _fnet_lat(p, q, m1, g2)
        ctx_in = None if last else (xc, ysc, _fnet_ctx(pc, qc, mc))
        xl, xc = _layer_out((xl, ys, yf), ctx_in, li, mods, norm_g, wglu, wfm, wout, wgu2, wd2)
    return xl[None]
```

```python
import functools
import math

import numpy as np
import jax
import jax.numpy as jnp
from jax import lax
from jax.experimental import pallas as pl
from jax.experimental.pallas import tpu as pltpu

F32 = jnp.float32
BF16 = jnp.bfloat16

LANE = 128
SUBLANE = 8
VMEM_LIMIT = 56 * 1024 * 1024

EPS = 1e-6
GRID_W = 64
POS_BASE = 10000.0
N_MOD = 9
SSM_GROUP = 16
STATE = 64
N_FNET_HEADS = 4
CHUNK = 16
TM = 512
CTX, LAT = 0, 1


def _dot(a, b):
    return jnp.dot(a, b, preferred_element_type=F32)


def _params(sem=None):
    return pltpu.CompilerParams(dimension_semantics=sem, vmem_limit_bytes=VMEM_LIMIT)


def _const_spec(shape):
    nd = len(shape)
    return pl.BlockSpec(shape, lambda *_: (0,) * nd, pipeline_mode=pl.Buffered(1))


def _rms(x, g):
    ms = jnp.mean(x * x, axis=-1, keepdims=True)
    return x * lax.rsqrt(ms + EPS) * g


def _pre(x, g, mod_ref, who, k):
    return _rms(x, g) * (1.0 + mod_ref[who, 3 * k + 1:3 * k + 2, :]) + mod_ref[who, 3 * k:3 * k + 1, :]


def _post(x, y, g, mod_ref, who, k, weight):
    return x + (weight * mod_ref[who, 3 * k + 2:3 * k + 3, :]) * _rms(y, g)


def _on_last_step(fn):
    pl.when(pl.program_id(0) == pl.num_programs(0) - 1)(fn)


def _sigmoid(x):
    return 1.0 / (1.0 + jnp.exp(-x))


def _gelu_tanh(x):
    return 0.5 * x * (1.0 + jnp.tanh(math.sqrt(2.0 / math.pi) * (x + 0.044715 * (x * x * x))))


def _mod_kernel(cb_ref, w_ref, b_ref, o_ref, s_ref, *, d_model, tn):
    nb = tn // LANE

    @pl.when((pl.program_id(0) == 0) & (pl.program_id(1) == 0))
    def _():
        cv = cb_ref[...]
        s_ref[...] = cv * _sigmoid(cv)

    def body(t, accs):
        d0 = pl.multiple_of(t * SUBLANE, SUBLANE)
        s = [s_ref[r, pl.ds(d0, SUBLANE), :] for r in range(2)]
        new = list(accs)
        for j in range(nb):
            w = w_ref[0, pl.ds(d0, SUBLANE), j * LANE:(j + 1) * LANE]
            for r in range(2):
                new[r * nb + j] = accs[r * nb + j] + w * s[r]
        return tuple(new)

    init = tuple(jnp.zeros((SUBLANE, LANE), F32) for _ in range(2 * nb))
    accs = lax.fori_loop(0, d_model // SUBLANE, body, init, unroll=4)
    for r in range(2):
        for j in range(nb):
            o_ref[0, r:r + 1, j * LANE:(j + 1) * LANE] = (
                jnp.sum(accs[r * nb + j], axis=0, keepdims=True) + b_ref[0, :, j * LANE:(j + 1) * LANE])


def _modulation(c_ctx, c, w_ada, b_ada):
    depth, d_model, n_out = w_ada.shape
    tn = 9 * LANE
    assert n_out % tn == 0
    cb = jnp.broadcast_to(jnp.stack([c_ctx, c[0]])[:, :, None], (2, d_model, LANE))
    out = pl.pallas_call(
        functools.partial(_mod_kernel, d_model=d_model, tn=tn),
        out_shape=jax.ShapeDtypeStruct((depth, 2, n_out), F32),
        grid=(depth, n_out // tn),
        in_specs=[
            pl.BlockSpec((2, d_model, LANE), lambda l, j: (0, 0, 0)),
            pl.BlockSpec((1, d_model, tn), lambda l, j: (l, 0, j)),
            pl.BlockSpec((1, 1, tn), lambda l, j: (l, 0, j)),
        ],
        out_specs=pl.BlockSpec((1, 2, tn), lambda l, j: (l, 0, j)),
        scratch_shapes=[pltpu.VMEM((2, d_model, LANE), F32)],
        compiler_params=_params(("arbitrary", "arbitrary")),
        name="adaln_mod",
    )(cb, w_ada, b_ada.reshape(depth, 1, n_out))
    return out.reshape(depth, 2, N_MOD, d_model)


def _ffn_body(x, who, mod_ref, g_ref, wgu_ref, wd_ref, k, gi):
    f = wd_ref.shape[0]
    h = _pre(x, g_ref[gi:gi + 1, :], mod_ref, who, k).astype(BF16)
    gate = _dot(h, wgu_ref[:, :f])
    up = _dot(h, wgu_ref[:, f:])
    act = (gate * _sigmoid(gate) * up).astype(BF16)
    y = _dot(act, wd_ref[...])
    return _post(x, y, g_ref[gi + 1:gi + 2, :], mod_ref, who, k, 0.5)


def _inproj_body(x, who, mod_ref, g_ref, win_ref, cs_ref, u_ref, p_ref, q_ref):
    d_ssm = u_ref.shape[1]
    head = cs_ref.shape[0]
    h = _pre(x, g_ref[2:3, :], mod_ref, who, 1).astype(BF16)
    hh = _dot(h, win_ref[...])
    u_ref[...] = hh[:, :d_ssm]
    for n in range(N_FNET_HEADS):
        lo = d_ssm + n * head
        pq = _dot(hh[:, lo:lo + head].astype(BF16), cs_ref[...])
        p_ref[:, n * head:(n + 1) * head] = pq[:, :head]
        q_ref[:, n * head:(n + 1) * head] = pq[:, head:]


def _outproj_body(x, ys, yf, who, mod_ref, g_ref, wglu_ref, wfm_ref, wout_ref):
    d_ssm = wglu_ref.shape[0]
    head = wfm_ref.shape[1]
    yf = yf.astype(BF16)
    h = _gelu_tanh(ys)
    s = h * _sigmoid(_dot(h.astype(BF16), wglu_ref[...]))
    y = _dot(s.astype(BF16), wout_ref[:d_ssm, :])
    for n in range(N_FNET_HEADS):
        fm = _dot(yf[:, n * head:(n + 1) * head], wfm_ref[n])
        y = y + _dot(fm.astype(BF16), wout_ref[d_ssm + n * head:d_ssm + (n + 1) * head, :])
    return _post(x, y, g_ref[3:4, :], mod_ref, who, 1, 1.0)


def _layer_in_kernel(*refs, first):
    if first:
        (x_ref, xc_ref, renc_ref, cenc_ref, mod_ref, g_ref, wgu_ref, wd_ref, win_ref, cs_ref,
         o_ref, u_ref, p_ref, q_ref, oc_ref, uc_ref, pc_ref, qc_ref, xs_ref) = refs
        tm = x_ref.shape[0]
        half = renc_ref.shape[1]
        r0 = pl.program_id(0) * (tm // GRID_W)
        xs_ref[:, half:] = x_ref[:, half:] + cenc_ref[...]
        for q in range(tm // GRID_W):
            xs_ref[q * GRID_W:(q + 1) * GRID_W, :half] = (
                x_ref[q * GRID_W:(q + 1) * GRID_W, :half] + renc_ref[pl.ds(r0 + q, 1), :])
        x = xs_ref[...]
    else:
        (x_ref, xc_ref, mod_ref, g_ref, wgu_ref, wd_ref, win_ref, cs_ref,
         o_ref, u_ref, p_ref, q_ref, oc_ref, uc_ref, pc_ref, qc_ref) = refs
        x = x_ref[...]
    x1 = _ffn_body(x, LAT, mod_ref, g_ref, wgu_ref, wd_ref, 0, 0)
    o_ref[...] = x1
    _inproj_body(x1, LAT, mod_ref, g_ref, win_ref, cs_ref, u_ref, p_ref, q_ref)

    def _():
        xc1 = _ffn_body(xc_ref[...], CTX, mod_ref, g_ref, wgu_ref, wd_ref, 0, 0)
        oc_ref[...] = xc1
        _inproj_body(xc1, CTX, mod_ref, g_ref, win_ref, cs_ref, uc_ref, pc_ref, qc_ref)
    _on_last_step(_)


def _layer_out_kernel(*refs, with_ctx):
    if with_ctx:
        (x_ref, ys_ref, yf_ref, xc_ref, ysc_ref, yfc_ref, mod_ref, g_ref, wglu_ref, wfm_ref, wout_ref,
         wgu_ref, wd_ref, o_ref, oc_ref) = refs
    else:
        x_ref, ys_ref, yf_ref, mod_ref, g_ref, wglu_ref, wfm_ref, wout_ref, wgu_ref, wd_ref, o_ref = refs
    x2 = _outproj_body(x_ref[...], ys_ref[...], yf_ref[...], LAT, mod_ref, g_ref, wglu_ref, wfm_ref, wout_ref)
    o_ref[...] = _ffn_body(x2, LAT, mod_ref, g_ref, wgu_ref, wd_ref, 2, 4)
    if with_ctx:
        def _():
            xc2 = _outproj_body(xc_ref[...], ysc_ref[...], yfc_ref[...], CTX, mod_ref, g_ref, wglu_ref, wfm_ref,
                                wout_ref)
            oc_ref[...] = _ffn_body(xc2, CTX, mod_ref, g_ref, wgu_ref, wd_ref, 2, 4)
        _on_last_step(_)


def _tile_spec(cols):
    return pl.BlockSpec((TM, cols), lambda i: (i, 0))


def _whole_spec(shape):
    nd = len(shape)
    return pl.BlockSpec(shape, lambda *_: (0,) * nd)


def _layer_spec(arr, li):
    nd = arr.ndim - 1
    return pl.BlockSpec((None,) + arr.shape[1:], lambda *_: (li,) + (0,) * nd, pipeline_mode=pl.Buffered(1))


def _layer_in(xl, xc, pos, li, mods, gains, wgu, wd, win, cs, *, d_ssm):
    l, d = xl.shape
    lc = xc.shape[0]
    first = pos is not None
    d_f = win.shape[2] - d_ssm
    widths = (d, d_ssm, d_f, d_f)
    stacked = [mods, gains, wgu, wd, win]
    return pl.pallas_call(
        functools.partial(_layer_in_kernel, first=first),
        out_shape=[jax.ShapeDtypeStruct((l, wd_), F32) for wd_ in widths]
        + [jax.ShapeDtypeStruct((lc, wd_), F32) for wd_ in widths],
        grid=(l // TM,),
        in_specs=([_tile_spec(d), _whole_spec(xc.shape)] + ([_const_spec(a.shape) for a in pos] if first else [])
                  + [_layer_spec(a, li) for a in stacked] + [_const_spec(cs.shape)]),
        out_specs=[_tile_spec(wd_) for wd_ in widths] + [_whole_spec((lc, wd_)) for wd_ in widths],
        scratch_shapes=[pltpu.VMEM((TM, d), F32)] if first else [],
        compiler_params=_params(("arbitrary",)),
        name="layer_in",
    )(xl, xc, *(pos if first else ()), *stacked, cs)


def _layer_out(lat, ctx, li, mods, gains, wglu, wfm, wout, wgu, wd):
    xl, ys, yf = lat
    l, d = xl.shape
    with_ctx = ctx is not None
    ctx = list(ctx) if with_ctx else []
    stacked = [mods, gains, wglu, wfm, wout, wgu, wd]
    outs = pl.pallas_call(
        functools.partial(_layer_out_kernel, with_ctx=with_ctx),
        out_shape=[jax.ShapeDtypeStruct(xl.shape, F32)] + ([jax.ShapeDtypeStruct(ctx[0].shape, F32)] if with_ctx else []),
        grid=(l // TM,),
        in_specs=([_tile_spec(d), _tile_spec(ys.shape[1]), _tile_spec(yf.shape[1])]
                  + [_whole_spec(a.shape) for a in ctx] + [_layer_spec(a, li) for a in stacked]),
        out_specs=[_tile_spec(d)] + ([_whole_spec(ctx[0].shape)] if with_ctx else []),
        compiler_params=_params(("arbitrary",)),
        name="layer_out",
    )(xl, ys, yf, *ctx, *stacked)
    return (outs[0], outs[1]) if with_ctx else (outs[0], None)


NSEGS = 16
PAIRS = 2


def _cmul_add(ar, ai, xr, xi, zr, zi):
    return ar * xr - ai * xi + zr, ar * xi + ai * xr + zi


def _dot_nt(a, b):
    return lax.dot_general(a, b, (((1,), (1,)), ((), ())), precision=lax.Precision.HIGHEST,
                           preferred_element_type=F32)


def _s5_taps_kernel(e_re_ref, e_im_ref, ct_re_ref, ct_im_ref, kr_ref):
    t = CHUNK
    nh = ct_re_ref.shape[1]
    g8 = LANE // nh
    gw = e_re_ref.shape[3] // g8
    same = (lax.broadcasted_iota(jnp.int32, (LANE, g8 * gw), 0) // nh
            == lax.broadcasted_iota(jnp.int32, (LANE, g8 * gw), 1) // gw)
    taps = []
    for d in range(2):
        cre = jnp.where(same, jnp.concatenate([ct_re_ref[d]] * g8, axis=0), 0.0)
        cim = jnp.where(same, jnp.concatenate([ct_im_ref[d]] * g8, axis=0), 0.0)
        er = jnp.concatenate([e_re_ref[m, d] for m in range(t)], axis=0)
        ei = jnp.concatenate([e_im_ref[m, d] for m in range(t)], axis=0)
        taps.append(_dot_nt(er, cre) - _dot_nt(ei, cim))
    blk = (lax.broadcasted_iota(jnp.int32, (LANE, LANE), 0) // nh
           == lax.broadcasted_iota(jnp.int32, (LANE, LANE), 1) // nh)
    for lag in range(-(t - 1), t):
        if lag > 0:
            src = taps[0][lag * nh:(lag + 1) * nh]
        elif lag < 0:
            src = taps[1][-lag * nh:(1 - lag) * nh]
        else:
            src = taps[0][:nh] + taps[1][:nh]
        kr_ref[0, lag + t - 1] = jnp.where(blk, jnp.concatenate([src] * g8, axis=0), 0.0).astype(BF16)


def _s5_state_kernel(u_ref, uc_ref, e_re_ref, e_im_ref, mu_ref, d_ref, xp_ref, wz_ref, z_ref, xs_ref, *, seglen, ncc):
    w = LANE
    nc = d_ref.shape[1]
    nlat = NSEGS * seglen
    nv = NSEGS // SUBLANE
    nh = e_re_ref.shape[2]
    ns = w // 2

    @pl.when(pl.program_id(1) == 0)
    def _():
        def fold(i, carry):
            r0 = pl.multiple_of(i * NSEGS, NSEGS)
            tiles = [u_ref[pl.ds(pl.multiple_of((s * seglen + i) * CHUNK, CHUNK), CHUNK), :] for s in range(NSEGS)]
            d_ref[0, pl.ds(r0, NSEGS), :] = jnp.concatenate(tiles, axis=0).reshape(NSEGS, CHUNK * w).astype(BF16)
            return carry

        lax.fori_loop(0, seglen, fold, 0)
        d_ref[0, nlat:nlat + ncc, :] = uc_ref[...].reshape(ncc, CHUNK * w).astype(BF16)

    wz_ref[...] = jnp.zeros(wz_ref.shape, BF16)
    first = (lax.broadcasted_iota(jnp.int32, (nh, 4 * w), 1) % w) < ns
    for pb in range(PAIRS):
        pair = pl.program_id(1) * PAIRS + pb
        lanes = slice(pb * w, (pb + 1) * w)
        for i in range(CHUNK):
            ez = jnp.concatenate([e_re_ref[CHUNK - 1 - i, 0, :, lanes], e_im_ref[CHUNK - 1 - i, 0, :, lanes],
                                  e_re_ref[i, 1, :, lanes], e_im_ref[i, 1, :, lanes]], axis=1).astype(BF16)
            both = jnp.concatenate([jnp.where(first, ez, jnp.zeros_like(ez)),
                                    jnp.where(first, jnp.zeros_like(ez), ez)], axis=0)
            r0 = pl.multiple_of(i * w + pair * 2 * nh, 2 * nh)
            wz_ref[pl.ds(r0, 2 * nh), pb * 4 * w:(pb + 1) * 4 * w] = both

    for pb in range(PAIRS):
        zz = _dot(d_ref[0], wz_ref[:, pb * 4 * w:(pb + 1) * 4 * w])
        for c in range(4):
            z_ref[pb * 4 + c] = zz[:, c * w:(c + 1) * w]

    mu = [[mu_ref[0, pb, 0, c:c + 1, :] for c in range(4)] for pb in range(PAIRS)]
    mun = [[mu_ref[0, pb, 1, c:c + 1, :] for c in range(4)] for pb in range(PAIRS)]

    pre = []
    for pb in range(PAIRS):
        zc = [z_ref[pb * 4 + c, nlat:nlat + ncc, :] for c in range(4)]
        sf = (jnp.zeros((1, w), F32), jnp.zeros((1, w), F32))
        sb = (jnp.zeros((1, w), F32), jnp.zeros((1, w), F32))
        for j in range(ncc):
            jb = ncc - 1 - j
            xs_ref[pb * 4 + 0, nlat + j:nlat + j + 1, :] = sf[0]
            xs_ref[pb * 4 + 1, nlat + j:nlat + j + 1, :] = sf[1]
            xs_ref[pb * 4 + 2, nlat + jb:nlat + jb + 1, :] = sb[0]
            xs_ref[pb * 4 + 3, nlat + jb:nlat + jb + 1, :] = sb[1]
            sf = _cmul_add(mu[pb][0], mu[pb][1], sf[0], sf[1], zc[0][j:j + 1], zc[1][j:j + 1])
            sb = _cmul_add(mu[pb][2], mu[pb][3], sb[0], sb[1], zc[2][jb:jb + 1], zc[3][jb:jb + 1])
        pre.append((sf, sb))

    mub = [[jnp.broadcast_to(m, (SUBLANE, w)) for m in mu[pb]] for pb in range(PAIRS)]

    def rows(i, v):
        return pl.ds(pl.multiple_of(i * NSEGS + v * SUBLANE, SUBLANE), SUBLANE)

    def step(i, st):
        ib = seglen - 1 - i
        new = []
        for pb in range(PAIRS):
            for v in range(nv):
                k = (pb * nv + v) * 4
                m = mub[pb]
                fr, fi = _cmul_add(m[0], m[1], st[k], st[k + 1],
                                   z_ref[pb * 4 + 0, rows(i, v), :], z_ref[pb * 4 + 1, rows(i, v), :])
                br, bi = _cmul_add(m[2], m[3], st[k + 2], st[k + 3],
                                   z_ref[pb * 4 + 2, rows(ib, v), :], z_ref[pb * 4 + 3, rows(ib, v), :])
                new += [fr, fi, br, bi]
        return tuple(new)

    zero = jnp.zeros((SUBLANE, w), F32)
    fin = lax.fori_loop(0, seglen, step, (zero,) * (PAIRS * nv * 4))

    carry = []
    for pb in range(PAIRS):
        cf, cb = pre[pb]
        rows_f, rows_b = [None] * NSEGS, [None] * NSEGS
        for s in range(NSEGS):
            sr = NSEGS - 1 - s
            rows_f[s] = cf
            rows_b[sr] = cb
            kf = (pb * nv + s // SUBLANE) * 4
            kb = (pb * nv + sr // SUBLANE) * 4
            sl, srl = s % SUBLANE, sr % SUBLANE
            cf = _cmul_add(mun[pb][0], mun[pb][1], cf[0], cf[1], fin[kf][sl:sl + 1, :], fin[kf + 1][sl:sl + 1, :])
            cb = _cmul_add(mun[pb][2], mun[pb][3], cb[0], cb[1],
                           fin[kb + 2][srl:srl + 1, :], fin[kb + 3][srl:srl + 1, :])
        for v in range(nv):
            seg = slice(v * SUBLANE, (v + 1) * SUBLANE)
            carry += [jnp.concatenate([r[0] for r in rows_f[seg]], axis=0),
                      jnp.concatenate([r[1] for r in rows_f[seg]], axis=0),
                      jnp.concatenate([r[0] for r in rows_b[seg]], axis=0),
                      jnp.concatenate([r[1] for r in rows_b[seg]], axis=0)]

    def step2(i, st):
        ib = seglen - 1 - i
        for pb in range(PAIRS):
            for v in range(nv):
                k = (pb * nv + v) * 4
                xs_ref[pb * 4 + 0, rows(i, v), :] = st[k]
                xs_ref[pb * 4 + 1, rows(i, v), :] = st[k + 1]
                xs_ref[pb * 4 + 2, rows(ib, v), :] = st[k + 2]
                xs_ref[pb * 4 + 3, rows(ib, v), :] = st[k + 3]
        return step(i, st)

    lax.fori_loop(0, seglen, step2, tuple(carry))
    for k in range(PAIRS * 4):
        xp_ref[0, :, k * w:(k + 1) * w] = xs_ref[k].astype(BF16)


def _s5_readout_kernel(d_ref, xp_ref, kr_ref, sy_ref, rm_ref, dsk_ref, y_ref, yc_ref, bt_ref, wy_ref, yb_ref,
                       *, seglen):
    w = LANE
    cw = 4 * w
    fw = CHUNK * w
    ncg = fw // cw

    @pl.when(pl.program_id(1) == 0)
    def _():
        for i in range(CHUNK):
            for j in range(CHUNK):
                bt_ref[i * w:(i + 1) * w, j * w:(j + 1) * w] = kr_ref[0, j - i + CHUNK - 1]
        nrow = sy_ref.shape[1]
        nh = rm_ref.shape[0] // CHUNK
        row = lax.broadcasted_iota(jnp.int32, (nrow, cw), 0)
        row_g = 2 * (row // cw) + (row % w) // (w // 2)
        col_g = (lax.broadcasted_iota(jnp.int32, (nrow, cw), 1) % w) // nh
        same = row_g == col_g
        for cg in range(ncg):
            cols = slice(cg * cw, (cg + 1) * cw)
            wy_ref[:, cols] = jnp.where(same, _dot(sy_ref[0], rm_ref[:, cols]), 0.0).astype(BF16)

    nrows = d_ref.shape[1]
    for cg in range(ncg):
        cols = slice(cg * cw, (cg + 1) * cw)
        yb_ref[:, cols] = (_dot(d_ref[0], bt_ref[:, cols]) + _dot(xp_ref[0], wy_ref[:, cols])
                           + d_ref[0, :, cols].astype(F32) * dsk_ref[0, :, cols])

    ngrp = nrows // NSEGS

    def unfold(j, carry):
        gi = pl.program_id(1) * ngrp + j
        r0 = pl.multiple_of(j * NSEGS, NSEGS)
        tok = yb_ref[pl.ds(r0, NSEGS), :].reshape(NSEGS * CHUNK, w)

        @pl.when(gi < seglen)
        def _():
            for s in range(NSEGS):
                t0 = pl.multiple_of((s * seglen + gi) * CHUNK, CHUNK)
                y_ref[pl.ds(t0, CHUNK), :] = tok[s * CHUNK:(s + 1) * CHUNK]

        @pl.when(gi >= seglen)
        def _():
            t0 = pl.multiple_of((gi - seglen) * NSEGS * CHUNK, NSEGS * CHUNK)
            yc_ref[pl.ds(t0, NSEGS * CHUNK), :] = tok

        return carry

    lax.fori_loop(0, ngrp, unfold, 0)


def _s5_tables(lam_re, lam_im, log_dt, b_re, b_im, c_re, c_im, d_skip, *, seglen):
    hp = lax.Precision.HIGHEST
    t = CHUNK
    ng, ns = lam_re.shape[1], lam_re.shape[2]
    nh = b_re.shape[-1]
    gp = ng * ns
    g8 = LANE // nh
    na = ng // g8
    nb = g8 // 2
    assert 2 * ns == LANE and nb % PAIRS == 0
    dt = jnp.exp(log_dt)[..., None]
    a, b = (lam_re * dt).reshape(2, gp), (lam_im * dt).reshape(2, gp)

    def lam_pow(m):
        mm = jnp.asarray(m, F32).reshape(-1, 1, 1)
        mag = jnp.exp(a[None] * mm)
        return mag * jnp.cos(b[None] * mm), mag * jnp.sin(b[None] * mm)

    pr, pi = lam_pow(np.arange(t + 1))
    lr, li = lam_re.reshape(2, gp), lam_im.reshape(2, gp)
    nr, ni = pr[1] - 1.0, pi[1]
    den = lr * lr + li * li
    qr, qi = (nr * lr + ni * li) / den, (ni * lr - nr * li) / den
    bt_re = b_re.transpose(0, 3, 1, 2).reshape(2, nh, gp)
    bt_im = b_im.transpose(0, 3, 1, 2).reshape(2, nh, gp)
    bb_re = qr[:, None] * bt_re - qi[:, None] * bt_im
    bb_im = qr[:, None] * bt_im + qi[:, None] * bt_re
    e_re = pr[:t, :, None] * bb_re[None] - pi[:t, :, None] * bb_im[None]
    e_im = pr[:t, :, None] * bb_im[None] + pi[:t, :, None] * bb_re[None]
    ct_re = c_re.transpose(0, 2, 1, 3).reshape(2, nh, gp)
    ct_im = c_im.transpose(0, 2, 1, 3).reshape(2, nh, gp)

    cgp_re = c_re.transpose(0, 1, 3, 2).reshape(2, gp, nh)
    cgp_im = c_im.transpose(0, 1, 3, 2).reshape(2, gp, nh)
    tile_j = np.tile(np.eye(nh, dtype=np.float32), (1, t))
    rep_j = np.kron(np.eye(t, dtype=np.float32), np.ones((1, nh), np.float32))
    ctl_re = jnp.dot(cgp_re, tile_j, precision=hp)
    ctl_im = jnp.dot(cgp_im, tile_j, precision=hp)
    sl = slice(1, t + 1)
    pw_re = jnp.stack([pr[sl, 0], pr[sl, 1][::-1]]).transpose(0, 2, 1)
    pw_im = jnp.stack([pi[sl, 0], pi[sl, 1][::-1]]).transpose(0, 2, 1)
    pw_re = jnp.dot(pw_re, rep_j, precision=hp)
    pw_im = jnp.dot(pw_im, rep_j, precision=hp)
    wy_re = ctl_re * pw_re - ctl_im * pw_im
    wy_im = ctl_re * pw_im + ctl_im * pw_re
    sy = jnp.stack([wy_re, -wy_im], axis=1).reshape(2, 2, na, nb, LANE, t * nh)
    sy = sy.transpose(2, 3, 0, 1, 4, 5).reshape(na, nb * 4 * LANE, t * nh).astype(BF16)
    mr, mi = lam_pow([t, t * seglen])
    mu = jnp.stack([mr, mi], axis=2).reshape(2, 4, na, nb, LANE).transpose(2, 3, 0, 1, 4)
    dsk = jnp.tile(d_skip.reshape(na, 1, LANE), (1, 1, t))
    return e_re, e_im, ct_re, ct_im, sy, mu, dsk


def _s5_operators(params, seglen):
    e_re, e_im, ct_re, ct_im, sy, mu, dsk = jax.vmap(functools.partial(_s5_tables, seglen=seglen))(*params)
    depth, t, _, nh, gp = e_re.shape
    na = sy.shape[1]
    tw = gp // na
    nlag = 2 * CHUNK - 1
    kr = pl.pallas_call(
        _s5_taps_kernel,
        out_shape=jax.ShapeDtypeStruct((depth, na, nlag, LANE, LANE), BF16),
        grid=(depth, na),
        in_specs=[pl.BlockSpec((None, t, 2, nh, tw), lambda l, i: (l, 0, 0, 0, i)),
                  pl.BlockSpec((None, t, 2, nh, tw), lambda l, i: (l, 0, 0, 0, i)),
                  pl.BlockSpec((None, 2, nh, tw), lambda l, i: (l, 0, 0, i)),
                  pl.BlockSpec((None, 2, nh, tw), lambda l, i: (l, 0, 0, i))],
        out_specs=pl.BlockSpec((None, 1, nlag, LANE, LANE), lambda l, i: (l, i, 0, 0, 0)),
        compiler_params=_params(("arbitrary", "arbitrary")),
        name="s5_taps",
    )(e_re, e_im, ct_re, ct_im)
    r_mat = np.kron(np.eye(CHUNK, dtype=np.float32), np.tile(np.eye(nh, dtype=np.float32), (1, LANE // nh)))
    return kr, e_re, e_im, sy, jnp.asarray(r_mat, F32).astype(BF16), mu, dsk


def _s5(u, uc, li, ops):
    kr, e_re, e_im, sy, r_mat, mu, dsk = ops
    l, d_ssm = u.shape
    lc = uc.shape[0]
    na, nb = mu.shape[1], mu.shape[2]
    fw = CHUNK * LANE
    nlat, ncc = l // CHUNK, lc // CHUNK
    nc = nlat + ncc
    seglen = nlat // NSEGS
    sw = PAIRS * 4 * LANE
    assert seglen * NSEGS == nlat and ncc * CHUNK == lc
    d, xp = pl.pallas_call(
        functools.partial(_s5_state_kernel, seglen=seglen, ncc=ncc),
        out_shape=(jax.ShapeDtypeStruct((na, nc, fw), BF16), jax.ShapeDtypeStruct((na, nc, nb * 4 * LANE), BF16)),
        grid=(na, nb // PAIRS),
        in_specs=[
            pl.BlockSpec((l, LANE), lambda a, b: (0, a)),
            pl.BlockSpec((lc, LANE), lambda a, b: (0, a)),
            pl.BlockSpec((None,) + e_re.shape[1:4] + (PAIRS * LANE,), lambda a, b: (li, 0, 0, 0, a * (nb // PAIRS) + b)),
            pl.BlockSpec((None,) + e_im.shape[1:4] + (PAIRS * LANE,), lambda a, b: (li, 0, 0, 0, a * (nb // PAIRS) + b)),
            pl.BlockSpec((None, 1, PAIRS, 2, 4, LANE), lambda a, b: (li, a, b, 0, 0, 0)),
        ],
        out_specs=(pl.BlockSpec((1, nc, fw), lambda a, b: (a, 0, 0)),
                   pl.BlockSpec((1, nc, sw), lambda a, b: (a, 0, b))),
        scratch_shapes=[pltpu.VMEM((fw, sw), BF16),
                        pltpu.VMEM((PAIRS * 4, nc, LANE), F32), pltpu.VMEM((PAIRS * 4, nc, LANE), F32)],
        compiler_params=_params(("arbitrary", "arbitrary")),
        name="s5_state",
    )(u, uc, e_re, e_im, mu)
    rbs = max(r for r in range(NSEGS, min(nc, 512) + 1, NSEGS) if nc % r == 0)
    nrb = nc // rbs
    assert NSEGS % 16 == 0 and ncc % NSEGS == 0
    return pl.pallas_call(
        functools.partial(_s5_readout_kernel, seglen=seglen),
        out_shape=(jax.ShapeDtypeStruct((l, d_ssm), F32), jax.ShapeDtypeStruct((lc, d_ssm), F32)),
        grid=(na, nrb),
        in_specs=[
            pl.BlockSpec((1, rbs, fw), lambda a, r: (a, r, 0)),
            pl.BlockSpec((1, rbs, nb * 4 * LANE), lambda a, r: (a, r, 0)),
            pl.BlockSpec((None, 1) + kr.shape[2:], lambda a, r: (li, a, 0, 0, 0)),
            pl.BlockSpec((None, 1) + sy.shape[2:], lambda a, r: (li, a, 0, 0)),
            _const_spec(r_mat.shape),
            pl.BlockSpec((None, 1, 1, fw), lambda a, r: (li, a, 0, 0)),
        ],
        out_specs=(pl.BlockSpec((l, LANE), lambda a, r: (0, a)), pl.BlockSpec((lc, LANE), lambda a, r: (0, a))),
        scratch_shapes=[pltpu.VMEM((fw, fw), BF16), pltpu.VMEM((nb * 4 * LANE, fw), BF16),
                        pltpu.VMEM((rbs, fw), F32)],
        compiler_params=_params(("arbitrary", "arbitrary")),
        name="s5_readout",
    )(d, xp, kr, sy, r_mat, dsk)


def _fft1_kernel(p_ref, q_ref, m_ref, o_ref, s_ref, *, nb):
    for n in range(nb):
        rhs = jnp.concatenate([p_ref[:, n, :], q_ref[:, n, :]], axis=0).astype(BF16)
        s_ref[:, n, :] = _dot(m_ref[...], rhs)
    o_ref[...] = s_ref[...].astype(BF16)


def _fft2_kernel(yr_ref, yi_ref, ca_ref, sa_ref, cb_ref, sb_ref, o_ref, s_ref, *, kb):
    cb, sb = cb_ref[...], sb_ref[...]
    for k in range(kb):
        ca, sa = ca_ref[k:k + 1, :], sa_ref[k:k + 1, :]
        g = jnp.concatenate([ca * cb - sa * sb, sa * cb + ca * sb], axis=1).astype(BF16)
        rhs = jnp.concatenate([yr_ref[k], yi_ref[k]], axis=0)
        s_ref[:, k, :] = _dot(g, rhs)
    o_ref[...] = s_ref[...].astype(BF16)


def _fft_ctx_kernel(p_ref, q_ref, m_ref, o_ref):
    rhs = jnp.concatenate([p_ref[...], q_ref[...]], axis=0).astype(BF16)
    o_ref[...] = _dot(m_ref[...], rhs).astype(BF16)


def _fnet_tables(l, lc):
    r = math.isqrt(l)
    assert r * r == l
    k = np.arange(r)
    th = 2.0 * np.pi * (np.outer(k, k) % r) / r
    c, s = np.cos(th), np.sin(th)
    m1 = np.block([[c, -s], [-s, -c]])
    ta = 2.0 * np.pi * np.outer(k, k) / l
    scale = 1.0 / math.sqrt(l)
    tw2 = tuple(jnp.asarray(v, F32) for v in (np.cos(ta), np.sin(ta), c * scale, s * scale))
    kc = np.arange(lc)
    thc = 2.0 * np.pi * (np.outer(kc, kc) % lc) / lc
    mc = np.concatenate([np.cos(thc), -np.sin(thc)], axis=1) / math.sqrt(lc)
    return (jnp.asarray(m1, F32).astype(BF16), tw2, jnp.asarray(mc, F32).astype(BF16))


def _fnet_lat(p, q, m1, tw2):
    l, c = p.shape
    r = m1.shape[0] // 2
    assert r * r == l
    p3 = p.reshape(r, r, c)
    q3 = q.reshape(r, r, c)
    nb = 16
    y1 = pl.pallas_call(
        functools.partial(_fft1_kernel, nb=nb),
        out_shape=jax.ShapeDtypeStruct((2 * r, r, c), BF16),
        grid=(r // nb,),
        in_specs=[pl.BlockSpec((r, nb, c), lambda j: (0, j, 0)),
                  pl.BlockSpec((r, nb, c), lambda j: (0, j, 0)),
                  _const_spec(m1.shape)],
        out_specs=pl.BlockSpec((2 * r, nb, c), lambda j: (0, j, 0)),
        scratch_shapes=[pltpu.VMEM((2 * r, nb, c), F32)],
        compiler_params=_params(("arbitrary",)),
        name="fft_stage1",
    )(p3, q3, m1)
    kb = 16
    ca, sa, cb, sb = tw2
    out = pl.pallas_call(
        functools.partial(_fft2_kernel, kb=kb),
        out_shape=jax.ShapeDtypeStruct((r, r, c), BF16),
        grid=(r // kb,),
        in_specs=[pl.BlockSpec((kb, r, c), lambda j: (j, 0, 0)),
                  pl.BlockSpec((kb, r, c), lambda j: (r // kb + j, 0, 0)),
                  pl.BlockSpec((kb, r), lambda j: (j, 0)),
                  pl.BlockSpec((kb, r), lambda j: (j, 0)),
                  _const_spec(cb.shape), _const_spec(sb.shape)],
        out_specs=pl.BlockSpec((r, kb, c), lambda j: (0, j, 0)),
        scratch_shapes=[pltpu.VMEM((r, kb, c), F32)],
        compiler_params=_params(("arbitrary",)),
        name="fft_stage2",
    )(y1, y1, ca, sa, cb, sb)
    return out.reshape(l, c)


def _fnet_ctx(pc, qc, mc):
    return pl.pallas_call(
        _fft_ctx_kernel,
        out_shape=jax.ShapeDtypeStruct(pc.shape, BF16),
        grid=(1,),
        in_specs=[_whole_spec(pc.shape), _whole_spec(qc.shape), _const_spec(mc.shape)],
        out_specs=_whole_spec(pc.shape),
        compiler_params=_params(("arbitrary",)),
        name="fft_ctx",
    )(pc, qc, mc)


def _pos_tables(l, d):
    quarter = d // 4
    omega = 1.0 / (POS_BASE ** (jnp.arange(quarter, dtype=F32) / quarter))

    def enc(pv):
        ang = pv[:, None] * omega[None, :]
        return jnp.concatenate([jnp.sin(ang), jnp.cos(ang)], axis=-1)

    renc = enc(jnp.arange(l // GRID_W, dtype=F32))
    cenc = jnp.tile(enc(jnp.arange(GRID_W, dtype=F32)), (TM // GRID_W, 1))
    return renc, cenc


def kernel(x, c, ctx, c_ctx, w_ada, b_ada, norm_g, ffn1_gu, ffn1_down, ffn2_gu, ffn2_down, w_in, w_out,
           ssm_lam_re, ssm_lam_im, ssm_log_dt, ssm_b_re, ssm_b_im, ssm_c_re, ssm_c_im, ssm_d, w_glu, w_fmix):
    bsz, l, d = x.shape
    lc = ctx.shape[1]
    depth = w_ada.shape[0]
    d_ssm = w_glu.shape[1]
    head = w_fmix.shape[2]
    assert bsz == 1 and l % TM == 0 and TM % GRID_W == 0
    seglen = (l // CHUNK) // NSEGS

    mods = _modulation(c_ctx, c, w_ada, b_ada)
    pos = _pos_tables(l, d)
    m1, tw2, mc = _fnet_tables(l, lc)
    kc = np.arange(head)
    thc = 2.0 * np.pi * (np.outer(kc, kc) % head) / head
    cs = jnp.asarray(np.concatenate([np.cos(thc), np.sin(thc)], axis=1) / math.sqrt(head), F32).astype(BF16)
    wgu1, wd1, wgu2, wd2 = (a.astype(BF16) for a in (ffn1_gu, ffn1_down, ffn2_gu, ffn2_down))
    win, wout, wglu, wfm = (a.astype(BF16) for a in (w_in, w_out, w_glu, w_fmix))

    ops = _s5_operators((ssm_lam_re, ssm_lam_im, ssm_log_dt, ssm_b_re, ssm_b_im, ssm_c_re, ssm_c_im, ssm_d), seglen)

    xl, xc = x[0], ctx[0]
    for li in range(depth):
        last = li == depth - 1
        xl, u, p, q, xc, uc, pc, qc = _layer_in(xl, xc, pos if li == 0 else None, li, mods, norm_g, wgu1, wd1,
                                                win, cs, d_ssm=d_ssm)
        ys, ysc = _s5(u, uc, li, ops)
        yf = _fnet_lat(p, q, m1, tw2)
        ctx_in = None if last else (xc, ysc, _fnet_ctx(pc, qc, mc))
        xl, xc = _layer_out((xl, ys, yf), ctx_in, li, mods, norm_g, wglu, wfm, wout, wgu2, wd2)
    return xl[None]
```

```python
import functools
import math

import numpy as np
import jax
import jax.numpy as jnp
from jax import lax
from jax.experimental import pallas as pl
from jax.experimental.pallas import tpu as pltpu

F32 = jnp.float32
BF16 = jnp.bfloat16

LANE = 128
SUBLANE = 8
VMEM_LIMIT = 56 * 1024 * 1024

EPS = 1e-6
GRID_W = 64
POS_BASE = 10000.0
N_MOD = 9
SSM_GROUP = 16
STATE = 64
N_FNET_HEADS = 4
CHUNK = 16
TM = 512
CTX, LAT = 0, 1


def _dot(a, b):
    return jnp.dot(a, b, preferred_element_type=F32)


def _params(sem=None):
    return pltpu.CompilerParams(dimension_semantics=sem, vmem_limit_bytes=VMEM_LIMIT)


def _const_spec(shape):
    nd = len(shape)
    return pl.BlockSpec(shape, lambda *_: (0,) * nd, pipeline_mode=pl.Buffered(1))


def _rms(x, g):
    ms = jnp.mean(x * x, axis=-1, keepdims=True)
    return x * lax.rsqrt(ms + EPS) * g


def _pre(x, g, mod_ref, who, k):
    return _rms(x, g) * (1.0 + mod_ref[who, 3 * k + 1:3 * k + 2, :]) + mod_ref[who, 3 * k:3 * k + 1, :]


def _post(x, y, g, mod_ref, who, k, weight):
    return x + (weight * mod_ref[who, 3 * k + 2:3 * k + 3, :]) * _rms(y, g)


def _on_last_step(fn):
    pl.when(pl.program_id(0) == pl.num_programs(0) - 1)(fn)


def _sigmoid(x):
    return 1.0 / (1.0 + jnp.exp(-x))


def _gelu_tanh(x):
    return 0.5 * x * (1.0 + jnp.tanh(math.sqrt(2.0 / math.pi) * (x + 0.044715 * (x * x * x))))


def _mod_kernel(cb_ref, w_ref, b_ref, o_ref, s_ref, *, d_model, tn):
    nb = tn // LANE

    @pl.when((pl.program_id(0) == 0) & (pl.program_id(1) == 0))
    def _():
        cv = cb_ref[...]
        s_ref[...] = cv * _sigmoid(cv)

    def body(t, accs):
        d0 = pl.multiple_of(t * SUBLANE, SUBLANE)
        s = [s_ref[r, pl.ds(d0, SUBLANE), :] for r in range(2)]
        new = list(accs)
        for j in range(nb):
            w = w_ref[0, pl.ds(d0, SUBLANE), j * LANE:(j + 1) * LANE]
            for r in range(2):
                new[r * nb + j] = accs[r * nb + j] + w * s[r]
        return tuple(new)

    init = tuple(jnp.zeros((SUBLANE, LANE), F32) for _ in range(2 * nb))
    accs = lax.fori_loop(0, d_model // SUBLANE, body, init, unroll=4)
    for r in range(2):
        for j in range(nb):
            o_ref[0, r:r + 1, j * LANE:(j + 1) * LANE] = (
                jnp.sum(accs[r * nb + j], axis=0, keepdims=True) + b_ref[0, :, j * LANE:(j + 1) * LANE])


def _modulation(c_ctx, c, w_ada, b_ada):
    depth, d_model, n_out = w_ada.shape
    tn = 9 * LANE
    assert n_out % tn == 0
    cb = jnp.broadcast_to(jnp.stack([c_ctx, c[0]])[:, :, None], (2, d_model, LANE))
    out = pl.pallas_call(
        functools.partial(_mod_kernel, d_model=d_model, tn=tn),
        out_shape=jax.ShapeDtypeStruct((depth, 2, n_out), F32),
        grid=(depth, n_out // tn),
        in_specs=[
            pl.BlockSpec((2, d_model, LANE), lambda l, j: (0, 0, 0)),
            pl.BlockSpec((1, d_model, tn), lambda l, j: (l, 0, j)),
            pl.BlockSpec((1, 1, tn), lambda l, j: (l, 0, j)),
        ],
        out_specs=pl.BlockSpec((1, 2, tn), lambda l, j: (l, 0, j)),
        scratch_shapes=[pltpu.VMEM((2, d_model, LANE), F32)],
        compiler_params=_params(("arbitrary", "arbitrary")),
        name="adaln_mod",
    )(cb, w_ada, b_ada.reshape(depth, 1, n_out))
    return out.reshape(depth, 2, N_MOD, d_model)


def _ffn_body(x, who, mod_ref, g_ref, wgu_ref, wd_ref, k, gi):
    f = wd_ref.shape[0]
    h = _pre(x, g_ref[gi:gi + 1, :], mod_ref, who, k).astype(BF16)
    gate = _dot(h, wgu_ref[:, :f])
    up = _dot(h, wgu_ref[:, f:])
    act = (gate * _sigmoid(gate) * up).astype(BF16)
    y = _dot(act, wd_ref[...])
    return _post(x, y, g_ref[gi + 1:gi + 2, :], mod_ref, who, k, 0.5)


def _inproj_body(x, who, mod_ref, g_ref, win_ref, cs_ref, u_ref, p_ref, q_ref):
    d_ssm = u_ref.shape[1]
    head = cs_ref.shape[0]
    h = _pre(x, g_ref[2:3, :], mod_ref, who, 1).astype(BF16)
    hh = _dot(h, win_ref[...])
    u_ref[...] = hh[:, :d_ssm]
    for n in range(N_FNET_HEADS):
        lo = d_ssm + n * head
        pq = _dot(hh[:, lo:lo + head].astype(BF16), cs_ref[...])
        p_ref[:, n * head:(n + 1) * head] = pq[:, :head]
        q_ref[:, n * head:(n + 1) * head] = pq[:, head:]


def _outproj_body(x, ys, yf, who, mod_ref, g_ref, wglu_ref, wfm_ref, wout_ref):
    d_ssm = wglu_ref.shape[0]
    head = wfm_ref.shape[1]
    yf = yf.astype(BF16)
    h = _gelu_tanh(ys)
    s = h * _sigmoid(_dot(h.astype(BF16), wglu_ref[...]))
    y = _dot(s.astype(BF16), wout_ref[:d_ssm, :])
    for n in range(N_FNET_HEADS):
        fm = _dot(yf[:, n * head:(n + 1) * head], wfm_ref[n])
        y = y + _dot(fm.astype(BF16), wout_ref[d_ssm + n * head:d_ssm + (n + 1) * head, :])
    return _post(x, y, g_ref[3:4, :], mod_ref, who, 1, 1.0)


def _layer_in_kernel(*refs, first):
    if first:
        (x_ref, xc_ref, renc_ref, cenc_ref, mod_ref, g_ref, wgu_ref, wd_ref, win_ref, cs_ref,
         o_ref, u_ref, p_ref, q_ref, oc_ref, uc_ref, pc_ref, qc_ref, xs_ref) = refs
        tm = x_ref.shape[0]
        half = renc_ref.shape[1]
        r0 = pl.program_id(0) * (tm // GRID_W)
        xs_ref[:, half:] = x_ref[:, half:] + cenc_ref[...]
        for q in range(tm // GRID_W):
            xs_ref[q * GRID_W:(q + 1) * GRID_W, :half] = (
                x_ref[q * GRID_W:(q + 1) * GRID_W, :half] + renc_ref[pl.ds(r0 + q, 1), :])
        x = xs_ref[...]
    else:
        (x_ref, xc_ref, mod_ref, g_ref, wgu_ref, wd_ref, win_ref, cs_ref,
         o_ref, u_ref, p_ref, q_ref, oc_ref, uc_ref, pc_ref, qc_ref) = refs
        x = x_ref[...]
    x1 = _ffn_body(x, LAT, mod_ref, g_ref, wgu_ref, wd_ref, 0, 0)
    o_ref[...] = x1
    _inproj_body(x1, LAT, mod_ref, g_ref, win_ref, cs_ref, u_ref, p_ref, q_ref)

    def _():
        xc1 = _ffn_body(xc_ref[...], CTX, mod_ref, g_ref, wgu_ref, wd_ref, 0, 0)
        oc_ref[...] = xc1
        _inproj_body(xc1, CTX, mod_ref, g_ref, win_ref, cs_ref, uc_ref, pc_ref, qc_ref)
    _on_last_step(_)


def _layer_out_kernel(*refs, with_ctx):
    if with_ctx:
        (x_ref, ys_ref, yf_ref, xc_ref, ysc_ref, yfc_ref, mod_ref, g_ref, wglu_ref, wfm_ref, wout_ref,
         wgu_ref, wd_ref, o_ref, oc_ref) = refs
    else:
        x_ref, ys_ref, yf_ref, mod_ref, g_ref, wglu_ref, wfm_ref, wout_ref, wgu_ref, wd_ref, o_ref = refs
    x2 = _outproj_body(x_ref[...], ys_ref[...], yf_ref[...], LAT, mod_ref, g_ref, wglu_ref, wfm_ref, wout_ref)
    o_ref[...] = _ffn_body(x2, LAT, mod_ref, g_ref, wgu_ref, wd_ref, 2, 4)
    if with_ctx:
        def _():
            xc2 = _outproj_body(xc_ref[...], ysc_ref[...], yfc_ref[...], CTX, mod_ref, g_ref, wglu_ref, wfm_ref,
                                wout_ref)
            oc_ref[...] = _ffn_body(xc2, CTX, mod_ref, g_ref, wgu_ref, wd_ref, 2, 4)
        _on_last_step(_)


def _tile_spec(cols):
    return pl.BlockSpec((TM, cols), lambda i: (i, 0))


def _whole_spec(shape):
    nd = len(shape)
    return pl.BlockSpec(shape, lambda *_: (0,) * nd)


def _layer_spec(arr, li):
    nd = arr.ndim - 1
    return pl.BlockSpec((None,) + arr.shape[1:], lambda *_: (li,) + (0,) * nd, pipeline_mode=pl.Buffered(1))


def _layer_in(xl, xc, pos, li, mods, gains, wgu, wd, win, cs, *, d_ssm):
    l, d = xl.shape
    lc = xc.shape[0]
    first = pos is not None
    d_f = win.shape[2] - d_ssm
    widths = (d, d_ssm, d_f, d_f)
    stacked = [mods, gains, wgu, wd, win]
    return pl.pallas_call(
        functools.partial(_layer_in_kernel, first=first),
        out_shape=[jax.ShapeDtypeStruct((l, wd_), F32) for wd_ in widths]
        + [jax.ShapeDtypeStruct((lc, wd_), F32) for wd_ in widths],
        grid=(l // TM,),
        in_specs=([_tile_spec(d), _whole_spec(xc.shape)] + ([_const_spec(a.shape) for a in pos] if first else [])
                  + [_layer_spec(a, li) for a in stacked] + [_const_spec(cs.shape)]),
        out_specs=[_tile_spec(wd_) for wd_ in widths] + [_whole_spec((lc, wd_)) for wd_ in widths],
        scratch_shapes=[pltpu.VMEM((TM, d), F32)] if first else [],
        compiler_params=_params(("arbitrary",)),
        name="layer_in",
    )(xl, xc, *(pos if first else ()), *stacked, cs)


def _layer_out(lat, ctx, li, mods, gains, wglu, wfm, wout, wgu, wd):
    xl, ys, yf = lat
    l, d = xl.shape
    with_ctx = ctx is not None
    ctx = list(ctx) if with_ctx else []
    stacked = [mods, gains, wglu, wfm, wout, wgu, wd]
    outs = pl.pallas_call(
        functools.partial(_layer_out_kernel, with_ctx=with_ctx),
        out_shape=[jax.ShapeDtypeStruct(xl.shape, F32)] + ([jax.ShapeDtypeStruct(ctx[0].shape, F32)] if with_ctx else []),
        grid=(l // TM,),
        in_specs=([_tile_spec(d), _tile_spec(ys.shape[1]), _tile_spec(yf.shape[1])]
                  + [_whole_spec(a.shape) for a in ctx] + [_layer_spec(a, li) for a in stacked]),
        out_specs=[_tile_spec(d)] + ([_whole_spec(ctx[0].shape)] if with_ctx else []),
        compiler_params=_params(("arbitrary",)),
        name="layer_out",
    )(xl, ys, yf, *ctx, *stacked)
    return (outs[0], outs[1]) if with_ctx else (outs[0], None)


NSEGS = 16
PAIRS = 2


def _cmul_add(ar, ai, xr, xi, zr, zi):
    return ar * xr - ai * xi + zr, ar * xi + ai * xr + zi


def _dot_nt(a, b):
    return lax.dot_general(a, b, (((1,), (1,)), ((), ())), precision=lax.Precision.HIGHEST,
                           preferred_element_type=F32)


def _s5_taps_kernel(e_re_ref, e_im_ref, ct_re_ref, ct_im_ref, kr_ref):
    t = CHUNK
    nh = ct_re_ref.shape[1]
    g8 = LANE // nh
    gw = e_re_ref.shape[3] // g8
    same = (lax.broadcasted_iota(jnp.int32, (LANE, g8 * gw), 0) // nh
            == lax.broadcasted_iota(jnp.int32, (LANE, g8 * gw), 1) // gw)
    taps = []
    for d in range(2):
        cre = jnp.where(same, jnp.concatenate([ct_re_ref[d]] * g8, axis=0), 0.0)
        cim = jnp.where(same, jnp.concatenate([ct_im_ref[d]] * g8, axis=0), 0.0)
        er = jnp.concatenate([e_re_ref[m, d] for m in range(t)], axis=0)
        ei = jnp.concatenate([e_im_ref[m, d] for m in range(t)], axis=0)
        taps.append(_dot_nt(er, cre) - _dot_nt(ei, cim))
    blk = (lax.broadcasted_iota(jnp.int32, (LANE, LANE), 0) // nh
           == lax.broadcasted_iota(jnp.int32, (LANE, LANE), 1) // nh)
    for lag in range(-(t - 1), t):
        if lag > 0:
            src = taps[0][lag * nh:(lag + 1) * nh]
        elif lag < 0:
            src = taps[1][-lag * nh:(1 - lag) * nh]
        else:
            src = taps[0][:nh] + taps[1][:nh]
        kr_ref[0, lag + t - 1] = jnp.where(blk, jnp.concatenate([src] * g8, axis=0), 0.0).astype(BF16)


def _s5_state_kernel(u_ref, uc_ref, e_re_ref, e_im_ref, mu_ref, d_ref, xp_ref, wz_ref, z_ref, xs_ref, *, seglen, ncc):
    w = LANE
    nc = d_ref.shape[1]
    nlat = NSEGS * seglen
    nv = NSEGS // SUBLANE
    nh = e_re_ref.shape[2]
    ns = w // 2

    @pl.when(pl.program_id(1) == 0)
    def _():
        def fold(i, carry):
            r0 = pl.multiple_of(i * NSEGS, NSEGS)
            tiles = [u_ref[pl.ds(pl.multiple_of((s * seglen + i) * CHUNK, CHUNK), CHUNK), :] for s in range(NSEGS)]
            d_ref[0, pl.ds(r0, NSEGS), :] = jnp.concatenate(tiles, axis=0).reshape(NSEGS, CHUNK * w).astype(BF16)
            return carry

        lax.fori_loop(0, seglen, fold, 0)
        d_ref[0, nlat:nlat + ncc, :] = uc_ref[...].reshape(ncc, CHUNK * w).astype(BF16)

    wz_ref[...] = jnp.zeros(wz_ref.shape, BF16)
    first = (lax.broadcasted_iota(jnp.int32, (nh, 4 * w), 1) % w) < ns
    for pb in range(PAIRS):
        pair = pl.program_id(1) * PAIRS + pb
        lanes = slice(pb * w, (pb + 1) * w)
        for i in range(CHUNK):
            ez = jnp.concatenate([e_re_ref[CHUNK - 1 - i, 0, :, lanes], e_im_ref[CHUNK - 1 - i, 0, :, lanes],
                                  e_re_ref[i, 1, :, lanes], e_im_ref[i, 1, :, lanes]], axis=1).astype(BF16)
            both = jnp.concatenate([jnp.where(first, ez, jnp.zeros_like(ez)),
                                    jnp.where(first, jnp.zeros_like(ez), ez)], axis=0)
            r0 = pl.multiple_of(i * w + pair * 2 * nh, 2 * nh)
            wz_ref[pl.ds(r0, 2 * nh), pb * 4 * w:(pb + 1) * 4 * w] = both

    for pb in range(PAIRS):
        zz = _dot(d_ref[0], wz_ref[:, pb * 4 * w:(pb + 1) * 4 * w])
        for c in range(4):
            z_ref[pb * 4 + c] = zz[:, c * w:(c + 1) * w]

    mu = [[mu_ref[0, pb, 0, c:c + 1, :] for c in range(4)] for pb in range(PAIRS)]
    mun = [[mu_ref[0, pb, 1, c:c + 1, :] for c in range(4)] for pb in range(PAIRS)]

    pre = []
    for pb in range(PAIRS):
        zc = [z_ref[pb * 4 + c, nlat:nlat + ncc, :] for c in range(4)]
        sf = (jnp.zeros((1, w), F32), jnp.zeros((1, w), F32))
        sb = (jnp.zeros((1, w), F32), jnp.zeros((1, w), F32))
        for j in range(ncc):
            jb = ncc - 1 - j
            xs_ref[pb * 4 + 0, nlat + j:nlat + j + 1, :] = sf[0]
            xs_ref[pb * 4 + 1, nlat + j:nlat + j + 1, :] = sf[1]
            xs_ref[pb * 4 + 2, nlat + jb:nlat + jb + 1, :] = sb[0]
            xs_ref[pb * 4 + 3, nlat + jb:nlat + jb + 1, :] = sb[1]
            sf = _cmul_add(mu[pb][0], mu[pb][1], sf[0], sf[1], zc[0][j:j + 1], zc[1][j:j + 1])
            sb = _cmul_add(mu[pb][2], mu[pb][3], sb[0], sb[1], zc[2][jb:jb + 1], zc[3][jb:jb + 1])
        pre.append((sf, sb))

    mub = [[jnp.broadcast_to(m, (SUBLANE, w)) for m in mu[pb]] for pb in range(PAIRS)]

    def rows(i, v):
        return pl.ds(pl.multiple_of(i * NSEGS + v * SUBLANE, SUBLANE), SUBLANE)

    def step(i, st):
        ib = seglen - 1 - i
        new = []
        for pb in range(PAIRS):
            for v in range(nv):
                k = (pb * nv + v) * 4
                m = mub[pb]
                fr, fi = _cmul_add(m[0], m[1], st[k], st[k + 1],
                                   z_ref[pb * 4 + 0, rows(i, v), :], z_ref[pb * 4 + 1, rows(i, v), :])
                br, bi = _cmul_add(m[2], m[3], st[k + 2], st[k + 3],
                                   z_ref[pb * 4 + 2, rows(ib, v), :], z_ref[pb * 4 + 3, rows(ib, v), :])
                new += [fr, fi, br, bi]
        return tuple(new)

    zero = jnp.zeros((SUBLANE, w), F32)
    fin = lax.fori_loop(0, seglen, step, (zero,) * (PAIRS * nv * 4))

    carry = []
    for pb in range(PAIRS):
        cf, cb = pre[pb]
        rows_f, rows_b = [None] * NSEGS, [None] * NSEGS
        for s in range(NSEGS):
            sr = NSEGS - 1 - s
            rows_f[s] = cf
            rows_b[sr] = cb
            kf = (pb * nv + s // SUBLANE) * 4
            kb = (pb * nv + sr // SUBLANE) * 4
            sl, srl = s % SUBLANE, sr % SUBLANE
            cf = _cmul_add(mun[pb][0], mun[pb][1], cf[0], cf[1], fin[kf][sl:sl + 1, :], fin[kf + 1][sl:sl + 1, :])
            cb = _cmul_add(mun[pb][2], mun[pb][3], cb[0], cb[1],
                           fin[kb + 2][srl:srl + 1, :], fin[kb + 3][srl:srl + 1, :])
        for v in range(nv):
            seg = slice(v * SUBLANE, (v + 1) * SUBLANE)
            carry += [jnp.concatenate([r[0] for r in rows_f[seg]], axis=0),
                      jnp.concatenate([r[1] for r in rows_f[seg]], axis=0),
                      jnp.concatenate([r[0] for r in rows_b[seg]], axis=0),
                      jnp.concatenate([r[1] for r in rows_b[seg]], axis=0)]

    def step2(i, st):
        ib = seglen - 1 - i
        for pb in range(PAIRS):
            for v in range(nv):
                k = (pb * nv + v) * 4
                xs_ref[pb * 4 + 0, rows(i, v), :] = st[k]
                xs_ref[pb * 4 + 1, rows(i, v), :] = st[k + 1]
                xs_ref[pb * 4 + 2, rows(ib, v), :] = st[k + 2]
                xs_ref[pb * 4 + 3, rows(ib, v), :] = st[k + 3]
        return step(i, st)

    lax.fori_loop(0, seglen, step2, tuple(carry))
    for k in range(PAIRS * 4):
        xp_ref[0, :, k * w:(k + 1) * w] = xs_ref[k].astype(BF16)


def _s5_readout_kernel(d_ref, xp_ref, kr_ref, syt_ref, rm_ref, dsk_ref, y_ref, yc_ref, bt_ref, wy_ref, yb_ref,
                       sy_ref, *, seglen):
    w = LANE
    cw = 4 * w
    fw = CHUNK * w
    ncg = fw // cw

    @pl.when(pl.program_id(1) == 0)
    def _():
        for i in range(CHUNK):
            for j in range(CHUNK):
                bt_ref[i * w:(i + 1) * w, j * w:(j + 1) * w] = kr_ref[0, j - i + CHUNK - 1]
        for b in range(syt_ref.shape[3] // w):
            for dc in range(4):
                r0 = (b * 4 + dc) * w
                sy_ref[r0:r0 + w, :] = syt_ref[dc // 2, dc % 2, :, b * w:(b + 1) * w].T.astype(BF16)
        nrow = sy_ref.shape[0]
        nh = rm_ref.shape[0] // CHUNK
        row = lax.broadcasted_iota(jnp.int32, (nrow, cw), 0)
        row_g = 2 * (row // cw) + (row % w) // (w // 2)
        col_g = (lax.broadcasted_iota(jnp.int32, (nrow, cw), 1) % w) // nh
        same = row_g == col_g
        for cg in range(ncg):
            cols = slice(cg * cw, (cg + 1) * cw)
            wy_ref[:, cols] = jnp.where(same, _dot(sy_ref[...], rm_ref[:, cols]), 0.0).astype(BF16)

    nrows = d_ref.shape[1]
    for cg in range(ncg):
        cols = slice(cg * cw, (cg + 1) * cw)
        yb_ref[:, cols] = (_dot(d_ref[0], bt_ref[:, cols]) + _dot(xp_ref[0], wy_ref[:, cols])
                           + d_ref[0, :, cols].astype(F32) * dsk_ref[0, :, cols])

    ngrp = nrows // NSEGS

    def unfold(j, carry):
        gi = pl.program_id(1) * ngrp + j
        r0 = pl.multiple_of(j * NSEGS, NSEGS)
        tok = yb_ref[pl.ds(r0, NSEGS), :].reshape(NSEGS * CHUNK, w)

        @pl.when(gi < seglen)
        def _():
            for s in range(NSEGS):
                t0 = pl.multiple_of((s * seglen + gi) * CHUNK, CHUNK)
                y_ref[pl.ds(t0, CHUNK), :] = tok[s * CHUNK:(s + 1) * CHUNK]

        @pl.when(gi >= seglen)
        def _():
            t0 = pl.multiple_of((gi - seglen) * NSEGS * CHUNK, NSEGS * CHUNK)
            yc_ref[pl.ds(t0, NSEGS * CHUNK), :] = tok

        return carry

    lax.fori_loop(0, ngrp, unfold, 0)


def _s5_tables(lam_re, lam_im, log_dt, b_re, b_im, c_re, c_im, d_skip, *, seglen):
    t = CHUNK
    ng, ns = lam_re.shape[1], lam_re.shape[2]
    nh = b_re.shape[-1]
    gp = ng * ns
    g8 = LANE // nh
    na = ng // g8
    nb = g8 // 2
    assert 2 * ns == LANE and nb % PAIRS == 0
    dt = jnp.exp(log_dt)[..., None]
    a, b = (lam_re * dt).reshape(2, gp), (lam_im * dt).reshape(2, gp)

    def lam_pow(m):
        mm = jnp.asarray(m, F32).reshape(-1, 1, 1)
        mag = jnp.exp(a[None] * mm)
        return mag * jnp.cos(b[None] * mm), mag * jnp.sin(b[None] * mm)

    pr, pi = lam_pow(np.arange(t + 1))
    lr, li = lam_re.reshape(2, gp), lam_im.reshape(2, gp)
    nr, ni = pr[1] - 1.0, pi[1]
    den = lr * lr + li * li
    qr, qi = (nr * lr + ni * li) / den, (ni * lr - nr * li) / den
    bt_re = b_re.transpose(0, 3, 1, 2).reshape(2, nh, gp)
    bt_im = b_im.transpose(0, 3, 1, 2).reshape(2, nh, gp)
    bb_re = qr[:, None] * bt_re - qi[:, None] * bt_im
    bb_im = qr[:, None] * bt_im + qi[:, None] * bt_re
    e_re = pr[:t, :, None] * bb_re[None] - pi[:t, :, None] * bb_im[None]
    e_im = pr[:t, :, None] * bb_im[None] + pi[:t, :, None] * bb_re[None]
    ct_re = c_re.transpose(0, 2, 1, 3).reshape(2, nh, gp)
    ct_im = c_im.transpose(0, 2, 1, 3).reshape(2, nh, gp)

    sl = slice(1, t + 1)
    pw_re = jnp.stack([pr[sl, 0], pr[sl, 1][::-1]])[:, :, None]
    pw_im = jnp.stack([pi[sl, 0], pi[sl, 1][::-1]])[:, :, None]
    wy_re = ct_re[:, None] * pw_re - ct_im[:, None] * pw_im
    wy_im = ct_re[:, None] * pw_im + ct_im[:, None] * pw_re
    syt = jnp.stack([wy_re, -wy_im], axis=1).reshape(2, 2, t * nh, gp)
    mr, mi = lam_pow([t, t * seglen])
    mu = jnp.stack([mr, mi], axis=2).reshape(2, 4, na, nb, LANE).transpose(2, 3, 0, 1, 4)
    dsk = jnp.tile(d_skip.reshape(na, 1, LANE), (1, 1, t))
    return e_re, e_im, ct_re, ct_im, syt, mu, dsk


def _s5_operators(params, seglen):
    e_re, e_im, ct_re, ct_im, syt, mu, dsk = jax.vmap(functools.partial(_s5_tables, seglen=seglen))(*params)
    depth, t, _, nh, gp = e_re.shape
    na = mu.shape[1]
    tw = gp // na
    nlag = 2 * CHUNK - 1
    kr = pl.pallas_call(
        _s5_taps_kernel,
        out_shape=jax.ShapeDtypeStruct((depth, na, nlag, LANE, LANE), BF16),
        grid=(depth, na),
        in_specs=[pl.BlockSpec((None, t, 2, nh, tw), lambda l, i: (l, 0, 0, 0, i)),
                  pl.BlockSpec((None, t, 2, nh, tw), lambda l, i: (l, 0, 0, 0, i)),
                  pl.BlockSpec((None, 2, nh, tw), lambda l, i: (l, 0, 0, i)),
                  pl.BlockSpec((None, 2, nh, tw), lambda l, i: (l, 0, 0, i))],
        out_specs=pl.BlockSpec((None, 1, nlag, LANE, LANE), lambda l, i: (l, i, 0, 0, 0)),
        compiler_params=_params(("arbitrary", "arbitrary")),
        name="s5_taps",
    )(e_re, e_im, ct_re, ct_im)
    r_mat = np.kron(np.eye(CHUNK, dtype=np.float32), np.tile(np.eye(nh, dtype=np.float32), (1, LANE // nh)))
    return kr, e_re, e_im, syt, jnp.asarray(r_mat, F32).astype(BF16), mu, dsk


def _s5(u, uc, li, ops):
    kr, e_re, e_im, syt, r_mat, mu, dsk = ops
    l, d_ssm = u.shape
    lc = uc.shape[0]
    na, nb = mu.shape[1], mu.shape[2]
    fw = CHUNK * LANE
    nlat, ncc = l // CHUNK, lc // CHUNK
    nc = nlat + ncc
    seglen = nlat // NSEGS
    sw = PAIRS * 4 * LANE
    assert seglen * NSEGS == nlat and ncc * CHUNK == lc
    d, xp = pl.pallas_call(
        functools.partial(_s5_state_kernel, seglen=seglen, ncc=ncc),
        out_shape=(jax.ShapeDtypeStruct((na, nc, fw), BF16), jax.ShapeDtypeStruct((na, nc, nb * 4 * LANE), BF16)),
        grid=(na, nb // PAIRS),
        in_specs=[
            pl.BlockSpec((l, LANE), lambda a, b: (0, a)),
            pl.BlockSpec((lc, LANE), lambda a, b: (0, a)),
            pl.BlockSpec((None,) + e_re.shape[1:4] + (PAIRS * LANE,), lambda a, b: (li, 0, 0, 0, a * (nb // PAIRS) + b)),
            pl.BlockSpec((None,) + e_im.shape[1:4] + (PAIRS * LANE,), lambda a, b: (li, 0, 0, 0, a * (nb // PAIRS) + b)),
            pl.BlockSpec((None, 1, PAIRS, 2, 4, LANE), lambda a, b: (li, a, b, 0, 0, 0)),
        ],
        out_specs=(pl.BlockSpec((1, nc, fw), lambda a, b: (a, 0, 0)),
                   pl.BlockSpec((1, nc, sw), lambda a, b: (a, 0, b))),
        scratch_shapes=[pltpu.VMEM((fw, sw), BF16),
                        pltpu.VMEM((PAIRS * 4, nc, LANE), F32), pltpu.VMEM((PAIRS * 4, nc, LANE), F32)],
        compiler_params=_params(("arbitrary", "arbitrary")),
        name="s5_state",
    )(u, uc, e_re, e_im, mu)
    rbs = max(r for r in range(NSEGS, min(nc, 512) + 1, NSEGS) if nc % r == 0)
    nrb = nc // rbs
    assert NSEGS % 16 == 0 and ncc % NSEGS == 0
    return pl.pallas_call(
        functools.partial(_s5_readout_kernel, seglen=seglen),
        out_shape=(jax.ShapeDtypeStruct((l, d_ssm), F32), jax.ShapeDtypeStruct((lc, d_ssm), F32)),
        grid=(na, nrb),
        in_specs=[
            pl.BlockSpec((1, rbs, fw), lambda a, r: (a, r, 0)),
            pl.BlockSpec((1, rbs, nb * 4 * LANE), lambda a, r: (a, r, 0)),
            pl.BlockSpec((None, 1) + kr.shape[2:], lambda a, r: (li, a, 0, 0, 0)),
            pl.BlockSpec((None,) + syt.shape[1:4] + (nb * LANE,), lambda a, r: (li, 0, 0, 0, a)),
            _const_spec(r_mat.shape),
            pl.BlockSpec((None, 1, 1, fw), lambda a, r: (li, a, 0, 0)),
        ],
        out_specs=(pl.BlockSpec((l, LANE), lambda a, r: (0, a)), pl.BlockSpec((lc, LANE), lambda a, r: (0, a))),
        scratch_shapes=[pltpu.VMEM((fw, fw), BF16), pltpu.VMEM((nb * 4 * LANE, fw), BF16),
                        pltpu.VMEM((rbs, fw), F32), pltpu.VMEM((nb * 4 * LANE, syt.shape[3]), BF16)],
        compiler_params=_params(("arbitrary", "arbitrary")),
        name="s5_readout",
    )(d, xp, kr, syt, r_mat, dsk)


def _fft1_kernel(p_ref, q_ref, m_ref, o_ref, s_ref, *, nb):
    for n in range(nb):
        rhs = jnp.concatenate([p_ref[:, n, :], q_ref[:, n, :]], axis=0).astype(BF16)
        s_ref[:, n, :] = _dot(m_ref[...], rhs)
    o_ref[...] = s_ref[...].astype(BF16)


def _fft2_kernel(yr_ref, yi_ref, ca_ref, sa_ref, cb_ref, sb_ref, o_ref, s_ref, *, kb):
    cb, sb = cb_ref[...], sb_ref[...]
    for k in range(kb):
        ca, sa = ca_ref[k:k + 1, :], sa_ref[k:k + 1, :]
        g = jnp.concatenate([ca * cb - sa * sb, sa * cb + ca * sb], axis=1).astype(BF16)
        rhs = jnp.concatenate([yr_ref[k], yi_ref[k]], axis=0)
        s_ref[:, k, :] = _dot(g, rhs)
    o_ref[...] = s_ref[...].astype(BF16)


def _fft_ctx_kernel(p_ref, q_ref, m_ref, o_ref):
    rhs = jnp.concatenate([p_ref[...], q_ref[...]], axis=0).astype(BF16)
    o_ref[...] = _dot(m_ref[...], rhs).astype(BF16)


def _fnet_tables(l, lc):
    r = math.isqrt(l)
    assert r * r == l
    k = np.arange(r)
    th = 2.0 * np.pi * (np.outer(k, k) % r) / r
    c, s = np.cos(th), np.sin(th)
    m1 = np.block([[c, -s], [-s, -c]])
    ta = 2.0 * np.pi * np.outer(k, k) / l
    scale = 1.0 / math.sqrt(l)
    tw2 = tuple(jnp.asarray(v, F32) for v in (np.cos(ta), np.sin(ta), c * scale, s * scale))
    kc = np.arange(lc)
    thc = 2.0 * np.pi * (np.outer(kc, kc) % lc) / lc
    mc = np.concatenate([np.cos(thc), -np.sin(thc)], axis=1) / math.sqrt(lc)
    return (jnp.asarray(m1, F32).astype(BF16), tw2, jnp.asarray(mc, F32).astype(BF16))


def _fnet_lat(p, q, m1, tw2):
    l, c = p.shape
    r = m1.shape[0] // 2
    assert r * r == l
    p3 = p.reshape(r, r, c)
    q3 = q.reshape(r, r, c)
    nb = 16
    y1 = pl.pallas_call(
        functools.partial(_fft1_kernel, nb=nb),
        out_shape=jax.ShapeDtypeStruct((2 * r, r, c), BF16),
        grid=(r // nb,),
        in_specs=[pl.BlockSpec((r, nb, c), lambda j: (0, j, 0)),
                  pl.BlockSpec((r, nb, c), lambda j: (0, j, 0)),
                  _const_spec(m1.shape)],
        out_specs=pl.BlockSpec((2 * r, nb, c), lambda j: (0, j, 0)),
        scratch_shapes=[pltpu.VMEM((2 * r, nb, c), F32)],
        compiler_params=_params(("arbitrary",)),
        name="fft_stage1",
    )(p3, q3, m1)
    kb = 16
    ca, sa, cb, sb = tw2
    out = pl.pallas_call(
        functools.partial(_fft2_kernel, kb=kb),
        out_shape=jax.ShapeDtypeStruct((r, r, c), BF16),
        grid=(r // kb,),
        in_specs=[pl.BlockSpec((kb, r, c), lambda j: (j, 0, 0)),
                  pl.BlockSpec((kb, r, c), lambda j: (r // kb + j, 0, 0)),
                  pl.BlockSpec((kb, r), lambda j: (j, 0)),
                  pl.BlockSpec((kb, r), lambda j: (j, 0)),
                  _const_spec(cb.shape), _const_spec(sb.shape)],
        out_specs=pl.BlockSpec((r, kb, c), lambda j: (0, j, 0)),
        scratch_shapes=[pltpu.VMEM((r, kb, c), F32)],
        compiler_params=_params(("arbitrary",)),
        name="fft_stage2",
    )(y1, y1, ca, sa, cb, sb)
    return out.reshape(l, c)


def _fnet_ctx(pc, qc, mc):
    return pl.pallas_call(
        _fft_ctx_kernel,
        out_shape=jax.ShapeDtypeStruct(pc.shape, BF16),
        grid=(1,),
        in_specs=[_whole_spec(pc.shape), _whole_spec(qc.shape), _const_spec(mc.shape)],
        out_specs=_whole_spec(pc.shape),
        compiler_params=_params(("arbitrary",)),
        name="fft_ctx",
    )(pc, qc, mc)


def _pos_tables(l, d):
    quarter = d // 4
    omega = 1.0 / (POS_BASE ** (jnp.arange(quarter, dtype=F32) / quarter))

    def enc(pv):
        ang = pv[:, None] * omega[None, :]
        return jnp.concatenate([jnp.sin(ang), jnp.cos(ang)], axis=-1)

    renc = enc(jnp.arange(l // GRID_W, dtype=F32))
    cenc = jnp.tile(enc(jnp.arange(GRID_W, dtype=F32)), (TM // GRID_W, 1))
    return renc, cenc


def kernel(x, c, ctx, c_ctx, w_ada, b_ada, norm_g, ffn1_gu, ffn1_down, ffn2_gu, ffn2_down, w_in, w_out,
           ssm_lam_re, ssm_lam_im, ssm_log_dt, ssm_b_re, ssm_b_im, ssm_c_re, ssm_c_im, ssm_d, w_glu, w_fmix):
    bsz, l, d = x.shape
    lc = ctx.shape[1]
    depth = w_ada.shape[0]
    d_ssm = w_glu.shape[1]
    head = w_fmix.shape[2]
    assert bsz == 1 and l % TM == 0 and TM % GRID_W == 0
    seglen = (l // CHUNK) // NSEGS

    mods = _modulation(c_ctx, c, w_ada, b_ada)
    pos = _pos_tables(l, d)
    m1, tw2, mc = _fnet_tables(l, lc)
    kc = np.arange(head)
    thc = 2.0 * np.pi * (np.outer(kc, kc) % head) / head
    cs = jnp.asarray(np.concatenate([np.cos(thc), np.sin(thc)], axis=1) / math.sqrt(head), F32).astype(BF16)
    wgu1, wd1, wgu2, wd2 = (a.astype(BF16) for a in (ffn1_gu, ffn1_down, ffn2_gu, ffn2_down))
    win, wout, wglu, wfm = (a.astype(BF16) for a in (w_in, w_out, w_glu, w_fmix))

    ops = _s5_operators((ssm_lam_re, ssm_lam_im, ssm_log_dt, ssm_b_re, ssm_b_im, ssm_c_re, ssm_c_im, ssm_d), seglen)

    xl, xc = x[0], ctx[0]
    for li in range(depth):
        last = li == depth - 1
        xl, u, p, q, xc, uc, pc, qc = _layer_in(xl, xc, pos if li == 0 else None, li, mods, norm_g, wgu1, wd1,
                                                win, cs, d_ssm=d_ssm)
        ys, ysc = _s5(u, uc, li, ops)
        yf = _fnet_lat(p, q, m1, tw2)
        ctx_in = None if last else (xc, ysc, _fnet_ctx(pc, qc, mc))
        xl, xc = _layer_out((xl, ys, yf), ctx_in, li, mods, norm_g, wglu, wfm, wout, wgu2, wd2)
    return xl[None]
```

```python
import functools
import math

import numpy as np
import jax
import jax.numpy as jnp
from jax import lax
from jax.experimental import pallas as pl
from jax.experimental.pallas import tpu as pltpu

F32 = jnp.float32
BF16 = jnp.bfloat16

LANE = 128
SUBLANE = 8
VMEM_LIMIT = 56 * 1024 * 1024

EPS = 1e-6
GRID_W = 64
POS_BASE = 10000.0
N_MOD = 9
SSM_GROUP = 16
STATE = 64
N_FNET_HEADS = 4
CHUNK = 16
TM = 512
CTX, LAT = 0, 1


def _dot(a, b):
    return jnp.dot(a, b, preferred_element_type=F32)


def _params(sem=None):
    return pltpu.CompilerParams(dimension_semantics=sem, vmem_limit_bytes=VMEM_LIMIT)


def _const_spec(shape):
    nd = len(shape)
    return pl.BlockSpec(shape, lambda *_: (0,) * nd, pipeline_mode=pl.Buffered(1))


def _rms(x, g):
    ms = jnp.mean(x * x, axis=-1, keepdims=True)
    return x * lax.rsqrt(ms + EPS) * g


def _pre(x, g, mod_ref, who, k):
    return _rms(x, g) * (1.0 + mod_ref[who, 3 * k + 1:3 * k + 2, :]) + mod_ref[who, 3 * k:3 * k + 1, :]


def _post(x, y, g, mod_ref, who, k, weight):
    return x + (weight * mod_ref[who, 3 * k + 2:3 * k + 3, :]) * _rms(y, g)


def _on_last_step(fn):
    pl.when(pl.program_id(0) == pl.num_programs(0) - 1)(fn)


def _sigmoid(x):
    return 1.0 / (1.0 + jnp.exp(-x))


def _gelu_tanh(x):
    return 0.5 * x * (1.0 + jnp.tanh(math.sqrt(2.0 / math.pi) * (x + 0.044715 * (x * x * x))))


def _mod_kernel(cb_ref, w_ref, b_ref, o_ref, s_ref, *, d_model, tn):
    nb = tn // LANE

    @pl.when((pl.program_id(0) == 0) & (pl.program_id(1) == 0))
    def _():
        cv = cb_ref[...]
        s_ref[...] = cv * _sigmoid(cv)

    def body(t, accs):
        d0 = pl.multiple_of(t * SUBLANE, SUBLANE)
        s = [s_ref[r, pl.ds(d0, SUBLANE), :] for r in range(2)]
        new = list(accs)
        for j in range(nb):
            w = w_ref[0, pl.ds(d0, SUBLANE), j * LANE:(j + 1) * LANE]
            for r in range(2):
                new[r * nb + j] = accs[r * nb + j] + w * s[r]
        return tuple(new)

    init = tuple(jnp.zeros((SUBLANE, LANE), F32) for _ in range(2 * nb))
    accs = lax.fori_loop(0, d_model // SUBLANE, body, init, unroll=4)
    for r in range(2):
        for j in range(nb):
            o_ref[0, r:r + 1, j * LANE:(j + 1) * LANE] = (
                jnp.sum(accs[r * nb + j], axis=0, keepdims=True) + b_ref[0, :, j * LANE:(j + 1) * LANE])


def _modulation(c_ctx, c, w_ada, b_ada):
    depth, d_model, n_out = w_ada.shape
    tn = 9 * LANE
    assert n_out % tn == 0
    cb = jnp.broadcast_to(jnp.stack([c_ctx, c[0]])[:, :, None], (2, d_model, LANE))
    out = pl.pallas_call(
        functools.partial(_mod_kernel, d_model=d_model, tn=tn),
        out_shape=jax.ShapeDtypeStruct((depth, 2, n_out), F32),
        grid=(depth, n_out // tn),
        in_specs=[
            pl.BlockSpec((2, d_model, LANE), lambda l, j: (0, 0, 0)),
            pl.BlockSpec((1, d_model, tn), lambda l, j: (l, 0, j)),
            pl.BlockSpec((1, 1, tn), lambda l, j: (l, 0, j)),
        ],
        out_specs=pl.BlockSpec((1, 2, tn), lambda l, j: (l, 0, j)),
        scratch_shapes=[pltpu.VMEM((2, d_model, LANE), F32)],
        compiler_params=_params(("arbitrary", "arbitrary")),
        name="adaln_mod",
    )(cb, w_ada, b_ada.reshape(depth, 1, n_out))
    return out.reshape(depth, 2, N_MOD, d_model)


def _ffn_body(x, who, mod_ref, g_ref, wgu_ref, wd_ref, k, gi):
    f = wd_ref.shape[0]
    h = _pre(x, g_ref[gi:gi + 1, :], mod_ref, who, k).astype(BF16)
    gate = _dot(h, wgu_ref[:, :f])
    up = _dot(h, wgu_ref[:, f:])
    act = (gate * _sigmoid(gate) * up).astype(BF16)
    y = _dot(act, wd_ref[...])
    return _post(x, y, g_ref[gi + 1:gi + 2, :], mod_ref, who, k, 0.5)


def _inproj_body(x, who, mod_ref, g_ref, win_ref, cs_ref, u_ref, p_ref, q_ref):
    d_ssm = u_ref.shape[1]
    head = cs_ref.shape[0]
    h = _pre(x, g_ref[2:3, :], mod_ref, who, 1).astype(BF16)
    hh = _dot(h, win_ref[...])
    u_ref[...] = hh[:, :d_ssm]
    for n in range(N_FNET_HEADS):
        lo = d_ssm + n * head
        pq = _dot(hh[:, lo:lo + head].astype(BF16), cs_ref[...])
        p_ref[:, n * head:(n + 1) * head] = pq[:, :head]
        q_ref[:, n * head:(n + 1) * head] = pq[:, head:]


def _outproj_body(x, ys, yf, who, mod_ref, g_ref, wglu_ref, wfm_ref, wout_ref):
    d_ssm = wglu_ref.shape[0]
    head = wfm_ref.shape[1]
    yf = yf.astype(BF16)
    h = _gelu_tanh(ys)
    s = h * _sigmoid(_dot(h.astype(BF16), wglu_ref[...]))
    y = _dot(s.astype(BF16), wout_ref[:d_ssm, :])
    for n in range(N_FNET_HEADS):
        fm = _dot(yf[:, n * head:(n + 1) * head], wfm_ref[n])
        y = y + _dot(fm.astype(BF16), wout_ref[d_ssm + n * head:d_ssm + (n + 1) * head, :])
    return _post(x, y, g_ref[3:4, :], mod_ref, who, 1, 1.0)


def _cast_chunks(src_refs, dst_refs):
    for src_ref, dst_ref in zip(src_refs, dst_refs):
        dst_ref[...] = src_ref[...].astype(BF16)


def _layer_in_kernel(*refs, first):
    if first:
        (x_ref, xc_ref, renc_ref, cenc_ref, mod_ref, g_ref, wgu_ref, wd_ref, win_ref, cs_ref, ng_ref, nd_ref,
         o_ref, u_ref, p_ref, q_ref, oc_ref, uc_ref, pc_ref, qc_ref, ngo_ref, ndo_ref, xs_ref) = refs
        tm = x_ref.shape[0]
        half = renc_ref.shape[1]
        r0 = pl.program_id(0) * (tm // GRID_W)
        xs_ref[:, half:] = x_ref[:, half:] + cenc_ref[...]
        for q in range(tm // GRID_W):
            xs_ref[q * GRID_W:(q + 1) * GRID_W, :half] = (
                x_ref[q * GRID_W:(q + 1) * GRID_W, :half] + renc_ref[pl.ds(r0 + q, 1), :])
        x = xs_ref[...]
    else:
        (x_ref, xc_ref, mod_ref, g_ref, wgu_ref, wd_ref, win_ref, cs_ref, ng_ref, nd_ref,
         o_ref, u_ref, p_ref, q_ref, oc_ref, uc_ref, pc_ref, qc_ref, ngo_ref, ndo_ref) = refs
        x = x_ref[...]
    _cast_chunks((ng_ref, nd_ref), (ngo_ref, ndo_ref))
    x1 = _ffn_body(x, LAT, mod_ref, g_ref, wgu_ref, wd_ref, 0, 0)
    o_ref[...] = x1
    _inproj_body(x1, LAT, mod_ref, g_ref, win_ref, cs_ref, u_ref, p_ref, q_ref)

    def _():
        xc1 = _ffn_body(xc_ref[...], CTX, mod_ref, g_ref, wgu_ref, wd_ref, 0, 0)
        oc_ref[...] = xc1
        _inproj_body(xc1, CTX, mod_ref, g_ref, win_ref, cs_ref, uc_ref, pc_ref, qc_ref)
    _on_last_step(_)


def _layer_out_kernel(*refs, with_ctx):
    if with_ctx:
        (x_ref, ys_ref, yf_ref, xc_ref, ysc_ref, yfc_ref, mod_ref, g_ref, wglu_ref, wfm_ref, wout_ref,
         wgu_ref, wd_ref, ng_ref, nd_ref, o_ref, oc_ref, ngo_ref, ndo_ref) = refs
        _cast_chunks((ng_ref, nd_ref), (ngo_ref, ndo_ref))
    else:
        x_ref, ys_ref, yf_ref, mod_ref, g_ref, wglu_ref, wfm_ref, wout_ref, wgu_ref, wd_ref, o_ref = refs
    x2 = _outproj_body(x_ref[...], ys_ref[...], yf_ref[...], LAT, mod_ref, g_ref, wglu_ref, wfm_ref, wout_ref)
    o_ref[...] = _ffn_body(x2, LAT, mod_ref, g_ref, wgu_ref, wd_ref, 2, 4)
    if with_ctx:
        def _():
            xc2 = _outproj_body(xc_ref[...], ysc_ref[...], yfc_ref[...], CTX, mod_ref, g_ref, wglu_ref, wfm_ref,
                                wout_ref)
            oc_ref[...] = _ffn_body(xc2, CTX, mod_ref, g_ref, wgu_ref, wd_ref, 2, 4)
        _on_last_step(_)


def _tile_spec(cols):
    return pl.BlockSpec((TM, cols), lambda i: (i, 0))


def _whole_spec(shape):
    nd = len(shape)
    return pl.BlockSpec(shape, lambda *_: (0,) * nd)


def _layer_spec(arr, li):
    nd = arr.ndim - 1
    return pl.BlockSpec((None,) + arr.shape[1:], lambda *_: (li,) + (0,) * nd, pipeline_mode=pl.Buffered(1))


def _cast_specs(arr, li, nt):
    rows, cols = arr.shape[1:]
    k = 1
    while (rows * k) % nt or (rows * k // nt) % 16:
        k *= 2
    rc = rows * k // nt
    assert rows % rc == 0 and nt % k == 0
    return (pl.BlockSpec((None, rc, cols), lambda i: (li, i // k, 0)), pl.BlockSpec((rc, cols), lambda i: (i // k, 0)),
            jax.ShapeDtypeStruct((rows, cols), BF16))


def _layer_in(xl, xc, pos, li, mods, gains, wgu, wd, win, cs, nxt, *, d_ssm):
    l, d = xl.shape
    lc = xc.shape[0]
    first = pos is not None
    d_f = win.shape[2] - d_ssm
    widths = (d, d_ssm, d_f, d_f)
    stacked = [mods, gains]
    nt = l // TM
    casts = [_cast_specs(a, li, nt) for a in nxt]
    return pl.pallas_call(
        functools.partial(_layer_in_kernel, first=first),
        out_shape=[jax.ShapeDtypeStruct((l, wd_), F32) for wd_ in widths]
        + [jax.ShapeDtypeStruct((lc, wd_), F32) for wd_ in widths] + [c[2] for c in casts],
        grid=(nt,),
        in_specs=([_tile_spec(d), _whole_spec(xc.shape)] + ([_const_spec(a.shape) for a in pos] if first else [])
                  + [_layer_spec(a, li) for a in stacked] + [_const_spec(wgu.shape), _const_spec(wd.shape)]
                  + [_layer_spec(win, li), _const_spec(cs.shape)] + [c[0] for c in casts]),
        out_specs=([_tile_spec(wd_) for wd_ in widths] + [_whole_spec((lc, wd_)) for wd_ in widths]
                   + [c[1] for c in casts]),
        scratch_shapes=[pltpu.VMEM((TM, d), F32)] if first else [],
        compiler_params=_params(("arbitrary",)),
        name="layer_in",
    )(xl, xc, *(pos if first else ()), *stacked, wgu, wd, win, cs, *nxt)


def _layer_out(lat, ctx, li, mods, gains, wglu, wfm, wout, wgu, wd, nxt):
    xl, ys, yf = lat
    l, d = xl.shape
    with_ctx = ctx is not None
    ctx = list(ctx) if with_ctx else []
    stacked = [mods, gains, wglu, wfm, wout]
    nt = l // TM
    casts = [_cast_specs(a, li + 1, nt) for a in nxt] if with_ctx else []
    outs = pl.pallas_call(
        functools.partial(_layer_out_kernel, with_ctx=with_ctx),
        out_shape=([jax.ShapeDtypeStruct(xl.shape, F32)] + ([jax.ShapeDtypeStruct(ctx[0].shape, F32)] if with_ctx else [])
                   + [c[2] for c in casts]),
        grid=(nt,),
        in_specs=([_tile_spec(d), _tile_spec(ys.shape[1]), _tile_spec(yf.shape[1])]
                  + [_whole_spec(a.shape) for a in ctx] + [_layer_spec(a, li) for a in stacked]
                  + [_const_spec(wgu.shape), _const_spec(wd.shape)] + [c[0] for c in casts]),
        out_specs=([_tile_spec(d)] + ([_whole_spec(ctx[0].shape)] if with_ctx else []) + [c[1] for c in casts]),
        compiler_params=_params(("arbitrary",)),
        name="layer_out",
    )(xl, ys, yf, *ctx, *stacked, wgu, wd, *(nxt if with_ctx else ()))
    return tuple(outs) if with_ctx else (outs[0], None, None, None)


NSEGS = 16
PAIRS = 2


def _cmul_add(ar, ai, xr, xi, zr, zi):
    return ar * xr - ai * xi + zr, ar * xi + ai * xr + zi


def _dot_nt(a, b):
    return lax.dot_general(a, b, (((1,), (1,)), ((), ())), precision=lax.Precision.HIGHEST,
                           preferred_element_type=F32)


def _s5_taps_kernel(e_re_ref, e_im_ref, ct_re_ref, ct_im_ref, kr_ref):
    t = CHUNK
    nh = ct_re_ref.shape[1]
    g8 = LANE // nh
    gw = e_re_ref.shape[3] // g8
    same = (lax.broadcasted_iota(jnp.int32, (LANE, g8 * gw), 0) // nh
            == lax.broadcasted_iota(jnp.int32, (LANE, g8 * gw), 1) // gw)
    taps = []
    for d in range(2):
        cre = jnp.where(same, jnp.concatenate([ct_re_ref[d]] * g8, axis=0), 0.0)
        cim = jnp.where(same, jnp.concatenate([ct_im_ref[d]] * g8, axis=0), 0.0)
        er = jnp.concatenate([e_re_ref[m, d] for m in range(t)], axis=0)
        ei = jnp.concatenate([e_im_ref[m, d] for m in range(t)], axis=0)
        taps.append(_dot_nt(er, cre) - _dot_nt(ei, cim))
    blk = (lax.broadcasted_iota(jnp.int32, (LANE, LANE), 0) // nh
           == lax.broadcasted_iota(jnp.int32, (LANE, LANE), 1) // nh)
    for lag in range(-(t - 1), t):
        if lag > 0:
            src = taps[0][lag * nh:(lag + 1) * nh]
        elif lag < 0:
            src = taps[1][-lag * nh:(1 - lag) * nh]
        else:
            src = taps[0][:nh] + taps[1][:nh]
        kr_ref[0, lag + t - 1] = jnp.where(blk, jnp.concatenate([src] * g8, axis=0), 0.0).astype(BF16)


def _s5_state_kernel(u_ref, uc_ref, e_re_ref, e_im_ref, mu_ref, d_ref, xp_ref, wz_ref, z_ref, xs_ref, *, seglen, ncc):
    w = LANE
    nc = d_ref.shape[1]
    nlat = NSEGS * seglen
    nv = NSEGS // SUBLANE
    nh = e_re_ref.shape[2]
    ns = w // 2

    @pl.when(pl.program_id(1) == 0)
    def _():
        def fold(i, carry):
            r0 = pl.multiple_of(i * NSEGS, NSEGS)
            tiles = [u_ref[pl.ds(pl.multiple_of((s * seglen + i) * CHUNK, CHUNK), CHUNK), :] for s in range(NSEGS)]
            d_ref[0, pl.ds(r0, NSEGS), :] = jnp.concatenate(tiles, axis=0).reshape(NSEGS, CHUNK * w).astype(BF16)
            return carry

        lax.fori_loop(0, seglen, fold, 0)
        d_ref[0, nlat:nlat + ncc, :] = uc_ref[...].reshape(ncc, CHUNK * w).astype(BF16)

    wz_ref[...] = jnp.zeros(wz_ref.shape, BF16)
    first = (lax.broadcasted_iota(jnp.int32, (nh, 4 * w), 1) % w) < ns
    for pb in range(PAIRS):
        pair = pl.program_id(1) * PAIRS + pb
        lanes = slice(pb * w, (pb + 1) * w)
        for i in range(CHUNK):
            ez = jnp.concatenate([e_re_ref[CHUNK - 1 - i, 0, :, lanes], e_im_ref[CHUNK - 1 - i, 0, :, lanes],
                                  e_re_ref[i, 1, :, lanes], e_im_ref[i, 1, :, lanes]], axis=1).astype(BF16)
            both = jnp.concatenate([jnp.where(first, ez, jnp.zeros_like(ez)),
                                    jnp.where(first, jnp.zeros_like(ez), ez)], axis=0)
            r0 = pl.multiple_of(i * w + pair * 2 * nh, 2 * nh)
            wz_ref[pl.ds(r0, 2 * nh), pb * 4 * w:(pb + 1) * 4 * w] = both

    for pb in range(PAIRS):
        zz = _dot(d_ref[0], wz_ref[:, pb * 4 * w:(pb + 1) * 4 * w])
        for c in range(4):
            z_ref[pb * 4 + c] = zz[:, c * w:(c + 1) * w]

    mu = [[mu_ref[0, pb, 0, c:c + 1, :] for c in range(4)] for pb in range(PAIRS)]
    mun = [[mu_ref[0, pb, 1, c:c + 1, :] for c in range(4)] for pb in range(PAIRS)]

    pre = []
    for pb in range(PAIRS):
        zc = [z_ref[pb * 4 + c, nlat:nlat + ncc, :] for c in range(4)]
        sf = (jnp.zeros((1, w), F32), jnp.zeros((1, w), F32))
        sb = (jnp.zeros((1, w), F32), jnp.zeros((1, w), F32))
        for j in range(ncc):
            jb = ncc - 1 - j
            xs_ref[pb * 4 + 0, nlat + j:nlat + j + 1, :] = sf[0]
            xs_ref[pb * 4 + 1, nlat + j:nlat + j + 1, :] = sf[1]
            xs_ref[pb * 4 + 2, nlat + jb:nlat + jb + 1, :] = sb[0]
            xs_ref[pb * 4 + 3, nlat + jb:nlat + jb + 1, :] = sb[1]
            sf = _cmul_add(mu[pb][0], mu[pb][1], sf[0], sf[1], zc[0][j:j + 1], zc[1][j:j + 1])
            sb = _cmul_add(mu[pb][2], mu[pb][3], sb[0], sb[1], zc[2][jb:jb + 1], zc[3][jb:jb + 1])
        pre.append((sf, sb))

    mub = [[jnp.broadcast_to(m, (SUBLANE, w)) for m in mu[pb]] for pb in range(PAIRS)]

    def rows(i, v):
        return pl.ds(pl.multiple_of(i * NSEGS + v * SUBLANE, SUBLANE), SUBLANE)

    def step(i, st):
        ib = seglen - 1 - i
        new = []
        for pb in range(PAIRS):
            for v in range(nv):
                k = (pb * nv + v) * 4
                m = mub[pb]
                fr, fi = _cmul_add(m[0], m[1], st[k], st[k + 1],
                                   z_ref[pb * 4 + 0, rows(i, v), :], z_ref[pb * 4 + 1, rows(i, v), :])
                br, bi = _cmul_add(m[2], m[3], st[k + 2], st[k + 3],
                                   z_ref[pb * 4 + 2, rows(ib, v), :], z_ref[pb * 4 + 3, rows(ib, v), :])
                new += [fr, fi, br, bi]
        return tuple(new)

    zero = jnp.zeros((SUBLANE, w), F32)
    fin = lax.fori_loop(0, seglen, step, (zero,) * (PAIRS * nv * 4))

    carry = []
    for pb in range(PAIRS):
        cf, cb = pre[pb]
        rows_f, rows_b = [None] * NSEGS, [None] * NSEGS
        for s in range(NSEGS):
            sr = NSEGS - 1 - s
            rows_f[s] = cf
            rows_b[sr] = cb
            kf = (pb * nv + s // SUBLANE) * 4
            kb = (pb * nv + sr // SUBLANE) * 4
            sl, srl = s % SUBLANE, sr % SUBLANE
            cf = _cmul_add(mun[pb][0], mun[pb][1], cf[0], cf[1], fin[kf][sl:sl + 1, :], fin[kf + 1][sl:sl + 1, :])
            cb = _cmul_add(mun[pb][2], mun[pb][3], cb[0], cb[1],
                           fin[kb + 2][srl:srl + 1, :], fin[kb + 3][srl:srl + 1, :])
        for v in range(nv):
            seg = slice(v * SUBLANE, (v + 1) * SUBLANE)
            carry += [jnp.concatenate([r[0] for r in rows_f[seg]], axis=0),
                      jnp.concatenate([r[1] for r in rows_f[seg]], axis=0),
                      jnp.concatenate([r[0] for r in rows_b[seg]], axis=0),
                      jnp.concatenate([r[1] for r in rows_b[seg]], axis=0)]

    def step2(i, st):
        ib = seglen - 1 - i
        for pb in range(PAIRS):
            for v in range(nv):
                k = (pb * nv + v) * 4
                xs_ref[pb * 4 + 0, rows(i, v), :] = st[k]
                xs_ref[pb * 4 + 1, rows(i, v), :] = st[k + 1]
                xs_ref[pb * 4 + 2, rows(ib, v), :] = st[k + 2]
                xs_ref[pb * 4 + 3, rows(ib, v), :] = st[k + 3]
        return step(i, st)

    lax.fori_loop(0, seglen, step2, tuple(carry))
    for k in range(PAIRS * 4):
        xp_ref[0, :, k * w:(k + 1) * w] = xs_ref[k].astype(BF16)


def _s5_readout_kernel(d_ref, xp_ref, kr_ref, syt_ref, rm_ref, dsk_ref, y_ref, yc_ref, bt_ref, wy_ref, yb_ref,
                       sy_ref, *, seglen):
    w = LANE
    cw = 4 * w
    fw = CHUNK * w
    ncg = fw // cw

    @pl.when(pl.program_id(1) == 0)
    def _():
        for i in range(CHUNK):
            for j in range(CHUNK):
                bt_ref[i * w:(i + 1) * w, j * w:(j + 1) * w] = kr_ref[0, j - i + CHUNK - 1]
        for b in range(syt_ref.shape[3] // w):
            for dc in range(4):
                r0 = (b * 4 + dc) * w
                sy_ref[r0:r0 + w, :] = syt_ref[dc // 2, dc % 2, :, b * w:(b + 1) * w].T.astype(BF16)
        nrow = sy_ref.shape[0]
        nh = rm_ref.shape[0] // CHUNK
        row = lax.broadcasted_iota(jnp.int32, (nrow, cw), 0)
        row_g = 2 * (row // cw) + (row % w) // (w // 2)
        col_g = (lax.broadcasted_iota(jnp.int32, (nrow, cw), 1) % w) // nh
        same = row_g == col_g
        for cg in range(ncg):
            cols = slice(cg * cw, (cg + 1) * cw)
            wy_ref[:, cols] = jnp.where(same, _dot(sy_ref[...], rm_ref[:, cols]), 0.0).astype(BF16)

    nrows = d_ref.shape[1]
    for cg in range(ncg):
        cols = slice(cg * cw, (cg + 1) * cw)
        yb_ref[:, cols] = (_dot(d_ref[0], bt_ref[:, cols]) + _dot(xp_ref[0], wy_ref[:, cols])
                           + d_ref[0, :, cols].astype(F32) * dsk_ref[0, :, cols])

    ngrp = nrows // NSEGS

    def unfold(j, carry):
        gi = pl.program_id(1) * ngrp + j
        r0 = pl.multiple_of(j * NSEGS, NSEGS)
        tok = yb_ref[pl.ds(r0, NSEGS), :].reshape(NSEGS * CHUNK, w)

        @pl.when(gi < seglen)
        def _():
            for s in range(NSEGS):
                t0 = pl.multiple_of((s * seglen + gi) * CHUNK, CHUNK)
                y_ref[pl.ds(t0, CHUNK), :] = tok[s * CHUNK:(s + 1) * CHUNK]

        @pl.when(gi >= seglen)
        def _():
            t0 = pl.multiple_of((gi - seglen) * NSEGS * CHUNK, NSEGS * CHUNK)
            yc_ref[pl.ds(t0, NSEGS * CHUNK), :] = tok

        return carry

    lax.fori_loop(0, ngrp, unfold, 0)


def _s5_tables(lam_re, lam_im, log_dt, b_re, b_im, c_re, c_im, d_skip, *, seglen):
    t = CHUNK
    ng, ns = lam_re.shape[1], lam_re.shape[2]
    nh = b_re.shape[-1]
    gp = ng * ns
    g8 = LANE // nh
    na = ng // g8
    nb = g8 // 2
    assert 2 * ns == LANE and nb % PAIRS == 0
    dt = jnp.exp(log_dt)[..., None]
    a, b = (lam_re * dt).reshape(2, gp), (lam_im * dt).reshape(2, gp)

    def lam_pow(m):
        mm = jnp.asarray(m, F32).reshape(-1, 1, 1)
        mag = jnp.exp(a[None] * mm)
        return mag * jnp.cos(b[None] * mm), mag * jnp.sin(b[None] * mm)

    pr, pi = lam_pow(np.arange(t + 1))
    lr, li = lam_re.reshape(2, gp), lam_im.reshape(2, gp)
    nr, ni = pr[1] - 1.0, pi[1]
    den = lr * lr + li * li
    qr, qi = (nr * lr + ni * li) / den, (ni * lr - nr * li) / den
    bt_re = b_re.transpose(0, 3, 1, 2).reshape(2, nh, gp)
    bt_im = b_im.transpose(0, 3, 1, 2).reshape(2, nh, gp)
    bb_re = qr[:, None] * bt_re - qi[:, None] * bt_im
    bb_im = qr[:, None] * bt_im + qi[:, None] * bt_re
    e_re = pr[:t, :, None] * bb_re[None] - pi[:t, :, None] * bb_im[None]
    e_im = pr[:t, :, None] * bb_im[None] + pi[:t, :, None] * bb_re[None]
    ct_re = c_re.transpose(0, 2, 1, 3).reshape(2, nh, gp)
    ct_im = c_im.transpose(0, 2, 1, 3).reshape(2, nh, gp)

    sl = slice(1, t + 1)
    pw_re = jnp.stack([pr[sl, 0], pr[sl, 1][::-1]])[:, :, None]
    pw_im = jnp.stack([pi[sl, 0], pi[sl, 1][::-1]])[:, :, None]
    wy_re = ct_re[:, None] * pw_re - ct_im[:, None] * pw_im
    wy_im = ct_re[:, None] * pw_im + ct_im[:, None] * pw_re
    syt = jnp.stack([wy_re, -wy_im], axis=1).reshape(2, 2, t * nh, gp)
    mr, mi = lam_pow([t, t * seglen])
    mu = jnp.stack([mr, mi], axis=2).reshape(2, 4, na, nb, LANE).transpose(2, 3, 0, 1, 4)
    dsk = jnp.tile(d_skip.reshape(na, 1, LANE), (1, 1, t))
    return e_re, e_im, ct_re, ct_im, syt, mu, dsk


def _s5_operators(params, seglen):
    e_re, e_im, ct_re, ct_im, syt, mu, dsk = jax.vmap(functools.partial(_s5_tables, seglen=seglen))(*params)
    depth, t, _, nh, gp = e_re.shape
    na = mu.shape[1]
    tw = gp // na
    nlag = 2 * CHUNK - 1
    kr = pl.pallas_call(
        _s5_taps_kernel,
        out_shape=jax.ShapeDtypeStruct((depth, na, nlag, LANE, LANE), BF16),
        grid=(depth, na),
        in_specs=[pl.BlockSpec((None, t, 2, nh, tw), lambda l, i: (l, 0, 0, 0, i)),
                  pl.BlockSpec((None, t, 2, nh, tw), lambda l, i: (l, 0, 0, 0, i)),
                  pl.BlockSpec((None, 2, nh, tw), lambda l, i: (l, 0, 0, i)),
                  pl.BlockSpec((None, 2, nh, tw), lambda l, i: (l, 0, 0, i))],
        out_specs=pl.BlockSpec((None, 1, nlag, LANE, LANE), lambda l, i: (l, i, 0, 0, 0)),
        compiler_params=_params(("arbitrary", "arbitrary")),
        name="s5_taps",
    )(e_re, e_im, ct_re, ct_im)
    r_mat = np.kron(np.eye(CHUNK, dtype=np.float32), np.tile(np.eye(nh, dtype=np.float32), (1, LANE // nh)))
    return kr, e_re, e_im, syt, jnp.asarray(r_mat, F32).astype(BF16), mu, dsk


def _s5(u, uc, li, ops):
    kr, e_re, e_im, syt, r_mat, mu, dsk = ops
    l, d_ssm = u.shape
    lc = uc.shape[0]
    na, nb = mu.shape[1], mu.shape[2]
    fw = CHUNK * LANE
    nlat, ncc = l // CHUNK, lc // CHUNK
    nc = nlat + ncc
    seglen = nlat // NSEGS
    sw = PAIRS * 4 * LANE
    assert seglen * NSEGS == nlat and ncc * CHUNK == lc
    d, xp = pl.pallas_call(
        functools.partial(_s5_state_kernel, seglen=seglen, ncc=ncc),
        out_shape=(jax.ShapeDtypeStruct((na, nc, fw), BF16), jax.ShapeDtypeStruct((na, nc, nb * 4 * LANE), BF16)),
        grid=(na, nb // PAIRS),
        in_specs=[
            pl.BlockSpec((l, LANE), lambda a, b: (0, a)),
            pl.BlockSpec((lc, LANE), lambda a, b: (0, a)),
            pl.BlockSpec((None,) + e_re.shape[1:4] + (PAIRS * LANE,), lambda a, b: (li, 0, 0, 0, a * (nb // PAIRS) + b)),
            pl.BlockSpec((None,) + e_im.shape[1:4] + (PAIRS * LANE,), lambda a, b: (li, 0, 0, 0, a * (nb // PAIRS) + b)),
            pl.BlockSpec((None, 1, PAIRS, 2, 4, LANE), lambda a, b: (li, a, b, 0, 0, 0)),
        ],
        out_specs=(pl.BlockSpec((1, nc, fw), lambda a, b: (a, 0, 0)),
                   pl.BlockSpec((1, nc, sw), lambda a, b: (a, 0, b))),
        scratch_shapes=[pltpu.VMEM((fw, sw), BF16),
                        pltpu.VMEM((PAIRS * 4, nc, LANE), F32), pltpu.VMEM((PAIRS * 4, nc, LANE), F32)],
        compiler_params=_params(("arbitrary", "arbitrary")),
        name="s5_state",
    )(u, uc, e_re, e_im, mu)
    rbs = max(r for r in range(NSEGS, min(nc, 512) + 1, NSEGS) if nc % r == 0)
    nrb = nc // rbs
    assert NSEGS % 16 == 0 and ncc % NSEGS == 0
    return pl.pallas_call(
        functools.partial(_s5_readout_kernel, seglen=seglen),
        out_shape=(jax.ShapeDtypeStruct((l, d_ssm), F32), jax.ShapeDtypeStruct((lc, d_ssm), F32)),
        grid=(na, nrb),
        in_specs=[
            pl.BlockSpec((1, rbs, fw), lambda a, r: (a, r, 0)),
            pl.BlockSpec((1, rbs, nb * 4 * LANE), lambda a, r: (a, r, 0)),
            pl.BlockSpec((None, 1) + kr.shape[2:], lambda a, r: (li, a, 0, 0, 0)),
            pl.BlockSpec((None,) + syt.shape[1:4] + (nb * LANE,), lambda a, r: (li, 0, 0, 0, a)),
            _const_spec(r_mat.shape),
            pl.BlockSpec((None, 1, 1, fw), lambda a, r: (li, a, 0, 0)),
        ],
        out_specs=(pl.BlockSpec((l, LANE), lambda a, r: (0, a)), pl.BlockSpec((lc, LANE), lambda a, r: (0, a))),
        scratch_shapes=[pltpu.VMEM((fw, fw), BF16), pltpu.VMEM((nb * 4 * LANE, fw), BF16),
                        pltpu.VMEM((rbs, fw), F32), pltpu.VMEM((nb * 4 * LANE, syt.shape[3]), BF16)],
        compiler_params=_params(("arbitrary", "arbitrary")),
        name="s5_readout",
    )(d, xp, kr, syt, r_mat, dsk)


def _fft1_kernel(p_ref, q_ref, m_ref, o_ref, s_ref, *, nb):
    for n in range(nb):
        rhs = jnp.concatenate([p_ref[:, n, :], q_ref[:, n, :]], axis=0).astype(BF16)
        s_ref[:, n, :] = _dot(m_ref[...], rhs)
    o_ref[...] = s_ref[...].astype(BF16)


def _fft2_kernel(yr_ref, yi_ref, ca_ref, sa_ref, cb_ref, sb_ref, o_ref, s_ref, *, kb):
    cb, sb = cb_ref[...], sb_ref[...]
    for k in range(kb):
        ca, sa = ca_ref[k:k + 1, :], sa_ref[k:k + 1, :]
        g = jnp.concatenate([ca * cb - sa * sb, sa * cb + ca * sb], axis=1).astype(BF16)
        rhs = jnp.concatenate([yr_ref[k], yi_ref[k]], axis=0)
        s_ref[:, k, :] = _dot(g, rhs)
    o_ref[...] = s_ref[...].astype(BF16)


def _fft_ctx_kernel(p_ref, q_ref, m_ref, o_ref):
    rhs = jnp.concatenate([p_ref[...], q_ref[...]], axis=0).astype(BF16)
    o_ref[...] = _dot(m_ref[...], rhs).astype(BF16)


def _fnet_tables(l, lc):
    r = math.isqrt(l)
    assert r * r == l
    k = np.arange(r)
    th = 2.0 * np.pi * (np.outer(k, k) % r) / r
    c, s = np.cos(th), np.sin(th)
    m1 = np.block([[c, -s], [-s, -c]])
    ta = 2.0 * np.pi * np.outer(k, k) / l
    scale = 1.0 / math.sqrt(l)
    tw2 = tuple(jnp.asarray(v, F32) for v in (np.cos(ta), np.sin(ta), c * scale, s * scale))
    kc = np.arange(lc)
    thc = 2.0 * np.pi * (np.outer(kc, kc) % lc) / lc
    mc = np.concatenate([np.cos(thc), -np.sin(thc)], axis=1) / math.sqrt(lc)
    return (jnp.asarray(m1, F32).astype(BF16), tw2, jnp.asarray(mc, F32).astype(BF16))


def _fnet_lat(p, q, m1, tw2):
    l, c = p.shape
    r = m1.shape[0] // 2
    assert r * r == l
    p3 = p.reshape(r, r, c)
    q3 = q.reshape(r, r, c)
    nb = 16
    y1 = pl.pallas_call(
        functools.partial(_fft1_kernel, nb=nb),
        out_shape=jax.ShapeDtypeStruct((2 * r, r, c), BF16),
        grid=(r // nb,),
        in_specs=[pl.BlockSpec((r, nb, c), lambda j: (0, j, 0)),
                  pl.BlockSpec((r, nb, c), lambda j: (0, j, 0)),
                  _const_spec(m1.shape)],
        out_specs=pl.BlockSpec((2 * r, nb, c), lambda j: (0, j, 0)),
        scratch_shapes=[pltpu.VMEM((2 * r, nb, c), F32)],
        compiler_params=_params(("arbitrary",)),
        name="fft_stage1",
    )(p3, q3, m1)
    kb = 16
    ca, sa, cb, sb = tw2
    out = pl.pallas_call(
        functools.partial(_fft2_kernel, kb=kb),
        out_shape=jax.ShapeDtypeStruct((r, r, c), BF16),
        grid=(r // kb,),
        in_specs=[pl.BlockSpec((kb, r, c), lambda j: (j, 0, 0)),
                  pl.BlockSpec((kb, r, c), lambda j: (r // kb + j, 0, 0)),
                  pl.BlockSpec((kb, r), lambda j: (j, 0)),
                  pl.BlockSpec((kb, r), lambda j: (j, 0)),
                  _const_spec(cb.shape), _const_spec(sb.shape)],
        out_specs=pl.BlockSpec((r, kb, c), lambda j: (0, j, 0)),
        scratch_shapes=[pltpu.VMEM((r, kb, c), F32)],
        compiler_params=_params(("arbitrary",)),
        name="fft_stage2",
    )(y1, y1, ca, sa, cb, sb)
    return out.reshape(l, c)


def _fnet_ctx(pc, qc, mc):
    return pl.pallas_call(
        _fft_ctx_kernel,
        out_shape=jax.ShapeDtypeStruct(pc.shape, BF16),
        grid=(1,),
        in_specs=[_whole_spec(pc.shape), _whole_spec(qc.shape), _const_spec(mc.shape)],
        out_specs=_whole_spec(pc.shape),
        compiler_params=_params(("arbitrary",)),
        name="fft_ctx",
    )(pc, qc, mc)


def _pos_tables(l, d):
    quarter = d // 4
    omega = 1.0 / (POS_BASE ** (jnp.arange(quarter, dtype=F32) / quarter))

    def enc(pv):
        ang = pv[:, None] * omega[None, :]
        return jnp.concatenate([jnp.sin(ang), jnp.cos(ang)], axis=-1)

    renc = enc(jnp.arange(l // GRID_W, dtype=F32))
    cenc = jnp.tile(enc(jnp.arange(GRID_W, dtype=F32)), (TM // GRID_W, 1))
    return renc, cenc


def kernel(x, c, ctx, c_ctx, w_ada, b_ada, norm_g, ffn1_gu, ffn1_down, ffn2_gu, ffn2_down, w_in, w_out,
           ssm_lam_re, ssm_lam_im, ssm_log_dt, ssm_b_re, ssm_b_im, ssm_c_re, ssm_c_im, ssm_d, w_glu, w_fmix):
    bsz, l, d = x.shape
    lc = ctx.shape[1]
    depth = w_ada.shape[0]
    d_ssm = w_glu.shape[1]
    head = w_fmix.shape[2]
    assert bsz == 1 and l % TM == 0 and TM % GRID_W == 0
    seglen = (l // CHUNK) // NSEGS

    mods = _modulation(c_ctx, c, w_ada, b_ada)
    pos = _pos_tables(l, d)
    m1, tw2, mc = _fnet_tables(l, lc)
    kc = np.arange(head)
    thc = 2.0 * np.pi * (np.outer(kc, kc) % head) / head
    cs = jnp.asarray(np.concatenate([np.cos(thc), np.sin(thc)], axis=1) / math.sqrt(head), F32).astype(BF16)
    win, wout, wglu, wfm = (a.astype(BF16) for a in (w_in, w_out, w_glu, w_fmix))
    wgu1, wd1 = ffn1_gu[0].astype(BF16), ffn1_down[0].astype(BF16)

    ops = _s5_operators((ssm_lam_re, ssm_lam_im, ssm_log_dt, ssm_b_re, ssm_b_im, ssm_c_re, ssm_c_im, ssm_d), seglen)

    xl, xc = x[0], ctx[0]
    for li in range(depth):
        last = li == depth - 1
        xl, u, p, q, xc, uc, pc, qc, wgu2, wd2 = _layer_in(xl, xc, pos if li == 0 else None, li, mods, norm_g, wgu1, wd1,
                                                           win, cs, (ffn2_gu, ffn2_down), d_ssm=d_ssm)
        ys, ysc = _s5(u, uc, li, ops)
        yf = _fnet_lat(p, q, m1, tw2)
        ctx_in = None if last else (xc, ysc, _fnet_ctx(pc, qc, mc))
        xl, xc, wgu1, wd1 = _layer_out((xl, ys, yf), ctx_in, li, mods, norm_g, wglu, wfm, wout, wgu2, wd2,
                                       (ffn1_gu, ffn1_down))
    return xl[None]
```

```python
import functools
import math

import numpy as np
import jax
import jax.numpy as jnp
from jax import lax
from jax.experimental import pallas as pl
from jax.experimental.pallas import tpu as pltpu

F32 = jnp.float32
BF16 = jnp.bfloat16

LANE = 128
SUBLANE = 8
VMEM_LIMIT = 56 * 1024 * 1024

EPS = 1e-6
GRID_W = 64
POS_BASE = 10000.0
N_MOD = 9
SSM_GROUP = 16
STATE = 64
N_FNET_HEADS = 4
CHUNK = 16
TM = 512
CTX, LAT = 0, 1


def _dot(a, b):
    return jnp.dot(a, b, preferred_element_type=F32)


def _params(sem=None):
    return pltpu.CompilerParams(dimension_semantics=sem, vmem_limit_bytes=VMEM_LIMIT)


def _const_spec(shape):
    nd = len(shape)
    return pl.BlockSpec(shape, lambda *_: (0,) * nd, pipeline_mode=pl.Buffered(1))


def _rms(x, g):
    ms = jnp.mean(x * x, axis=-1, keepdims=True)
    return x * lax.rsqrt(ms + EPS) * g


def _pre(x, g, mod_ref, who, k):
    return _rms(x, g) * (1.0 + mod_ref[who, 3 * k + 1:3 * k + 2, :]) + mod_ref[who, 3 * k:3 * k + 1, :]


def _post(x, y, g, mod_ref, who, k, weight):
    return x + (weight * mod_ref[who, 3 * k + 2:3 * k + 3, :]) * _rms(y, g)


def _on_last_step(fn):
    pl.when(pl.program_id(0) == pl.num_programs(0) - 1)(fn)


def _sigmoid(x):
    return 1.0 / (1.0 + jnp.exp(-x))


def _gelu_tanh(x):
    return 0.5 * x * (1.0 + jnp.tanh(math.sqrt(2.0 / math.pi) * (x + 0.044715 * (x * x * x))))


def _mod_kernel(cb_ref, w_ref, b_ref, o_ref, s_ref, *, d_model, tn):
    nb = tn // LANE

    @pl.when((pl.program_id(0) == 0) & (pl.program_id(1) == 0))
    def _():
        cv = cb_ref[...]
        s_ref[...] = cv * _sigmoid(cv)

    def body(t, accs):
        d0 = pl.multiple_of(t * SUBLANE, SUBLANE)
        s = [s_ref[r, pl.ds(d0, SUBLANE), :] for r in range(2)]
        new = list(accs)
        for j in range(nb):
            w = w_ref[0, pl.ds(d0, SUBLANE), j * LANE:(j + 1) * LANE]
            for r in range(2):
                new[r * nb + j] = accs[r * nb + j] + w * s[r]
        return tuple(new)

    init = tuple(jnp.zeros((SUBLANE, LANE), F32) for _ in range(2 * nb))
    accs = lax.fori_loop(0, d_model // SUBLANE, body, init, unroll=4)
    for r in range(2):
        for j in range(nb):
            o_ref[0, r:r + 1, j * LANE:(j + 1) * LANE] = (
                jnp.sum(accs[r * nb + j], axis=0, keepdims=True) + b_ref[0, :, j * LANE:(j + 1) * LANE])


def _modulation(c_ctx, c, w_ada, b_ada):
    depth, d_model, n_out = w_ada.shape
    tn = 9 * LANE
    assert n_out % tn == 0
    cb = jnp.broadcast_to(jnp.stack([c_ctx, c[0]])[:, :, None], (2, d_model, LANE))
    out = pl.pallas_call(
        functools.partial(_mod_kernel, d_model=d_model, tn=tn),
        out_shape=jax.ShapeDtypeStruct((depth, 2, n_out), F32),
        grid=(depth, n_out // tn),
        in_specs=[
            pl.BlockSpec((2, d_model, LANE), lambda l, j: (0, 0, 0)),
            pl.BlockSpec((1, d_model, tn), lambda l, j: (l, 0, j)),
            pl.BlockSpec((1, 1, tn), lambda l, j: (l, 0, j)),
        ],
        out_specs=pl.BlockSpec((1, 2, tn), lambda l, j: (l, 0, j)),
        scratch_shapes=[pltpu.VMEM((2, d_model, LANE), F32)],
        compiler_params=_params(("arbitrary", "arbitrary")),
        name="adaln_mod",
    )(cb, w_ada, b_ada.reshape(depth, 1, n_out))
    return out.reshape(depth, 2, N_MOD, d_model)


def _ffn_body(x, who, mod_ref, g_ref, wgu_ref, wd_ref, k, gi):
    f = wd_ref.shape[0]
    h = _pre(x, g_ref[gi:gi + 1, :], mod_ref, who, k).astype(BF16)
    gate = _dot(h, wgu_ref[:, :f])
    up = _dot(h, wgu_ref[:, f:])
    act = (gate * _sigmoid(gate) * up).astype(BF16)
    y = _dot(act, wd_ref[...])
    return _post(x, y, g_ref[gi + 1:gi + 2, :], mod_ref, who, k, 0.5)


def _inproj_body(x, who, mod_ref, g_ref, win_ref, cs_ref, u_ref, p_ref, q_ref):
    d_ssm = u_ref.shape[1]
    head = cs_ref.shape[0]
    h = _pre(x, g_ref[2:3, :], mod_ref, who, 1).astype(BF16)
    hh = _dot(h, win_ref[...])
    u_ref[...] = hh[:, :d_ssm]
    for n in range(N_FNET_HEADS):
        lo = d_ssm + n * head
        pq = _dot(hh[:, lo:lo + head].astype(BF16), cs_ref[...])
        p_ref[:, n * head:(n + 1) * head] = pq[:, :head]
        q_ref[:, n * head:(n + 1) * head] = pq[:, head:]


def _outproj_body(x, ys, yf, who, mod_ref, g_ref, wglu_ref, wfm_ref, wout_ref):
    head = wfm_ref.shape[1]
    yf = yf.astype(BF16)
    h = _gelu_tanh(ys)
    s = h * _sigmoid(_dot(h.astype(BF16), wglu_ref[...]))
    mixed = [s.astype(BF16)]
    for n in range(N_FNET_HEADS):
        mixed.append(_dot(yf[:, n * head:(n + 1) * head], wfm_ref[n]).astype(BF16))
    y = _dot(jnp.concatenate(mixed, axis=1), wout_ref[...])
    return _post(x, y, g_ref[3:4, :], mod_ref, who, 1, 1.0)


def _cast_chunks(src_refs, dst_refs):
    for src_ref, dst_ref in zip(src_refs, dst_refs):
        dst_ref[...] = src_ref[...].astype(BF16)


def _layer_in_kernel(*refs, first):
    if first:
        (x_ref, xc_ref, renc_ref, cenc_ref, mod_ref, g_ref, wgu_ref, wd_ref, win_ref, cs_ref, ng_ref, nd_ref,
         o_ref, u_ref, p_ref, q_ref, oc_ref, uc_ref, pc_ref, qc_ref, ngo_ref, ndo_ref, xs_ref) = refs
        tm = x_ref.shape[0]
        half = renc_ref.shape[1]
        r0 = pl.program_id(0) * (tm // GRID_W)
        xs_ref[:, half:] = x_ref[:, half:] + cenc_ref[...]
        for q in range(tm // GRID_W):
            xs_ref[q * GRID_W:(q + 1) * GRID_W, :half] = (
                x_ref[q * GRID_W:(q + 1) * GRID_W, :half] + renc_ref[pl.ds(r0 + q, 1), :])
        x = xs_ref[...]
    else:
        (x_ref, xc_ref, mod_ref, g_ref, wgu_ref, wd_ref, win_ref, cs_ref, ng_ref, nd_ref,
         o_ref, u_ref, p_ref, q_ref, oc_ref, uc_ref, pc_ref, qc_ref, ngo_ref, ndo_ref) = refs
        x = x_ref[...]
    _cast_chunks((ng_ref, nd_ref), (ngo_ref, ndo_ref))
    x1 = _ffn_body(x, LAT, mod_ref, g_ref, wgu_ref, wd_ref, 0, 0)
    o_ref[...] = x1
    _inproj_body(x1, LAT, mod_ref, g_ref, win_ref, cs_ref, u_ref, p_ref, q_ref)

    def _():
        xc1 = _ffn_body(xc_ref[...], CTX, mod_ref, g_ref, wgu_ref, wd_ref, 0, 0)
        oc_ref[...] = xc1
        _inproj_body(xc1, CTX, mod_ref, g_ref, win_ref, cs_ref, uc_ref, pc_ref, qc_ref)
    _on_last_step(_)


def _layer_out_kernel(*refs, with_ctx):
    if with_ctx:
        (x_ref, ys_ref, yf_ref, xc_ref, ysc_ref, yfc_ref, mod_ref, g_ref, wglu_ref, wfm_ref, wout_ref,
         wgu_ref, wd_ref, ng_ref, nd_ref, o_ref, oc_ref, ngo_ref, ndo_ref) = refs
        _cast_chunks((ng_ref, nd_ref), (ngo_ref, ndo_ref))
    else:
        x_ref, ys_ref, yf_ref, mod_ref, g_ref, wglu_ref, wfm_ref, wout_ref, wgu_ref, wd_ref, o_ref = refs
    x2 = _outproj_body(x_ref[...], ys_ref[...], yf_ref[...], LAT, mod_ref, g_ref, wglu_ref, wfm_ref, wout_ref)
    o_ref[...] = _ffn_body(x2, LAT, mod_ref, g_ref, wgu_ref, wd_ref, 2, 4)
    if with_ctx:
        def _():
            xc2 = _outproj_body(xc_ref[...], ysc_ref[...], yfc_ref[...], CTX, mod_ref, g_ref, wglu_ref, wfm_ref,
                                wout_ref)
            oc_ref[...] = _ffn_body(xc2, CTX, mod_ref, g_ref, wgu_ref, wd_ref, 2, 4)
        _on_last_step(_)


def _tile_spec(cols):
    return pl.BlockSpec((TM, cols), lambda i: (i, 0))


def _whole_spec(shape):
    nd = len(shape)
    return pl.BlockSpec(shape, lambda *_: (0,) * nd)


def _layer_spec(arr, li):
    nd = arr.ndim - 1
    return pl.BlockSpec((None,) + arr.shape[1:], lambda *_: (li,) + (0,) * nd, pipeline_mode=pl.Buffered(1))


def _cast_specs(arr, li, nt):
    rows, cols = arr.shape[1:]
    k = 1
    while (rows * k) % nt or (rows * k // nt) % 16:
        k *= 2
    rc = rows * k // nt
    assert rows % rc == 0 and nt % k == 0
    return (pl.BlockSpec((None, rc, cols), lambda i: (li, i // k, 0)), pl.BlockSpec((rc, cols), lambda i: (i // k, 0)),
            jax.ShapeDtypeStruct((rows, cols), BF16))


def _layer_in(xl, xc, pos, li, mods, gains, wgu, wd, win, cs, nxt, *, d_ssm):
    l, d = xl.shape
    lc = xc.shape[0]
    first = pos is not None
    d_f = win.shape[2] - d_ssm
    widths = (d, d_ssm, d_f, d_f)
    stacked = [mods, gains]
    nt = l // TM
    casts = [_cast_specs(a, li, nt) for a in nxt]
    return pl.pallas_call(
        functools.partial(_layer_in_kernel, first=first),
        out_shape=[jax.ShapeDtypeStruct((l, wd_), F32) for wd_ in widths]
        + [jax.ShapeDtypeStruct((lc, wd_), F32) for wd_ in widths] + [c[2] for c in casts],
        grid=(nt,),
        in_specs=([_tile_spec(d), _whole_spec(xc.shape)] + ([_const_spec(a.shape) for a in pos] if first else [])
                  + [_layer_spec(a, li) for a in stacked] + [_const_spec(wgu.shape), _const_spec(wd.shape)]
                  + [_layer_spec(win, li), _const_spec(cs.shape)] + [c[0] for c in casts]),
        out_specs=([_tile_spec(wd_) for wd_ in widths] + [_whole_spec((lc, wd_)) for wd_ in widths]
                   + [c[1] for c in casts]),
        scratch_shapes=[pltpu.VMEM((TM, d), F32)] if first else [],
        compiler_params=_params(("arbitrary",)),
        name="layer_in",
    )(xl, xc, *(pos if first else ()), *stacked, wgu, wd, win, cs, *nxt)


def _layer_out(lat, ctx, li, mods, gains, wglu, wfm, wout, wgu, wd, nxt):
    xl, ys, yf = lat
    l, d = xl.shape
    with_ctx = ctx is not None
    ctx = list(ctx) if with_ctx else []
    stacked = [mods, gains, wglu, wfm, wout]
    nt = l // TM
    casts = [_cast_specs(a, li + 1, nt) for a in nxt] if with_ctx else []
    outs = pl.pallas_call(
        functools.partial(_layer_out_kernel, with_ctx=with_ctx),
        out_shape=([jax.ShapeDtypeStruct(xl.shape, F32)] + ([jax.ShapeDtypeStruct(ctx[0].shape, F32)] if with_ctx else [])
                   + [c[2] for c in casts]),
        grid=(nt,),
        in_specs=([_tile_spec(d), _tile_spec(ys.shape[1]), _tile_spec(yf.shape[1])]
                  + [_whole_spec(a.shape) for a in ctx] + [_layer_spec(a, li) for a in stacked]
                  + [_const_spec(wgu.shape), _const_spec(wd.shape)] + [c[0] for c in casts]),
        out_specs=([_tile_spec(d)] + ([_whole_spec(ctx[0].shape)] if with_ctx else []) + [c[1] for c in casts]),
        compiler_params=_params(("arbitrary",)),
        name="layer_out",
    )(xl, ys, yf, *ctx, *stacked, wgu, wd, *(nxt if with_ctx else ()))
    return tuple(outs) if with_ctx else (outs[0], None, None, None)


NSEGS = 16
PAIRS = 2


def _cmul_add(ar, ai, xr, xi, zr, zi):
    return ar * xr - ai * xi + zr, ar * xi + ai * xr + zi


def _dot_nt(a, b):
    return lax.dot_general(a, b, (((1,), (1,)), ((), ())), precision=lax.Precision.HIGHEST,
                           preferred_element_type=F32)


def _s5_taps_kernel(e_re_ref, e_im_ref, ct_re_ref, ct_im_ref, kr_ref):
    t = CHUNK
    nh = ct_re_ref.shape[1]
    g8 = LANE // nh
    gw = e_re_ref.shape[3] // g8
    same = (lax.broadcasted_iota(jnp.int32, (LANE, g8 * gw), 0) // nh
            == lax.broadcasted_iota(jnp.int32, (LANE, g8 * gw), 1) // gw)
    taps = []
    for d in range(2):
        cre = jnp.where(same, jnp.concatenate([ct_re_ref[d]] * g8, axis=0), 0.0)
        cim = jnp.where(same, jnp.concatenate([ct_im_ref[d]] * g8, axis=0), 0.0)
        er = jnp.concatenate([e_re_ref[m, d] for m in range(t)], axis=0)
        ei = jnp.concatenate([e_im_ref[m, d] for m in range(t)], axis=0)
        taps.append(_dot_nt(er, cre) - _dot_nt(ei, cim))
    blk = (lax.broadcasted_iota(jnp.int32, (LANE, LANE), 0) // nh
           == lax.broadcasted_iota(jnp.int32, (LANE, LANE), 1) // nh)
    for lag in range(-(t - 1), t):
        if lag > 0:
            src = taps[0][lag * nh:(lag + 1) * nh]
        elif lag < 0:
            src = taps[1][-lag * nh:(1 - lag) * nh]
        else:
            src = taps[0][:nh] + taps[1][:nh]
        kr_ref[0, lag + t - 1] = jnp.where(blk, jnp.concatenate([src] * g8, axis=0), 0.0).astype(BF16)


def _s5_state_kernel(u_ref, uc_ref, e_re_ref, e_im_ref, mu_ref, d_ref, xp_ref, wz_ref, z_ref, xs_ref, *, seglen, ncc):
    w = LANE
    nc = d_ref.shape[1]
    nlat = NSEGS * seglen
    nv = NSEGS // SUBLANE
    nh = e_re_ref.shape[2]
    ns = w // 2

    @pl.when(pl.program_id(1) == 0)
    def _():
        def fold(i, carry):
            r0 = pl.multiple_of(i * NSEGS, NSEGS)
            tiles = [u_ref[pl.ds(pl.multiple_of((s * seglen + i) * CHUNK, CHUNK), CHUNK), :] for s in range(NSEGS)]
            d_ref[0, pl.ds(r0, NSEGS), :] = jnp.concatenate(tiles, axis=0).reshape(NSEGS, CHUNK * w).astype(BF16)
            return carry

        lax.fori_loop(0, seglen, fold, 0)
        d_ref[0, nlat:nlat + ncc, :] = uc_ref[...].reshape(ncc, CHUNK * w).astype(BF16)

    wz_ref[...] = jnp.zeros(wz_ref.shape, BF16)
    first = (lax.broadcasted_iota(jnp.int32, (nh, 4 * w), 1) % w) < ns
    for pb in range(PAIRS):
        pair = pl.program_id(1) * PAIRS + pb
        lanes = slice(pb * w, (pb + 1) * w)
        for i in range(CHUNK):
            ez = jnp.concatenate([e_re_ref[CHUNK - 1 - i, 0, :, lanes], e_im_ref[CHUNK - 1 - i, 0, :, lanes],
                                  e_re_ref[i, 1, :, lanes], e_im_ref[i, 1, :, lanes]], axis=1).astype(BF16)
            both = jnp.concatenate([jnp.where(first, ez, jnp.zeros_like(ez)),
                                    jnp.where(first, jnp.zeros_like(ez), ez)], axis=0)
            r0 = pl.multiple_of(i * w + pair * 2 * nh, 2 * nh)
            wz_ref[pl.ds(r0, 2 * nh), pb * 4 * w:(pb + 1) * 4 * w] = both

    for pb in range(PAIRS):
        zz = _dot(d_ref[0], wz_ref[:, pb * 4 * w:(pb + 1) * 4 * w])
        for c in range(4):
            z_ref[pb * 4 + c] = zz[:, c * w:(c + 1) * w]

    mu = [[mu_ref[0, pb, 0, c:c + 1, :] for c in range(4)] for pb in range(PAIRS)]
    mun = [[mu_ref[0, pb, 1, c:c + 1, :] for c in range(4)] for pb in range(PAIRS)]

    pre = []
    for pb in range(PAIRS):
        zc = [z_ref[pb * 4 + c, nlat:nlat + ncc, :] for c in range(4)]
        sf = (jnp.zeros((1, w), F32), jnp.zeros((1, w), F32))
        sb = (jnp.zeros((1, w), F32), jnp.zeros((1, w), F32))
        for j in range(ncc):
            jb = ncc - 1 - j
            xs_ref[pb * 4 + 0, nlat + j:nlat + j + 1, :] = sf[0]
            xs_ref[pb * 4 + 1, nlat + j:nlat + j + 1, :] = sf[1]
            xs_ref[pb * 4 + 2, nlat + jb:nlat + jb + 1, :] = sb[0]
            xs_ref[pb * 4 + 3, nlat + jb:nlat + jb + 1, :] = sb[1]
            sf = _cmul_add(mu[pb][0], mu[pb][1], sf[0], sf[1], zc[0][j:j + 1], zc[1][j:j + 1])
            sb = _cmul_add(mu[pb][2], mu[pb][3], sb[0], sb[1], zc[2][jb:jb + 1], zc[3][jb:jb + 1])
        pre.append((sf, sb))

    mub = [[jnp.broadcast_to(m, (SUBLANE, w)) for m in mu[pb]] for pb in range(PAIRS)]

    def rows(i, v):
        return pl.ds(pl.multiple_of(i * NSEGS + v * SUBLANE, SUBLANE), SUBLANE)

    def step(i, st):
        ib = seglen - 1 - i
        new = []
        for pb in range(PAIRS):
            for v in range(nv):
                k = (pb * nv + v) * 4
                m = mub[pb]
                fr, fi = _cmul_add(m[0], m[1], st[k], st[k + 1],
                                   z_ref[pb * 4 + 0, rows(i, v), :], z_ref[pb * 4 + 1, rows(i, v), :])
                br, bi = _cmul_add(m[2], m[3], st[k + 2], st[k + 3],
                                   z_ref[pb * 4 + 2, rows(ib, v), :], z_ref[pb * 4 + 3, rows(ib, v), :])
                new += [fr, fi, br, bi]
        return tuple(new)

    zero = jnp.zeros((SUBLANE, w), F32)
    fin = lax.fori_loop(0, seglen, step, (zero,) * (PAIRS * nv * 4))

    carry = []
    for pb in range(PAIRS):
        cf, cb = pre[pb]
        rows_f, rows_b = [None] * NSEGS, [None] * NSEGS
        for s in range(NSEGS):
            sr = NSEGS - 1 - s
            rows_f[s] = cf
            rows_b[sr] = cb
            kf = (pb * nv + s // SUBLANE) * 4
            kb = (pb * nv + sr // SUBLANE) * 4
            sl, srl = s % SUBLANE, sr % SUBLANE
            cf = _cmul_add(mun[pb][0], mun[pb][1], cf[0], cf[1], fin[kf][sl:sl + 1, :], fin[kf + 1][sl:sl + 1, :])
            cb = _cmul_add(mun[pb][2], mun[pb][3], cb[0], cb[1],
                           fin[kb + 2][srl:srl + 1, :], fin[kb + 3][srl:srl + 1, :])
        for v in range(nv):
            seg = slice(v * SUBLANE, (v + 1) * SUBLANE)
            carry += [jnp.concatenate([r[0] for r in rows_f[seg]], axis=0),
                      jnp.concatenate([r[1] for r in rows_f[seg]], axis=0),
                      jnp.concatenate([r[0] for r in rows_b[seg]], axis=0),
                      jnp.concatenate([r[1] for r in rows_b[seg]], axis=0)]

    def step2(i, st):
        ib = seglen - 1 - i
        for pb in range(PAIRS):
            for v in range(nv):
                k = (pb * nv + v) * 4
                xs_ref[pb * 4 + 0, rows(i, v), :] = st[k]
                xs_ref[pb * 4 + 1, rows(i, v), :] = st[k + 1]
                xs_ref[pb * 4 + 2, rows(ib, v), :] = st[k + 2]
                xs_ref[pb * 4 + 3, rows(ib, v), :] = st[k + 3]
        return step(i, st)

    lax.fori_loop(0, seglen, step2, tuple(carry))
    for k in range(PAIRS * 4):
        xp_ref[0, :, k * w:(k + 1) * w] = xs_ref[k].astype(BF16)


def _s5_readout_kernel(d_ref, xp_ref, kr_ref, syt_ref, rm_ref, dsk_ref, y_ref, yc_ref, bt_ref, wy_ref, yb_ref,
                       sy_ref, *, seglen):
    w = LANE
    cw = 4 * w
    fw = CHUNK * w
    ncg = fw // cw

    @pl.when(pl.program_id(1) == 0)
    def _():
        for i in range(CHUNK):
            for j in range(CHUNK):
                bt_ref[i * w:(i + 1) * w, j * w:(j + 1) * w] = kr_ref[0, j - i + CHUNK - 1]
        for b in range(syt_ref.shape[3] // w):
            for dc in range(4):
                r0 = (b * 4 + dc) * w
                sy_ref[r0:r0 + w, :] = syt_ref[dc // 2, dc % 2, :, b * w:(b + 1) * w].T.astype(BF16)
        nrow = sy_ref.shape[0]
        nh = rm_ref.shape[0] // CHUNK
        row = lax.broadcasted_iota(jnp.int32, (nrow, cw), 0)
        row_g = 2 * (row // cw) + (row % w) // (w // 2)
        col_g = (lax.broadcasted_iota(jnp.int32, (nrow, cw), 1) % w) // nh
        same = row_g == col_g
        for cg in range(ncg):
            cols = slice(cg * cw, (cg + 1) * cw)
            wy_ref[:, cols] = jnp.where(same, _dot(sy_ref[...], rm_ref[:, cols]), 0.0).astype(BF16)

    nrows = d_ref.shape[1]
    for cg in range(ncg):
        cols = slice(cg * cw, (cg + 1) * cw)
        yb_ref[:, cols] = (_dot(d_ref[0], bt_ref[:, cols]) + _dot(xp_ref[0], wy_ref[:, cols])
                           + d_ref[0, :, cols].astype(F32) * dsk_ref[0, :, cols])

    ngrp = nrows // NSEGS

    def unfold(j, carry):
        gi = pl.program_id(1) * ngrp + j
        r0 = pl.multiple_of(j * NSEGS, NSEGS)
        tok = yb_ref[pl.ds(r0, NSEGS), :].reshape(NSEGS * CHUNK, w)

        @pl.when(gi < seglen)
        def _():
            for s in range(NSEGS):
                t0 = pl.multiple_of((s * seglen + gi) * CHUNK, CHUNK)
                y_ref[pl.ds(t0, CHUNK), :] = tok[s * CHUNK:(s + 1) * CHUNK]

        @pl.when(gi >= seglen)
        def _():
            t0 = pl.multiple_of((gi - seglen) * NSEGS * CHUNK, NSEGS * CHUNK)
            yc_ref[pl.ds(t0, NSEGS * CHUNK), :] = tok

        return carry

    lax.fori_loop(0, ngrp, unfold, 0)


def _s5_tables(lam_re, lam_im, log_dt, b_re, b_im, c_re, c_im, d_skip, *, seglen):
    t = CHUNK
    ng, ns = lam_re.shape[1], lam_re.shape[2]
    nh = b_re.shape[-1]
    gp = ng * ns
    g8 = LANE // nh
    na = ng // g8
    nb = g8 // 2
    assert 2 * ns == LANE and nb % PAIRS == 0
    dt = jnp.exp(log_dt)[..., None]
    a, b = (lam_re * dt).reshape(2, gp), (lam_im * dt).reshape(2, gp)

    def lam_pow(m):
        mm = jnp.asarray(m, F32).reshape(-1, 1, 1)
        mag = jnp.exp(a[None] * mm)
        return mag * jnp.cos(b[None] * mm), mag * jnp.sin(b[None] * mm)

    pr, pi = lam_pow(np.arange(t + 1))
    lr, li = lam_re.reshape(2, gp), lam_im.reshape(2, gp)
    nr, ni = pr[1] - 1.0, pi[1]
    den = lr * lr + li * li
    qr, qi = (nr * lr + ni * li) / den, (ni * lr - nr * li) / den
    bt_re = b_re.transpose(0, 3, 1, 2).reshape(2, nh, gp)
    bt_im = b_im.transpose(0, 3, 1, 2).reshape(2, nh, gp)
    bb_re = qr[:, None] * bt_re - qi[:, None] * bt_im
    bb_im = qr[:, None] * bt_im + qi[:, None] * bt_re
    e_re = pr[:t, :, None] * bb_re[None] - pi[:t, :, None] * bb_im[None]
    e_im = pr[:t, :, None] * bb_im[None] + pi[:t, :, None] * bb_re[None]
    ct_re = c_re.transpose(0, 2, 1, 3).reshape(2, nh, gp)
    ct_im = c_im.transpose(0, 2, 1, 3).reshape(2, nh, gp)

    sl = slice(1, t + 1)
    pw_re = jnp.stack([pr[sl, 0], pr[sl, 1][::-1]])[:, :, None]
    pw_im = jnp.stack([pi[sl, 0], pi[sl, 1][::-1]])[:, :, None]
    wy_re = ct_re[:, None] * pw_re - ct_im[:, None] * pw_im
    wy_im = ct_re[:, None] * pw_im + ct_im[:, None] * pw_re
    syt = jnp.stack([wy_re, -wy_im], axis=1).reshape(2, 2, t * nh, gp)
    mr, mi = lam_pow([t, t * seglen])
    mu = jnp.stack([mr, mi], axis=2).reshape(2, 4, na, nb, LANE).transpose(2, 3, 0, 1, 4)
    dsk = jnp.tile(d_skip.reshape(na, 1, LANE), (1, 1, t))
    return e_re, e_im, ct_re, ct_im, syt, mu, dsk


def _s5_operators(params, seglen):
    e_re, e_im, ct_re, ct_im, syt, mu, dsk = jax.vmap(functools.partial(_s5_tables, seglen=seglen))(*params)
    depth, t, _, nh, gp = e_re.shape
    na = mu.shape[1]
    tw = gp // na
    nlag = 2 * CHUNK - 1
    kr = pl.pallas_call(
        _s5_taps_kernel,
        out_shape=jax.ShapeDtypeStruct((depth, na, nlag, LANE, LANE), BF16),
        grid=(depth, na),
        in_specs=[pl.BlockSpec((None, t, 2, nh, tw), lambda l, i: (l, 0, 0, 0, i)),
                  pl.BlockSpec((None, t, 2, nh, tw), lambda l, i: (l, 0, 0, 0, i)),
                  pl.BlockSpec((None, 2, nh, tw), lambda l, i: (l, 0, 0, i)),
                  pl.BlockSpec((None, 2, nh, tw), lambda l, i: (l, 0, 0, i))],
        out_specs=pl.BlockSpec((None, 1, nlag, LANE, LANE), lambda l, i: (l, i, 0, 0, 0)),
        compiler_params=_params(("arbitrary", "arbitrary")),
        name="s5_taps",
    )(e_re, e_im, ct_re, ct_im)
    r_mat = np.kron(np.eye(CHUNK, dtype=np.float32), np.tile(np.eye(nh, dtype=np.float32), (1, LANE // nh)))
    return kr, e_re, e_im, syt, jnp.asarray(r_mat, F32).astype(BF16), mu, dsk


def _s5(u, uc, li, ops):
    kr, e_re, e_im, syt, r_mat, mu, dsk = ops
    l, d_ssm = u.shape
    lc = uc.shape[0]
    na, nb = mu.shape[1], mu.shape[2]
    fw = CHUNK * LANE
    nlat, ncc = l // CHUNK, lc // CHUNK
    nc = nlat + ncc
    seglen = nlat // NSEGS
    sw = PAIRS * 4 * LANE
    assert seglen * NSEGS == nlat and ncc * CHUNK == lc
    d, xp = pl.pallas_call(
        functools.partial(_s5_state_kernel, seglen=seglen, ncc=ncc),
        out_shape=(jax.ShapeDtypeStruct((na, nc, fw), BF16), jax.ShapeDtypeStruct((na, nc, nb * 4 * LANE), BF16)),
        grid=(na, nb // PAIRS),
        in_specs=[
            pl.BlockSpec((l, LANE), lambda a, b: (0, a)),
            pl.BlockSpec((lc, LANE), lambda a, b: (0, a)),
            pl.BlockSpec((None,) + e_re.shape[1:4] + (PAIRS * LANE,), lambda a, b: (li, 0, 0, 0, a * (nb // PAIRS) + b)),
            pl.BlockSpec((None,) + e_im.shape[1:4] + (PAIRS * LANE,), lambda a, b: (li, 0, 0, 0, a * (nb // PAIRS) + b)),
            pl.BlockSpec((None, 1, PAIRS, 2, 4, LANE), lambda a, b: (li, a, b, 0, 0, 0)),
        ],
        out_specs=(pl.BlockSpec((1, nc, fw), lambda a, b: (a, 0, 0)),
                   pl.BlockSpec((1, nc, sw), lambda a, b: (a, 0, b))),
        scratch_shapes=[pltpu.VMEM((fw, sw), BF16),
                        pltpu.VMEM((PAIRS * 4, nc, LANE), F32), pltpu.VMEM((PAIRS * 4, nc, LANE), F32)],
        compiler_params=_params(("arbitrary", "arbitrary")),
        name="s5_state",
    )(u, uc, e_re, e_im, mu)
    rbs = max(r for r in range(NSEGS, min(nc, 512) + 1, NSEGS) if nc % r == 0)
    nrb = nc // rbs
    assert NSEGS % 16 == 0 and ncc % NSEGS == 0
    return pl.pallas_call(
        functools.partial(_s5_readout_kernel, seglen=seglen),
        out_shape=(jax.ShapeDtypeStruct((l, d_ssm), F32), jax.ShapeDtypeStruct((lc, d_ssm), F32)),
        grid=(na, nrb),
        in_specs=[
            pl.BlockSpec((1, rbs, fw), lambda a, r: (a, r, 0)),
            pl.BlockSpec((1, rbs, nb * 4 * LANE), lambda a, r: (a, r, 0)),
            pl.BlockSpec((None, 1) + kr.shape[2:], lambda a, r: (li, a, 0, 0, 0)),
            pl.BlockSpec((None,) + syt.shape[1:4] + (nb * LANE,), lambda a, r: (li, 0, 0, 0, a)),
            _const_spec(r_mat.shape),
            pl.BlockSpec((None, 1, 1, fw), lambda a, r: (li, a, 0, 0)),
        ],
        out_specs=(pl.BlockSpec((l, LANE), lambda a, r: (0, a)), pl.BlockSpec((lc, LANE), lambda a, r: (0, a))),
        scratch_shapes=[pltpu.VMEM((fw, fw), BF16), pltpu.VMEM((nb * 4 * LANE, fw), BF16),
                        pltpu.VMEM((rbs, fw), F32), pltpu.VMEM((nb * 4 * LANE, syt.shape[3]), BF16)],
        compiler_params=_params(("arbitrary", "arbitrary")),
        name="s5_readout",
    )(d, xp, kr, syt, r_mat, dsk)


def _fft1_kernel(p_ref, q_ref, m_ref, o_ref, s_ref, *, nb):
    for n in range(nb):
        rhs = jnp.concatenate([p_ref[:, n, :], q_ref[:, n, :]], axis=0).astype(BF16)
        s_ref[:, n, :] = _dot(m_ref[...], rhs)
    o_ref[...] = s_ref[...].astype(BF16)


def _fft2_kernel(yr_ref, yi_ref, ca_ref, sa_ref, cb_ref, sb_ref, o_ref, s_ref, *, kb):
    cb, sb = cb_ref[...], sb_ref[...]
    for k in range(kb):
        ca, sa = ca_ref[k:k + 1, :], sa_ref[k:k + 1, :]
        g = jnp.concatenate([ca * cb - sa * sb, sa * cb + ca * sb], axis=1).astype(BF16)
        rhs = jnp.concatenate([yr_ref[k], yi_ref[k]], axis=0)
        s_ref[:, k, :] = _dot(g, rhs)
    o_ref[...] = s_ref[...].astype(BF16)


def _fft_ctx_kernel(p_ref, q_ref, m_ref, o_ref):
    rhs = jnp.concatenate([p_ref[...], q_ref[...]], axis=0).astype(BF16)
    o_ref[...] = _dot(m_ref[...], rhs).astype(BF16)


def _fnet_tables(l, lc):
    r = math.isqrt(l)
    assert r * r == l
    k = np.arange(r)
    th = 2.0 * np.pi * (np.outer(k, k) % r) / r
    c, s = np.cos(th), np.sin(th)
    m1 = np.block([[c, -s], [-s, -c]])
    ta = 2.0 * np.pi * np.outer(k, k) / l
    scale = 1.0 / math.sqrt(l)
    tw2 = tuple(jnp.asarray(v, F32) for v in (np.cos(ta), np.sin(ta), c * scale, s * scale))
    kc = np.arange(lc)
    thc = 2.0 * np.pi * (np.outer(kc, kc) % lc) / lc
    mc = np.concatenate([np.cos(thc), -np.sin(thc)], axis=1) / math.sqrt(lc)
    return (jnp.asarray(m1, F32).astype(BF16), tw2, jnp.asarray(mc, F32).astype(BF16))


def _fnet_lat(p, q, m1, tw2):
    l, c = p.shape
    r = m1.shape[0] // 2
    assert r * r == l
    p3 = p.reshape(r, r, c)
    q3 = q.reshape(r, r, c)
    nb = 16
    y1 = pl.pallas_call(
        functools.partial(_fft1_kernel, nb=nb),
        out_shape=jax.ShapeDtypeStruct((2 * r, r, c), BF16),
        grid=(r // nb,),
        in_specs=[pl.BlockSpec((r, nb, c), lambda j: (0, j, 0)),
                  pl.BlockSpec((r, nb, c), lambda j: (0, j, 0)),
                  _const_spec(m1.shape)],
        out_specs=pl.BlockSpec((2 * r, nb, c), lambda j: (0, j, 0)),
        scratch_shapes=[pltpu.VMEM((2 * r, nb, c), F32)],
        compiler_params=_params(("arbitrary",)),
        name="fft_stage1",
    )(p3, q3, m1)
    kb = 16
    ca, sa, cb, sb = tw2
    out = pl.pallas_call(
        functools.partial(_fft2_kernel, kb=kb),
        out_shape=jax.ShapeDtypeStruct((r, r, c), BF16),
        grid=(r // kb,),
        in_specs=[pl.BlockSpec((kb, r, c), lambda j: (j, 0, 0)),
                  pl.BlockSpec((kb, r, c), lambda j: (r // kb + j, 0, 0)),
                  pl.BlockSpec((kb, r), lambda j: (j, 0)),
                  pl.BlockSpec((kb, r), lambda j: (j, 0)),
                  _const_spec(cb.shape), _const_spec(sb.shape)],
        out_specs=pl.BlockSpec((r, kb, c), lambda j: (0, j, 0)),
        scratch_shapes=[pltpu.VMEM((r, kb, c), F32)],
        compiler_params=_params(("arbitrary",)),
        name="fft_stage2",
    )(y1, y1, ca, sa, cb, sb)
    return out.reshape(l, c)


def _fnet_ctx(pc, qc, mc):
    return pl.pallas_call(
        _fft_ctx_kernel,
        out_shape=jax.ShapeDtypeStruct(pc.shape, BF16),
        grid=(1,),
        in_specs=[_whole_spec(pc.shape), _whole_spec(qc.shape), _const_spec(mc.shape)],
        out_specs=_whole_spec(pc.shape),
        compiler_params=_params(("arbitrary",)),
        name="fft_ctx",
    )(pc, qc, mc)


def _pos_tables(l, d):
    quarter = d // 4
    omega = 1.0 / (POS_BASE ** (jnp.arange(quarter, dtype=F32) / quarter))

    def enc(pv):
        ang = pv[:, None] * omega[None, :]
        return jnp.concatenate([jnp.sin(ang), jnp.cos(ang)], axis=-1)

    renc = enc(jnp.arange(l // GRID_W, dtype=F32))
    cenc = jnp.tile(enc(jnp.arange(GRID_W, dtype=F32)), (TM // GRID_W, 1))
    return renc, cenc


def kernel(x, c, ctx, c_ctx, w_ada, b_ada, norm_g, ffn1_gu, ffn1_down, ffn2_gu, ffn2_down, w_in, w_out,
           ssm_lam_re, ssm_lam_im, ssm_log_dt, ssm_b_re, ssm_b_im, ssm_c_re, ssm_c_im, ssm_d, w_glu, w_fmix):
    bsz, l, d = x.shape
    lc = ctx.shape[1]
    depth = w_ada.shape[0]
    d_ssm = w_glu.shape[1]
    head = w_fmix.shape[2]
    assert bsz == 1 and l % TM == 0 and TM % GRID_W == 0
    seglen = (l // CHUNK) // NSEGS

    mods = _modulation(c_ctx, c, w_ada, b_ada)
    pos = _pos_tables(l, d)
    m1, tw2, mc = _fnet_tables(l, lc)
    kc = np.arange(head)
    thc = 2.0 * np.pi * (np.outer(kc, kc) % head) / head
    cs = jnp.asarray(np.concatenate([np.cos(thc), np.sin(thc)], axis=1) / math.sqrt(head), F32).astype(BF16)
    win, wout, wglu, wfm = (a.astype(BF16) for a in (w_in, w_out, w_glu, w_fmix))
    wgu1, wd1 = ffn1_gu[0].astype(BF16), ffn1_down[0].astype(BF16)

    ops = _s5_operators((ssm_lam_re, ssm_lam_im, ssm_log_dt, ssm_b_re, ssm_b_im, ssm_c_re, ssm_c_im, ssm_d), seglen)

    xl, xc = x[0], ctx[0]
    for li in range(depth):
        last = li == depth - 1
        xl, u, p, q, xc, uc, pc, qc, wgu2, wd2 = _layer_in(xl, xc, pos if li == 0 else None, li, mods, norm_g, wgu1, wd1,
                                                           win, cs, (ffn2_gu, ffn2_down), d_ssm=d_ssm)
        ys, ysc = _s5(u, uc, li, ops)
        yf = _fnet_lat(p, q, m1, tw2)
        ctx_in = None if last else (xc, ysc, _fnet_ctx(pc, qc, mc))
        xl, xc, wgu1, wd1 = _layer_out((xl, ys, yf), ctx_in, li, mods, norm_g, wglu, wfm, wout, wgu2, wd2,
                                       (ffn1_gu, ffn1_down))
    return xl[None]
```

```python
import functools
import math

import numpy as np
import jax
import jax.numpy as jnp
from jax import lax
from jax.experimental import pallas as pl
from jax.experimental.pallas import tpu as pltpu

F32 = jnp.float32
BF16 = jnp.bfloat16

LANE = 128
SUBLANE = 8
VMEM_LIMIT = 56 * 1024 * 1024

EPS = 1e-6
GRID_W = 64
POS_BASE = 10000.0
N_MOD = 9
SSM_GROUP = 16
STATE = 64
N_FNET_HEADS = 4
CHUNK = 16
TM = 512
CTX, LAT = 0, 1


def _dot(a, b):
    return jnp.dot(a, b, preferred_element_type=F32)


def _params(sem=None):
    return pltpu.CompilerParams(dimension_semantics=sem, vmem_limit_bytes=VMEM_LIMIT)


def _const_spec(shape):
    nd = len(shape)
    return pl.BlockSpec(shape, lambda *_: (0,) * nd, pipeline_mode=pl.Buffered(1))


def _rms(x, g):
    ms = jnp.mean(x * x, axis=-1, keepdims=True)
    return x * lax.rsqrt(ms + EPS) * g


def _pre(x, g, mod_ref, who, k):
    return _rms(x, g) * (1.0 + mod_ref[who, 3 * k + 1:3 * k + 2, :]) + mod_ref[who, 3 * k:3 * k + 1, :]


def _post(x, y, g, mod_ref, who, k, weight):
    return x + (weight * mod_ref[who, 3 * k + 2:3 * k + 3, :]) * _rms(y, g)


def _on_last_step(fn):
    pl.when(pl.program_id(0) == pl.num_programs(0) - 1)(fn)


def _sigmoid(x):
    return 1.0 / (1.0 + jnp.exp(-x))


def _gelu_tanh(x):
    return 0.5 * x * (1.0 + jnp.tanh(math.sqrt(2.0 / math.pi) * (x + 0.044715 * (x * x * x))))


def _mod_kernel(cb_ref, w_ref, b_ref, o_ref, s_ref, *, d_model, tn):
    nb = tn // LANE

    @pl.when((pl.program_id(0) == 0) & (pl.program_id(1) == 0))
    def _():
        cv = cb_ref[...]
        s_ref[...] = cv * _sigmoid(cv)

    def body(t, accs):
        d0 = pl.multiple_of(t * SUBLANE, SUBLANE)
        s = [s_ref[r, pl.ds(d0, SUBLANE), :] for r in range(2)]
        new = list(accs)
        for j in range(nb):
            w = w_ref[0, pl.ds(d0, SUBLANE), j * LANE:(j + 1) * LANE]
            for r in range(2):
                new[r * nb + j] = accs[r * nb + j] + w * s[r]
        return tuple(new)

    init = tuple(jnp.zeros((SUBLANE, LANE), F32) for _ in range(2 * nb))
    accs = lax.fori_loop(0, d_model // SUBLANE, body, init, unroll=4)
    for r in range(2):
        for j in range(nb):
            o_ref[0, r:r + 1, j * LANE:(j + 1) * LANE] = (
                jnp.sum(accs[r * nb + j], axis=0, keepdims=True) + b_ref[0, :, j * LANE:(j + 1) * LANE])


def _modulation(c_ctx, c, w_ada, b_ada):
    depth, d_model, n_out = w_ada.shape
    tn = 9 * LANE
    assert n_out % tn == 0
    cb = jnp.broadcast_to(jnp.stack([c_ctx, c[0]])[:, :, None], (2, d_model, LANE))
    out = pl.pallas_call(
        functools.partial(_mod_kernel, d_model=d_model, tn=tn),
        out_shape=jax.ShapeDtypeStruct((depth, 2, n_out), F32),
        grid=(depth, n_out // tn),
        in_specs=[
            pl.BlockSpec((2, d_model, LANE), lambda l, j: (0, 0, 0)),
            pl.BlockSpec((1, d_model, tn), lambda l, j: (l, 0, j)),
            pl.BlockSpec((1, 1, tn), lambda l, j: (l, 0, j)),
        ],
        out_specs=pl.BlockSpec((1, 2, tn), lambda l, j: (l, 0, j)),
        scratch_shapes=[pltpu.VMEM((2, d_model, LANE), F32)],
        compiler_params=_params(("arbitrary", "arbitrary")),
        name="adaln_mod",
    )(cb, w_ada, b_ada.reshape(depth, 1, n_out))
    return out.reshape(depth, 2, N_MOD, d_model)


SPLIT = 2


def _row_chunks(n, parts):
    step = n // parts
    return [slice(i * step, (i + 1) * step) for i in range(parts)]


def _ffn_stage(xs, who, mod_ref, g_ref, wgu_ref, wd_ref, k, gi):
    f = wd_ref.shape[0]
    hs = [_pre(x, g_ref[gi:gi + 1, :], mod_ref, who, k).astype(BF16) for x in xs]
    gus = [_dot(h, wgu_ref[...]) for h in hs]
    acts = [(gu[:, :f] * _sigmoid(gu[:, :f]) * gu[:, f:]).astype(BF16) for gu in gus]
    ys = [_dot(a, wd_ref[...]) for a in acts]
    return [_post(x, y, g_ref[gi + 1:gi + 2, :], mod_ref, who, k, 0.5) for x, y in zip(xs, ys)]


def _inproj_stage(xs, rows, who, mod_ref, g_ref, win_ref, cs_ref, u_ref, p_ref, q_ref):
    d_ssm = u_ref.shape[1]
    head = cs_ref.shape[0]
    hs = [_pre(x, g_ref[2:3, :], mod_ref, who, 1).astype(BF16) for x in xs]
    hhs = [_dot(h, win_ref[...]) for h in hs]
    for r, hh in zip(rows, hhs):
        u_ref[r, :] = hh[:, :d_ssm]
        for n in range(N_FNET_HEADS):
            lo = d_ssm + n * head
            pq = _dot(hh[:, lo:lo + head].astype(BF16), cs_ref[...])
            p_ref[r, n * head:(n + 1) * head] = pq[:, :head]
            q_ref[r, n * head:(n + 1) * head] = pq[:, head:]


def _outproj_stage(xs, yss, yfs, who, mod_ref, g_ref, wglu_ref, wfm_ref, wout_ref):
    head = wfm_ref.shape[1]
    hs = [_gelu_tanh(ys) for ys in yss]
    gls = [_dot(h.astype(BF16), wglu_ref[...]) for h in hs]
    mixed = []
    for h, gl, yf in zip(hs, gls, yfs):
        yf = yf.astype(BF16)
        parts = [(h * _sigmoid(gl)).astype(BF16)]
        parts += [_dot(yf[:, n * head:(n + 1) * head], wfm_ref[n]).astype(BF16) for n in range(N_FNET_HEADS)]
        mixed.append(jnp.concatenate(parts, axis=1))
    ys = [_dot(m, wout_ref[...]) for m in mixed]
    return [_post(x, y, g_ref[3:4, :], mod_ref, who, 1, 1.0) for x, y in zip(xs, ys)]


def _cast_chunks(src_refs, dst_refs):
    for src_ref, dst_ref in zip(src_refs, dst_refs):
        dst_ref[...] = src_ref[...].astype(BF16)


def _layer_in_kernel(*refs, first):
    if first:
        (x_ref, xc_ref, renc_ref, cenc_ref, mod_ref, g_ref, wgu_ref, wd_ref, win_ref, cs_ref, ng_ref, nd_ref,
         o_ref, u_ref, p_ref, q_ref, oc_ref, uc_ref, pc_ref, qc_ref, ngo_ref, ndo_ref, xs_ref) = refs
        tm = x_ref.shape[0]
        half = renc_ref.shape[1]
        r0 = pl.program_id(0) * (tm // GRID_W)
        xs_ref[:, half:] = x_ref[:, half:] + cenc_ref[...]
        for q in range(tm // GRID_W):
            xs_ref[q * GRID_W:(q + 1) * GRID_W, :half] = (
                x_ref[q * GRID_W:(q + 1) * GRID_W, :half] + renc_ref[pl.ds(r0 + q, 1), :])
        src_ref = xs_ref
    else:
        (x_ref, xc_ref, mod_ref, g_ref, wgu_ref, wd_ref, win_ref, cs_ref, ng_ref, nd_ref,
         o_ref, u_ref, p_ref, q_ref, oc_ref, uc_ref, pc_ref, qc_ref, ngo_ref, ndo_ref) = refs
        src_ref = x_ref
    _cast_chunks((ng_ref, nd_ref), (ngo_ref, ndo_ref))
    rows = _row_chunks(x_ref.shape[0], SPLIT)
    x1 = _ffn_stage([src_ref[r, :] for r in rows], LAT, mod_ref, g_ref, wgu_ref, wd_ref, 0, 0)
    for r, v in zip(rows, x1):
        o_ref[r, :] = v
    _inproj_stage(x1, rows, LAT, mod_ref, g_ref, win_ref, cs_ref, u_ref, p_ref, q_ref)

    def _():
        rows_c = _row_chunks(xc_ref.shape[0], 1)
        xc1 = _ffn_stage([xc_ref[...]], CTX, mod_ref, g_ref, wgu_ref, wd_ref, 0, 0)
        oc_ref[...] = xc1[0]
        _inproj_stage(xc1, rows_c, CTX, mod_ref, g_ref, win_ref, cs_ref, uc_ref, pc_ref, qc_ref)
    _on_last_step(_)


def _layer_out_kernel(*refs, with_ctx):
    if with_ctx:
        (x_ref, ys_ref, yf_ref, xc_ref, ysc_ref, yfc_ref, mod_ref, g_ref, wglu_ref, wfm_ref, wout_ref,
         wgu_ref, wd_ref, ng_ref, nd_ref, o_ref, oc_ref, ngo_ref, ndo_ref) = refs
        _cast_chunks((ng_ref, nd_ref), (ngo_ref, ndo_ref))
    else:
        x_ref, ys_ref, yf_ref, mod_ref, g_ref, wglu_ref, wfm_ref, wout_ref, wgu_ref, wd_ref, o_ref = refs
    rows = _row_chunks(x_ref.shape[0], SPLIT)
    x2 = _outproj_stage([x_ref[r, :] for r in rows], [ys_ref[r, :] for r in rows], [yf_ref[r, :] for r in rows],
                        LAT, mod_ref, g_ref, wglu_ref, wfm_ref, wout_ref)
    x3 = _ffn_stage(x2, LAT, mod_ref, g_ref, wgu_ref, wd_ref, 2, 4)
    for r, v in zip(rows, x3):
        o_ref[r, :] = v
    if with_ctx:
        def _():
            xc2 = _outproj_stage([xc_ref[...]], [ysc_ref[...]], [yfc_ref[...]], CTX, mod_ref, g_ref, wglu_ref,
                                 wfm_ref, wout_ref)
            oc_ref[...] = _ffn_stage(xc2, CTX, mod_ref, g_ref, wgu_ref, wd_ref, 2, 4)[0]
        _on_last_step(_)


def _tile_spec(cols):
    return pl.BlockSpec((TM, cols), lambda i: (i, 0))


def _whole_spec(shape):
    nd = len(shape)
    return pl.BlockSpec(shape, lambda *_: (0,) * nd)


def _layer_spec(arr, li):
    nd = arr.ndim - 1
    return pl.BlockSpec((None,) + arr.shape[1:], lambda *_: (li,) + (0,) * nd, pipeline_mode=pl.Buffered(1))


def _cast_specs(arr, li, nt):
    rows, cols = arr.shape[1:]
    k = 1
    while (rows * k) % nt or (rows * k // nt) % 16:
        k *= 2
    rc = rows * k // nt
    assert rows % rc == 0 and nt % k == 0
    return (pl.BlockSpec((None, rc, cols), lambda i: (li, i // k, 0)), pl.BlockSpec((rc, cols), lambda i: (i // k, 0)),
            jax.ShapeDtypeStruct((rows, cols), BF16))


def _layer_in(xl, xc, pos, li, mods, gains, wgu, wd, win, cs, nxt, *, d_ssm):
    l, d = xl.shape
    lc = xc.shape[0]
    first = pos is not None
    d_f = win.shape[2] - d_ssm
    widths = (d, d_ssm, d_f, d_f)
    stacked = [mods, gains]
    nt = l // TM
    casts = [_cast_specs(a, li, nt) for a in nxt]
    return pl.pallas_call(
        functools.partial(_layer_in_kernel, first=first),
        out_shape=[jax.ShapeDtypeStruct((l, wd_), F32) for wd_ in widths]
        + [jax.ShapeDtypeStruct((lc, wd_), F32) for wd_ in widths] + [c[2] for c in casts],
        grid=(nt,),
        in_specs=([_tile_spec(d), _whole_spec(xc.shape)] + ([_const_spec(a.shape) for a in pos] if first else [])
                  + [_layer_spec(a, li) for a in stacked] + [_const_spec(wgu.shape), _const_spec(wd.shape)]
                  + [_layer_spec(win, li), _const_spec(cs.shape)] + [c[0] for c in casts]),
        out_specs=([_tile_spec(wd_) for wd_ in widths] + [_whole_spec((lc, wd_)) for wd_ in widths]
                   + [c[1] for c in casts]),
        scratch_shapes=[pltpu.VMEM((TM, d), F32)] if first else [],
        compiler_params=_params(("arbitrary",)),
        name="layer_in",
    )(xl, xc, *(pos if first else ()), *stacked, wgu, wd, win, cs, *nxt)


def _layer_out(lat, ctx, li, mods, gains, wglu, wfm, wout, wgu, wd, nxt):
    xl, ys, yf = lat
    l, d = xl.shape
    with_ctx = ctx is not None
    ctx = list(ctx) if with_ctx else []
    stacked = [mods, gains, wglu, wfm, wout]
    nt = l // TM
    casts = [_cast_specs(a, li + 1, nt) for a in nxt] if with_ctx else []
    outs = pl.pallas_call(
        functools.partial(_layer_out_kernel, with_ctx=with_ctx),
        out_shape=([jax.ShapeDtypeStruct(xl.shape, F32)] + ([jax.ShapeDtypeStruct(ctx[0].shape, F32)] if with_ctx else [])
                   + [c[2] for c in casts]),
        grid=(nt,),
        in_specs=([_tile_spec(d), _tile_spec(ys.shape[1]), _tile_spec(yf.shape[1])]
                  + [_whole_spec(a.shape) for a in ctx] + [_layer_spec(a, li) for a in stacked]
                  + [_const_spec(wgu.shape), _const_spec(wd.shape)] + [c[0] for c in casts]),
        out_specs=([_tile_spec(d)] + ([_whole_spec(ctx[0].shape)] if with_ctx else []) + [c[1] for c in casts]),
        compiler_params=_params(("arbitrary",)),
        name="layer_out",
    )(xl, ys, yf, *ctx, *stacked, wgu, wd, *(nxt if with_ctx else ()))
    return tuple(outs) if with_ctx else (outs[0], None, None, None)


NSEGS = 16
PAIRS = 2


def _cmul_add(ar, ai, xr, xi, zr, zi):
    return ar * xr - ai * xi + zr, ar * xi + ai * xr + zi


def _dot_nt(a, b):
    return lax.dot_general(a, b, (((1,), (1,)), ((), ())), precision=lax.Precision.HIGHEST,
                           preferred_element_type=F32)


def _s5_taps_kernel(e_re_ref, e_im_ref, ct_re_ref, ct_im_ref, kr_ref):
    t = CHUNK
    nh = ct_re_ref.shape[1]
    g8 = LANE // nh
    gw = e_re_ref.shape[3] // g8
    same = (lax.broadcasted_iota(jnp.int32, (LANE, g8 * gw), 0) // nh
            == lax.broadcasted_iota(jnp.int32, (LANE, g8 * gw), 1) // gw)
    taps = []
    for d in range(2):
        cre = jnp.where(same, jnp.concatenate([ct_re_ref[d]] * g8, axis=0), 0.0)
        cim = jnp.where(same, jnp.concatenate([ct_im_ref[d]] * g8, axis=0), 0.0)
        er = jnp.concatenate([e_re_ref[m, d] for m in range(t)], axis=0)
        ei = jnp.concatenate([e_im_ref[m, d] for m in range(t)], axis=0)
        taps.append(_dot_nt(er, cre) - _dot_nt(ei, cim))
    blk = (lax.broadcasted_iota(jnp.int32, (LANE, LANE), 0) // nh
           == lax.broadcasted_iota(jnp.int32, (LANE, LANE), 1) // nh)
    for lag in range(-(t - 1), t):
        if lag > 0:
            src = taps[0][lag * nh:(lag + 1) * nh]
        elif lag < 0:
            src = taps[1][-lag * nh:(1 - lag) * nh]
        else:
            src = taps[0][:nh] + taps[1][:nh]
        kr_ref[0, lag + t - 1] = jnp.where(blk, jnp.concatenate([src] * g8, axis=0), 0.0).astype(BF16)


def _s5_state_kernel(u_ref, uc_ref, e_re_ref, e_im_ref, mu_ref, d_ref, xp_ref, wz_ref, z_ref, xs_ref, *, seglen, ncc):
    w = LANE
    nc = d_ref.shape[1]
    nlat = NSEGS * seglen
    nv = NSEGS // SUBLANE
    nh = e_re_ref.shape[2]
    ns = w // 2

    @pl.when(pl.program_id(1) == 0)
    def _():
        def fold(i, carry):
            r0 = pl.multiple_of(i * NSEGS, NSEGS)
            tiles = [u_ref[pl.ds(pl.multiple_of((s * seglen + i) * CHUNK, CHUNK), CHUNK), :] for s in range(NSEGS)]
            d_ref[0, pl.ds(r0, NSEGS), :] = jnp.concatenate(tiles, axis=0).reshape(NSEGS, CHUNK * w).astype(BF16)
            return carry

        lax.fori_loop(0, seglen, fold, 0)
        d_ref[0, nlat:nlat + ncc, :] = uc_ref[...].reshape(ncc, CHUNK * w).astype(BF16)

    wz_ref[...] = jnp.zeros(wz_ref.shape, BF16)
    first = (lax.broadcasted_iota(jnp.int32, (nh, 4 * w), 1) % w) < ns
    for pb in range(PAIRS):
        pair = pl.program_id(1) * PAIRS + pb
        lanes = slice(pb * w, (pb + 1) * w)
        for i in range(CHUNK):
            ez = jnp.concatenate([e_re_ref[CHUNK - 1 - i, 0, :, lanes], e_im_ref[CHUNK - 1 - i, 0, :, lanes],
                                  e_re_ref[i, 1, :, lanes], e_im_ref[i, 1, :, lanes]], axis=1).astype(BF16)
            both = jnp.concatenate([jnp.where(first, ez, jnp.zeros_like(ez)),
                                    jnp.where(first, jnp.zeros_like(ez), ez)], axis=0)
            r0 = pl.multiple_of(i * w + pair * 2 * nh, 2 * nh)
            wz_ref[pl.ds(r0, 2 * nh), pb * 4 * w:(pb + 1) * 4 * w] = both

    for pb in range(PAIRS):
        zz = _dot(d_ref[0], wz_ref[:, pb * 4 * w:(pb + 1) * 4 * w])
        for c in range(4):
            z_ref[pb * 4 + c] = zz[:, c * w:(c + 1) * w]

    mu = [[mu_ref[0, pb, 0, c:c + 1, :] for c in range(4)] for pb in range(PAIRS)]
    mun = [[mu_ref[0, pb, 1, c:c + 1, :] for c in range(4)] for pb in range(PAIRS)]

    pre = []
    for pb in range(PAIRS):
        zc = [z_ref[pb * 4 + c, nlat:nlat + ncc, :] for c in range(4)]
        sf = (jnp.zeros((1, w), F32), jnp.zeros((1, w), F32))
        sb = (jnp.zeros((1, w), F32), jnp.zeros((1, w), F32))
        for j in range(ncc):
            jb = ncc - 1 - j
            xs_ref[pb * 4 + 0, nlat + j:nlat + j + 1, :] = sf[0]
            xs_ref[pb * 4 + 1, nlat + j:nlat + j + 1, :] = sf[1]
            xs_ref[pb * 4 + 2, nlat + jb:nlat + jb + 1, :] = sb[0]
            xs_ref[pb * 4 + 3, nlat + jb:nlat + jb + 1, :] = sb[1]
            sf = _cmul_add(mu[pb][0], mu[pb][1], sf[0], sf[1], zc[0][j:j + 1], zc[1][j:j + 1])
            sb = _cmul_add(mu[pb][2], mu[pb][3], sb[0], sb[1], zc[2][jb:jb + 1], zc[3][jb:jb + 1])
        pre.append((sf, sb))

    mub = [[jnp.broadcast_to(m, (SUBLANE, w)) for m in mu[pb]] for pb in range(PAIRS)]

    def rows(i, v):
        return pl.ds(pl.multiple_of(i * NSEGS + v * SUBLANE, SUBLANE), SUBLANE)

    def step(i, st):
        ib = seglen - 1 - i
        new = []
        for pb in range(PAIRS):
            for v in range(nv):
                k = (pb * nv + v) * 4
                m = mub[pb]
                fr, fi = _cmul_add(m[0], m[1], st[k], st[k + 1],
                                   z_ref[pb * 4 + 0, rows(i, v), :], z_ref[pb * 4 + 1, rows(i, v), :])
                br, bi = _cmul_add(m[2], m[3], st[k + 2], st[k + 3],
                                   z_ref[pb * 4 + 2, rows(ib, v), :], z_ref[pb * 4 + 3, rows(ib, v), :])
                new += [fr, fi, br, bi]
        return tuple(new)

    zero = jnp.zeros((SUBLANE, w), F32)
    fin = lax.fori_loop(0, seglen, step, (zero,) * (PAIRS * nv * 4))

    carry = []
    for pb in range(PAIRS):
        cf, cb = pre[pb]
        rows_f, rows_b = [None] * NSEGS, [None] * NSEGS
        for s in range(NSEGS):
            sr = NSEGS - 1 - s
            rows_f[s] = cf
            rows_b[sr] = cb
            kf = (pb * nv + s // SUBLANE) * 4
            kb = (pb * nv + sr // SUBLANE) * 4
            sl, srl = s % SUBLANE, sr % SUBLANE
            cf = _cmul_add(mun[pb][0], mun[pb][1], cf[0], cf[1], fin[kf][sl:sl + 1, :], fin[kf + 1][sl:sl + 1, :])
            cb = _cmul_add(mun[pb][2], mun[pb][3], cb[0], cb[1],
                           fin[kb + 2][srl:srl + 1, :], fin[kb + 3][srl:srl + 1, :])
        for v in range(nv):
            seg = slice(v * SUBLANE, (v + 1) * SUBLANE)
            carry += [jnp.concatenate([r[0] for r in rows_f[seg]], axis=0),
                      jnp.concatenate([r[1] for r in rows_f[seg]], axis=0),
                      jnp.concatenate([r[0] for r in rows_b[seg]], axis=0),
                      jnp.concatenate([r[1] for r in rows_b[seg]], axis=0)]

    def step2(i, st):
        ib = seglen - 1 - i
        for pb in range(PAIRS):
            for v in range(nv):
                k = (pb * nv + v) * 4
                xs_ref[pb * 4 + 0, rows(i, v), :] = st[k]
                xs_ref[pb * 4 + 1, rows(i, v), :] = st[k + 1]
                xs_ref[pb * 4 + 2, rows(ib, v), :] = st[k + 2]
                xs_ref[pb * 4 + 3, rows(ib, v), :] = st[k + 3]
        return step(i, st)

    lax.fori_loop(0, seglen, step2, tuple(carry))
    for k in range(PAIRS * 4):
        xp_ref[0, :, k * w:(k + 1) * w] = xs_ref[k].astype(BF16)


def _s5_readout_kernel(d_ref, xp_ref, kr_ref, syt_ref, rm_ref, dsk_ref, y_ref, yc_ref, bt_ref, wy_ref, yb_ref,
                       sy_ref, *, seglen):
    w = LANE
    cw = 4 * w
    fw = CHUNK * w
    ncg = fw // cw

    @pl.when(pl.program_id(1) == 0)
    def _():
        for i in range(CHUNK):
            for j in range(CHUNK):
                bt_ref[i * w:(i + 1) * w, j * w:(j + 1) * w] = kr_ref[0, j - i + CHUNK - 1]
        for b in range(syt_ref.shape[3] // w):
            for dc in range(4):
                r0 = (b * 4 + dc) * w
                sy_ref[r0:r0 + w, :] = syt_ref[dc // 2, dc % 2, :, b * w:(b + 1) * w].T.astype(BF16)
        nrow = sy_ref.shape[0]
        nh = rm_ref.shape[0] // CHUNK
        row = lax.broadcasted_iota(jnp.int32, (nrow, cw), 0)
        row_g = 2 * (row // cw) + (row % w) // (w // 2)
        col_g = (lax.broadcasted_iota(jnp.int32, (nrow, cw), 1) % w) // nh
        same = row_g == col_g
        for cg in range(ncg):
            cols = slice(cg * cw, (cg + 1) * cw)
            wy_ref[:, cols] = jnp.where(same, _dot(sy_ref[...], rm_ref[:, cols]), 0.0).astype(BF16)

    nrows = d_ref.shape[1]
    for cg in range(ncg):
        cols = slice(cg * cw, (cg + 1) * cw)
        yb_ref[:, cols] = (_dot(d_ref[0], bt_ref[:, cols]) + _dot(xp_ref[0], wy_ref[:, cols])
                           + d_ref[0, :, cols].astype(F32) * dsk_ref[0, :, cols])

    ngrp = nrows // NSEGS

    def unfold(j, carry):
        gi = pl.program_id(1) * ngrp + j
        r0 = pl.multiple_of(j * NSEGS, NSEGS)
        tok = yb_ref[pl.ds(r0, NSEGS), :].reshape(NSEGS * CHUNK, w)

        @pl.when(gi < seglen)
        def _():
            for s in range(NSEGS):
                t0 = pl.multiple_of((s * seglen + gi) * CHUNK, CHUNK)
                y_ref[pl.ds(t0, CHUNK), :] = tok[s * CHUNK:(s + 1) * CHUNK]

        @pl.when(gi >= seglen)
        def _():
            t0 = pl.multiple_of((gi - seglen) * NSEGS * CHUNK, NSEGS * CHUNK)
            yc_ref[pl.ds(t0, NSEGS * CHUNK), :] = tok

        return carry

    lax.fori_loop(0, ngrp, unfold, 0)


def _s5_tables(lam_re, lam_im, log_dt, b_re, b_im, c_re, c_im, d_skip, *, seglen):
    t = CHUNK
    ng, ns = lam_re.shape[1], lam_re.shape[2]
    nh = b_re.shape[-1]
    gp = ng * ns
    g8 = LANE // nh
    na = ng // g8
    nb = g8 // 2
    assert 2 * ns == LANE and nb % PAIRS == 0
    dt = jnp.exp(log_dt)[..., None]
    a, b = (lam_re * dt).reshape(2, gp), (lam_im * dt).reshape(2, gp)

    def lam_pow(m):
        mm = jnp.asarray(m, F32).reshape(-1, 1, 1)
        mag = jnp.exp(a[None] * mm)
        return mag * jnp.cos(b[None] * mm), mag * jnp.sin(b[None] * mm)

    pr, pi = lam_pow(np.arange(t + 1))
    lr, li = lam_re.reshape(2, gp), lam_im.reshape(2, gp)
    nr, ni = pr[1] - 1.0, pi[1]
    den = lr * lr + li * li
    qr, qi = (nr * lr + ni * li) / den, (ni * lr - nr * li) / den
    bt_re = b_re.transpose(0, 3, 1, 2).reshape(2, nh, gp)
    bt_im = b_im.transpose(0, 3, 1, 2).reshape(2, nh, gp)
    bb_re = qr[:, None] * bt_re - qi[:, None] * bt_im
    bb_im = qr[:, None] * bt_im + qi[:, None] * bt_re
    e_re = pr[:t, :, None] * bb_re[None] - pi[:t, :, None] * bb_im[None]
    e_im = pr[:t, :, None] * bb_im[None] + pi[:t, :, None] * bb_re[None]
    ct_re = c_re.transpose(0, 2, 1, 3).reshape(2, nh, gp)
    ct_im = c_im.transpose(0, 2, 1, 3).reshape(2, nh, gp)

    sl = slice(1, t + 1)
    pw_re = jnp.stack([pr[sl, 0], pr[sl, 1][::-1]])[:, :, None]
    pw_im = jnp.stack([pi[sl, 0], pi[sl, 1][::-1]])[:, :, None]
    wy_re = ct_re[:, None] * pw_re - ct_im[:, None] * pw_im
    wy_im = ct_re[:, None] * pw_im + ct_im[:, None] * pw_re
    syt = jnp.stack([wy_re, -wy_im], axis=1).reshape(2, 2, t * nh, gp)
    mr, mi = lam_pow([t, t * seglen])
    mu = jnp.stack([mr, mi], axis=2).reshape(2, 4, na, nb, LANE).transpose(2, 3, 0, 1, 4)
    dsk = jnp.tile(d_skip.reshape(na, 1, LANE), (1, 1, t))
    return e_re, e_im, ct_re, ct_im, syt, mu, dsk


def _s5_operators(params, seglen):
    e_re, e_im, ct_re, ct_im, syt, mu, dsk = jax.vmap(functools.partial(_s5_tables, seglen=seglen))(*params)
    depth, t, _, nh, gp = e_re.shape
    na = mu.shape[1]
    tw = gp // na
    nlag = 2 * CHUNK - 1
    kr = pl.pallas_call(
        _s5_taps_kernel,
        out_shape=jax.ShapeDtypeStruct((depth, na, nlag, LANE, LANE), BF16),
        grid=(depth, na),
        in_specs=[pl.BlockSpec((None, t, 2, nh, tw), lambda l, i: (l, 0, 0, 0, i)),
                  pl.BlockSpec((None, t, 2, nh, tw), lambda l, i: (l, 0, 0, 0, i)),
                  pl.BlockSpec((None, 2, nh, tw), lambda l, i: (l, 0, 0, i)),
                  pl.BlockSpec((None, 2, nh, tw), lambda l, i: (l, 0, 0, i))],
        out_specs=pl.BlockSpec((None, 1, nlag, LANE, LANE), lambda l, i: (l, i, 0, 0, 0)),
        compiler_params=_params(("arbitrary", "arbitrary")),
        name="s5_taps",
    )(e_re, e_im, ct_re, ct_im)
    r_mat = np.kron(np.eye(CHUNK, dtype=np.float32), np.tile(np.eye(nh, dtype=np.float32), (1, LANE // nh)))
    return kr, e_re, e_im, syt, jnp.asarray(r_mat, F32).astype(BF16), mu, dsk


def _s5(u, uc, li, ops):
    kr, e_re, e_im, syt, r_mat, mu, dsk = ops
    l, d_ssm = u.shape
    lc = uc.shape[0]
    na, nb = mu.shape[1], mu.shape[2]
    fw = CHUNK * LANE
    nlat, ncc = l // CHUNK, lc // CHUNK
    nc = nlat + ncc
    seglen = nlat // NSEGS
    sw = PAIRS * 4 * LANE
    assert seglen * NSEGS == nlat and ncc * CHUNK == lc
    d, xp = pl.pallas_call(
        functools.partial(_s5_state_kernel, seglen=seglen, ncc=ncc),
        out_shape=(jax.ShapeDtypeStruct((na, nc, fw), BF16), jax.ShapeDtypeStruct((na, nc, nb * 4 * LANE), BF16)),
        grid=(na, nb // PAIRS),
        in_specs=[
            pl.BlockSpec((l, LANE), lambda a, b: (0, a)),
            pl.BlockSpec((lc, LANE), lambda a, b: (0, a)),
            pl.BlockSpec((None,) + e_re.shape[1:4] + (PAIRS * LANE,), lambda a, b: (li, 0, 0, 0, a * (nb // PAIRS) + b)),
            pl.BlockSpec((None,) + e_im.shape[1:4] + (PAIRS * LANE,), lambda a, b: (li, 0, 0, 0, a * (nb // PAIRS) + b)),
            pl.BlockSpec((None, 1, PAIRS, 2, 4, LANE), lambda a, b: (li, a, b, 0, 0, 0)),
        ],
        out_specs=(pl.BlockSpec((1, nc, fw), lambda a, b: (a, 0, 0)),
                   pl.BlockSpec((1, nc, sw), lambda a, b: (a, 0, b))),
        scratch_shapes=[pltpu.VMEM((fw, sw), BF16),
                        pltpu.VMEM((PAIRS * 4, nc, LANE), F32), pltpu.VMEM((PAIRS * 4, nc, LANE), F32)],
        compiler_params=_params(("arbitrary", "arbitrary")),
        name="s5_state",
    )(u, uc, e_re, e_im, mu)
    rbs = max(r for r in range(NSEGS, min(nc, 512) + 1, NSEGS) if nc % r == 0)
    nrb = nc // rbs
    assert NSEGS % 16 == 0 and ncc % NSEGS == 0
    return pl.pallas_call(
        functools.partial(_s5_readout_kernel, seglen=seglen),
        out_shape=(jax.ShapeDtypeStruct((l, d_ssm), F32), jax.ShapeDtypeStruct((lc, d_ssm), F32)),
        grid=(na, nrb),
        in_specs=[
            pl.BlockSpec((1, rbs, fw), lambda a, r: (a, r, 0)),
            pl.BlockSpec((1, rbs, nb * 4 * LANE), lambda a, r: (a, r, 0)),
            pl.BlockSpec((None, 1) + kr.shape[2:], lambda a, r: (li, a, 0, 0, 0)),
            pl.BlockSpec((None,) + syt.shape[1:4] + (nb * LANE,), lambda a, r: (li, 0, 0, 0, a)),
            _const_spec(r_mat.shape),
            pl.BlockSpec((None, 1, 1, fw), lambda a, r: (li, a, 0, 0)),
        ],
        out_specs=(pl.BlockSpec((l, LANE), lambda a, r: (0, a)), pl.BlockSpec((lc, LANE), lambda a, r: (0, a))),
        scratch_shapes=[pltpu.VMEM((fw, fw), BF16), pltpu.VMEM((nb * 4 * LANE, fw), BF16),
                        pltpu.VMEM((rbs, fw), F32), pltpu.VMEM((nb * 4 * LANE, syt.shape[3]), BF16)],
        compiler_params=_params(("arbitrary", "arbitrary")),
        name="s5_readout",
    )(d, xp, kr, syt, r_mat, dsk)


def _fft1_kernel(p_ref, q_ref, m_ref, o_ref, s_ref, *, nb):
    for n in range(nb):
        rhs = jnp.concatenate([p_ref[:, n, :], q_ref[:, n, :]], axis=0).astype(BF16)
        s_ref[:, n, :] = _dot(m_ref[...], rhs)
    o_ref[...] = s_ref[...].astype(BF16)


def _fft2_kernel(yr_ref, yi_ref, ca_ref, sa_ref, cb_ref, sb_ref, o_ref, s_ref, *, kb):
    cb, sb = cb_ref[...], sb_ref[...]
    for k in range(kb):
        ca, sa = ca_ref[k:k + 1, :], sa_ref[k:k + 1, :]
        g = jnp.concatenate([ca * cb - sa * sb, sa * cb + ca * sb], axis=1).astype(BF16)
        rhs = jnp.concatenate([yr_ref[k], yi_ref[k]], axis=0)
        s_ref[:, k, :] = _dot(g, rhs)
    o_ref[...] = s_ref[...].astype(BF16)


def _fft_ctx_kernel(p_ref, q_ref, m_ref, o_ref):
    rhs = jnp.concatenate([p_ref[...], q_ref[...]], axis=0).astype(BF16)
    o_ref[...] = _dot(m_ref[...], rhs).astype(BF16)


def _fnet_tables(l, lc):
    r = math.isqrt(l)
    assert r * r == l
    k = np.arange(r)
    th = 2.0 * np.pi * (np.outer(k, k) % r) / r
    c, s = np.cos(th), np.sin(th)
    m1 = np.block([[c, -s], [-s, -c]])
    ta = 2.0 * np.pi * np.outer(k, k) / l
    scale = 1.0 / math.sqrt(l)
    tw2 = tuple(jnp.asarray(v, F32) for v in (np.cos(ta), np.sin(ta), c * scale, s * scale))
    kc = np.arange(lc)
    thc = 2.0 * np.pi * (np.outer(kc, kc) % lc) / lc
    mc = np.concatenate([np.cos(thc), -np.sin(thc)], axis=1) / math.sqrt(lc)
    return (jnp.asarray(m1, F32).astype(BF16), tw2, jnp.asarray(mc, F32).astype(BF16))


def _fnet_lat(p, q, m1, tw2):
    l, c = p.shape
    r = m1.shape[0] // 2
    assert r * r == l
    p3 = p.reshape(r, r, c)
    q3 = q.reshape(r, r, c)
    nb = 16
    y1 = pl.pallas_call(
        functools.partial(_fft1_kernel, nb=nb),
        out_shape=jax.ShapeDtypeStruct((2 * r, r, c), BF16),
        grid=(r // nb,),
        in_specs=[pl.BlockSpec((r, nb, c), lambda j: (0, j, 0)),
                  pl.BlockSpec((r, nb, c), lambda j: (0, j, 0)),
                  _const_spec(m1.shape)],
        out_specs=pl.BlockSpec((2 * r, nb, c), lambda j: (0, j, 0)),
        scratch_shapes=[pltpu.VMEM((2 * r, nb, c), F32)],
        compiler_params=_params(("arbitrary",)),
        name="fft_stage1",
    )(p3, q3, m1)
    kb = 16
    ca, sa, cb, sb = tw2
    out = pl.pallas_call(
        functools.partial(_fft2_kernel, kb=kb),
        out_shape=jax.ShapeDtypeStruct((r, r, c), BF16),
        grid=(r // kb,),
        in_specs=[pl.BlockSpec((kb, r, c), lambda j: (j, 0, 0)),
                  pl.BlockSpec((kb, r, c), lambda j: (r // kb + j, 0, 0)),
                  pl.BlockSpec((kb, r), lambda j: (j, 0)),
                  pl.BlockSpec((kb, r), lambda j: (j, 0)),
                  _const_spec(cb.shape), _const_spec(sb.shape)],
        out_specs=pl.BlockSpec((r, kb, c), lambda j: (0, j, 0)),
        scratch_shapes=[pltpu.VMEM((r, kb, c), F32)],
        compiler_params=_params(("arbitrary",)),
        name="fft_stage2",
    )(y1, y1, ca, sa, cb, sb)
    return out.reshape(l, c)


def _fnet_ctx(pc, qc, mc):
    return pl.pallas_call(
        _fft_ctx_kernel,
        out_shape=jax.ShapeDtypeStruct(pc.shape, BF16),
        grid=(1,),
        in_specs=[_whole_spec(pc.shape), _whole_spec(qc.shape), _const_spec(mc.shape)],
        out_specs=_whole_spec(pc.shape),
        compiler_params=_params(("arbitrary",)),
        name="fft_ctx",
    )(pc, qc, mc)


def _pos_tables(l, d):
    quarter = d // 4
    omega = 1.0 / (POS_BASE ** (jnp.arange(quarter, dtype=F32) / quarter))

    def enc(pv):
        ang = pv[:, None] * omega[None, :]
        return jnp.concatenate([jnp.sin(ang), jnp.cos(ang)], axis=-1)

    renc = enc(jnp.arange(l // GRID_W, dtype=F32))
    cenc = jnp.tile(enc(jnp.arange(GRID_W, dtype=F32)), (TM // GRID_W, 1))
    return renc, cenc


def kernel(x, c, ctx, c_ctx, w_ada, b_ada, norm_g, ffn1_gu, ffn1_down, ffn2_gu, ffn2_down, w_in, w_out,
           ssm_lam_re, ssm_lam_im, ssm_log_dt, ssm_b_re, ssm_b_im, ssm_c_re, ssm_c_im, ssm_d, w_glu, w_fmix):
    bsz, l, d = x.shape
    lc = ctx.shape[1]
    depth = w_ada.shape[0]
    d_ssm = w_glu.shape[1]
    head = w_fmix.shape[2]
    assert bsz == 1 and l % TM == 0 and TM % GRID_W == 0
    seglen = (l // CHUNK) // NSEGS

    mods = _modulation(c_ctx, c, w_ada, b_ada)
    pos = _pos_tables(l, d)
    m1, tw2, mc = _fnet_tables(l, lc)
    kc = np.arange(head)
    thc = 2.0 * np.pi * (np.outer(kc, kc) % head) / head
    cs = jnp.asarray(np.concatenate([np.cos(thc), np.sin(thc)], axis=1) / math.sqrt(head), F32).astype(BF16)
    win, wout, wglu, wfm = (a.astype(BF16) for a in (w_in, w_out, w_glu, w_fmix))
    wgu1, wd1 = ffn1_gu[0].astype(BF16), ffn1_down[0].astype(BF16)

    ops = _s5_operators((ssm_lam_re, ssm_lam_im, ssm_log_dt, ssm_b_re, ssm_b_im, ssm_c_re, ssm_c_im, ssm_d), seglen)

    xl, xc = x[0], ctx[0]
    for li in range(depth):
        last = li == depth - 1
        xl, u, p, q, xc, uc, pc, qc, wgu2, wd2 = _layer_in(xl, xc, pos if li == 0 else None, li, mods, norm_g, wgu1, wd1,
                                                           win, cs, (ffn2_gu, ffn2_down), d_ssm=d_ssm)
        ys, ysc = _s5(u, uc, li, ops)
        yf = _fnet_lat(p, q, m1, tw2)
        ctx_in = None if last else (xc, ysc, _fnet_ctx(pc, qc, mc))
        xl, xc, wgu1, wd1 = _layer_out((xl, ys, yf), ctx_in, li, mods, norm_g, wglu, wfm, wout, wgu2, wd2,
                                       (ffn1_gu, ffn1_down))
    return xl[None]
```

```python
import functools
import math

import numpy as np
import jax
import jax.numpy as jnp
from jax import lax
from jax.experimental import pallas as pl
from jax.experimental.pallas import tpu as pltpu

F32 = jnp.float32
BF16 = jnp.bfloat16

LANE = 128
SUBLANE = 8
VMEM_LIMIT = 56 * 1024 * 1024

EPS = 1e-6
GRID_W = 64
POS_BASE = 10000.0
N_MOD = 9
SSM_GROUP = 16
STATE = 64
N_FNET_HEADS = 4
CHUNK = 16
TM = 512
CTX, LAT = 0, 1


def _dot(a, b):
    return jnp.dot(a, b, preferred_element_type=F32)


def _params(sem=None):
    return pltpu.CompilerParams(dimension_semantics=sem, vmem_limit_bytes=VMEM_LIMIT)


def _const_spec(shape):
    nd = len(shape)
    return pl.BlockSpec(shape, lambda *_: (0,) * nd, pipeline_mode=pl.Buffered(1))


def _rms(x, g):
    ms = jnp.mean(x * x, axis=-1, keepdims=True)
    return x * lax.rsqrt(ms + EPS) * g


def _pre(x, g, mod_ref, who, k):
    return _rms(x, g) * (1.0 + mod_ref[who, 3 * k + 1:3 * k + 2, :]) + mod_ref[who, 3 * k:3 * k + 1, :]


def _post(x, y, g, mod_ref, who, k, weight):
    return x + (weight * mod_ref[who, 3 * k + 2:3 * k + 3, :]) * _rms(y, g)


def _on_last_step(fn):
    pl.when(pl.program_id(0) == pl.num_programs(0) - 1)(fn)


def _sigmoid(x):
    return 1.0 / (1.0 + jnp.exp(-x))


def _gelu_tanh(x):
    return 0.5 * x * (1.0 + jnp.tanh(math.sqrt(2.0 / math.pi) * (x + 0.044715 * (x * x * x))))


def _mod_kernel(cb_ref, w_ref, b_ref, o_ref, s_ref, *, tn):
    rows, n_out = w_ref.shape[1], w_ref.shape[2]
    nb = tn // LANE
    j = pl.program_id(1)

    @pl.when((pl.program_id(0) == 0) & (j == 0))
    def _():
        cv = cb_ref[...]
        s_ref[...] = cv * _sigmoid(cv)

    @pl.when(j == 0)
    def _():
        o_ref[0] = jnp.broadcast_to(b_ref[0], (2, n_out))

    for grp in range(n_out // tn):
        def body(t, accs):
            d0 = pl.multiple_of(t * SUBLANE, SUBLANE)
            s = [s_ref[r, pl.ds(j * rows + d0, SUBLANE), :] for r in range(2)]
            new = list(accs)
            for jj in range(nb):
                w = w_ref[0, pl.ds(d0, SUBLANE), grp * tn + jj * LANE:grp * tn + (jj + 1) * LANE]
                for r in range(2):
                    new[r * nb + jj] = accs[r * nb + jj] + w * s[r]
            return tuple(new)

        init = tuple(jnp.zeros((SUBLANE, LANE), F32) for _ in range(2 * nb))
        accs = lax.fori_loop(0, rows // SUBLANE, body, init, unroll=4)
        for r in range(2):
            for jj in range(nb):
                lanes = slice(grp * tn + jj * LANE, grp * tn + (jj + 1) * LANE)
                o_ref[0, r:r + 1, lanes] = o_ref[0, r:r + 1, lanes] + jnp.sum(accs[r * nb + jj], axis=0, keepdims=True)


def _modulation(c_ctx, c, w_ada, b_ada):
    depth, d_model, n_out = w_ada.shape
    tn = 9 * LANE
    rows = 256
    assert n_out % tn == 0 and d_model % rows == 0
    cb = jnp.broadcast_to(jnp.stack([c_ctx, c[0]])[:, :, None], (2, d_model, LANE))
    out = pl.pallas_call(
        functools.partial(_mod_kernel, tn=tn),
        out_shape=jax.ShapeDtypeStruct((depth, 2, n_out), F32),
        grid=(depth, d_model // rows),
        in_specs=[
            pl.BlockSpec((2, d_model, LANE), lambda l, j: (0, 0, 0)),
            pl.BlockSpec((1, rows, n_out), lambda l, j: (l, j, 0)),
            pl.BlockSpec((1, 1, n_out), lambda l, j: (l, 0, 0)),
        ],
        out_specs=pl.BlockSpec((1, 2, n_out), lambda l, j: (l, 0, 0)),
        scratch_shapes=[pltpu.VMEM((2, d_model, LANE), F32)],
        compiler_params=_params(("arbitrary", "arbitrary")),
        name="adaln_mod",
    )(cb, w_ada, b_ada.reshape(depth, 1, n_out))
    return out.reshape(depth, 2, N_MOD, d_model)


SPLIT = 2


def _row_chunks(n, parts):
    step = n // parts
    return [slice(i * step, (i + 1) * step) for i in range(parts)]


def _ffn_stage(xs, who, mod_ref, g_ref, wgu_ref, wd_ref, k, gi):
    f = wd_ref.shape[0]
    hs = [_pre(x, g_ref[gi:gi + 1, :], mod_ref, who, k).astype(BF16) for x in xs]
    gus = [_dot(h, wgu_ref[...]) for h in hs]
    acts = [(gu[:, :f] * _sigmoid(gu[:, :f]) * gu[:, f:]).astype(BF16) for gu in gus]
    ys = [_dot(a, wd_ref[...]) for a in acts]
    return [_post(x, y, g_ref[gi + 1:gi + 2, :], mod_ref, who, k, 0.5) for x, y in zip(xs, ys)]


def _inproj_stage(xs, rows, who, mod_ref, g_ref, win_ref, cs_ref, u_ref, p_ref, q_ref):
    d_ssm = u_ref.shape[1]
    head = cs_ref.shape[0]
    hs = [_pre(x, g_ref[2:3, :], mod_ref, who, 1).astype(BF16) for x in xs]
    hhs = [_dot(h, win_ref[...]) for h in hs]
    for r, hh in zip(rows, hhs):
        u_ref[r, :] = hh[:, :d_ssm]
        for n in range(N_FNET_HEADS):
            lo = d_ssm + n * head
            pq = _dot(hh[:, lo:lo + head].astype(BF16), cs_ref[...])
            p_ref[r, n * head:(n + 1) * head] = pq[:, :head]
            q_ref[r, n * head:(n + 1) * head] = pq[:, head:]


def _outproj_stage(xs, yss, yfs, who, mod_ref, g_ref, wglu_ref, wfm_ref, wout_ref):
    head = wfm_ref.shape[1]
    hs = [_gelu_tanh(ys) for ys in yss]
    gls = [_dot(h.astype(BF16), wglu_ref[...]) for h in hs]
    mixed = []
    for h, gl, yf in zip(hs, gls, yfs):
        yf = yf.astype(BF16)
        parts = [(h * _sigmoid(gl)).astype(BF16)]
        parts += [_dot(yf[:, n * head:(n + 1) * head], wfm_ref[n]).astype(BF16) for n in range(N_FNET_HEADS)]
        mixed.append(jnp.concatenate(parts, axis=1))
    ys = [_dot(m, wout_ref[...]) for m in mixed]
    return [_post(x, y, g_ref[3:4, :], mod_ref, who, 1, 1.0) for x, y in zip(xs, ys)]


def _cast_chunks(src_refs, dst_refs):
    for src_ref, dst_ref in zip(src_refs, dst_refs):
        dst_ref[...] = src_ref[...].astype(BF16)


def _layer_in_kernel(*refs, first):
    if first:
        (x_ref, xc_ref, renc_ref, cenc_ref, mod_ref, g_ref, wgu_ref, wd_ref, win_ref, cs_ref, ng_ref, nd_ref,
         o_ref, u_ref, p_ref, q_ref, oc_ref, uc_ref, pc_ref, qc_ref, ngo_ref, ndo_ref, xs_ref) = refs
        tm = x_ref.shape[0]
        half = renc_ref.shape[1]
        r0 = pl.program_id(0) * (tm // GRID_W)
        xs_ref[:, half:] = x_ref[:, half:] + cenc_ref[...]
        for q in range(tm // GRID_W):
            xs_ref[q * GRID_W:(q + 1) * GRID_W, :half] = (
                x_ref[q * GRID_W:(q + 1) * GRID_W, :half] + renc_ref[pl.ds(r0 + q, 1), :])
        src_ref = xs_ref
    else:
        (x_ref, xc_ref, mod_ref, g_ref, wgu_ref, wd_ref, win_ref, cs_ref, ng_ref, nd_ref,
         o_ref, u_ref, p_ref, q_ref, oc_ref, uc_ref, pc_ref, qc_ref, ngo_ref, ndo_ref) = refs
        src_ref = x_ref
    _cast_chunks((ng_ref, nd_ref), (ngo_ref, ndo_ref))
    rows = _row_chunks(x_ref.shape[0], SPLIT)
    x1 = _ffn_stage([src_ref[r, :] for r in rows], LAT, mod_ref, g_ref, wgu_ref, wd_ref, 0, 0)
    for r, v in zip(rows, x1):
        o_ref[r, :] = v
    _inproj_stage(x1, rows, LAT, mod_ref, g_ref, win_ref, cs_ref, u_ref, p_ref, q_ref)

    def _():
        rows_c = _row_chunks(xc_ref.shape[0], 1)
        xc1 = _ffn_stage([xc_ref[...]], CTX, mod_ref, g_ref, wgu_ref, wd_ref, 0, 0)
        oc_ref[...] = xc1[0]
        _inproj_stage(xc1, rows_c, CTX, mod_ref, g_ref, win_ref, cs_ref, uc_ref, pc_ref, qc_ref)
    _on_last_step(_)


def _layer_out_kernel(*refs, with_ctx):
    if with_ctx:
        (x_ref, ys_ref, yf_ref, xc_ref, ysc_ref, yfc_ref, mod_ref, g_ref, wglu_ref, wfm_ref, wout_ref,
         wgu_ref, wd_ref, ng_ref, nd_ref, o_ref, oc_ref, ngo_ref, ndo_ref) = refs
        _cast_chunks((ng_ref, nd_ref), (ngo_ref, ndo_ref))
    else:
        x_ref, ys_ref, yf_ref, mod_ref, g_ref, wglu_ref, wfm_ref, wout_ref, wgu_ref, wd_ref, o_ref = refs
    rows = _row_chunks(x_ref.shape[0], SPLIT)
    x2 = _outproj_stage([x_ref[r, :] for r in rows], [ys_ref[r, :] for r in rows], [yf_ref[r, :] for r in rows],
                        LAT, mod_ref, g_ref, wglu_ref, wfm_ref, wout_ref)
    x3 = _ffn_stage(x2, LAT, mod_ref, g_ref, wgu_ref, wd_ref, 2, 4)
    for r, v in zip(rows, x3):
        o_ref[r, :] = v
    if with_ctx:
        def _():
            xc2 = _outproj_stage([xc_ref[...]], [ysc_ref[...]], [yfc_ref[...]], CTX, mod_ref, g_ref, wglu_ref,
                                 wfm_ref, wout_ref)
            oc_ref[...] = _ffn_stage(xc2, CTX, mod_ref, g_ref, wgu_ref, wd_ref, 2, 4)[0]
        _on_last_step(_)


def _tile_spec(cols):
    return pl.BlockSpec((TM, cols), lambda i: (i, 0))


def _whole_spec(shape):
    nd = len(shape)
    return pl.BlockSpec(shape, lambda *_: (0,) * nd)


def _layer_spec(arr, li):
    nd = arr.ndim - 1
    return pl.BlockSpec((None,) + arr.shape[1:], lambda *_: (li,) + (0,) * nd, pipeline_mode=pl.Buffered(1))


def _cast_specs(arr, li, nt):
    rows, cols = arr.shape[1:]
    k = 1
    while (rows * k) % nt or (rows * k // nt) % 16:
        k *= 2
    rc = rows * k // nt
    assert rows % rc == 0 and nt % k == 0
    return (pl.BlockSpec((None, rc, cols), lambda i: (li, i // k, 0)), pl.BlockSpec((rc, cols), lambda i: (i // k, 0)),
            jax.ShapeDtypeStruct((rows, cols), BF16))


def _layer_in(xl, xc, pos, li, mods, gains, wgu, wd, win, cs, nxt, *, d_ssm):
    l, d = xl.shape
    lc = xc.shape[0]
    first = pos is not None
    d_f = win.shape[2] - d_ssm
    widths = (d, d_ssm, d_f, d_f)
    stacked = [mods, gains]
    nt = l // TM
    casts = [_cast_specs(a, li, nt) for a in nxt]
    return pl.pallas_call(
        functools.partial(_layer_in_kernel, first=first),
        out_shape=[jax.ShapeDtypeStruct((l, wd_), F32) for wd_ in widths]
        + [jax.ShapeDtypeStruct((lc, wd_), F32) for wd_ in widths] + [c[2] for c in casts],
        grid=(nt,),
        in_specs=([_tile_spec(d), _whole_spec(xc.shape)] + ([_const_spec(a.shape) for a in pos] if first else [])
                  + [_layer_spec(a, li) for a in stacked] + [_const_spec(wgu.shape), _const_spec(wd.shape)]
                  + [_layer_spec(win, li), _const_spec(cs.shape)] + [c[0] for c in casts]),
        out_specs=([_tile_spec(wd_) for wd_ in widths] + [_whole_spec((lc, wd_)) for wd_ in widths]
                   + [c[1] for c in casts]),
        scratch_shapes=[pltpu.VMEM((TM, d), F32)] if first else [],
        compiler_params=_params(("arbitrary",)),
        name="layer_in",
    )(xl, xc, *(pos if first else ()), *stacked, wgu, wd, win, cs, *nxt)


def _layer_out(lat, ctx, li, mods, gains, wglu, wfm, wout, wgu, wd, nxt):
    xl, ys, yf = lat
    l, d = xl.shape
    with_ctx = ctx is not None
    ctx = list(ctx) if with_ctx else []
    stacked = [mods, gains, wglu, wfm, wout]
    nt = l // TM
    casts = [_cast_specs(a, li + 1, nt) for a in nxt] if with_ctx else []
    outs = pl.pallas_call(
        functools.partial(_layer_out_kernel, with_ctx=with_ctx),
        out_shape=([jax.ShapeDtypeStruct(xl.shape, F32)] + ([jax.ShapeDtypeStruct(ctx[0].shape, F32)] if with_ctx else [])
                   + [c[2] for c in casts]),
        grid=(nt,),
        in_specs=([_tile_spec(d), _tile_spec(ys.shape[1]), _tile_spec(yf.shape[1])]
                  + [_whole_spec(a.shape) for a in ctx] + [_layer_spec(a, li) for a in stacked]
                  + [_const_spec(wgu.shape), _const_spec(wd.shape)] + [c[0] for c in casts]),
        out_specs=([_tile_spec(d)] + ([_whole_spec(ctx[0].shape)] if with_ctx else []) + [c[1] for c in casts]),
        compiler_params=_params(("arbitrary",)),
        name="layer_out",
    )(xl, ys, yf, *ctx, *stacked, wgu, wd, *(nxt if with_ctx else ()))
    return tuple(outs) if with_ctx else (outs[0], None, None, None)


NSEGS = 16
PAIRS = 2


def _cmul_add(ar, ai, xr, xi, zr, zi):
    return ar * xr - ai * xi + zr, ar * xi + ai * xr + zi


def _dot_nt(a, b):
    def nt(x, y):
        return lax.dot_general(x, y, (((1,), (1,)), ((), ())), preferred_element_type=F32)
    a_hi, b_hi = a.astype(BF16), b.astype(BF16)
    a_lo, b_lo = (a - a_hi.astype(F32)).astype(BF16), (b - b_hi.astype(F32)).astype(BF16)
    return nt(a_hi, b_hi) + (nt(a_hi, b_lo) + nt(a_lo, b_hi))


def _s5_taps_kernel(e_re_ref, e_im_ref, ct_re_ref, ct_im_ref, kr_ref):
    t = CHUNK
    nh = ct_re_ref.shape[1]
    g8 = LANE // nh
    gw = e_re_ref.shape[3] // g8
    same = (lax.broadcasted_iota(jnp.int32, (LANE, g8 * gw), 0) // nh
            == lax.broadcasted_iota(jnp.int32, (LANE, g8 * gw), 1) // gw)
    taps = []
    for d in range(2):
        cre = jnp.where(same, jnp.concatenate([ct_re_ref[d]] * g8, axis=0), 0.0)
        cim = jnp.where(same, jnp.concatenate([ct_im_ref[d]] * g8, axis=0), 0.0)
        er = jnp.concatenate([e_re_ref[m, d] for m in range(t)], axis=0)
        ei = jnp.concatenate([e_im_ref[m, d] for m in range(t)], axis=0)
        taps.append(_dot_nt(er, cre) - _dot_nt(ei, cim))
    blk = (lax.broadcasted_iota(jnp.int32, (LANE, LANE), 0) // nh
           == lax.broadcasted_iota(jnp.int32, (LANE, LANE), 1) // nh)
    for lag in range(-(t - 1), t):
        if lag > 0:
            src = taps[0][lag * nh:(lag + 1) * nh]
        elif lag < 0:
            src = taps[1][-lag * nh:(1 - lag) * nh]
        else:
            src = taps[0][:nh] + taps[1][:nh]
        kr_ref[0, lag + t - 1] = jnp.where(blk, jnp.concatenate([src] * g8, axis=0), 0.0).astype(BF16)


def _s5_state_kernel(u_ref, uc_ref, e_re_ref, e_im_ref, mu_ref, d_ref, xp_ref, wz_ref, z_ref, xs_ref, *, seglen, ncc):
    w = LANE
    nc = d_ref.shape[1]
    nlat = NSEGS * seglen
    nv = NSEGS // SUBLANE
    nh = e_re_ref.shape[2]
    ns = w // 2

    @pl.when(pl.program_id(1) == 0)
    def _():
        def fold(i, carry):
            r0 = pl.multiple_of(i * NSEGS, NSEGS)
            tiles = [u_ref[pl.ds(pl.multiple_of((s * seglen + i) * CHUNK, CHUNK), CHUNK), :] for s in range(NSEGS)]
            d_ref[0, pl.ds(r0, NSEGS), :] = jnp.concatenate(tiles, axis=0).reshape(NSEGS, CHUNK * w).astype(BF16)
            return carry

        lax.fori_loop(0, seglen, fold, 0)
        d_ref[0, nlat:nlat + ncc, :] = uc_ref[...].reshape(ncc, CHUNK * w).astype(BF16)

    wz_ref[...] = jnp.zeros(wz_ref.shape, BF16)
    first = (lax.broadcasted_iota(jnp.int32, (nh, 4 * w), 1) % w) < ns
    for pb in range(PAIRS):
        pair = pl.program_id(1) * PAIRS + pb
        lanes = slice(pb * w, (pb + 1) * w)
        for i in range(CHUNK):
            ez = jnp.concatenate([e_re_ref[CHUNK - 1 - i, 0, :, lanes], e_im_ref[CHUNK - 1 - i, 0, :, lanes],
                                  e_re_ref[i, 1, :, lanes], e_im_ref[i, 1, :, lanes]], axis=1).astype(BF16)
            both = jnp.concatenate([jnp.where(first, ez, jnp.zeros_like(ez)),
                                    jnp.where(first, jnp.zeros_like(ez), ez)], axis=0)
            r0 = pl.multiple_of(i * w + pair * 2 * nh, 2 * nh)
            wz_ref[pl.ds(r0, 2 * nh), pb * 4 * w:(pb + 1) * 4 * w] = both

    for pb in range(PAIRS):
        zz = _dot(d_ref[0], wz_ref[:, pb * 4 * w:(pb + 1) * 4 * w])
        for c in range(4):
            z_ref[pb * 4 + c] = zz[:, c * w:(c + 1) * w]

    mu = [[mu_ref[0, pb, 0, c:c + 1, :] for c in range(4)] for pb in range(PAIRS)]
    mun = [[mu_ref[0, pb, 1, c:c + 1, :] for c in range(4)] for pb in range(PAIRS)]

    pre = []
    for pb in range(PAIRS):
        zc = [z_ref[pb * 4 + c, nlat:nlat + ncc, :] for c in range(4)]
        sf = (jnp.zeros((1, w), F32), jnp.zeros((1, w), F32))
        sb = (jnp.zeros((1, w), F32), jnp.zeros((1, w), F32))
        for j in range(ncc):
            jb = ncc - 1 - j
            xs_ref[pb * 4 + 0, nlat + j:nlat + j + 1, :] = sf[0]
            xs_ref[pb * 4 + 1, nlat + j:nlat + j + 1, :] = sf[1]
            xs_ref[pb * 4 + 2, nlat + jb:nlat + jb + 1, :] = sb[0]
            xs_ref[pb * 4 + 3, nlat + jb:nlat + jb + 1, :] = sb[1]
            sf = _cmul_add(mu[pb][0], mu[pb][1], sf[0], sf[1], zc[0][j:j + 1], zc[1][j:j + 1])
            sb = _cmul_add(mu[pb][2], mu[pb][3], sb[0], sb[1], zc[2][jb:jb + 1], zc[3][jb:jb + 1])
        pre.append((sf, sb))

    mub = [[jnp.broadcast_to(m, (SUBLANE, w)) for m in mu[pb]] for pb in range(PAIRS)]

    def rows(i, v):
        return pl.ds(pl.multiple_of(i * NSEGS + v * SUBLANE, SUBLANE), SUBLANE)

    def step(i, st):
        ib = seglen - 1 - i
        new = []
        for pb in range(PAIRS):
            for v in range(nv):
                k = (pb * nv + v) * 4
                m = mub[pb]
                fr, fi = _cmul_add(m[0], m[1], st[k], st[k + 1],
                                   z_ref[pb * 4 + 0, rows(i, v), :], z_ref[pb * 4 + 1, rows(i, v), :])
                br, bi = _cmul_add(m[2], m[3], st[k + 2], st[k + 3],
                                   z_ref[pb * 4 + 2, rows(ib, v), :], z_ref[pb * 4 + 3, rows(ib, v), :])
                new += [fr, fi, br, bi]
        return tuple(new)

    zero = jnp.zeros((SUBLANE, w), F32)
    fin = lax.fori_loop(0, seglen, step, (zero,) * (PAIRS * nv * 4))

    carry = []
    for pb in range(PAIRS):
        cf, cb = pre[pb]
        rows_f, rows_b = [None] * NSEGS, [None] * NSEGS
        for s in range(NSEGS):
            sr = NSEGS - 1 - s
            rows_f[s] = cf
            rows_b[sr] = cb
            kf = (pb * nv + s // SUBLANE) * 4
            kb = (pb * nv + sr // SUBLANE) * 4
            sl, srl = s % SUBLANE, sr % SUBLANE
            cf = _cmul_add(mun[pb][0], mun[pb][1], cf[0], cf[1], fin[kf][sl:sl + 1, :], fin[kf + 1][sl:sl + 1, :])
            cb = _cmul_add(mun[pb][2], mun[pb][3], cb[0], cb[1],
                           fin[kb + 2][srl:srl + 1, :], fin[kb + 3][srl:srl + 1, :])
        for v in range(nv):
            seg = slice(v * SUBLANE, (v + 1) * SUBLANE)
            carry += [jnp.concatenate([r[0] for r in rows_f[seg]], axis=0),
                      jnp.concatenate([r[1] for r in rows_f[seg]], axis=0),
                      jnp.concatenate([r[0] for r in rows_b[seg]], axis=0),
                      jnp.concatenate([r[1] for r in rows_b[seg]], axis=0)]

    def step2(i, st):
        ib = seglen - 1 - i
        for pb in range(PAIRS):
            for v in range(nv):
                k = (pb * nv + v) * 4
                xs_ref[pb * 4 + 0, rows(i, v), :] = st[k]
                xs_ref[pb * 4 + 1, rows(i, v), :] = st[k + 1]
                xs_ref[pb * 4 + 2, rows(ib, v), :] = st[k + 2]
                xs_ref[pb * 4 + 3, rows(ib, v), :] = st[k + 3]
        return step(i, st)

    lax.fori_loop(0, seglen, step2, tuple(carry))
    for k in range(PAIRS * 4):
        xp_ref[0, :, k * w:(k + 1) * w] = xs_ref[k].astype(BF16)


def _s5_readout_kernel(d_ref, xp_ref, kr_ref, syt_ref, rm_ref, dsk_ref, y_ref, yc_ref, bt_ref, wy_ref, yb_ref,
                       sy_ref, *, seglen):
    w = LANE
    cw = 4 * w
    fw = CHUNK * w
    ncg = fw // cw

    @pl.when(pl.program_id(1) == 0)
    def _():
        for i in range(CHUNK):
            for j in range(CHUNK):
                bt_ref[i * w:(i + 1) * w, j * w:(j + 1) * w] = kr_ref[0, j - i + CHUNK - 1]
        for b in range(syt_ref.shape[3] // w):
            for dc in range(4):
                r0 = (b * 4 + dc) * w
                sy_ref[r0:r0 + w, :] = syt_ref[dc // 2, dc % 2, :, b * w:(b + 1) * w].T.astype(BF16)
        nrow = sy_ref.shape[0]
        nh = rm_ref.shape[0] // CHUNK
        row = lax.broadcasted_iota(jnp.int32, (nrow, cw), 0)
        row_g = 2 * (row // cw) + (row % w) // (w // 2)
        col_g = (lax.broadcasted_iota(jnp.int32, (nrow, cw), 1) % w) // nh
        same = row_g == col_g
        for cg in range(ncg):
            cols = slice(cg * cw, (cg + 1) * cw)
            wy_ref[:, cols] = jnp.where(same, _dot(sy_ref[...], rm_ref[:, cols]), 0.0).astype(BF16)

    nrows = d_ref.shape[1]
    for cg in range(ncg):
        cols = slice(cg * cw, (cg + 1) * cw)
        yb_ref[:, cols] = (_dot(d_ref[0], bt_ref[:, cols]) + _dot(xp_ref[0], wy_ref[:, cols])
                           + d_ref[0, :, cols].astype(F32) * dsk_ref[0, :, cols])

    ngrp = nrows // NSEGS

    def unfold(j, carry):
        gi = pl.program_id(1) * ngrp + j
        r0 = pl.multiple_of(j * NSEGS, NSEGS)
        tok = yb_ref[pl.ds(r0, NSEGS), :].reshape(NSEGS * CHUNK, w)

        @pl.when(gi < seglen)
        def _():
            for s in range(NSEGS):
                t0 = pl.multiple_of((s * seglen + gi) * CHUNK, CHUNK)
                y_ref[pl.ds(t0, CHUNK), :] = tok[s * CHUNK:(s + 1) * CHUNK]

        @pl.when(gi >= seglen)
        def _():
            t0 = pl.multiple_of((gi - seglen) * NSEGS * CHUNK, NSEGS * CHUNK)
            yc_ref[pl.ds(t0, NSEGS * CHUNK), :] = tok

        return carry

    lax.fori_loop(0, ngrp, unfold, 0)


def _s5_tables(lam_re, lam_im, log_dt, b_re, b_im, c_re, c_im, d_skip, *, seglen):
    t = CHUNK
    ng, ns = lam_re.shape[1], lam_re.shape[2]
    nh = b_re.shape[-1]
    gp = ng * ns
    g8 = LANE // nh
    na = ng // g8
    nb = g8 // 2
    assert 2 * ns == LANE and nb % PAIRS == 0
    dt = jnp.exp(log_dt)[..., None]
    a, b = (lam_re * dt).reshape(2, gp), (lam_im * dt).reshape(2, gp)

    def lam_pow(m):
        mm = jnp.asarray(m, F32).reshape(-1, 1, 1)
        mag = jnp.exp(a[None] * mm)
        return mag * jnp.cos(b[None] * mm), mag * jnp.sin(b[None] * mm)

    pr, pi = lam_pow(np.arange(t + 1))
    lr, li = lam_re.reshape(2, gp), lam_im.reshape(2, gp)
    nr, ni = pr[1] - 1.0, pi[1]
    den = lr * lr + li * li
    qr, qi = (nr * lr + ni * li) / den, (ni * lr - nr * li) / den
    bt_re = b_re.transpose(0, 3, 1, 2).reshape(2, nh, gp)
    bt_im = b_im.transpose(0, 3, 1, 2).reshape(2, nh, gp)
    bb_re = qr[:, None] * bt_re - qi[:, None] * bt_im
    bb_im = qr[:, None] * bt_im + qi[:, None] * bt_re
    e_re = pr[:t, :, None] * bb_re[None] - pi[:t, :, None] * bb_im[None]
    e_im = pr[:t, :, None] * bb_im[None] + pi[:t, :, None] * bb_re[None]
    ct_re = c_re.transpose(0, 2, 1, 3).reshape(2, nh, gp)
    ct_im = c_im.transpose(0, 2, 1, 3).reshape(2, nh, gp)

    sl = slice(1, t + 1)
    pw_re = jnp.stack([pr[sl, 0], pr[sl, 1][::-1]])[:, :, None]
    pw_im = jnp.stack([pi[sl, 0], pi[sl, 1][::-1]])[:, :, None]
    wy_re = ct_re[:, None] * pw_re - ct_im[:, None] * pw_im
    wy_im = ct_re[:, None] * pw_im + ct_im[:, None] * pw_re
    syt = jnp.stack([wy_re, -wy_im], axis=1).reshape(2, 2, t * nh, gp)
    mr, mi = lam_pow([t, t * seglen])
    mu = jnp.stack([mr, mi], axis=2).reshape(2, 4, na, nb, LANE).transpose(2, 3, 0, 1, 4)
    dsk = jnp.tile(d_skip.reshape(na, 1, LANE), (1, 1, t))
    return e_re, e_im, ct_re, ct_im, syt, mu, dsk


def _s5_operators(params, seglen):
    e_re, e_im, ct_re, ct_im, syt, mu, dsk = jax.vmap(functools.partial(_s5_tables, seglen=seglen))(*params)
    depth, t, _, nh, gp = e_re.shape
    na = mu.shape[1]
    tw = gp // na
    nlag = 2 * CHUNK - 1
    kr = pl.pallas_call(
        _s5_taps_kernel,
        out_shape=jax.ShapeDtypeStruct((depth, na, nlag, LANE, LANE), BF16),
        grid=(depth, na),
        in_specs=[pl.BlockSpec((None, t, 2, nh, tw), lambda l, i: (l, 0, 0, 0, i)),
                  pl.BlockSpec((None, t, 2, nh, tw), lambda l, i: (l, 0, 0, 0, i)),
                  pl.BlockSpec((None, 2, nh, tw), lambda l, i: (l, 0, 0, i)),
                  pl.BlockSpec((None, 2, nh, tw), lambda l, i: (l, 0, 0, i))],
        out_specs=pl.BlockSpec((None, 1, nlag, LANE, LANE), lambda l, i: (l, i, 0, 0, 0)),
        compiler_params=_params(("arbitrary", "arbitrary")),
        name="s5_taps",
    )(e_re, e_im, ct_re, ct_im)
    r_mat = np.kron(np.eye(CHUNK, dtype=np.float32), np.tile(np.eye(nh, dtype=np.float32), (1, LANE // nh)))
    return kr, e_re, e_im, syt, jnp.asarray(r_mat, F32).astype(BF16), mu, dsk


def _s5(u, uc, li, ops):
    kr, e_re, e_im, syt, r_mat, mu, dsk = ops
    l, d_ssm = u.shape
    lc = uc.shape[0]
    na, nb = mu.shape[1], mu.shape[2]
    fw = CHUNK * LANE
    nlat, ncc = l // CHUNK, lc // CHUNK
    nc = nlat + ncc
    seglen = nlat // NSEGS
    sw = PAIRS * 4 * LANE
    assert seglen * NSEGS == nlat and ncc * CHUNK == lc
    d, xp = pl.pallas_call(
        functools.partial(_s5_state_kernel, seglen=seglen, ncc=ncc),
        out_shape=(jax.ShapeDtypeStruct((na, nc, fw), BF16), jax.ShapeDtypeStruct((na, nc, nb * 4 * LANE), BF16)),
        grid=(na, nb // PAIRS),
        in_specs=[
            pl.BlockSpec((l, LANE), lambda a, b: (0, a)),
            pl.BlockSpec((lc, LANE), lambda a, b: (0, a)),
            pl.BlockSpec((None,) + e_re.shape[1:4] + (PAIRS * LANE,), lambda a, b: (li, 0, 0, 0, a * (nb // PAIRS) + b)),
            pl.BlockSpec((None,) + e_im.shape[1:4] + (PAIRS * LANE,), lambda a, b: (li, 0, 0, 0, a * (nb // PAIRS) + b)),
            pl.BlockSpec((None, 1, PAIRS, 2, 4, LANE), lambda a, b: (li, a, b, 0, 0, 0)),
        ],
        out_specs=(pl.BlockSpec((1, nc, fw), lambda a, b: (a, 0, 0)),
                   pl.BlockSpec((1, nc, sw), lambda a, b: (a, 0, b))),
        scratch_shapes=[pltpu.VMEM((fw, sw), BF16),
                        pltpu.VMEM((PAIRS * 4, nc, LANE), F32), pltpu.VMEM((PAIRS * 4, nc, LANE), F32)],
        compiler_params=_params(("arbitrary", "arbitrary")),
        name="s5_state",
    )(u, uc, e_re, e_im, mu)
    rbs = max(r for r in range(NSEGS, min(nc, 512) + 1, NSEGS) if nc % r == 0)
    nrb = nc // rbs
    assert NSEGS % 16 == 0 and ncc % NSEGS == 0
    return pl.pallas_call(
        functools.partial(_s5_readout_kernel, seglen=seglen),
        out_shape=(jax.ShapeDtypeStruct((l, d_ssm), F32), jax.ShapeDtypeStruct((lc, d_ssm), F32)),
        grid=(na, nrb),
        in_specs=[
            pl.BlockSpec((1, rbs, fw), lambda a, r: (a, r, 0)),
            pl.BlockSpec((1, rbs, nb * 4 * LANE), lambda a, r: (a, r, 0)),
            pl.BlockSpec((None, 1) + kr.shape[2:], lambda a, r: (li, a, 0, 0, 0)),
            pl.BlockSpec((None,) + syt.shape[1:4] + (nb * LANE,), lambda a, r: (li, 0, 0, 0, a)),
            _const_spec(r_mat.shape),
            pl.BlockSpec((None, 1, 1, fw), lambda a, r: (li, a, 0, 0)),
        ],
        out_specs=(pl.BlockSpec((l, LANE), lambda a, r: (0, a)), pl.BlockSpec((lc, LANE), lambda a, r: (0, a))),
        scratch_shapes=[pltpu.VMEM((fw, fw), BF16), pltpu.VMEM((nb * 4 * LANE, fw), BF16),
                        pltpu.VMEM((rbs, fw), F32), pltpu.VMEM((nb * 4 * LANE, syt.shape[3]), BF16)],
        compiler_params=_params(("arbitrary", "arbitrary")),
        name="s5_readout",
    )(d, xp, kr, syt, r_mat, dsk)


def _fft1_kernel(p_ref, q_ref, m_ref, o_ref, s_ref, *, nb):
    for n in range(nb):
        rhs = jnp.concatenate([p_ref[:, n, :], q_ref[:, n, :]], axis=0).astype(BF16)
        s_ref[:, n, :] = _dot(m_ref[...], rhs)
    o_ref[...] = s_ref[...].astype(BF16)


def _fft2_kernel(yr_ref, yi_ref, ca_ref, sa_ref, cb_ref, sb_ref, o_ref, s_ref, *, kb):
    cb, sb = cb_ref[...], sb_ref[...]
    for k in range(kb):
        ca, sa = ca_ref[k:k + 1, :], sa_ref[k:k + 1, :]
        g = jnp.concatenate([ca * cb - sa * sb, sa * cb + ca * sb], axis=1).astype(BF16)
        rhs = jnp.concatenate([yr_ref[k], yi_ref[k]], axis=0)
        s_ref[:, k, :] = _dot(g, rhs)
    o_ref[...] = s_ref[...].astype(BF16)


def _fft_ctx_kernel(p_ref, q_ref, m_ref, o_ref):
    rhs = jnp.concatenate([p_ref[...], q_ref[...]], axis=0).astype(BF16)
    o_ref[...] = _dot(m_ref[...], rhs).astype(BF16)


def _fnet_tables(l, lc):
    r = math.isqrt(l)
    assert r * r == l
    k = np.arange(r)
    th = 2.0 * np.pi * (np.outer(k, k) % r) / r
    c, s = np.cos(th), np.sin(th)
    m1 = np.block([[c, -s], [-s, -c]])
    ta = 2.0 * np.pi * np.outer(k, k) / l
    scale = 1.0 / math.sqrt(l)
    tw2 = tuple(jnp.asarray(v, F32) for v in (np.cos(ta), np.sin(ta), c * scale, s * scale))
    kc = np.arange(lc)
    thc = 2.0 * np.pi * (np.outer(kc, kc) % lc) / lc
    mc = np.concatenate([np.cos(thc), -np.sin(thc)], axis=1) / math.sqrt(lc)
    return (jnp.asarray(m1, F32).astype(BF16), tw2, jnp.asarray(mc, F32).astype(BF16))


def _fnet_lat(p, q, m1, tw2):
    l, c = p.shape
    r = m1.shape[0] // 2
    assert r * r == l
    p3 = p.reshape(r, r, c)
    q3 = q.reshape(r, r, c)
    nb = 16
    y1 = pl.pallas_call(
        functools.partial(_fft1_kernel, nb=nb),
        out_shape=jax.ShapeDtypeStruct((2 * r, r, c), BF16),
        grid=(r // nb,),
        in_specs=[pl.BlockSpec((r, nb, c), lambda j: (0, j, 0)),
                  pl.BlockSpec((r, nb, c), lambda j: (0, j, 0)),
                  _const_spec(m1.shape)],
        out_specs=pl.BlockSpec((2 * r, nb, c), lambda j: (0, j, 0)),
        scratch_shapes=[pltpu.VMEM((2 * r, nb, c), F32)],
        compiler_params=_params(("arbitrary",)),
        name="fft_stage1",
    )(p3, q3, m1)
    kb = 16
    ca, sa, cb, sb = tw2
    out = pl.pallas_call(
        functools.partial(_fft2_kernel, kb=kb),
        out_shape=jax.ShapeDtypeStruct((r, r, c), BF16),
        grid=(r // kb,),
        in_specs=[pl.BlockSpec((kb, r, c), lambda j: (j, 0, 0)),
                  pl.BlockSpec((kb, r, c), lambda j: (r // kb + j, 0, 0)),
                  pl.BlockSpec((kb, r), lambda j: (j, 0)),
                  pl.BlockSpec((kb, r), lambda j: (j, 0)),
                  _const_spec(cb.shape), _const_spec(sb.shape)],
        out_specs=pl.BlockSpec((r, kb, c), lambda j: (0, j, 0)),
        scratch_shapes=[pltpu.VMEM((r, kb, c), F32)],
        compiler_params=_params(("arbitrary",)),
        name="fft_stage2",
    )(y1, y1, ca, sa, cb, sb)
    return out.reshape(l, c)


def _fnet_ctx(pc, qc, mc):
    return pl.pallas_call(
        _fft_ctx_kernel,
        out_shape=jax.ShapeDtypeStruct(pc.shape, BF16),
        grid=(1,),
        in_specs=[_whole_spec(pc.shape), _whole_spec(qc.shape), _const_spec(mc.shape)],
        out_specs=_whole_spec(pc.shape),
        compiler_params=_params(("arbitrary",)),
        name="fft_ctx",
    )(pc, qc, mc)


def _pos_tables(l, d):
    quarter = d // 4
    omega = 1.0 / (POS_BASE ** (jnp.arange(quarter, dtype=F32) / quarter))

    def enc(pv):
        ang = pv[:, None] * omega[None, :]
        return jnp.concatenate([jnp.sin(ang), jnp.cos(ang)], axis=-1)

    renc = enc(jnp.arange(l // GRID_W, dtype=F32))
    cenc = jnp.tile(enc(jnp.arange(GRID_W, dtype=F32)), (TM // GRID_W, 1))
    return renc, cenc


def kernel(x, c, ctx, c_ctx, w_ada, b_ada, norm_g, ffn1_gu, ffn1_down, ffn2_gu, ffn2_down, w_in, w_out,
           ssm_lam_re, ssm_lam_im, ssm_log_dt, ssm_b_re, ssm_b_im, ssm_c_re, ssm_c_im, ssm_d, w_glu, w_fmix):
    bsz, l, d = x.shape
    lc = ctx.shape[1]
    depth = w_ada.shape[0]
    d_ssm = w_glu.shape[1]
    head = w_fmix.shape[2]
    assert bsz == 1 and l % TM == 0 and TM % GRID_W == 0
    seglen = (l // CHUNK) // NSEGS

    mods = _modulation(c_ctx, c, w_ada, b_ada)
    pos = _pos_tables(l, d)
    m1, tw2, mc = _fnet_tables(l, lc)
    kc = np.arange(head)
    thc = 2.0 * np.pi * (np.outer(kc, kc) % head) / head
    cs = jnp.asarray(np.concatenate([np.cos(thc), np.sin(thc)], axis=1) / math.sqrt(head), F32).astype(BF16)
    win, wout, wglu, wfm = (a.astype(BF16) for a in (w_in, w_out, w_glu, w_fmix))
    wgu1, wd1 = ffn1_gu[0].astype(BF16), ffn1_down[0].astype(BF16)

    ops = _s5_operators((ssm_lam_re, ssm_lam_im, ssm_log_dt, ssm_b_re, ssm_b_im, ssm_c_re, ssm_c_im, ssm_d), seglen)

    xl, xc = x[0], ctx[0]
    for li in range(depth):
        last = li == depth - 1
        xl, u, p, q, xc, uc, pc, qc, wgu2, wd2 = _layer_in(xl, xc, pos if li == 0 else None, li, mods, norm_g, wgu1, wd1,
                                                           win, cs, (ffn2_gu, ffn2_down), d_ssm=d_ssm)
        ys, ysc = _s5(u, uc, li, ops)
        yf = _fnet_lat(p, q, m1, tw2)
        ctx_in = None if last else (xc, ysc, _fnet_ctx(pc, qc, mc))
        xl, xc, wgu1, wd1 = _layer_out((xl, ys, yf), ctx_in, li, mods, norm_g, wglu, wfm, wout, wgu2, wd2,
                                       (ffn1_gu, ffn1_down))
    return xl[None]
```

```python
import functools
import math

import numpy as np
import jax
import jax.numpy as jnp
from jax import lax
from jax.experimental import pallas as pl
from jax.experimental.pallas import tpu as pltpu

F32 = jnp.float32
BF16 = jnp.bfloat16

LANE = 128
SUBLANE = 8
VMEM_LIMIT = 56 * 1024 * 1024

EPS = 1e-6
GRID_W = 64
POS_BASE = 10000.0
N_MOD = 9
N_FNET_HEADS = 4
CHUNK = 16
TM = 512
CTX, LAT = 0, 1


def _dot(a, b):
    return jnp.dot(a, b, preferred_element_type=F32)


def _params(sem=None):
    return pltpu.CompilerParams(dimension_semantics=sem, vmem_limit_bytes=VMEM_LIMIT)


def _const_spec(shape):
    nd = len(shape)
    return pl.BlockSpec(shape, lambda *_: (0,) * nd, pipeline_mode=pl.Buffered(1))


def _rms(x, g):
    ms = jnp.mean(x * x, axis=-1, keepdims=True)
    return x * lax.rsqrt(ms + EPS) * g


def _pre(x, g, mod_ref, who, k):
    return _rms(x, g) * (1.0 + mod_ref[who, 3 * k + 1:3 * k + 2, :]) + mod_ref[who, 3 * k:3 * k + 1, :]


def _post(x, y, g, mod_ref, who, k, weight):
    return x + (weight * mod_ref[who, 3 * k + 2:3 * k + 3, :]) * _rms(y, g)


def _on_last_step(fn):
    pl.when(pl.program_id(0) == pl.num_programs(0) - 1)(fn)


def _sigmoid(x):
    return 1.0 / (1.0 + jnp.exp(-x))


def _gelu_tanh(x):
    return 0.5 * x * (1.0 + jnp.tanh(math.sqrt(2.0 / math.pi) * (x + 0.044715 * (x * x * x))))


def _mod_kernel(cb_ref, w_ref, b_ref, o_ref, s_ref, *, tn):
    rows, n_out = w_ref.shape[1], w_ref.shape[2]
    nb = tn // LANE
    j = pl.program_id(1)

    @pl.when((pl.program_id(0) == 0) & (j == 0))
    def _():
        cv = cb_ref[...]
        s_ref[...] = cv * _sigmoid(cv)

    @pl.when(j == 0)
    def _():
        o_ref[0] = jnp.broadcast_to(b_ref[0], (2, n_out))

    for grp in range(n_out // tn):
        def body(t, accs):
            d0 = pl.multiple_of(t * SUBLANE, SUBLANE)
            s = [s_ref[r, pl.ds(j * rows + d0, SUBLANE), :] for r in range(2)]
            new = list(accs)
            for jj in range(nb):
                w = w_ref[0, pl.ds(d0, SUBLANE), grp * tn + jj * LANE:grp * tn + (jj + 1) * LANE]
                for r in range(2):
                    new[r * nb + jj] = accs[r * nb + jj] + w * s[r]
            return tuple(new)

        init = tuple(jnp.zeros((SUBLANE, LANE), F32) for _ in range(2 * nb))
        accs = lax.fori_loop(0, rows // SUBLANE, body, init, unroll=4)
        for r in range(2):
            for jj in range(nb):
                lanes = slice(grp * tn + jj * LANE, grp * tn + (jj + 1) * LANE)
                o_ref[0, r:r + 1, lanes] = o_ref[0, r:r + 1, lanes] + jnp.sum(accs[r * nb + jj], axis=0, keepdims=True)


def _modulation(c_ctx, c, w_ada, b_ada):
    depth, d_model, n_out = w_ada.shape
    tn = 9 * LANE
    rows = 256
    assert n_out % tn == 0 and d_model % rows == 0
    cb = jnp.broadcast_to(jnp.stack([c_ctx, c[0]])[:, :, None], (2, d_model, LANE))
    out = pl.pallas_call(
        functools.partial(_mod_kernel, tn=tn),
        out_shape=jax.ShapeDtypeStruct((depth, 2, n_out), F32),
        grid=(depth, d_model // rows),
        in_specs=[
            pl.BlockSpec((2, d_model, LANE), lambda l, j: (0, 0, 0)),
            pl.BlockSpec((1, rows, n_out), lambda l, j: (l, j, 0)),
            pl.BlockSpec((1, 1, n_out), lambda l, j: (l, 0, 0)),
        ],
        out_specs=pl.BlockSpec((1, 2, n_out), lambda l, j: (l, 0, 0)),
        scratch_shapes=[pltpu.VMEM((2, d_model, LANE), F32)],
        compiler_params=_params(("arbitrary", "arbitrary")),
        name="adaln_mod",
    )(cb, w_ada, b_ada.reshape(depth, 1, n_out))
    return out.reshape(depth, 2, N_MOD, d_model)


SPLIT = 2


def _row_chunks(n, parts):
    step = n // parts
    return [slice(i * step, (i + 1) * step) for i in range(parts)]


def _ffn_stage(xs, who, mod_ref, g_ref, wgu_ref, wd_ref, k, gi):
    f = wd_ref.shape[0]
    hs = [_pre(x, g_ref[gi:gi + 1, :], mod_ref, who, k).astype(BF16) for x in xs]
    gus = [_dot(h, wgu_ref[...]) for h in hs]
    acts = [(gu[:, :f] * _sigmoid(gu[:, :f]) * gu[:, f:]).astype(BF16) for gu in gus]
    ys = [_dot(a, wd_ref[...]) for a in acts]
    return [_post(x, y, g_ref[gi + 1:gi + 2, :], mod_ref, who, k, 0.5) for x, y in zip(xs, ys)]


def _inproj_stage(xs, rows, who, mod_ref, g_ref, win_ref, cs_ref, u_ref, p_ref, q_ref):
    d_ssm = u_ref.shape[1]
    head = cs_ref.shape[0]
    hs = [_pre(x, g_ref[2:3, :], mod_ref, who, 1).astype(BF16) for x in xs]
    hhs = [_dot(h, win_ref[...]) for h in hs]
    for r, hh in zip(rows, hhs):
        u_ref[r, :] = hh[:, :d_ssm]
        for n in range(N_FNET_HEADS):
            lo = d_ssm + n * head
            pq = _dot(hh[:, lo:lo + head].astype(BF16), cs_ref[...])
            p_ref[r, n * head:(n + 1) * head] = pq[:, :head]
            q_ref[r, n * head:(n + 1) * head] = pq[:, head:]


def _outproj_stage(xs, yss, yfs, who, mod_ref, g_ref, wglu_ref, wfm_ref, wout_ref):
    head = wfm_ref.shape[1]
    hs = [_gelu_tanh(ys) for ys in yss]
    gls = [_dot(h.astype(BF16), wglu_ref[...]) for h in hs]
    mixed = []
    for h, gl, yf in zip(hs, gls, yfs):
        yf = yf.astype(BF16)
        parts = [(h * _sigmoid(gl)).astype(BF16)]
        parts += [_dot(yf[:, n * head:(n + 1) * head], wfm_ref[n]).astype(BF16) for n in range(N_FNET_HEADS)]
        mixed.append(jnp.concatenate(parts, axis=1))
    ys = [_dot(m, wout_ref[...]) for m in mixed]
    return [_post(x, y, g_ref[3:4, :], mod_ref, who, 1, 1.0) for x, y in zip(xs, ys)]


def _cast_chunks(src_refs, dst_refs):
    for src_ref, dst_ref in zip(src_refs, dst_refs):
        dst_ref[...] = src_ref[...].astype(BF16)


def _layer_in_kernel(*refs, first):
    if first:
        (x_ref, xc_ref, renc_ref, cenc_ref, mod_ref, g_ref, wgu_ref, wd_ref, win_ref, cs_ref, ng_ref, nd_ref,
         o_ref, u_ref, p_ref, q_ref, oc_ref, uc_ref, pc_ref, qc_ref, ngo_ref, ndo_ref, xs_ref) = refs
        tm = x_ref.shape[0]
        half = renc_ref.shape[1]
        r0 = pl.program_id(0) * (tm // GRID_W)
        xs_ref[:, half:] = x_ref[:, half:] + cenc_ref[...]
        for q in range(tm // GRID_W):
            xs_ref[q * GRID_W:(q + 1) * GRID_W, :half] = (
                x_ref[q * GRID_W:(q + 1) * GRID_W, :half] + renc_ref[pl.ds(r0 + q, 1), :])
        src_ref = xs_ref
    else:
        (x_ref, xc_ref, mod_ref, g_ref, wgu_ref, wd_ref, win_ref, cs_ref, ng_ref, nd_ref,
         o_ref, u_ref, p_ref, q_ref, oc_ref, uc_ref, pc_ref, qc_ref, ngo_ref, ndo_ref) = refs
        src_ref = x_ref
    _cast_chunks((ng_ref, nd_ref), (ngo_ref, ndo_ref))
    rows = _row_chunks(x_ref.shape[0], SPLIT)
    x1 = _ffn_stage([src_ref[r, :] for r in rows], LAT, mod_ref, g_ref, wgu_ref, wd_ref, 0, 0)
    for r, v in zip(rows, x1):
        o_ref[r, :] = v
    _inproj_stage(x1, rows, LAT, mod_ref, g_ref, win_ref, cs_ref, u_ref, p_ref, q_ref)

    def _():
        rows_c = _row_chunks(xc_ref.shape[0], 1)
        xc1 = _ffn_stage([xc_ref[...]], CTX, mod_ref, g_ref, wgu_ref, wd_ref, 0, 0)
        oc_ref[...] = xc1[0]
        _inproj_stage(xc1, rows_c, CTX, mod_ref, g_ref, win_ref, cs_ref, uc_ref, pc_ref, qc_ref)
    _on_last_step(_)


def _layer_out_kernel(*refs, with_ctx):
    if with_ctx:
        (x_ref, ys_ref, yf_ref, xc_ref, ysc_ref, yfc_ref, mod_ref, g_ref, wglu_ref, wfm_ref, wout_ref,
         wgu_ref, wd_ref, ng_ref, nd_ref, o_ref, oc_ref, ngo_ref, ndo_ref) = refs
        _cast_chunks((ng_ref, nd_ref), (ngo_ref, ndo_ref))
    else:
        x_ref, ys_ref, yf_ref, mod_ref, g_ref, wglu_ref, wfm_ref, wout_ref, wgu_ref, wd_ref, o_ref = refs
    rows = _row_chunks(x_ref.shape[0], SPLIT)
    x2 = _outproj_stage([x_ref[r, :] for r in rows], [ys_ref[r, :] for r in rows], [yf_ref[r, :] for r in rows],
                        LAT, mod_ref, g_ref, wglu_ref, wfm_ref, wout_ref)
    x3 = _ffn_stage(x2, LAT, mod_ref, g_ref, wgu_ref, wd_ref, 2, 4)
    for r, v in zip(rows, x3):
        o_ref[r, :] = v
    if with_ctx:
        def _():
            xc2 = _outproj_stage([xc_ref[...]], [ysc_ref[...]], [yfc_ref[...]], CTX, mod_ref, g_ref, wglu_ref,
                                 wfm_ref, wout_ref)
            oc_ref[...] = _ffn_stage(xc2, CTX, mod_ref, g_ref, wgu_ref, wd_ref, 2, 4)[0]
        _on_last_step(_)


def _tile_spec(cols):
    return pl.BlockSpec((TM, cols), lambda i: (i, 0))


def _whole_spec(shape):
    nd = len(shape)
    return pl.BlockSpec(shape, lambda *_: (0,) * nd)


def _layer_spec(arr, li):
    nd = arr.ndim - 1
    return pl.BlockSpec((None,) + arr.shape[1:], lambda *_: (li,) + (0,) * nd, pipeline_mode=pl.Buffered(1))


def _cast_specs(arr, li, nt):
    rows, cols = arr.shape[1:]
    k = 1
    while (rows * k) % nt or (rows * k // nt) % 16:
        k *= 2
    rc = rows * k // nt
    assert rows % rc == 0 and nt % k == 0
    return (pl.BlockSpec((None, rc, cols), lambda i: (li, i // k, 0)), pl.BlockSpec((rc, cols), lambda i: (i // k, 0)),
            jax.ShapeDtypeStruct((rows, cols), BF16))


def _layer_in(xl, xc, pos, li, mods, gains, wgu, wd, win, cs, nxt, *, d_ssm):
    l, d = xl.shape
    lc = xc.shape[0]
    first = pos is not None
    d_f = win.shape[2] - d_ssm
    widths = (d, d_ssm, d_f, d_f)
    stacked = [mods, gains]
    nt = l // TM
    casts = [_cast_specs(a, li, nt) for a in nxt]
    return pl.pallas_call(
        functools.partial(_layer_in_kernel, first=first),
        out_shape=[jax.ShapeDtypeStruct((l, wd_), F32) for wd_ in widths]
        + [jax.ShapeDtypeStruct((lc, wd_), F32) for wd_ in widths] + [c[2] for c in casts],
        grid=(nt,),
        in_specs=([_tile_spec(d), _whole_spec(xc.shape)] + ([_const_spec(a.shape) for a in pos] if first else [])
                  + [_layer_spec(a, li) for a in stacked] + [_const_spec(wgu.shape), _const_spec(wd.shape)]
                  + [_layer_spec(win, li), _const_spec(cs.shape)] + [c[0] for c in casts]),
        out_specs=([_tile_spec(wd_) for wd_ in widths] + [_whole_spec((lc, wd_)) for wd_ in widths]
                   + [c[1] for c in casts]),
        scratch_shapes=[pltpu.VMEM((TM, d), F32)] if first else [],
        compiler_params=_params(("arbitrary",)),
        name="layer_in",
    )(xl, xc, *(pos if first else ()), *stacked, wgu, wd, win, cs, *nxt)


def _layer_out(lat, ctx, li, mods, gains, wglu, wfm, wout, wgu, wd, nxt):
    xl, ys, yf = lat
    l, d = xl.shape
    with_ctx = ctx is not None
    ctx = list(ctx) if with_ctx else []
    stacked = [mods, gains, wglu, wfm, wout]
    nt = l // TM
    casts = [_cast_specs(a, li + 1, nt) for a in nxt] if with_ctx else []
    outs = pl.pallas_call(
        functools.partial(_layer_out_kernel, with_ctx=with_ctx),
        out_shape=([jax.ShapeDtypeStruct(xl.shape, F32)] + ([jax.ShapeDtypeStruct(ctx[0].shape, F32)] if with_ctx else [])
                   + [c[2] for c in casts]),
        grid=(nt,),
        in_specs=([_tile_spec(d), _tile_spec(ys.shape[1]), _tile_spec(yf.shape[1])]
                  + [_whole_spec(a.shape) for a in ctx] + [_layer_spec(a, li) for a in stacked]
                  + [_const_spec(wgu.shape), _const_spec(wd.shape)] + [c[0] for c in casts]),
        out_specs=([_tile_spec(d)] + ([_whole_spec(ctx[0].shape)] if with_ctx else []) + [c[1] for c in casts]),
        compiler_params=_params(("arbitrary",)),
        name="layer_out",
    )(xl, ys, yf, *ctx, *stacked, wgu, wd, *(nxt if with_ctx else ()))
    return tuple(outs) if with_ctx else (outs[0], None, None, None)


NSEGS = 16
PAIRS = 2


def _cmul_add(ar, ai, xr, xi, zr, zi):
    return ar * xr - ai * xi + zr, ar * xi + ai * xr + zi


def _dot_nt(a, b):
    def nt(x, y):
        return lax.dot_general(x, y, (((1,), (1,)), ((), ())), preferred_element_type=F32)
    a_hi, b_hi = a.astype(BF16), b.astype(BF16)
    a_lo, b_lo = (a - a_hi.astype(F32)).astype(BF16), (b - b_hi.astype(F32)).astype(BF16)
    return nt(a_hi, b_hi) + (nt(a_hi, b_lo) + nt(a_lo, b_hi))


def _s5_taps_kernel(e_re_ref, e_im_ref, ct_re_ref, ct_im_ref, kr_ref):
    t = CHUNK
    nh = ct_re_ref.shape[1]
    g8 = LANE // nh
    gw = e_re_ref.shape[3] // g8
    same = (lax.broadcasted_iota(jnp.int32, (LANE, g8 * gw), 0) // nh
            == lax.broadcasted_iota(jnp.int32, (LANE, g8 * gw), 1) // gw)
    taps = []
    for d in range(2):
        cre = jnp.where(same, jnp.concatenate([ct_re_ref[d]] * g8, axis=0), 0.0)
        cim = jnp.where(same, jnp.concatenate([ct_im_ref[d]] * g8, axis=0), 0.0)
        er = jnp.concatenate([e_re_ref[m, d] for m in range(t)], axis=0)
        ei = jnp.concatenate([e_im_ref[m, d] for m in range(t)], axis=0)
        taps.append(_dot_nt(er, cre) - _dot_nt(ei, cim))
    blk = (lax.broadcasted_iota(jnp.int32, (LANE, LANE), 0) // nh
           == lax.broadcasted_iota(jnp.int32, (LANE, LANE), 1) // nh)
    for lag in range(-(t - 1), t):
        if lag > 0:
            src = taps[0][lag * nh:(lag + 1) * nh]
        elif lag < 0:
            src = taps[1][-lag * nh:(1 - lag) * nh]
        else:
            src = taps[0][:nh] + taps[1][:nh]
        kr_ref[0, lag + t - 1] = jnp.where(blk, jnp.concatenate([src] * g8, axis=0), 0.0).astype(BF16)


def _s5_state_kernel(u_ref, uc_ref, e_re_ref, e_im_ref, mu_ref, d_ref, xp_ref, wz_ref, z_ref, xs_ref, *, seglen, ncc):
    w = LANE
    nc = d_ref.shape[1]
    nlat = NSEGS * seglen
    nv = NSEGS // SUBLANE
    nh = e_re_ref.shape[2]
    ns = w // 2

    @pl.when(pl.program_id(1) == 0)
    def _():
        def fold(i, carry):
            r0 = pl.multiple_of(i * NSEGS, NSEGS)
            tiles = [u_ref[pl.ds(pl.multiple_of((s * seglen + i) * CHUNK, CHUNK), CHUNK), :] for s in range(NSEGS)]
            d_ref[0, pl.ds(r0, NSEGS), :] = jnp.concatenate(tiles, axis=0).reshape(NSEGS, CHUNK * w).astype(BF16)
            return carry

        lax.fori_loop(0, seglen, fold, 0)
        d_ref[0, nlat:nlat + ncc, :] = uc_ref[...].reshape(ncc, CHUNK * w).astype(BF16)

    wz_ref[...] = jnp.zeros(wz_ref.shape, BF16)
    first = (lax.broadcasted_iota(jnp.int32, (nh, 4 * w), 1) % w) < ns
    for pb in range(PAIRS):
        pair = pl.program_id(1) * PAIRS + pb
        lanes = slice(pb * w, (pb + 1) * w)
        for i in range(CHUNK):
            ez = jnp.concatenate([e_re_ref[CHUNK - 1 - i, 0, :, lanes], e_im_ref[CHUNK - 1 - i, 0, :, lanes],
                                  e_re_ref[i, 1, :, lanes], e_im_ref[i, 1, :, lanes]], axis=1).astype(BF16)
            both = jnp.concatenate([jnp.where(first, ez, jnp.zeros_like(ez)),
                                    jnp.where(first, jnp.zeros_like(ez), ez)], axis=0)
            r0 = pl.multiple_of(i * w + pair * 2 * nh, 2 * nh)
            wz_ref[pl.ds(r0, 2 * nh), pb * 4 * w:(pb + 1) * 4 * w] = both

    for pb in range(PAIRS):
        zz = _dot(d_ref[0], wz_ref[:, pb * 4 * w:(pb + 1) * 4 * w])
        for c in range(4):
            z_ref[pb * 4 + c] = zz[:, c * w:(c + 1) * w]

    mu = [[mu_ref[0, pb, 0, c:c + 1, :] for c in range(4)] for pb in range(PAIRS)]
    mun = [[mu_ref[0, pb, 1, c:c + 1, :] for c in range(4)] for pb in range(PAIRS)]

    pre = []
    for pb in range(PAIRS):
        zc = [z_ref[pb * 4 + c, nlat:nlat + ncc, :] for c in range(4)]
        sf = (jnp.zeros((1, w), F32), jnp.zeros((1, w), F32))
        sb = (jnp.zeros((1, w), F32), jnp.zeros((1, w), F32))
        for j in range(ncc):
            jb = ncc - 1 - j
            xs_ref[pb * 4 + 0, nlat + j:nlat + j + 1, :] = sf[0]
            xs_ref[pb * 4 + 1, nlat + j:nlat + j + 1, :] = sf[1]
            xs_ref[pb * 4 + 2, nlat + jb:nlat + jb + 1, :] = sb[0]
            xs_ref[pb * 4 + 3, nlat + jb:nlat + jb + 1, :] = sb[1]
            sf = _cmul_add(mu[pb][0], mu[pb][1], sf[0], sf[1], zc[0][j:j + 1], zc[1][j:j + 1])
            sb = _cmul_add(mu[pb][2], mu[pb][3], sb[0], sb[1], zc[2][jb:jb + 1], zc[3][jb:jb + 1])
        pre.append((sf, sb))

    mub = [[jnp.broadcast_to(m, (SUBLANE, w)) for m in mu[pb]] for pb in range(PAIRS)]

    def rows(i, v):
        return pl.ds(pl.multiple_of(i * NSEGS + v * SUBLANE, SUBLANE), SUBLANE)

    def step(i, st):
        ib = seglen - 1 - i
        new = []
        for pb in range(PAIRS):
            for v in range(nv):
                k = (pb * nv + v) * 4
                m = mub[pb]
                fr, fi = _cmul_add(m[0], m[1], st[k], st[k + 1],
                                   z_ref[pb * 4 + 0, rows(i, v), :], z_ref[pb * 4 + 1, rows(i, v), :])
                br, bi = _cmul_add(m[2], m[3], st[k + 2], st[k + 3],
                                   z_ref[pb * 4 + 2, rows(ib, v), :], z_ref[pb * 4 + 3, rows(ib, v), :])
                new += [fr, fi, br, bi]
        return tuple(new)

    zero = jnp.zeros((SUBLANE, w), F32)
    fin = lax.fori_loop(0, seglen, step, (zero,) * (PAIRS * nv * 4))

    carry = []
    for pb in range(PAIRS):
        cf, cb = pre[pb]
        rows_f, rows_b = [None] * NSEGS, [None] * NSEGS
        for s in range(NSEGS):
            sr = NSEGS - 1 - s
            rows_f[s] = cf
            rows_b[sr] = cb
            kf = (pb * nv + s // SUBLANE) * 4
            kb = (pb * nv + sr // SUBLANE) * 4
            sl, srl = s % SUBLANE, sr % SUBLANE
            cf = _cmul_add(mun[pb][0], mun[pb][1], cf[0], cf[1], fin[kf][sl:sl + 1, :], fin[kf + 1][sl:sl + 1, :])
            cb = _cmul_add(mun[pb][2], mun[pb][3], cb[0], cb[1],
                           fin[kb + 2][srl:srl + 1, :], fin[kb + 3][srl:srl + 1, :])
        for v in range(nv):
            seg = slice(v * SUBLANE, (v + 1) * SUBLANE)
            carry += [jnp.concatenate([r[0] for r in rows_f[seg]], axis=0),
                      jnp.concatenate([r[1] for r in rows_f[seg]], axis=0),
                      jnp.concatenate([r[0] for r in rows_b[seg]], axis=0),
                      jnp.concatenate([r[1] for r in rows_b[seg]], axis=0)]

    def step2(i, st):
        ib = seglen - 1 - i
        for pb in range(PAIRS):
            for v in range(nv):
                k = (pb * nv + v) * 4
                xs_ref[pb * 4 + 0, rows(i, v), :] = st[k]
                xs_ref[pb * 4 + 1, rows(i, v), :] = st[k + 1]
                xs_ref[pb * 4 + 2, rows(ib, v), :] = st[k + 2]
                xs_ref[pb * 4 + 3, rows(ib, v), :] = st[k + 3]
        return step(i, st)

    lax.fori_loop(0, seglen, step2, tuple(carry))
    for k in range(PAIRS * 4):
        xp_ref[0, :, k * w:(k + 1) * w] = xs_ref[k].astype(BF16)


def _s5_readout_kernel(d_ref, xp_ref, kr_ref, syt_ref, rm_ref, dsk_ref, y_ref, yc_ref, bt_ref, wy_ref, yb_ref,
                       sy_ref, *, seglen):
    w = LANE
    cw = 4 * w
    fw = CHUNK * w
    ncg = fw // cw

    @pl.when(pl.program_id(1) == 0)
    def _():
        for i in range(CHUNK):
            for j in range(CHUNK):
                bt_ref[i * w:(i + 1) * w, j * w:(j + 1) * w] = kr_ref[0, j - i + CHUNK - 1]
        for b in range(syt_ref.shape[3] // w):
            for dc in range(4):
                r0 = (b * 4 + dc) * w
                sy_ref[r0:r0 + w, :] = syt_ref[dc // 2, dc % 2, :, b * w:(b + 1) * w].T.astype(BF16)
        nrow = sy_ref.shape[0]
        nh = rm_ref.shape[0] // CHUNK
        row = lax.broadcasted_iota(jnp.int32, (nrow, cw), 0)
        row_g = 2 * (row // cw) + (row % w) // (w // 2)
        col_g = (lax.broadcasted_iota(jnp.int32, (nrow, cw), 1) % w) // nh
        same = row_g == col_g
        for cg in range(ncg):
            cols = slice(cg * cw, (cg + 1) * cw)
            wy_ref[:, cols] = jnp.where(same, _dot(sy_ref[...], rm_ref[:, cols]), 0.0).astype(BF16)

    nrows = d_ref.shape[1]
    for cg in range(ncg):
        cols = slice(cg * cw, (cg + 1) * cw)
        yb_ref[:, cols] = (_dot(d_ref[0], bt_ref[:, cols]) + _dot(xp_ref[0], wy_ref[:, cols])
                           + d_ref[0, :, cols].astype(F32) * dsk_ref[0, :, cols])

    ngrp = nrows // NSEGS

    def unfold(j, carry):
        gi = pl.program_id(1) * ngrp + j
        r0 = pl.multiple_of(j * NSEGS, NSEGS)
        tok = yb_ref[pl.ds(r0, NSEGS), :].reshape(NSEGS * CHUNK, w)

        @pl.when(gi < seglen)
        def _():
            for s in range(NSEGS):
                t0 = pl.multiple_of((s * seglen + gi) * CHUNK, CHUNK)
                y_ref[pl.ds(t0, CHUNK), :] = tok[s * CHUNK:(s + 1) * CHUNK]

        @pl.when(gi >= seglen)
        def _():
            t0 = pl.multiple_of((gi - seglen) * NSEGS * CHUNK, NSEGS * CHUNK)
            yc_ref[pl.ds(t0, NSEGS * CHUNK), :] = tok

        return carry

    lax.fori_loop(0, ngrp, unfold, 0)


def _s5_tables(lam_re, lam_im, log_dt, b_re, b_im, c_re, c_im, d_skip, *, seglen):
    t = CHUNK
    ng, ns = lam_re.shape[1], lam_re.shape[2]
    nh = b_re.shape[-1]
    gp = ng * ns
    g8 = LANE // nh
    na = ng // g8
    nb = g8 // 2
    assert 2 * ns == LANE and nb % PAIRS == 0
    dt = jnp.exp(log_dt)[..., None]
    a, b = (lam_re * dt).reshape(2, gp), (lam_im * dt).reshape(2, gp)

    def lam_pow(m):
        mm = jnp.asarray(m, F32).reshape(-1, 1, 1)
        mag = jnp.exp(a[None] * mm)
        return mag * jnp.cos(b[None] * mm), mag * jnp.sin(b[None] * mm)

    pr, pi = lam_pow(np.arange(t + 1))
    lr, li = lam_re.reshape(2, gp), lam_im.reshape(2, gp)
    nr, ni = pr[1] - 1.0, pi[1]
    den = lr * lr + li * li
    qr, qi = (nr * lr + ni * li) / den, (ni * lr - nr * li) / den
    bt_re = b_re.transpose(0, 3, 1, 2).reshape(2, nh, gp)
    bt_im = b_im.transpose(0, 3, 1, 2).reshape(2, nh, gp)
    bb_re = qr[:, None] * bt_re - qi[:, None] * bt_im
    bb_im = qr[:, None] * bt_im + qi[:, None] * bt_re
    e_re = pr[:t, :, None] * bb_re[None] - pi[:t, :, None] * bb_im[None]
    e_im = pr[:t, :, None] * bb_im[None] + pi[:t, :, None] * bb_re[None]
    ct_re = c_re.transpose(0, 2, 1, 3).reshape(2, nh, gp)
    ct_im = c_im.transpose(0, 2, 1, 3).reshape(2, nh, gp)

    sl = slice(1, t + 1)
    pw_re = jnp.stack([pr[sl, 0], pr[sl, 1][::-1]])[:, :, None]
    pw_im = jnp.stack([pi[sl, 0], pi[sl, 1][::-1]])[:, :, None]
    wy_re = ct_re[:, None] * pw_re - ct_im[:, None] * pw_im
    wy_im = ct_re[:, None] * pw_im + ct_im[:, None] * pw_re
    syt = jnp.stack([wy_re, -wy_im], axis=1).reshape(2, 2, t * nh, gp)
    mr, mi = lam_pow([t, t * seglen])
    mu = jnp.stack([mr, mi], axis=2).reshape(2, 4, na, nb, LANE).transpose(2, 3, 0, 1, 4)
    dsk = jnp.tile(d_skip.reshape(na, 1, LANE), (1, 1, t))
    return e_re, e_im, ct_re, ct_im, syt, mu, dsk


def _s5_operators(params, seglen):
    e_re, e_im, ct_re, ct_im, syt, mu, dsk = jax.vmap(functools.partial(_s5_tables, seglen=seglen))(*params)
    depth, t, _, nh, gp = e_re.shape
    na = mu.shape[1]
    tw = gp // na
    nlag = 2 * CHUNK - 1
    kr = pl.pallas_call(
        _s5_taps_kernel,
        out_shape=jax.ShapeDtypeStruct((depth, na, nlag, LANE, LANE), BF16),
        grid=(depth, na),
        in_specs=[pl.BlockSpec((None, t, 2, nh, tw), lambda l, i: (l, 0, 0, 0, i)),
                  pl.BlockSpec((None, t, 2, nh, tw), lambda l, i: (l, 0, 0, 0, i)),
                  pl.BlockSpec((None, 2, nh, tw), lambda l, i: (l, 0, 0, i)),
                  pl.BlockSpec((None, 2, nh, tw), lambda l, i: (l, 0, 0, i))],
        out_specs=pl.BlockSpec((None, 1, nlag, LANE, LANE), lambda l, i: (l, i, 0, 0, 0)),
        compiler_params=_params(("arbitrary", "arbitrary")),
        name="s5_taps",
    )(e_re, e_im, ct_re, ct_im)
    r_mat = np.kron(np.eye(CHUNK, dtype=np.float32), np.tile(np.eye(nh, dtype=np.float32), (1, LANE // nh)))
    return kr, e_re, e_im, syt, jnp.asarray(r_mat, F32).astype(BF16), mu, dsk


def _s5(u, uc, li, ops):
    kr, e_re, e_im, syt, r_mat, mu, dsk = ops
    l, d_ssm = u.shape
    lc = uc.shape[0]
    na, nb = mu.shape[1], mu.shape[2]
    fw = CHUNK * LANE
    nlat, ncc = l // CHUNK, lc // CHUNK
    nc = nlat + ncc
    seglen = nlat // NSEGS
    sw = PAIRS * 4 * LANE
    assert seglen * NSEGS == nlat and ncc * CHUNK == lc
    d, xp = pl.pallas_call(
        functools.partial(_s5_state_kernel, seglen=seglen, ncc=ncc),
        out_shape=(jax.ShapeDtypeStruct((na, nc, fw), BF16), jax.ShapeDtypeStruct((na, nc, nb * 4 * LANE), BF16)),
        grid=(na, nb // PAIRS),
        in_specs=[
            pl.BlockSpec((l, LANE), lambda a, b: (0, a)),
            pl.BlockSpec((lc, LANE), lambda a, b: (0, a)),
            pl.BlockSpec((None,) + e_re.shape[1:4] + (PAIRS * LANE,), lambda a, b: (li, 0, 0, 0, a * (nb // PAIRS) + b)),
            pl.BlockSpec((None,) + e_im.shape[1:4] + (PAIRS * LANE,), lambda a, b: (li, 0, 0, 0, a * (nb // PAIRS) + b)),
            pl.BlockSpec((None, 1, PAIRS, 2, 4, LANE), lambda a, b: (li, a, b, 0, 0, 0)),
        ],
        out_specs=(pl.BlockSpec((1, nc, fw), lambda a, b: (a, 0, 0)),
                   pl.BlockSpec((1, nc, sw), lambda a, b: (a, 0, b))),
        scratch_shapes=[pltpu.VMEM((fw, sw), BF16),
                        pltpu.VMEM((PAIRS * 4, nc, LANE), F32), pltpu.VMEM((PAIRS * 4, nc, LANE), F32)],
        compiler_params=_params(("arbitrary", "arbitrary")),
        name="s5_state",
    )(u, uc, e_re, e_im, mu)
    rbs = max(r for r in range(NSEGS, min(nc, 512) + 1, NSEGS) if nc % r == 0)
    nrb = nc // rbs
    assert NSEGS % 16 == 0 and ncc % NSEGS == 0
    return pl.pallas_call(
        functools.partial(_s5_readout_kernel, seglen=seglen),
        out_shape=(jax.ShapeDtypeStruct((l, d_ssm), F32), jax.ShapeDtypeStruct((lc, d_ssm), F32)),
        grid=(na, nrb),
        in_specs=[
            pl.BlockSpec((1, rbs, fw), lambda a, r: (a, r, 0)),
            pl.BlockSpec((1, rbs, nb * 4 * LANE), lambda a, r: (a, r, 0)),
            pl.BlockSpec((None, 1) + kr.shape[2:], lambda a, r: (li, a, 0, 0, 0)),
            pl.BlockSpec((None,) + syt.shape[1:4] + (nb * LANE,), lambda a, r: (li, 0, 0, 0, a)),
            _const_spec(r_mat.shape),
            pl.BlockSpec((None, 1, 1, fw), lambda a, r: (li, a, 0, 0)),
        ],
        out_specs=(pl.BlockSpec((l, LANE), lambda a, r: (0, a)), pl.BlockSpec((lc, LANE), lambda a, r: (0, a))),
        scratch_shapes=[pltpu.VMEM((fw, fw), BF16), pltpu.VMEM((nb * 4 * LANE, fw), BF16),
                        pltpu.VMEM((rbs, fw), F32), pltpu.VMEM((nb * 4 * LANE, syt.shape[3]), BF16)],
        compiler_params=_params(("arbitrary", "arbitrary")),
        name="s5_readout",
    )(d, xp, kr, syt, r_mat, dsk)


def _fft1_kernel(p_ref, q_ref, m_ref, o_ref, s_ref, *, nb):
    for n in range(nb):
        rhs = jnp.concatenate([p_ref[:, n, :], q_ref[:, n, :]], axis=0).astype(BF16)
        s_ref[:, n, :] = _dot(m_ref[...], rhs)
    o_ref[...] = s_ref[...].astype(BF16)


def _fft2_kernel(yr_ref, yi_ref, ca_ref, sa_ref, cb_ref, sb_ref, o_ref, s_ref, *, kb):
    cb, sb = cb_ref[...], sb_ref[...]
    for k in range(kb):
        ca, sa = ca_ref[k:k + 1, :], sa_ref[k:k + 1, :]
        g = jnp.concatenate([ca * cb - sa * sb, sa * cb + ca * sb], axis=1).astype(BF16)
        rhs = jnp.concatenate([yr_ref[k], yi_ref[k]], axis=0)
        s_ref[:, k, :] = _dot(g, rhs)
    o_ref[...] = s_ref[...].astype(BF16)


def _fft_ctx_kernel(p_ref, q_ref, m_ref, o_ref):
    rhs = jnp.concatenate([p_ref[...], q_ref[...]], axis=0).astype(BF16)
    o_ref[...] = _dot(m_ref[...], rhs).astype(BF16)


def _fnet_tables(l, lc):
    r = math.isqrt(l)
    assert r * r == l
    k = np.arange(r)
    th = 2.0 * np.pi * (np.outer(k, k) % r) / r
    c, s = np.cos(th), np.sin(th)
    m1 = np.block([[c, -s], [-s, -c]])
    ta = 2.0 * np.pi * np.outer(k, k) / l
    scale = 1.0 / math.sqrt(l)
    tw2 = tuple(jnp.asarray(v, F32) for v in (np.cos(ta), np.sin(ta), c * scale, s * scale))
    kc = np.arange(lc)
    thc = 2.0 * np.pi * (np.outer(kc, kc) % lc) / lc
    mc = np.concatenate([np.cos(thc), -np.sin(thc)], axis=1) / math.sqrt(lc)
    return (jnp.asarray(m1, F32).astype(BF16), tw2, jnp.asarray(mc, F32).astype(BF16))


def _fnet_lat(p, q, m1, tw2):
    l, c = p.shape
    r = m1.shape[0] // 2
    assert r * r == l
    p3 = p.reshape(r, r, c)
    q3 = q.reshape(r, r, c)
    nb = 16
    y1 = pl.pallas_call(
        functools.partial(_fft1_kernel, nb=nb),
        out_shape=jax.ShapeDtypeStruct((2 * r, r, c), BF16),
        grid=(r // nb,),
        in_specs=[pl.BlockSpec((r, nb, c), lambda j: (0, j, 0)),
                  pl.BlockSpec((r, nb, c), lambda j: (0, j, 0)),
                  _const_spec(m1.shape)],
        out_specs=pl.BlockSpec((2 * r, nb, c), lambda j: (0, j, 0)),
        scratch_shapes=[pltpu.VMEM((2 * r, nb, c), F32)],
        compiler_params=_params(("arbitrary",)),
        name="fft_stage1",
    )(p3, q3, m1)
    kb = 16
    ca, sa, cb, sb = tw2
    out = pl.pallas_call(
        functools.partial(_fft2_kernel, kb=kb),
        out_shape=jax.ShapeDtypeStruct((r, r, c), BF16),
        grid=(r // kb,),
        in_specs=[pl.BlockSpec((kb, r, c), lambda j: (j, 0, 0)),
                  pl.BlockSpec((kb, r, c), lambda j: (r // kb + j, 0, 0)),
                  pl.BlockSpec((kb, r), lambda j: (j, 0)),
                  pl.BlockSpec((kb, r), lambda j: (j, 0)),
                  _const_spec(cb.shape), _const_spec(sb.shape)],
        out_specs=pl.BlockSpec((r, kb, c), lambda j: (0, j, 0)),
        scratch_shapes=[pltpu.VMEM((r, kb, c), F32)],
        compiler_params=_params(("arbitrary",)),
        name="fft_stage2",
    )(y1, y1, ca, sa, cb, sb)
    return out.reshape(l, c)


def _fnet_ctx(pc, qc, mc):
    return pl.pallas_call(
        _fft_ctx_kernel,
        out_shape=jax.ShapeDtypeStruct(pc.shape, BF16),
        grid=(1,),
        in_specs=[_whole_spec(pc.shape), _whole_spec(qc.shape), _const_spec(mc.shape)],
        out_specs=_whole_spec(pc.shape),
        compiler_params=_params(("arbitrary",)),
        name="fft_ctx",
    )(pc, qc, mc)


def _pos_tables(l, d):
    quarter = d // 4
    omega = 1.0 / (POS_BASE ** (jnp.arange(quarter, dtype=F32) / quarter))

    def enc(pv):
        ang = pv[:, None] * omega[None, :]
        return jnp.concatenate([jnp.sin(ang), jnp.cos(ang)], axis=-1)

    renc = enc(jnp.arange(l // GRID_W, dtype=F32))
    cenc = jnp.tile(enc(jnp.arange(GRID_W, dtype=F32)), (TM // GRID_W, 1))
    return renc, cenc


def kernel(x, c, ctx, c_ctx, w_ada, b_ada, norm_g, ffn1_gu, ffn1_down, ffn2_gu, ffn2_down, w_in, w_out,
           ssm_lam_re, ssm_lam_im, ssm_log_dt, ssm_b_re, ssm_b_im, ssm_c_re, ssm_c_im, ssm_d, w_glu, w_fmix):
    bsz, l, d = x.shape
    lc = ctx.shape[1]
    depth = w_ada.shape[0]
    d_ssm = w_glu.shape[1]
    head = w_fmix.shape[2]
    assert bsz == 1 and l % TM == 0 and TM % GRID_W == 0
    seglen = (l // CHUNK) // NSEGS

    mods = _modulation(c_ctx, c, w_ada, b_ada)
    pos = _pos_tables(l, d)
    m1, tw2, mc = _fnet_tables(l, lc)
    kc = np.arange(head)
    thc = 2.0 * np.pi * (np.outer(kc, kc) % head) / head
    cs = jnp.asarray(np.concatenate([np.cos(thc), np.sin(thc)], axis=1) / math.sqrt(head), F32).astype(BF16)
    win, wout, wglu, wfm = (a.astype(BF16) for a in (w_in, w_out, w_glu, w_fmix))
    wgu1, wd1 = ffn1_gu[0].astype(BF16), ffn1_down[0].astype(BF16)

    ops = _s5_operators((ssm_lam_re, ssm_lam_im, ssm_log_dt, ssm_b_re, ssm_b_im, ssm_c_re, ssm_c_im, ssm_d), seglen)

    xl, xc = x[0], ctx[0]
    for li in range(depth):
        last = li == depth - 1
        xl, u, p, q, xc, uc, pc, qc, wgu2, wd2 = _layer_in(xl, xc, pos if li == 0 else None, li, mods, norm_g, wgu1, wd1,
                                                           win, cs, (ffn2_gu, ffn2_down), d_ssm=d_ssm)
        ys, ysc = _s5(u, uc, li, ops)
        yf = _fnet_lat(p, q, m1, tw2)
        ctx_in = None if last else (xc, ysc, _fnet_ctx(pc, qc, mc))
        xl, xc, wgu1, wd1 = _layer_out((xl, ys, yf), ctx_in, li, mods, norm_g, wglu, wfm, wout, wgu2, wd2,
                                       (ffn1_gu, ffn1_down))
    return xl[None]
```

```python
import functools
import math

import numpy as np
import jax
import jax.numpy as jnp
from jax import lax
from jax.experimental import pallas as pl
from jax.experimental.pallas import tpu as pltpu

F32 = jnp.float32
BF16 = jnp.bfloat16

LANE = 128
SUBLANE = 8
VMEM_LIMIT = 56 * 1024 * 1024

EPS = 1e-6
GRID_W = 64
POS_BASE = 10000.0
N_MOD = 9
N_FNET_HEADS = 4
CHUNK = 16
TM = 512
CTX, LAT = 0, 1


def _dot(a, b):
    return jnp.dot(a, b, preferred_element_type=F32)


def _params(sem=None):
    return pltpu.CompilerParams(dimension_semantics=sem, vmem_limit_bytes=VMEM_LIMIT)


def _const_spec(shape):
    nd = len(shape)
    return pl.BlockSpec(shape, lambda *_: (0,) * nd, pipeline_mode=pl.Buffered(1))


def _rms(x, g):
    ms = jnp.mean(x * x, axis=-1, keepdims=True)
    return x * lax.rsqrt(ms + EPS) * g


def _pre(x, g, mod_ref, who, k):
    return _rms(x, g) * (1.0 + mod_ref[who, 3 * k + 1:3 * k + 2, :]) + mod_ref[who, 3 * k:3 * k + 1, :]


def _post(x, y, g, mod_ref, who, k, weight):
    return x + (weight * mod_ref[who, 3 * k + 2:3 * k + 3, :]) * _rms(y, g)


def _on_last_step(fn):
    pl.when(pl.program_id(0) == pl.num_programs(0) - 1)(fn)


def _sigmoid(x):
    return 1.0 / (1.0 + jnp.exp(-x))


def _gelu_tanh(x):
    return 0.5 * x * (1.0 + jnp.tanh(math.sqrt(2.0 / math.pi) * (x + 0.044715 * (x * x * x))))


def _mod_kernel(cb_ref, w_ref, b_ref, o_ref, s_ref, *, tn):
    rows, n_out = w_ref.shape[1], w_ref.shape[2]
    nb = tn // LANE
    j = pl.program_id(1)

    @pl.when((pl.program_id(0) == 0) & (j == 0))
    def _():
        cv = cb_ref[...]
        s_ref[...] = cv * _sigmoid(cv)

    @pl.when(j == 0)
    def _():
        o_ref[0] = jnp.broadcast_to(b_ref[0], (2, n_out))

    for grp in range(n_out // tn):
        def body(t, accs):
            d0 = pl.multiple_of(t * SUBLANE, SUBLANE)
            s = [s_ref[r, pl.ds(j * rows + d0, SUBLANE), :] for r in range(2)]
            new = list(accs)
            for jj in range(nb):
                w = w_ref[0, pl.ds(d0, SUBLANE), grp * tn + jj * LANE:grp * tn + (jj + 1) * LANE]
                for r in range(2):
                    new[r * nb + jj] = accs[r * nb + jj] + w * s[r]
            return tuple(new)

        init = tuple(jnp.zeros((SUBLANE, LANE), F32) for _ in range(2 * nb))
        accs = lax.fori_loop(0, rows // SUBLANE, body, init, unroll=4)
        for r in range(2):
            for jj in range(nb):
                lanes = slice(grp * tn + jj * LANE, grp * tn + (jj + 1) * LANE)
                o_ref[0, r:r + 1, lanes] = o_ref[0, r:r + 1, lanes] + jnp.sum(accs[r * nb + jj], axis=0, keepdims=True)


def _modulation(c_ctx, c, w_ada, b_ada):
    depth, d_model, n_out = w_ada.shape
    tn = 9 * LANE
    rows = 256
    assert n_out % tn == 0 and d_model % rows == 0
    cb = jnp.broadcast_to(jnp.stack([c_ctx, c[0]])[:, :, None], (2, d_model, LANE))
    out = pl.pallas_call(
        functools.partial(_mod_kernel, tn=tn),
        out_shape=jax.ShapeDtypeStruct((depth, 2, n_out), F32),
        grid=(depth, d_model // rows),
        in_specs=[
            pl.BlockSpec((2, d_model, LANE), lambda l, j: (0, 0, 0)),
            pl.BlockSpec((1, rows, n_out), lambda l, j: (l, j, 0)),
            pl.BlockSpec((1, 1, n_out), lambda l, j: (l, 0, 0)),
        ],
        out_specs=pl.BlockSpec((1, 2, n_out), lambda l, j: (l, 0, 0)),
        scratch_shapes=[pltpu.VMEM((2, d_model, LANE), F32)],
        compiler_params=_params(("arbitrary", "arbitrary")),
        name="adaln_mod",
    )(cb, w_ada, b_ada.reshape(depth, 1, n_out))
    return out.reshape(depth, 2, N_MOD, d_model)


SPLIT = 2


def _row_chunks(n, parts):
    step = n // parts
    return [slice(i * step, (i + 1) * step) for i in range(parts)]


def _ffn_stage(xs, who, mod_ref, g_ref, wgu_ref, wd_ref, k, gi):
    f = wd_ref.shape[0]
    hs = [_pre(x, g_ref[gi:gi + 1, :], mod_ref, who, k).astype(BF16) for x in xs]
    gus = [_dot(h, wgu_ref[...]) for h in hs]
    acts = [(gu[:, :f] * _sigmoid(gu[:, :f]) * gu[:, f:]).astype(BF16) for gu in gus]
    ys = [_dot(a, wd_ref[...]) for a in acts]
    return [_post(x, y, g_ref[gi + 1:gi + 2, :], mod_ref, who, k, 0.5) for x, y in zip(xs, ys)]


def _inproj_stage(xs, rows, who, mod_ref, g_ref, win_ref, cs_ref, u_ref, p_ref, q_ref):
    d_ssm = u_ref.shape[1]
    head = cs_ref.shape[0]
    hs = [_pre(x, g_ref[2:3, :], mod_ref, who, 1).astype(BF16) for x in xs]
    hhs = [_dot(h, win_ref[...]) for h in hs]
    for r, hh in zip(rows, hhs):
        u_ref[r, :] = hh[:, :d_ssm].astype(BF16)
        for n in range(N_FNET_HEADS):
            lo = d_ssm + n * head
            pq = _dot(hh[:, lo:lo + head].astype(BF16), cs_ref[...])
            p_ref[r, n * head:(n + 1) * head] = pq[:, :head]
            q_ref[r, n * head:(n + 1) * head] = pq[:, head:]


def _outproj_stage(xs, yss, yfs, who, mod_ref, g_ref, wglu_ref, wfm_ref, wout_ref):
    head = wfm_ref.shape[1]
    hs = [_gelu_tanh(ys.astype(F32)) for ys in yss]
    gls = [_dot(h.astype(BF16), wglu_ref[...]) for h in hs]
    mixed = []
    for h, gl, yf in zip(hs, gls, yfs):
        yf = yf.astype(BF16)
        parts = [(h * _sigmoid(gl)).astype(BF16)]
        parts += [_dot(yf[:, n * head:(n + 1) * head], wfm_ref[n]).astype(BF16) for n in range(N_FNET_HEADS)]
        mixed.append(jnp.concatenate(parts, axis=1))
    ys = [_dot(m, wout_ref[...]) for m in mixed]
    return [_post(x, y, g_ref[3:4, :], mod_ref, who, 1, 1.0) for x, y in zip(xs, ys)]


def _cast_chunks(src_refs, dst_refs):
    for src_ref, dst_ref in zip(src_refs, dst_refs):
        dst_ref[...] = src_ref[...].astype(BF16)


def _layer_in_kernel(*refs, first):
    if first:
        (x_ref, xc_ref, renc_ref, cenc_ref, mod_ref, g_ref, wgu_ref, wd_ref, win_ref, cs_ref, ng_ref, nd_ref,
         o_ref, u_ref, p_ref, q_ref, oc_ref, uc_ref, pc_ref, qc_ref, ngo_ref, ndo_ref, xs_ref) = refs
        tm = x_ref.shape[0]
        half = renc_ref.shape[1]
        r0 = pl.program_id(0) * (tm // GRID_W)
        xs_ref[:, half:] = x_ref[:, half:] + cenc_ref[...]
        for q in range(tm // GRID_W):
            xs_ref[q * GRID_W:(q + 1) * GRID_W, :half] = (
                x_ref[q * GRID_W:(q + 1) * GRID_W, :half] + renc_ref[pl.ds(r0 + q, 1), :])
        src_ref = xs_ref
    else:
        (x_ref, xc_ref, mod_ref, g_ref, wgu_ref, wd_ref, win_ref, cs_ref, ng_ref, nd_ref,
         o_ref, u_ref, p_ref, q_ref, oc_ref, uc_ref, pc_ref, qc_ref, ngo_ref, ndo_ref) = refs
        src_ref = x_ref
    _cast_chunks((ng_ref, nd_ref), (ngo_ref, ndo_ref))
    rows = _row_chunks(x_ref.shape[0], SPLIT)
    x1 = _ffn_stage([src_ref[r, :] for r in rows], LAT, mod_ref, g_ref, wgu_ref, wd_ref, 0, 0)
    for r, v in zip(rows, x1):
        o_ref[r, :] = v
    _inproj_stage(x1, rows, LAT, mod_ref, g_ref, win_ref, cs_ref, u_ref, p_ref, q_ref)

    def _():
        rows_c = _row_chunks(xc_ref.shape[0], 1)
        xc1 = _ffn_stage([xc_ref[...]], CTX, mod_ref, g_ref, wgu_ref, wd_ref, 0, 0)
        oc_ref[...] = xc1[0]
        _inproj_stage(xc1, rows_c, CTX, mod_ref, g_ref, win_ref, cs_ref, uc_ref, pc_ref, qc_ref)
    _on_last_step(_)


def _layer_out_kernel(*refs, with_ctx):
    if with_ctx:
        (x_ref, ys_ref, yf_ref, xc_ref, ysc_ref, yfc_ref, mod_ref, g_ref, wglu_ref, wfm_ref, wout_ref,
         wgu_ref, wd_ref, ng_ref, nd_ref, o_ref, oc_ref, ngo_ref, ndo_ref) = refs
        _cast_chunks((ng_ref, nd_ref), (ngo_ref, ndo_ref))
    else:
        x_ref, ys_ref, yf_ref, mod_ref, g_ref, wglu_ref, wfm_ref, wout_ref, wgu_ref, wd_ref, o_ref = refs
    rows = _row_chunks(x_ref.shape[0], SPLIT)
    x2 = _outproj_stage([x_ref[r, :] for r in rows], [ys_ref[r, :] for r in rows], [yf_ref[r, :] for r in rows],
                        LAT, mod_ref, g_ref, wglu_ref, wfm_ref, wout_ref)
    x3 = _ffn_stage(x2, LAT, mod_ref, g_ref, wgu_ref, wd_ref, 2, 4)
    for r, v in zip(rows, x3):
        o_ref[r, :] = v
    if with_ctx:
        def _():
            xc2 = _outproj_stage([xc_ref[...]], [ysc_ref[...]], [yfc_ref[...]], CTX, mod_ref, g_ref, wglu_ref,
                                 wfm_ref, wout_ref)
            oc_ref[...] = _ffn_stage(xc2, CTX, mod_ref, g_ref, wgu_ref, wd_ref, 2, 4)[0]
        _on_last_step(_)


def _tile_spec(cols):
    return pl.BlockSpec((TM, cols), lambda i: (i, 0))


def _whole_spec(shape):
    nd = len(shape)
    return pl.BlockSpec(shape, lambda *_: (0,) * nd)


def _layer_spec(arr, li):
    nd = arr.ndim - 1
    return pl.BlockSpec((None,) + arr.shape[1:], lambda *_: (li,) + (0,) * nd, pipeline_mode=pl.Buffered(1))


def _cast_specs(arr, li, nt):
    rows, cols = arr.shape[1:]
    k = 1
    while (rows * k) % nt or (rows * k // nt) % 16:
        k *= 2
    rc = rows * k // nt
    assert rows % rc == 0 and nt % k == 0
    return (pl.BlockSpec((None, rc, cols), lambda i: (li, i // k, 0)), pl.BlockSpec((rc, cols), lambda i: (i // k, 0)),
            jax.ShapeDtypeStruct((rows, cols), BF16))


def _layer_in(xl, xc, pos, li, mods, gains, wgu, wd, win, cs, nxt, *, d_ssm):
    l, d = xl.shape
    lc = xc.shape[0]
    first = pos is not None
    d_f = win.shape[2] - d_ssm
    widths = (d, d_ssm, d_f, d_f)
    dtypes = (F32, BF16, F32, F32)
    stacked = [mods, gains]
    nt = l // TM
    casts = [_cast_specs(a, li, nt) for a in nxt]
    return pl.pallas_call(
        functools.partial(_layer_in_kernel, first=first),
        out_shape=[jax.ShapeDtypeStruct((l, wd_), dt_) for wd_, dt_ in zip(widths, dtypes)]
        + [jax.ShapeDtypeStruct((lc, wd_), dt_) for wd_, dt_ in zip(widths, dtypes)] + [c[2] for c in casts],
        grid=(nt,),
        in_specs=([_tile_spec(d), _whole_spec(xc.shape)] + ([_const_spec(a.shape) for a in pos] if first else [])
                  + [_layer_spec(a, li) for a in stacked] + [_const_spec(wgu.shape), _const_spec(wd.shape)]
                  + [_layer_spec(win, li), _const_spec(cs.shape)] + [c[0] for c in casts]),
        out_specs=([_tile_spec(wd_) for wd_ in widths] + [_whole_spec((lc, wd_)) for wd_ in widths]
                   + [c[1] for c in casts]),
        scratch_shapes=[pltpu.VMEM((TM, d), F32)] if first else [],
        compiler_params=_params(("arbitrary",)),
        name="layer_in",
    )(xl, xc, *(pos if first else ()), *stacked, wgu, wd, win, cs, *nxt)


def _layer_out(lat, ctx, li, mods, gains, wglu, wfm, wout, wgu, wd, nxt):
    xl, ys, yf = lat
    l, d = xl.shape
    with_ctx = ctx is not None
    ctx = list(ctx) if with_ctx else []
    stacked = [mods, gains, wglu, wfm, wout]
    nt = l // TM
    casts = [_cast_specs(a, li + 1, nt) for a in nxt] if with_ctx else []
    outs = pl.pallas_call(
        functools.partial(_layer_out_kernel, with_ctx=with_ctx),
        out_shape=([jax.ShapeDtypeStruct(xl.shape, F32)] + ([jax.ShapeDtypeStruct(ctx[0].shape, F32)] if with_ctx else [])
                   + [c[2] for c in casts]),
        grid=(nt,),
        in_specs=([_tile_spec(d), _tile_spec(ys.shape[1]), _tile_spec(yf.shape[1])]
                  + [_whole_spec(a.shape) for a in ctx] + [_layer_spec(a, li) for a in stacked]
                  + [_const_spec(wgu.shape), _const_spec(wd.shape)] + [c[0] for c in casts]),
        out_specs=([_tile_spec(d)] + ([_whole_spec(ctx[0].shape)] if with_ctx else []) + [c[1] for c in casts]),
        compiler_params=_params(("arbitrary",)),
        name="layer_out",
    )(xl, ys, yf, *ctx, *stacked, wgu, wd, *(nxt if with_ctx else ()))
    return tuple(outs) if with_ctx else (outs[0], None, None, None)


NSEGS = 16
PAIRS = 4


def _cmul_add(ar, ai, xr, xi, zr, zi):
    return ar * xr - ai * xi + zr, ar * xi + ai * xr + zi


def _dot_nt(a, b):
    def nt(x, y):
        return lax.dot_general(x, y, (((1,), (1,)), ((), ())), preferred_element_type=F32)
    a_hi, b_hi = a.astype(BF16), b.astype(BF16)
    a_lo, b_lo = (a - a_hi.astype(F32)).astype(BF16), (b - b_hi.astype(F32)).astype(BF16)
    return nt(a_hi, b_hi) + (nt(a_hi, b_lo) + nt(a_lo, b_hi))


def _s5_taps_kernel(e_re_ref, e_im_ref, ct_re_ref, ct_im_ref, kr_ref):
    t = CHUNK
    nh = ct_re_ref.shape[1]
    g8 = LANE // nh
    gw = e_re_ref.shape[3] // g8
    same = (lax.broadcasted_iota(jnp.int32, (LANE, g8 * gw), 0) // nh
            == lax.broadcasted_iota(jnp.int32, (LANE, g8 * gw), 1) // gw)
    taps = []
    for d in range(2):
        cre = jnp.where(same, jnp.concatenate([ct_re_ref[d]] * g8, axis=0), 0.0)
        cim = jnp.where(same, jnp.concatenate([ct_im_ref[d]] * g8, axis=0), 0.0)
        er = jnp.concatenate([e_re_ref[m, d] for m in range(t)], axis=0)
        ei = jnp.concatenate([e_im_ref[m, d] for m in range(t)], axis=0)
        taps.append(_dot_nt(er, cre) - _dot_nt(ei, cim))
    blk = (lax.broadcasted_iota(jnp.int32, (LANE, LANE), 0) // nh
           == lax.broadcasted_iota(jnp.int32, (LANE, LANE), 1) // nh)
    for lag in range(-(t - 1), t):
        if lag > 0:
            src = taps[0][lag * nh:(lag + 1) * nh]
        elif lag < 0:
            src = taps[1][-lag * nh:(1 - lag) * nh]
        else:
            src = taps[0][:nh] + taps[1][:nh]
        kr_ref[0, lag + t - 1] = jnp.where(blk, jnp.concatenate([src] * g8, axis=0), 0.0).astype(BF16)


def _s5_state_kernel(u_ref, uc_ref, e_re_ref, e_im_ref, mu_ref, d_ref, xp_ref, wz_ref, z_ref, xs_ref, *, seglen, ncc):
    w = LANE
    nc = d_ref.shape[1]
    nlat = NSEGS * seglen
    nv = NSEGS // SUBLANE
    nh = e_re_ref.shape[2]
    ns = w // 2

    @pl.when(pl.program_id(1) == 0)
    def _():
        def fold(i, carry):
            r0 = pl.multiple_of(i * NSEGS, NSEGS)
            tiles = [u_ref[pl.ds(pl.multiple_of((s * seglen + i) * CHUNK, CHUNK), CHUNK), :] for s in range(NSEGS)]
            d_ref[0, pl.ds(r0, NSEGS), :] = jnp.concatenate(tiles, axis=0).reshape(NSEGS, CHUNK * w)
            return carry

        lax.fori_loop(0, seglen, fold, 0)
        d_ref[0, nlat:nlat + ncc, :] = uc_ref[...].reshape(ncc, CHUNK * w)

    wz_ref[...] = jnp.zeros(wz_ref.shape, BF16)
    first = (lax.broadcasted_iota(jnp.int32, (nh, 4 * w), 1) % w) < ns
    for pb in range(PAIRS):
        pair = pl.program_id(1) * PAIRS + pb
        lanes = slice(pb * w, (pb + 1) * w)
        for i in range(CHUNK):
            ez = jnp.concatenate([e_re_ref[CHUNK - 1 - i, 0, :, lanes], e_im_ref[CHUNK - 1 - i, 0, :, lanes],
                                  e_re_ref[i, 1, :, lanes], e_im_ref[i, 1, :, lanes]], axis=1).astype(BF16)
            both = jnp.concatenate([jnp.where(first, ez, jnp.zeros_like(ez)),
                                    jnp.where(first, jnp.zeros_like(ez), ez)], axis=0)
            r0 = pl.multiple_of(i * w + pair * 2 * nh, 2 * nh)
            wz_ref[pl.ds(r0, 2 * nh), pb * 4 * w:(pb + 1) * 4 * w] = both

    for pb in range(PAIRS):
        zz = _dot(d_ref[0], wz_ref[:, pb * 4 * w:(pb + 1) * 4 * w])
        for c in range(4):
            z_ref[pb * 4 + c] = zz[:, c * w:(c + 1) * w]

    mu = [[mu_ref[0, pb, 0, c:c + 1, :] for c in range(4)] for pb in range(PAIRS)]
    mun = [[mu_ref[0, pb, 1, c:c + 1, :] for c in range(4)] for pb in range(PAIRS)]

    pre = []
    for pb in range(PAIRS):
        zc = [z_ref[pb * 4 + c, nlat:nlat + ncc, :] for c in range(4)]
        sf = (jnp.zeros((1, w), F32), jnp.zeros((1, w), F32))
        sb = (jnp.zeros((1, w), F32), jnp.zeros((1, w), F32))
        for j in range(ncc):
            jb = ncc - 1 - j
            xs_ref[pb * 4 + 0, nlat + j:nlat + j + 1, :] = sf[0]
            xs_ref[pb * 4 + 1, nlat + j:nlat + j + 1, :] = sf[1]
            xs_ref[pb * 4 + 2, nlat + jb:nlat + jb + 1, :] = sb[0]
            xs_ref[pb * 4 + 3, nlat + jb:nlat + jb + 1, :] = sb[1]
            sf = _cmul_add(mu[pb][0], mu[pb][1], sf[0], sf[1], zc[0][j:j + 1], zc[1][j:j + 1])
            sb = _cmul_add(mu[pb][2], mu[pb][3], sb[0], sb[1], zc[2][jb:jb + 1], zc[3][jb:jb + 1])
        pre.append((sf, sb))

    mub = [[jnp.broadcast_to(m, (SUBLANE, w)) for m in mu[pb]] for pb in range(PAIRS)]

    def rows(i, v):
        return pl.ds(pl.multiple_of(i * NSEGS + v * SUBLANE, SUBLANE), SUBLANE)

    def step(i, st):
        ib = seglen - 1 - i
        new = []
        for pb in range(PAIRS):
            for v in range(nv):
                k = (pb * nv + v) * 4
                m = mub[pb]
                fr, fi = _cmul_add(m[0], m[1], st[k], st[k + 1],
                                   z_ref[pb * 4 + 0, rows(i, v), :], z_ref[pb * 4 + 1, rows(i, v), :])
                br, bi = _cmul_add(m[2], m[3], st[k + 2], st[k + 3],
                                   z_ref[pb * 4 + 2, rows(ib, v), :], z_ref[pb * 4 + 3, rows(ib, v), :])
                new += [fr, fi, br, bi]
        return tuple(new)

    zero = jnp.zeros((SUBLANE, w), F32)
    fin = lax.fori_loop(0, seglen, step, (zero,) * (PAIRS * nv * 4))

    carry = []
    for pb in range(PAIRS):
        cf, cb = pre[pb]
        rows_f, rows_b = [None] * NSEGS, [None] * NSEGS
        for s in range(NSEGS):
            sr = NSEGS - 1 - s
            rows_f[s] = cf
            rows_b[sr] = cb
            kf = (pb * nv + s // SUBLANE) * 4
            kb = (pb * nv + sr // SUBLANE) * 4
            sl, srl = s % SUBLANE, sr % SUBLANE
            cf = _cmul_add(mun[pb][0], mun[pb][1], cf[0], cf[1], fin[kf][sl:sl + 1, :], fin[kf + 1][sl:sl + 1, :])
            cb = _cmul_add(mun[pb][2], mun[pb][3], cb[0], cb[1],
                           fin[kb + 2][srl:srl + 1, :], fin[kb + 3][srl:srl + 1, :])
        for v in range(nv):
            seg = slice(v * SUBLANE, (v + 1) * SUBLANE)
            carry += [jnp.concatenate([r[0] for r in rows_f[seg]], axis=0),
                      jnp.concatenate([r[1] for r in rows_f[seg]], axis=0),
                      jnp.concatenate([r[0] for r in rows_b[seg]], axis=0),
                      jnp.concatenate([r[1] for r in rows_b[seg]], axis=0)]

    def step2(i, st):
        ib = seglen - 1 - i
        for pb in range(PAIRS):
            for v in range(nv):
                k = (pb * nv + v) * 4
                xs_ref[pb * 4 + 0, rows(i, v), :] = st[k]
                xs_ref[pb * 4 + 1, rows(i, v), :] = st[k + 1]
                xs_ref[pb * 4 + 2, rows(ib, v), :] = st[k + 2]
                xs_ref[pb * 4 + 3, rows(ib, v), :] = st[k + 3]
        return step(i, st)

    lax.fori_loop(0, seglen, step2, tuple(carry))
    for k in range(PAIRS * 4):
        xp_ref[0, :, k * w:(k + 1) * w] = xs_ref[k].astype(BF16)


def _s5_readout_kernel(d_ref, xp_ref, kr_ref, syt_ref, rm_ref, dsk_ref, y_ref, yc_ref, bt_ref, wy_ref, yb_ref,
                       sy_ref, *, seglen):
    w = LANE
    cw = 4 * w
    fw = CHUNK * w
    ncg = fw // cw

    @pl.when(pl.program_id(1) == 0)
    def _():
        for i in range(CHUNK):
            for j in range(CHUNK):
                bt_ref[i * w:(i + 1) * w, j * w:(j + 1) * w] = kr_ref[0, j - i + CHUNK - 1]
        for b in range(syt_ref.shape[3] // w):
            for dc in range(4):
                r0 = (b * 4 + dc) * w
                sy_ref[r0:r0 + w, :] = syt_ref[dc // 2, dc % 2, :, b * w:(b + 1) * w].T.astype(BF16)
        nrow = sy_ref.shape[0]
        nh = rm_ref.shape[0] // CHUNK
        row = lax.broadcasted_iota(jnp.int32, (nrow, cw), 0)
        row_g = 2 * (row // cw) + (row % w) // (w // 2)
        col_g = (lax.broadcasted_iota(jnp.int32, (nrow, cw), 1) % w) // nh
        same = row_g == col_g
        for cg in range(ncg):
            cols = slice(cg * cw, (cg + 1) * cw)
            wy_ref[:, cols] = jnp.where(same, _dot(sy_ref[...], rm_ref[:, cols]), 0.0).astype(BF16)

    nrows = d_ref.shape[1]
    for cg in range(ncg):
        cols = slice(cg * cw, (cg + 1) * cw)
        yb_ref[:, cols] = (_dot(d_ref[0], bt_ref[:, cols]) + _dot(xp_ref[0], wy_ref[:, cols])
                           + d_ref[0, :, cols].astype(F32) * dsk_ref[0, :, cols])

    ngrp = nrows // NSEGS

    def unfold(j, carry):
        gi = pl.program_id(1) * ngrp + j
        r0 = pl.multiple_of(j * NSEGS, NSEGS)
        tok = yb_ref[pl.ds(r0, NSEGS), :].astype(BF16).reshape(NSEGS * CHUNK, w)

        @pl.when(gi < seglen)
        def _():
            for s in range(NSEGS):
                t0 = pl.multiple_of((s * seglen + gi) * CHUNK, CHUNK)
                y_ref[pl.ds(t0, CHUNK), :] = tok[s * CHUNK:(s + 1) * CHUNK]

        @pl.when(gi >= seglen)
        def _():
            t0 = pl.multiple_of((gi - seglen) * NSEGS * CHUNK, NSEGS * CHUNK)
            yc_ref[pl.ds(t0, NSEGS * CHUNK), :] = tok

        return carry

    lax.fori_loop(0, ngrp, unfold, 0)


def _s5_tables(lam_re, lam_im, log_dt, b_re, b_im, c_re, c_im, d_skip, *, seglen):
    t = CHUNK
    ng, ns = lam_re.shape[1], lam_re.shape[2]
    nh = b_re.shape[-1]
    gp = ng * ns
    g8 = LANE // nh
    na = ng // g8
    nb = g8 // 2
    assert 2 * ns == LANE and nb % PAIRS == 0
    dt = jnp.exp(log_dt)[..., None]
    a, b = (lam_re * dt).reshape(2, gp), (lam_im * dt).reshape(2, gp)

    def lam_pow(m):
        mm = jnp.asarray(m, F32).reshape(-1, 1, 1)
        mag = jnp.exp(a[None] * mm)
        return mag * jnp.cos(b[None] * mm), mag * jnp.sin(b[None] * mm)

    pr, pi = lam_pow(np.arange(t + 1))
    lr, li = lam_re.reshape(2, gp), lam_im.reshape(2, gp)
    nr, ni = pr[1] - 1.0, pi[1]
    den = lr * lr + li * li
    qr, qi = (nr * lr + ni * li) / den, (ni * lr - nr * li) / den
    bt_re = b_re.transpose(0, 3, 1, 2).reshape(2, nh, gp)
    bt_im = b_im.transpose(0, 3, 1, 2).reshape(2, nh, gp)
    bb_re = qr[:, None] * bt_re - qi[:, None] * bt_im
    bb_im = qr[:, None] * bt_im + qi[:, None] * bt_re
    e_re = pr[:t, :, None] * bb_re[None] - pi[:t, :, None] * bb_im[None]
    e_im = pr[:t, :, None] * bb_im[None] + pi[:t, :, None] * bb_re[None]
    ct_re = c_re.transpose(0, 2, 1, 3).reshape(2, nh, gp)
    ct_im = c_im.transpose(0, 2, 1, 3).reshape(2, nh, gp)

    sl = slice(1, t + 1)
    pw_re = jnp.stack([pr[sl, 0], pr[sl, 1][::-1]])[:, :, None]
    pw_im = jnp.stack([pi[sl, 0], pi[sl, 1][::-1]])[:, :, None]
    wy_re = ct_re[:, None] * pw_re - ct_im[:, None] * pw_im
    wy_im = ct_re[:, None] * pw_im + ct_im[:, None] * pw_re
    syt = jnp.stack([wy_re, -wy_im], axis=1).reshape(2, 2, t * nh, gp)
    mr, mi = lam_pow([t, t * seglen])
    mu = jnp.stack([mr, mi], axis=2).reshape(2, 4, na, nb, LANE).transpose(2, 3, 0, 1, 4)
    dsk = jnp.tile(d_skip.reshape(na, 1, LANE), (1, 1, t))
    return e_re, e_im, ct_re, ct_im, syt, mu, dsk


def _s5_operators(params, seglen):
    e_re, e_im, ct_re, ct_im, syt, mu, dsk = jax.vmap(functools.partial(_s5_tables, seglen=seglen))(*params)
    depth, t, _, nh, gp = e_re.shape
    na = mu.shape[1]
    tw = gp // na
    nlag = 2 * CHUNK - 1
    kr = pl.pallas_call(
        _s5_taps_kernel,
        out_shape=jax.ShapeDtypeStruct((depth, na, nlag, LANE, LANE), BF16),
        grid=(depth, na),
        in_specs=[pl.BlockSpec((None, t, 2, nh, tw), lambda l, i: (l, 0, 0, 0, i)),
                  pl.BlockSpec((None, t, 2, nh, tw), lambda l, i: (l, 0, 0, 0, i)),
                  pl.BlockSpec((None, 2, nh, tw), lambda l, i: (l, 0, 0, i)),
                  pl.BlockSpec((None, 2, nh, tw), lambda l, i: (l, 0, 0, i))],
        out_specs=pl.BlockSpec((None, 1, nlag, LANE, LANE), lambda l, i: (l, i, 0, 0, 0)),
        compiler_params=_params(("arbitrary", "arbitrary")),
        name="s5_taps",
    )(e_re, e_im, ct_re, ct_im)
    r_mat = np.kron(np.eye(CHUNK, dtype=np.float32), np.tile(np.eye(nh, dtype=np.float32), (1, LANE // nh)))
    return kr, e_re, e_im, syt, jnp.asarray(r_mat, F32).astype(BF16), mu, dsk


def _s5(u, uc, li, ops):
    kr, e_re, e_im, syt, r_mat, mu, dsk = ops
    l, d_ssm = u.shape
    lc = uc.shape[0]
    na, nb = mu.shape[1], mu.shape[2]
    fw = CHUNK * LANE
    nlat, ncc = l // CHUNK, lc // CHUNK
    nc = nlat + ncc
    seglen = nlat // NSEGS
    sw = PAIRS * 4 * LANE
    assert seglen * NSEGS == nlat and ncc * CHUNK == lc
    d, xp = pl.pallas_call(
        functools.partial(_s5_state_kernel, seglen=seglen, ncc=ncc),
        out_shape=(jax.ShapeDtypeStruct((na, nc, fw), BF16), jax.ShapeDtypeStruct((na, nc, nb * 4 * LANE), BF16)),
        grid=(na, nb // PAIRS),
        in_specs=[
            pl.BlockSpec((l, LANE), lambda a, b: (0, a)),
            pl.BlockSpec((lc, LANE), lambda a, b: (0, a)),
            pl.BlockSpec((None,) + e_re.shape[1:4] + (PAIRS * LANE,), lambda a, b: (li, 0, 0, 0, a * (nb // PAIRS) + b)),
            pl.BlockSpec((None,) + e_im.shape[1:4] + (PAIRS * LANE,), lambda a, b: (li, 0, 0, 0, a * (nb // PAIRS) + b)),
            pl.BlockSpec((None, 1, PAIRS, 2, 4, LANE), lambda a, b: (li, a, b, 0, 0, 0)),
        ],
        out_specs=(pl.BlockSpec((1, nc, fw), lambda a, b: (a, 0, 0)),
                   pl.BlockSpec((1, nc, sw), lambda a, b: (a, 0, b))),
        scratch_shapes=[pltpu.VMEM((fw, sw), BF16),
                        pltpu.VMEM((PAIRS * 4, nc, LANE), F32), pltpu.VMEM((PAIRS * 4, nc, LANE), F32)],
        compiler_params=_params(("arbitrary", "arbitrary")),
        name="s5_state",
    )(u, uc, e_re, e_im, mu)
    rbs = max(r for r in range(NSEGS, min(nc, 512) + 1, NSEGS) if nc % r == 0)
    nrb = nc // rbs
    assert NSEGS % 16 == 0 and ncc % NSEGS == 0
    return pl.pallas_call(
        functools.partial(_s5_readout_kernel, seglen=seglen),
        out_shape=(jax.ShapeDtypeStruct((l, d_ssm), BF16), jax.ShapeDtypeStruct((lc, d_ssm), BF16)),
        grid=(na, nrb),
        in_specs=[
            pl.BlockSpec((1, rbs, fw), lambda a, r: (a, r, 0)),
            pl.BlockSpec((1, rbs, nb * 4 * LANE), lambda a, r: (a, r, 0)),
            pl.BlockSpec((None, 1) + kr.shape[2:], lambda a, r: (li, a, 0, 0, 0)),
            pl.BlockSpec((None,) + syt.shape[1:4] + (nb * LANE,), lambda a, r: (li, 0, 0, 0, a)),
            _const_spec(r_mat.shape),
            pl.BlockSpec((None, 1, 1, fw), lambda a, r: (li, a, 0, 0)),
        ],
        out_specs=(pl.BlockSpec((l, LANE), lambda a, r: (0, a)), pl.BlockSpec((lc, LANE), lambda a, r: (0, a))),
        scratch_shapes=[pltpu.VMEM((fw, fw), BF16), pltpu.VMEM((nb * 4 * LANE, fw), BF16),
                        pltpu.VMEM((rbs, fw), F32), pltpu.VMEM((nb * 4 * LANE, syt.shape[3]), BF16)],
        compiler_params=_params(("arbitrary", "arbitrary")),
        name="s5_readout",
    )(d, xp, kr, syt, r_mat, dsk)


def _fft1_kernel(p_ref, q_ref, m_ref, o_ref, s_ref, *, nb):
    for n in range(nb):
        rhs = jnp.concatenate([p_ref[:, n, :], q_ref[:, n, :]], axis=0).astype(BF16)
        s_ref[:, n, :] = _dot(m_ref[...], rhs)
    o_ref[...] = s_ref[...].astype(BF16)


def _fft2_kernel(yr_ref, yi_ref, ca_ref, sa_ref, cb_ref, sb_ref, o_ref, s_ref, *, kb):
    cb, sb = cb_ref[...], sb_ref[...]
    for k in range(kb):
        ca, sa = ca_ref[k:k + 1, :], sa_ref[k:k + 1, :]
        g = jnp.concatenate([ca * cb - sa * sb, sa * cb + ca * sb], axis=1).astype(BF16)
        rhs = jnp.concatenate([yr_ref[k], yi_ref[k]], axis=0)
        s_ref[:, k, :] = _dot(g, rhs)
    o_ref[...] = s_ref[...].astype(BF16)


def _fft_ctx_kernel(p_ref, q_ref, m_ref, o_ref):
    rhs = jnp.concatenate([p_ref[...], q_ref[...]], axis=0).astype(BF16)
    o_ref[...] = _dot(m_ref[...], rhs).astype(BF16)


def _fnet_tables(l, lc):
    r = math.isqrt(l)
    assert r * r == l
    k = np.arange(r)
    th = 2.0 * np.pi * (np.outer(k, k) % r) / r
    c, s = np.cos(th), np.sin(th)
    m1 = np.block([[c, -s], [-s, -c]])
    ta = 2.0 * np.pi * np.outer(k, k) / l
    scale = 1.0 / math.sqrt(l)
    tw2 = tuple(jnp.asarray(v, F32) for v in (np.cos(ta), np.sin(ta), c * scale, s * scale))
    kc = np.arange(lc)
    thc = 2.0 * np.pi * (np.outer(kc, kc) % lc) / lc
    mc = np.concatenate([np.cos(thc), -np.sin(thc)], axis=1) / math.sqrt(lc)
    return (jnp.asarray(m1, F32).astype(BF16), tw2, jnp.asarray(mc, F32).astype(BF16))


def _fnet_lat(p, q, m1, tw2):
    l, c = p.shape
    r = m1.shape[0] // 2
    assert r * r == l
    p3 = p.reshape(r, r, c)
    q3 = q.reshape(r, r, c)
    nb = 16
    y1 = pl.pallas_call(
        functools.partial(_fft1_kernel, nb=nb),
        out_shape=jax.ShapeDtypeStruct((2 * r, r, c), BF16),
        grid=(r // nb,),
        in_specs=[pl.BlockSpec((r, nb, c), lambda j: (0, j, 0)),
                  pl.BlockSpec((r, nb, c), lambda j: (0, j, 0)),
                  _const_spec(m1.shape)],
        out_specs=pl.BlockSpec((2 * r, nb, c), lambda j: (0, j, 0)),
        scratch_shapes=[pltpu.VMEM((2 * r, nb, c), F32)],
        compiler_params=_params(("arbitrary",)),
        name="fft_stage1",
    )(p3, q3, m1)
    kb = 16
    ca, sa, cb, sb = tw2
    out = pl.pallas_call(
        functools.partial(_fft2_kernel, kb=kb),
        out_shape=jax.ShapeDtypeStruct((r, r, c), BF16),
        grid=(r // kb,),
        in_specs=[pl.BlockSpec((kb, r, c), lambda j: (j, 0, 0)),
                  pl.BlockSpec((kb, r, c), lambda j: (r // kb + j, 0, 0)),
                  pl.BlockSpec((kb, r), lambda j: (j, 0)),
                  pl.BlockSpec((kb, r), lambda j: (j, 0)),
                  _const_spec(cb.shape), _const_spec(sb.shape)],
        out_specs=pl.BlockSpec((r, kb, c), lambda j: (0, j, 0)),
        scratch_shapes=[pltpu.VMEM((r, kb, c), F32)],
        compiler_params=_params(("arbitrary",)),
        name="fft_stage2",
    )(y1, y1, ca, sa, cb, sb)
    return out.reshape(l, c)


def _fnet_ctx(pc, qc, mc):
    return pl.pallas_call(
        _fft_ctx_kernel,
        out_shape=jax.ShapeDtypeStruct(pc.shape, BF16),
        grid=(1,),
        in_specs=[_whole_spec(pc.shape), _whole_spec(qc.shape), _const_spec(mc.shape)],
        out_specs=_whole_spec(pc.shape),
        compiler_params=_params(("arbitrary",)),
        name="fft_ctx",
    )(pc, qc, mc)


def _pos_tables(l, d):
    quarter = d // 4
    omega = 1.0 / (POS_BASE ** (jnp.arange(quarter, dtype=F32) / quarter))

    def enc(pv):
        ang = pv[:, None] * omega[None, :]
        return jnp.concatenate([jnp.sin(ang), jnp.cos(ang)], axis=-1)

    renc = enc(jnp.arange(l // GRID_W, dtype=F32))
    cenc = jnp.tile(enc(jnp.arange(GRID_W, dtype=F32)), (TM // GRID_W, 1))
    return renc, cenc


def kernel(x, c, ctx, c_ctx, w_ada, b_ada, norm_g, ffn1_gu, ffn1_down, ffn2_gu, ffn2_down, w_in, w_out,
           ssm_lam_re, ssm_lam_im, ssm_log_dt, ssm_b_re, ssm_b_im, ssm_c_re, ssm_c_im, ssm_d, w_glu, w_fmix):
    bsz, l, d = x.shape
    lc = ctx.shape[1]
    depth = w_ada.shape[0]
    d_ssm = w_glu.shape[1]
    head = w_fmix.shape[2]
    assert bsz == 1 and l % TM == 0 and TM % GRID_W == 0
    seglen = (l // CHUNK) // NSEGS

    mods = _modulation(c_ctx, c, w_ada, b_ada)
    pos = _pos_tables(l, d)
    m1, tw2, mc = _fnet_tables(l, lc)
    kc = np.arange(head)
    thc = 2.0 * np.pi * (np.outer(kc, kc) % head) / head
    cs = jnp.asarray(np.concatenate([np.cos(thc), np.sin(thc)], axis=1) / math.sqrt(head), F32).astype(BF16)
    win, wout, wglu, wfm = (a.astype(BF16) for a in (w_in, w_out, w_glu, w_fmix))
    wgu1, wd1 = ffn1_gu[0].astype(BF16), ffn1_down[0].astype(BF16)

    ops = _s5_operators((ssm_lam_re, ssm_lam_im, ssm_log_dt, ssm_b_re, ssm_b_im, ssm_c_re, ssm_c_im, ssm_d), seglen)

    xl, xc = x[0], ctx[0]
    for li in range(depth):
        last = li == depth - 1
        xl, u, p, q, xc, uc, pc, qc, wgu2, wd2 = _layer_in(xl, xc, pos if li == 0 else None, li, mods, norm_g, wgu1, wd1,
                                                           win, cs, (ffn2_gu, ffn2_down), d_ssm=d_ssm)
        ys, ysc = _s5(u, uc, li, ops)
        yf = _fnet_lat(p, q, m1, tw2)
        ctx_in = None if last else (xc, ysc, _fnet_ctx(pc, qc, mc))
        xl, xc, wgu1, wd1 = _layer_out((xl, ys, yf), ctx_in, li, mods, norm_g, wglu, wfm, wout, wgu2, wd2,
                                       (ffn1_gu, ffn1_down))
    return xl[None]
```

```python
import functools
import math

import numpy as np
import jax
import jax.numpy as jnp
from jax import lax
from jax.experimental import pallas as pl
from jax.experimental.pallas import tpu as pltpu

F32 = jnp.float32
BF16 = jnp.bfloat16

LANE = 128
SUBLANE = 8
VMEM_LIMIT = 56 * 1024 * 1024

EPS = 1e-6
GRID_W = 64
POS_BASE = 10000.0
N_MOD = 9
N_FNET_HEADS = 4
CHUNK = 16
TM = 512
CTX, LAT = 0, 1


def _dot(a, b):
    return jnp.dot(a, b, preferred_element_type=F32)


def _params(sem=None):
    return pltpu.CompilerParams(dimension_semantics=sem, vmem_limit_bytes=VMEM_LIMIT)


def _const_spec(shape):
    nd = len(shape)
    return pl.BlockSpec(shape, lambda *_: (0,) * nd, pipeline_mode=pl.Buffered(1))


def _rms(x, g):
    ms = jnp.mean(x * x, axis=-1, keepdims=True)
    return x * lax.rsqrt(ms + EPS) * g


def _pre(x, g, mod_ref, who, k):
    return _rms(x, g) * (1.0 + mod_ref[who, 3 * k + 1:3 * k + 2, :]) + mod_ref[who, 3 * k:3 * k + 1, :]


def _post(x, y, g, mod_ref, who, k, weight):
    return x + (weight * mod_ref[who, 3 * k + 2:3 * k + 3, :]) * _rms(y, g)


def _on_last_step(fn):
    pl.when(pl.program_id(0) == pl.num_programs(0) - 1)(fn)


def _sigmoid(x):
    return 1.0 / (1.0 + jnp.exp(-x))


def _gelu_tanh(x):
    return 0.5 * x * (1.0 + jnp.tanh(math.sqrt(2.0 / math.pi) * (x + 0.044715 * (x * x * x))))


def _mod_kernel(cb_ref, w_ref, b_ref, o_ref, s_ref, *, tn):
    rows, n_out = w_ref.shape[1], w_ref.shape[2]
    nb = tn // LANE
    j = pl.program_id(1)

    @pl.when((pl.program_id(0) == 0) & (j == 0))
    def _():
        cv = cb_ref[...]
        s_ref[...] = cv * _sigmoid(cv)

    @pl.when(j == 0)
    def _():
        o_ref[0] = jnp.broadcast_to(b_ref[0], (2, n_out))

    for grp in range(n_out // tn):
        def body(t, accs):
            d0 = pl.multiple_of(t * SUBLANE, SUBLANE)
            s = [s_ref[r, pl.ds(j * rows + d0, SUBLANE), :] for r in range(2)]
            new = list(accs)
            for jj in range(nb):
                w = w_ref[0, pl.ds(d0, SUBLANE), grp * tn + jj * LANE:grp * tn + (jj + 1) * LANE]
                for r in range(2):
                    new[r * nb + jj] = accs[r * nb + jj] + w * s[r]
            return tuple(new)

        init = tuple(jnp.zeros((SUBLANE, LANE), F32) for _ in range(2 * nb))
        accs = lax.fori_loop(0, rows // SUBLANE, body, init, unroll=4)
        for r in range(2):
            for jj in range(nb):
                lanes = slice(grp * tn + jj * LANE, grp * tn + (jj + 1) * LANE)
                o_ref[0, r:r + 1, lanes] = o_ref[0, r:r + 1, lanes] + jnp.sum(accs[r * nb + jj], axis=0, keepdims=True)


def _modulation(c_ctx, c, w_ada, b_ada):
    depth, d_model, n_out = w_ada.shape
    tn = 9 * LANE
    rows = 256
    assert n_out % tn == 0 and d_model % rows == 0
    cb = jnp.broadcast_to(jnp.stack([c_ctx, c[0]])[:, :, None], (2, d_model, LANE))
    out = pl.pallas_call(
        functools.partial(_mod_kernel, tn=tn),
        out_shape=jax.ShapeDtypeStruct((depth, 2, n_out), F32),
        grid=(depth, d_model // rows),
        in_specs=[
            pl.BlockSpec((2, d_model, LANE), lambda l, j: (0, 0, 0)),
            pl.BlockSpec((1, rows, n_out), lambda l, j: (l, j, 0)),
            pl.BlockSpec((1, 1, n_out), lambda l, j: (l, 0, 0)),
        ],
        out_specs=pl.BlockSpec((1, 2, n_out), lambda l, j: (l, 0, 0)),
        scratch_shapes=[pltpu.VMEM((2, d_model, LANE), F32)],
        compiler_params=_params(("arbitrary", "arbitrary")),
        name="adaln_mod",
    )(cb, w_ada, b_ada.reshape(depth, 1, n_out))
    return out.reshape(depth, 2, N_MOD, d_model)


SPLIT = 2


def _row_chunks(n, parts):
    step = n // parts
    return [slice(i * step, (i + 1) * step) for i in range(parts)]


def _ffn_stage(xs, who, mod_ref, g_ref, wgu_ref, wd_ref, k, gi):
    f = wd_ref.shape[0]
    hs = [_pre(x, g_ref[gi:gi + 1, :], mod_ref, who, k).astype(BF16) for x in xs]
    gus = [_dot(h, wgu_ref[...]) for h in hs]
    acts = [(gu[:, :f] * _sigmoid(gu[:, :f]) * gu[:, f:]).astype(BF16) for gu in gus]
    ys = [_dot(a, wd_ref[...]) for a in acts]
    return [_post(x, y, g_ref[gi + 1:gi + 2, :], mod_ref, who, k, 0.5) for x, y in zip(xs, ys)]


def _inproj_stage(xs, rows, who, mod_ref, g_ref, win_ref, cs_ref, u_ref, p_ref, q_ref):
    d_ssm = u_ref.shape[1]
    head = cs_ref.shape[0]
    hs = [_pre(x, g_ref[2:3, :], mod_ref, who, 1).astype(BF16) for x in xs]
    hhs = [_dot(h, win_ref[...]) for h in hs]
    for r, hh in zip(rows, hhs):
        u_ref[r, :] = hh[:, :d_ssm].astype(BF16)
        for n in range(N_FNET_HEADS):
            lo = d_ssm + n * head
            pq = _dot(hh[:, lo:lo + head].astype(BF16), cs_ref[...])
            p_ref[r, n * head:(n + 1) * head] = pq[:, :head].astype(BF16)
            q_ref[r, n * head:(n + 1) * head] = pq[:, head:].astype(BF16)


def _outproj_stage(xs, yss, yfs, who, mod_ref, g_ref, wglu_ref, wfm_ref, wout_ref):
    head = wfm_ref.shape[1]
    hs = [_gelu_tanh(ys.astype(F32)) for ys in yss]
    gls = [_dot(h.astype(BF16), wglu_ref[...]) for h in hs]
    mixed = []
    for h, gl, yf in zip(hs, gls, yfs):
        yf = yf.astype(BF16)
        parts = [(h * _sigmoid(gl)).astype(BF16)]
        parts += [_dot(yf[:, n * head:(n + 1) * head], wfm_ref[n]).astype(BF16) for n in range(N_FNET_HEADS)]
        mixed.append(jnp.concatenate(parts, axis=1))
    ys = [_dot(m, wout_ref[...]) for m in mixed]
    return [_post(x, y, g_ref[3:4, :], mod_ref, who, 1, 1.0) for x, y in zip(xs, ys)]


def _cast_chunks(src_refs, dst_refs):
    for src_ref, dst_ref in zip(src_refs, dst_refs):
        dst_ref[...] = src_ref[...].astype(BF16)


def _layer_in_kernel(*refs, first):
    if first:
        (x_ref, xc_ref, renc_ref, cenc_ref, mod_ref, g_ref, wgu_ref, wd_ref, win_ref, cs_ref, ng_ref, nd_ref,
         o_ref, u_ref, p_ref, q_ref, oc_ref, uc_ref, pc_ref, qc_ref, ngo_ref, ndo_ref, xs_ref) = refs
        tm = x_ref.shape[0]
        half = renc_ref.shape[1]
        r0 = pl.program_id(0) * (tm // GRID_W)
        xs_ref[:, half:] = x_ref[:, half:] + cenc_ref[...]
        for q in range(tm // GRID_W):
            xs_ref[q * GRID_W:(q + 1) * GRID_W, :half] = (
                x_ref[q * GRID_W:(q + 1) * GRID_W, :half] + renc_ref[pl.ds(r0 + q, 1), :])
        src_ref = xs_ref
    else:
        (x_ref, xc_ref, mod_ref, g_ref, wgu_ref, wd_ref, win_ref, cs_ref, ng_ref, nd_ref,
         o_ref, u_ref, p_ref, q_ref, oc_ref, uc_ref, pc_ref, qc_ref, ngo_ref, ndo_ref) = refs
        src_ref = x_ref
    _cast_chunks((ng_ref, nd_ref), (ngo_ref, ndo_ref))
    rows = _row_chunks(x_ref.shape[0], SPLIT)
    x1 = _ffn_stage([src_ref[r, :] for r in rows], LAT, mod_ref, g_ref, wgu_ref, wd_ref, 0, 0)
    for r, v in zip(rows, x1):
        o_ref[r, :] = v
    _inproj_stage(x1, rows, LAT, mod_ref, g_ref, win_ref, cs_ref, u_ref, p_ref, q_ref)

    def _():
        rows_c = _row_chunks(xc_ref.shape[0], 1)
        xc1 = _ffn_stage([xc_ref[...]], CTX, mod_ref, g_ref, wgu_ref, wd_ref, 0, 0)
        oc_ref[...] = xc1[0]
        _inproj_stage(xc1, rows_c, CTX, mod_ref, g_ref, win_ref, cs_ref, uc_ref, pc_ref, qc_ref)
    _on_last_step(_)


def _layer_out_kernel(*refs, with_ctx):
    if with_ctx:
        (x_ref, ys_ref, yf_ref, xc_ref, ysc_ref, yfc_ref, mod_ref, g_ref, wglu_ref, wfm_ref, wout_ref,
         wgu_ref, wd_ref, ng_ref, nd_ref, o_ref, oc_ref, ngo_ref, ndo_ref) = refs
        _cast_chunks((ng_ref, nd_ref), (ngo_ref, ndo_ref))
    else:
        x_ref, ys_ref, yf_ref, mod_ref, g_ref, wglu_ref, wfm_ref, wout_ref, wgu_ref, wd_ref, o_ref = refs
    rows = _row_chunks(x_ref.shape[0], SPLIT)
    x2 = _outproj_stage([x_ref[r, :] for r in rows], [ys_ref[r, :] for r in rows], [yf_ref[r, :] for r in rows],
                        LAT, mod_ref, g_ref, wglu_ref, wfm_ref, wout_ref)
    x3 = _ffn_stage(x2, LAT, mod_ref, g_ref, wgu_ref, wd_ref, 2, 4)
    for r, v in zip(rows, x3):
        o_ref[r, :] = v
    if with_ctx:
        def _():
            xc2 = _outproj_stage([xc_ref[...]], [ysc_ref[...]], [yfc_ref[...]], CTX, mod_ref, g_ref, wglu_ref,
                                 wfm_ref, wout_ref)
            oc_ref[...] = _ffn_stage(xc2, CTX, mod_ref, g_ref, wgu_ref, wd_ref, 2, 4)[0]
        _on_last_step(_)


def _tile_spec(cols):
    return pl.BlockSpec((TM, cols), lambda i: (i, 0))


def _whole_spec(shape):
    nd = len(shape)
    return pl.BlockSpec(shape, lambda *_: (0,) * nd)


def _layer_spec(arr, li):
    nd = arr.ndim - 1
    return pl.BlockSpec((None,) + arr.shape[1:], lambda *_: (li,) + (0,) * nd, pipeline_mode=pl.Buffered(1))


def _cast_specs(arr, li, nt):
    rows, cols = arr.shape[1:]
    k = 1
    while (rows * k) % nt or (rows * k // nt) % 16:
        k *= 2
    rc = rows * k // nt
    assert rows % rc == 0 and nt % k == 0
    return (pl.BlockSpec((None, rc, cols), lambda i: (li, i // k, 0)), pl.BlockSpec((rc, cols), lambda i: (i // k, 0)),
            jax.ShapeDtypeStruct((rows, cols), BF16))


def _layer_in(xl, xc, pos, li, mods, gains, wgu, wd, win, cs, nxt, *, d_ssm):
    l, d = xl.shape
    lc = xc.shape[0]
    first = pos is not None
    d_f = win.shape[2] - d_ssm
    widths = (d, d_ssm, d_f, d_f)
    dtypes = (F32, BF16, BF16, BF16)
    stacked = [mods, gains]
    nt = l // TM
    casts = [_cast_specs(a, li, nt) for a in nxt]
    return pl.pallas_call(
        functools.partial(_layer_in_kernel, first=first),
        out_shape=[jax.ShapeDtypeStruct((l, wd_), dt_) for wd_, dt_ in zip(widths, dtypes)]
        + [jax.ShapeDtypeStruct((lc, wd_), dt_) for wd_, dt_ in zip(widths, dtypes)] + [c[2] for c in casts],
        grid=(nt,),
        in_specs=([_tile_spec(d), _whole_spec(xc.shape)] + ([_const_spec(a.shape) for a in pos] if first else [])
                  + [_layer_spec(a, li) for a in stacked] + [_const_spec(wgu.shape), _const_spec(wd.shape)]
                  + [_layer_spec(win, li), _const_spec(cs.shape)] + [c[0] for c in casts]),
        out_specs=([_tile_spec(wd_) for wd_ in widths] + [_whole_spec((lc, wd_)) for wd_ in widths]
                   + [c[1] for c in casts]),
        scratch_shapes=[pltpu.VMEM((TM, d), F32)] if first else [],
        compiler_params=_params(("arbitrary",)),
        name="layer_in",
    )(xl, xc, *(pos if first else ()), *stacked, wgu, wd, win, cs, *nxt)


def _layer_out(lat, ctx, li, mods, gains, wglu, wfm, wout, wgu, wd, nxt):
    xl, ys, yf = lat
    l, d = xl.shape
    with_ctx = ctx is not None
    ctx = list(ctx) if with_ctx else []
    stacked = [mods, gains, wglu, wfm, wout]
    nt = l // TM
    casts = [_cast_specs(a, li + 1, nt) for a in nxt] if with_ctx else []
    outs = pl.pallas_call(
        functools.partial(_layer_out_kernel, with_ctx=with_ctx),
        out_shape=([jax.ShapeDtypeStruct(xl.shape, F32)] + ([jax.ShapeDtypeStruct(ctx[0].shape, F32)] if with_ctx else [])
                   + [c[2] for c in casts]),
        grid=(nt,),
        in_specs=([_tile_spec(d), _tile_spec(ys.shape[1]), _tile_spec(yf.shape[1])]
                  + [_whole_spec(a.shape) for a in ctx] + [_layer_spec(a, li) for a in stacked]
                  + [_const_spec(wgu.shape), _const_spec(wd.shape)] + [c[0] for c in casts]),
        out_specs=([_tile_spec(d)] + ([_whole_spec(ctx[0].shape)] if with_ctx else []) + [c[1] for c in casts]),
        compiler_params=_params(("arbitrary",)),
        name="layer_out",
    )(xl, ys, yf, *ctx, *stacked, wgu, wd, *(nxt if with_ctx else ()))
    return tuple(outs) if with_ctx else (outs[0], None, None, None)


NSEGS = 16
PAIRS = 4


def _cmul_add(ar, ai, xr, xi, zr, zi):
    return ar * xr - ai * xi + zr, ar * xi + ai * xr + zi


def _dot_nt(a, b):
    def nt(x, y):
        return lax.dot_general(x, y, (((1,), (1,)), ((), ())), preferred_element_type=F32)
    a_hi, b_hi = a.astype(BF16), b.astype(BF16)
    a_lo, b_lo = (a - a_hi.astype(F32)).astype(BF16), (b - b_hi.astype(F32)).astype(BF16)
    return nt(a_hi, b_hi) + (nt(a_hi, b_lo) + nt(a_lo, b_hi))


def _s5_taps_kernel(e_re_ref, e_im_ref, ct_re_ref, ct_im_ref, kr_ref):
    t = CHUNK
    nh = ct_re_ref.shape[1]
    g8 = LANE // nh
    gw = e_re_ref.shape[3] // g8
    same = (lax.broadcasted_iota(jnp.int32, (LANE, g8 * gw), 0) // nh
            == lax.broadcasted_iota(jnp.int32, (LANE, g8 * gw), 1) // gw)
    taps = []
    for d in range(2):
        cre = jnp.where(same, jnp.concatenate([ct_re_ref[d]] * g8, axis=0), 0.0)
        cim = jnp.where(same, jnp.concatenate([ct_im_ref[d]] * g8, axis=0), 0.0)
        er = jnp.concatenate([e_re_ref[m, d] for m in range(t)], axis=0)
        ei = jnp.concatenate([e_im_ref[m, d] for m in range(t)], axis=0)
        taps.append(_dot_nt(er, cre) - _dot_nt(ei, cim))
    blk = (lax.broadcasted_iota(jnp.int32, (LANE, LANE), 0) // nh
           == lax.broadcasted_iota(jnp.int32, (LANE, LANE), 1) // nh)
    for lag in range(-(t - 1), t):
        if lag > 0:
            src = taps[0][lag * nh:(lag + 1) * nh]
        elif lag < 0:
            src = taps[1][-lag * nh:(1 - lag) * nh]
        else:
            src = taps[0][:nh] + taps[1][:nh]
        kr_ref[0, lag + t - 1] = jnp.where(blk, jnp.concatenate([src] * g8, axis=0), 0.0).astype(BF16)


def _s5_state_kernel(u_ref, uc_ref, e_re_ref, e_im_ref, mu_ref, d_ref, xp_ref, wz_ref, z_ref, xs_ref, *, seglen, ncc):
    w = LANE
    nc = d_ref.shape[1]
    nlat = NSEGS * seglen
    nv = NSEGS // SUBLANE
    nh = e_re_ref.shape[2]
    ns = w // 2

    @pl.when(pl.program_id(1) == 0)
    def _():
        def fold(i, carry):
            r0 = pl.multiple_of(i * NSEGS, NSEGS)
            tiles = [u_ref[pl.ds(pl.multiple_of((s * seglen + i) * CHUNK, CHUNK), CHUNK), :] for s in range(NSEGS)]
            d_ref[0, pl.ds(r0, NSEGS), :] = jnp.concatenate(tiles, axis=0).reshape(NSEGS, CHUNK * w)
            return carry

        lax.fori_loop(0, seglen, fold, 0)
        d_ref[0, nlat:nlat + ncc, :] = uc_ref[...].reshape(ncc, CHUNK * w)

    wz_ref[...] = jnp.zeros(wz_ref.shape, BF16)
    first = (lax.broadcasted_iota(jnp.int32, (nh, 4 * w), 1) % w) < ns
    for pb in range(PAIRS):
        pair = pl.program_id(1) * PAIRS + pb
        lanes = slice(pb * w, (pb + 1) * w)
        for i in range(CHUNK):
            ez = jnp.concatenate([e_re_ref[CHUNK - 1 - i, 0, :, lanes], e_im_ref[CHUNK - 1 - i, 0, :, lanes],
                                  e_re_ref[i, 1, :, lanes], e_im_ref[i, 1, :, lanes]], axis=1).astype(BF16)
            both = jnp.concatenate([jnp.where(first, ez, jnp.zeros_like(ez)),
                                    jnp.where(first, jnp.zeros_like(ez), ez)], axis=0)
            r0 = pl.multiple_of(i * w + pair * 2 * nh, 2 * nh)
            wz_ref[pl.ds(r0, 2 * nh), pb * 4 * w:(pb + 1) * 4 * w] = both

    for pb in range(PAIRS):
        zz = _dot(d_ref[0], wz_ref[:, pb * 4 * w:(pb + 1) * 4 * w])
        for c in range(4):
            z_ref[pb * 4 + c] = zz[:, c * w:(c + 1) * w]

    mu = [[mu_ref[0, pb, 0, c:c + 1, :] for c in range(4)] for pb in range(PAIRS)]
    mun = [[mu_ref[0, pb, 1, c:c + 1, :] for c in range(4)] for pb in range(PAIRS)]

    pre = []
    for pb in range(PAIRS):
        zc = [z_ref[pb * 4 + c, nlat:nlat + ncc, :] for c in range(4)]
        sf = (jnp.zeros((1, w), F32), jnp.zeros((1, w), F32))
        sb = (jnp.zeros((1, w), F32), jnp.zeros((1, w), F32))
        for j in range(ncc):
            jb = ncc - 1 - j
            xs_ref[pb * 4 + 0, nlat + j:nlat + j + 1, :] = sf[0]
            xs_ref[pb * 4 + 1, nlat + j:nlat + j + 1, :] = sf[1]
            xs_ref[pb * 4 + 2, nlat + jb:nlat + jb + 1, :] = sb[0]
            xs_ref[pb * 4 + 3, nlat + jb:nlat + jb + 1, :] = sb[1]
            sf = _cmul_add(mu[pb][0], mu[pb][1], sf[0], sf[1], zc[0][j:j + 1], zc[1][j:j + 1])
            sb = _cmul_add(mu[pb][2], mu[pb][3], sb[0], sb[1], zc[2][jb:jb + 1], zc[3][jb:jb + 1])
        pre.append((sf, sb))

    mub = [[jnp.broadcast_to(m, (SUBLANE, w)) for m in mu[pb]] for pb in range(PAIRS)]

    def rows(i, v):
        return pl.ds(pl.multiple_of(i * NSEGS + v * SUBLANE, SUBLANE), SUBLANE)

    def step(i, st):
        ib = seglen - 1 - i
        new = []
        for pb in range(PAIRS):
            for v in range(nv):
                k = (pb * nv + v) * 4
                m = mub[pb]
                fr, fi = _cmul_add(m[0], m[1], st[k], st[k + 1],
                                   z_ref[pb * 4 + 0, rows(i, v), :], z_ref[pb * 4 + 1, rows(i, v), :])
                br, bi = _cmul_add(m[2], m[3], st[k + 2], st[k + 3],
                                   z_ref[pb * 4 + 2, rows(ib, v), :], z_ref[pb * 4 + 3, rows(ib, v), :])
                new += [fr, fi, br, bi]
        return tuple(new)

    zero = jnp.zeros((SUBLANE, w), F32)
    fin = lax.fori_loop(0, seglen, step, (zero,) * (PAIRS * nv * 4))

    carry = []
    for pb in range(PAIRS):
        cf, cb = pre[pb]
        rows_f, rows_b = [None] * NSEGS, [None] * NSEGS
        for s in range(NSEGS):
            sr = NSEGS - 1 - s
            rows_f[s] = cf
            rows_b[sr] = cb
            kf = (pb * nv + s // SUBLANE) * 4
            kb = (pb * nv + sr // SUBLANE) * 4
            sl, srl = s % SUBLANE, sr % SUBLANE
            cf = _cmul_add(mun[pb][0], mun[pb][1], cf[0], cf[1], fin[kf][sl:sl + 1, :], fin[kf + 1][sl:sl + 1, :])
            cb = _cmul_add(mun[pb][2], mun[pb][3], cb[0], cb[1],
                           fin[kb + 2][srl:srl + 1, :], fin[kb + 3][srl:srl + 1, :])
        for v in range(nv):
            seg = slice(v * SUBLANE, (v + 1) * SUBLANE)
            carry += [jnp.concatenate([r[0] for r in rows_f[seg]], axis=0),
                      jnp.concatenate([r[1] for r in rows_f[seg]], axis=0),
                      jnp.concatenate([r[0] for r in rows_b[seg]], axis=0),
                      jnp.concatenate([r[1] for r in rows_b[seg]], axis=0)]

    def step2(i, st):
        ib = seglen - 1 - i
        for pb in range(PAIRS):
            for v in range(nv):
                k = (pb * nv + v) * 4
                xs_ref[pb * 4 + 0, rows(i, v), :] = st[k]
                xs_ref[pb * 4 + 1, rows(i, v), :] = st[k + 1]
                xs_ref[pb * 4 + 2, rows(ib, v), :] = st[k + 2]
                xs_ref[pb * 4 + 3, rows(ib, v), :] = st[k + 3]
        return step(i, st)

    lax.fori_loop(0, seglen, step2, tuple(carry))
    for k in range(PAIRS * 4):
        xp_ref[0, :, k * w:(k + 1) * w] = xs_ref[k].astype(BF16)


def _s5_readout_kernel(d_ref, xp_ref, kr_ref, syt_ref, rm_ref, dsk_ref, y_ref, yc_ref, bt_ref, wy_ref, yb_ref,
                       sy_ref, *, seglen):
    w = LANE
    cw = 4 * w
    fw = CHUNK * w
    ncg = fw // cw

    @pl.when(pl.program_id(1) == 0)
    def _():
        for i in range(CHUNK):
            for j in range(CHUNK):
                bt_ref[i * w:(i + 1) * w, j * w:(j + 1) * w] = kr_ref[0, j - i + CHUNK - 1]
        for b in range(syt_ref.shape[3] // w):
            for dc in range(4):
                r0 = (b * 4 + dc) * w
                sy_ref[r0:r0 + w, :] = syt_ref[dc // 2, dc % 2, :, b * w:(b + 1) * w].T.astype(BF16)
        nrow = sy_ref.shape[0]
        nh = rm_ref.shape[0] // CHUNK
        row = lax.broadcasted_iota(jnp.int32, (nrow, cw), 0)
        row_g = 2 * (row // cw) + (row % w) // (w // 2)
        col_g = (lax.broadcasted_iota(jnp.int32, (nrow, cw), 1) % w) // nh
        same = row_g == col_g
        for cg in range(ncg):
            cols = slice(cg * cw, (cg + 1) * cw)
            wy_ref[:, cols] = jnp.where(same, _dot(sy_ref[...], rm_ref[:, cols]), 0.0).astype(BF16)

    nrows = d_ref.shape[1]
    for cg in range(ncg):
        cols = slice(cg * cw, (cg + 1) * cw)
        yb_ref[:, cols] = (_dot(d_ref[0], bt_ref[:, cols]) + _dot(xp_ref[0], wy_ref[:, cols])
                           + d_ref[0, :, cols].astype(F32) * dsk_ref[0, :, cols])

    ngrp = nrows // NSEGS

    def unfold(j, carry):
        gi = pl.program_id(1) * ngrp + j
        r0 = pl.multiple_of(j * NSEGS, NSEGS)
        tok = yb_ref[pl.ds(r0, NSEGS), :].astype(BF16).reshape(NSEGS * CHUNK, w)

        @pl.when(gi < seglen)
        def _():
            for s in range(NSEGS):
                t0 = pl.multiple_of((s * seglen + gi) * CHUNK, CHUNK)
                y_ref[pl.ds(t0, CHUNK), :] = tok[s * CHUNK:(s + 1) * CHUNK]

        @pl.when(gi >= seglen)
        def _():
            t0 = pl.multiple_of((gi - seglen) * NSEGS * CHUNK, NSEGS * CHUNK)
            yc_ref[pl.ds(t0, NSEGS * CHUNK), :] = tok

        return carry

    lax.fori_loop(0, ngrp, unfold, 0)


def _s5_tables(lam_re, lam_im, log_dt, b_re, b_im, c_re, c_im, d_skip, *, seglen):
    t = CHUNK
    ng, ns = lam_re.shape[1], lam_re.shape[2]
    nh = b_re.shape[-1]
    gp = ng * ns
    g8 = LANE // nh
    na = ng // g8
    nb = g8 // 2
    assert 2 * ns == LANE and nb % PAIRS == 0
    dt = jnp.exp(log_dt)[..., None]
    a, b = (lam_re * dt).reshape(2, gp), (lam_im * dt).reshape(2, gp)

    def lam_pow(m):
        mm = jnp.asarray(m, F32).reshape(-1, 1, 1)
        mag = jnp.exp(a[None] * mm)
        return mag * jnp.cos(b[None] * mm), mag * jnp.sin(b[None] * mm)

    pr, pi = lam_pow(np.arange(t + 1))
    lr, li = lam_re.reshape(2, gp), lam_im.reshape(2, gp)
    nr, ni = pr[1] - 1.0, pi[1]
    den = lr * lr + li * li
    qr, qi = (nr * lr + ni * li) / den, (ni * lr - nr * li) / den
    bt_re = b_re.transpose(0, 3, 1, 2).reshape(2, nh, gp)
    bt_im = b_im.transpose(0, 3, 1, 2).reshape(2, nh, gp)
    bb_re = qr[:, None] * bt_re - qi[:, None] * bt_im
    bb_im = qr[:, None] * bt_im + qi[:, None] * bt_re
    e_re = pr[:t, :, None] * bb_re[None] - pi[:t, :, None] * bb_im[None]
    e_im = pr[:t, :, None] * bb_im[None] + pi[:t, :, None] * bb_re[None]
    ct_re = c_re.transpose(0, 2, 1, 3).reshape(2, nh, gp)
    ct_im = c_im.transpose(0, 2, 1, 3).reshape(2, nh, gp)

    sl = slice(1, t + 1)
    pw_re = jnp.stack([pr[sl, 0], pr[sl, 1][::-1]])[:, :, None]
    pw_im = jnp.stack([pi[sl, 0], pi[sl, 1][::-1]])[:, :, None]
    wy_re = ct_re[:, None] * pw_re - ct_im[:, None] * pw_im
    wy_im = ct_re[:, None] * pw_im + ct_im[:, None] * pw_re
    syt = jnp.stack([wy_re, -wy_im], axis=1).reshape(2, 2, t * nh, gp)
    mr, mi = lam_pow([t, t * seglen])
    mu = jnp.stack([mr, mi], axis=2).reshape(2, 4, na, nb, LANE).transpose(2, 3, 0, 1, 4)
    dsk = jnp.tile(d_skip.reshape(na, 1, LANE), (1, 1, t))
    return e_re, e_im, ct_re, ct_im, syt, mu, dsk


def _s5_operators(params, seglen):
    e_re, e_im, ct_re, ct_im, syt, mu, dsk = jax.vmap(functools.partial(_s5_tables, seglen=seglen))(*params)
    depth, t, _, nh, gp = e_re.shape
    na = mu.shape[1]
    tw = gp // na
    nlag = 2 * CHUNK - 1
    kr = pl.pallas_call(
        _s5_taps_kernel,
        out_shape=jax.ShapeDtypeStruct((depth, na, nlag, LANE, LANE), BF16),
        grid=(depth, na),
        in_specs=[pl.BlockSpec((None, t, 2, nh, tw), lambda l, i: (l, 0, 0, 0, i)),
                  pl.BlockSpec((None, t, 2, nh, tw), lambda l, i: (l, 0, 0, 0, i)),
                  pl.BlockSpec((None, 2, nh, tw), lambda l, i: (l, 0, 0, i)),
                  pl.BlockSpec((None, 2, nh, tw), lambda l, i: (l, 0, 0, i))],
        out_specs=pl.BlockSpec((None, 1, nlag, LANE, LANE), lambda l, i: (l, i, 0, 0, 0)),
        compiler_params=_params(("arbitrary", "arbitrary")),
        name="s5_taps",
    )(e_re, e_im, ct_re, ct_im)
    r_mat = np.kron(np.eye(CHUNK, dtype=np.float32), np.tile(np.eye(nh, dtype=np.float32), (1, LANE // nh)))
    return kr, e_re, e_im, syt, jnp.asarray(r_mat, F32).astype(BF16), mu, dsk


def _s5(u, uc, li, ops):
    kr, e_re, e_im, syt, r_mat, mu, dsk = ops
    l, d_ssm = u.shape
    lc = uc.shape[0]
    na, nb = mu.shape[1], mu.shape[2]
    fw = CHUNK * LANE
    nlat, ncc = l // CHUNK, lc // CHUNK
    nc = nlat + ncc
    seglen = nlat // NSEGS
    sw = PAIRS * 4 * LANE
    assert seglen * NSEGS == nlat and ncc * CHUNK == lc
    d, xp = pl.pallas_call(
        functools.partial(_s5_state_kernel, seglen=seglen, ncc=ncc),
        out_shape=(jax.ShapeDtypeStruct((na, nc, fw), BF16), jax.ShapeDtypeStruct((na, nc, nb * 4 * LANE), BF16)),
        grid=(na, nb // PAIRS),
        in_specs=[
            pl.BlockSpec((l, LANE), lambda a, b: (0, a)),
            pl.BlockSpec((lc, LANE), lambda a, b: (0, a)),
            pl.BlockSpec((None,) + e_re.shape[1:4] + (PAIRS * LANE,), lambda a, b: (li, 0, 0, 0, a * (nb // PAIRS) + b)),
            pl.BlockSpec((None,) + e_im.shape[1:4] + (PAIRS * LANE,), lambda a, b: (li, 0, 0, 0, a * (nb // PAIRS) + b)),
            pl.BlockSpec((None, 1, PAIRS, 2, 4, LANE), lambda a, b: (li, a, b, 0, 0, 0)),
        ],
        out_specs=(pl.BlockSpec((1, nc, fw), lambda a, b: (a, 0, 0)),
                   pl.BlockSpec((1, nc, sw), lambda a, b: (a, 0, b))),
        scratch_shapes=[pltpu.VMEM((fw, sw), BF16),
                        pltpu.VMEM((PAIRS * 4, nc, LANE), F32), pltpu.VMEM((PAIRS * 4, nc, LANE), F32)],
        compiler_params=_params(("arbitrary", "arbitrary")),
        name="s5_state",
    )(u, uc, e_re, e_im, mu)
    rbs = max(r for r in range(NSEGS, min(nc, 512) + 1, NSEGS) if nc % r == 0)
    nrb = nc // rbs
    assert NSEGS % 16 == 0 and ncc % NSEGS == 0
    return pl.pallas_call(
        functools.partial(_s5_readout_kernel, seglen=seglen),
        out_shape=(jax.ShapeDtypeStruct((l, d_ssm), BF16), jax.ShapeDtypeStruct((lc, d_ssm), BF16)),
        grid=(na, nrb),
        in_specs=[
            pl.BlockSpec((1, rbs, fw), lambda a, r: (a, r, 0)),
            pl.BlockSpec((1, rbs, nb * 4 * LANE), lambda a, r: (a, r, 0)),
            pl.BlockSpec((None, 1) + kr.shape[2:], lambda a, r: (li, a, 0, 0, 0)),
            pl.BlockSpec((None,) + syt.shape[1:4] + (nb * LANE,), lambda a, r: (li, 0, 0, 0, a)),
            _const_spec(r_mat.shape),
            pl.BlockSpec((None, 1, 1, fw), lambda a, r: (li, a, 0, 0)),
        ],
        out_specs=(pl.BlockSpec((l, LANE), lambda a, r: (0, a)), pl.BlockSpec((lc, LANE), lambda a, r: (0, a))),
        scratch_shapes=[pltpu.VMEM((fw, fw), BF16), pltpu.VMEM((nb * 4 * LANE, fw), BF16),
                        pltpu.VMEM((rbs, fw), F32), pltpu.VMEM((nb * 4 * LANE, syt.shape[3]), BF16)],
        compiler_params=_params(("arbitrary", "arbitrary")),
        name="s5_readout",
    )(d, xp, kr, syt, r_mat, dsk)


def _fft1_kernel(p_ref, q_ref, m_ref, o_ref):
    r, nb, c = p_ref.shape
    p2 = p_ref[...].reshape(r * nb, c).reshape(r, nb * c)
    q2 = q_ref[...].reshape(r * nb, c).reshape(r, nb * c)
    y = _dot(m_ref[...], jnp.concatenate([p2, q2], axis=0)).astype(BF16)
    o_ref[...] = y.reshape(2 * r * nb, c).reshape(2 * r, nb, c)


def _fft2_kernel(yr_ref, yi_ref, ca_ref, sa_ref, cb_ref, sb_ref, o_ref, s_ref, *, kb):
    cb, sb = cb_ref[...], sb_ref[...]
    for k in range(kb):
        ca, sa = ca_ref[k:k + 1, :], sa_ref[k:k + 1, :]
        g = jnp.concatenate([ca * cb - sa * sb, sa * cb + ca * sb], axis=1).astype(BF16)
        rhs = jnp.concatenate([yr_ref[k], yi_ref[k]], axis=0)
        s_ref[:, k, :] = _dot(g, rhs)
    o_ref[...] = s_ref[...].astype(BF16)


def _fft_ctx_kernel(p_ref, q_ref, m_ref, o_ref):
    rhs = jnp.concatenate([p_ref[...], q_ref[...]], axis=0)
    o_ref[...] = _dot(m_ref[...], rhs).astype(BF16)


def _fnet_tables(l, lc):
    r = math.isqrt(l)
    assert r * r == l
    k = np.arange(r)
    th = 2.0 * np.pi * (np.outer(k, k) % r) / r
    c, s = np.cos(th), np.sin(th)
    m1 = np.block([[c, -s], [-s, -c]])
    ta = 2.0 * np.pi * np.outer(k, k) / l
    scale = 1.0 / math.sqrt(l)
    tw2 = tuple(jnp.asarray(v, F32) for v in (np.cos(ta), np.sin(ta), c * scale, s * scale))
    kc = np.arange(lc)
    thc = 2.0 * np.pi * (np.outer(kc, kc) % lc) / lc
    mc = np.concatenate([np.cos(thc), -np.sin(thc)], axis=1) / math.sqrt(lc)
    return (jnp.asarray(m1, F32).astype(BF16), tw2, jnp.asarray(mc, F32).astype(BF16))


def _fnet_lat(p, q, m1, tw2):
    l, c = p.shape
    r = m1.shape[0] // 2
    assert r * r == l
    p3 = p.reshape(r, r, c)
    q3 = q.reshape(r, r, c)
    nb = 16
    y1 = pl.pallas_call(
        _fft1_kernel,
        out_shape=jax.ShapeDtypeStruct((2 * r, r, c), BF16),
        grid=(r // nb,),
        in_specs=[pl.BlockSpec((r, nb, c), lambda j: (0, j, 0)),
                  pl.BlockSpec((r, nb, c), lambda j: (0, j, 0)),
                  _const_spec(m1.shape)],
        out_specs=pl.BlockSpec((2 * r, nb, c), lambda j: (0, j, 0)),
        compiler_params=_params(("arbitrary",)),
        name="fft_stage1",
    )(p3, q3, m1)
    kb = 16
    ca, sa, cb, sb = tw2
    out = pl.pallas_call(
        functools.partial(_fft2_kernel, kb=kb),
        out_shape=jax.ShapeDtypeStruct((r, r, c), BF16),
        grid=(r // kb,),
        in_specs=[pl.BlockSpec((kb, r, c), lambda j: (j, 0, 0)),
                  pl.BlockSpec((kb, r, c), lambda j: (r // kb + j, 0, 0)),
                  pl.BlockSpec((kb, r), lambda j: (j, 0)),
                  pl.BlockSpec((kb, r), lambda j: (j, 0)),
                  _const_spec(cb.shape), _const_spec(sb.shape)],
        out_specs=pl.BlockSpec((r, kb, c), lambda j: (0, j, 0)),
        scratch_shapes=[pltpu.VMEM((r, kb, c), F32)],
        compiler_params=_params(("arbitrary",)),
        name="fft_stage2",
    )(y1, y1, ca, sa, cb, sb)
    return out.reshape(l, c)


def _fnet_ctx(pc, qc, mc):
    return pl.pallas_call(
        _fft_ctx_kernel,
        out_shape=jax.ShapeDtypeStruct(pc.shape, BF16),
        grid=(1,),
        in_specs=[_whole_spec(pc.shape), _whole_spec(qc.shape), _const_spec(mc.shape)],
        out_specs=_whole_spec(pc.shape),
        compiler_params=_params(("arbitrary",)),
        name="fft_ctx",
    )(pc, qc, mc)


def _pos_tables(l, d):
    quarter = d // 4
    omega = 1.0 / (POS_BASE ** (jnp.arange(quarter, dtype=F32) / quarter))

    def enc(pv):
        ang = pv[:, None] * omega[None, :]
        return jnp.concatenate([jnp.sin(ang), jnp.cos(ang)], axis=-1)

    renc = enc(jnp.arange(l // GRID_W, dtype=F32))
    cenc = jnp.tile(enc(jnp.arange(GRID_W, dtype=F32)), (TM // GRID_W, 1))
    return renc, cenc


def kernel(x, c, ctx, c_ctx, w_ada, b_ada, norm_g, ffn1_gu, ffn1_down, ffn2_gu, ffn2_down, w_in, w_out,
           ssm_lam_re, ssm_lam_im, ssm_log_dt, ssm_b_re, ssm_b_im, ssm_c_re, ssm_c_im, ssm_d, w_glu, w_fmix):
    bsz, l, d = x.shape
    lc = ctx.shape[1]
    depth = w_ada.shape[0]
    d_ssm = w_glu.shape[1]
    head = w_fmix.shape[2]
    assert bsz == 1 and l % TM == 0 and TM % GRID_W == 0
    seglen = (l // CHUNK) // NSEGS

    mods = _modulation(c_ctx, c, w_ada, b_ada)
    pos = _pos_tables(l, d)
    m1, tw2, mc = _fnet_tables(l, lc)
    kc = np.arange(head)
    thc = 2.0 * np.pi * (np.outer(kc, kc) % head) / head
    cs = jnp.asarray(np.concatenate([np.cos(thc), np.sin(thc)], axis=1) / math.sqrt(head), F32).astype(BF16)
    win, wout, wglu, wfm = (a.astype(BF16) for a in (w_in, w_out, w_glu, w_fmix))
    wgu1, wd1 = ffn1_gu[0].astype(BF16), ffn1_down[0].astype(BF16)

    ops = _s5_operators((ssm_lam_re, ssm_lam_im, ssm_log_dt, ssm_b_re, ssm_b_im, ssm_c_re, ssm_c_im, ssm_d), seglen)

    xl, xc = x[0], ctx[0]
    for li in range(depth):
        last = li == depth - 1
        xl, u, p, q, xc, uc, pc, qc, wgu2, wd2 = _layer_in(xl, xc, pos if li == 0 else None, li, mods, norm_g, wgu1, wd1,
                                                           win, cs, (ffn2_gu, ffn2_down), d_ssm=d_ssm)
        ys, ysc = _s5(u, uc, li, ops)
        yf = _fnet_lat(p, q, m1, tw2)
        ctx_in = None if last else (xc, ysc, _fnet_ctx(pc, qc, mc))
        xl, xc, wgu1, wd1 = _layer_out((xl, ys, yf), ctx_in, li, mods, norm_g, wglu, wfm, wout, wgu2, wd2,
                                       (ffn1_gu, ffn1_down))
    return xl[None]
```

```python
import functools
import math

import numpy as np
import jax
import jax.numpy as jnp
from jax import lax
from jax.experimental import pallas as pl
from jax.experimental.pallas import tpu as pltpu

F32 = jnp.float32
BF16 = jnp.bfloat16

LANE = 128
SUBLANE = 8
VMEM_LIMIT = 56 * 1024 * 1024

EPS = 1e-6
GRID_W = 64
POS_BASE = 10000.0
N_MOD = 9
N_FNET_HEADS = 4
CHUNK = 16
TM = 512
CTX, LAT = 0, 1


def _dot(a, b):
    return jnp.dot(a, b, preferred_element_type=F32)


def _params(sem=None):
    return pltpu.CompilerParams(dimension_semantics=sem, vmem_limit_bytes=VMEM_LIMIT)


def _const_spec(shape):
    nd = len(shape)
    return pl.BlockSpec(shape, lambda *_: (0,) * nd, pipeline_mode=pl.Buffered(1))


def _rms(x, g):
    ms = jnp.mean(x * x, axis=-1, keepdims=True)
    return x * lax.rsqrt(ms + EPS) * g


def _pre(x, g, mod_ref, who, k):
    return _rms(x, g) * (1.0 + mod_ref[who, 3 * k + 1:3 * k + 2, :]) + mod_ref[who, 3 * k:3 * k + 1, :]


def _post(x, y, g, mod_ref, who, k, weight):
    return x + (weight * mod_ref[who, 3 * k + 2:3 * k + 3, :]) * _rms(y, g)


def _on_last_step(fn):
    pl.when(pl.program_id(0) == pl.num_programs(0) - 1)(fn)


def _sigmoid(x):
    return 1.0 / (1.0 + jnp.exp(-x))


def _gelu_tanh(x):
    return 0.5 * x * (1.0 + jnp.tanh(math.sqrt(2.0 / math.pi) * (x + 0.044715 * (x * x * x))))


def _mod_kernel(cb_ref, w_ref, b_ref, o_ref, s_ref, *, tn):
    rows, n_out = w_ref.shape[1], w_ref.shape[2]
    nb = tn // LANE
    j = pl.program_id(1)

    @pl.when((pl.program_id(0) == 0) & (j == 0))
    def _():
        cv = cb_ref[...]
        s_ref[...] = cv * _sigmoid(cv)

    @pl.when(j == 0)
    def _():
        o_ref[0] = jnp.broadcast_to(b_ref[0], (2, n_out))

    for grp in range(n_out // tn):
        def body(t, accs):
            d0 = pl.multiple_of(t * SUBLANE, SUBLANE)
            s = [s_ref[r, pl.ds(j * rows + d0, SUBLANE), :] for r in range(2)]
            new = list(accs)
            for jj in range(nb):
                w = w_ref[0, pl.ds(d0, SUBLANE), grp * tn + jj * LANE:grp * tn + (jj + 1) * LANE]
                for r in range(2):
                    new[r * nb + jj] = accs[r * nb + jj] + w * s[r]
            return tuple(new)

        init = tuple(jnp.zeros((SUBLANE, LANE), F32) for _ in range(2 * nb))
        accs = lax.fori_loop(0, rows // SUBLANE, body, init, unroll=4)
        for r in range(2):
            for jj in range(nb):
                lanes = slice(grp * tn + jj * LANE, grp * tn + (jj + 1) * LANE)
                o_ref[0, r:r + 1, lanes] = o_ref[0, r:r + 1, lanes] + jnp.sum(accs[r * nb + jj], axis=0, keepdims=True)


def _modulation(c_ctx, c, w_ada, b_ada):
    depth, d_model, n_out = w_ada.shape
    tn = 9 * LANE
    rows = 256
    assert n_out % tn == 0 and d_model % rows == 0
    cb = jnp.broadcast_to(jnp.stack([c_ctx, c[0]])[:, :, None], (2, d_model, LANE))
    out = pl.pallas_call(
        functools.partial(_mod_kernel, tn=tn),
        out_shape=jax.ShapeDtypeStruct((depth, 2, n_out), F32),
        grid=(depth, d_model // rows),
        in_specs=[
            pl.BlockSpec((2, d_model, LANE), lambda l, j: (0, 0, 0)),
            pl.BlockSpec((1, rows, n_out), lambda l, j: (l, j, 0)),
            pl.BlockSpec((1, 1, n_out), lambda l, j: (l, 0, 0)),
        ],
        out_specs=pl.BlockSpec((1, 2, n_out), lambda l, j: (l, 0, 0)),
        scratch_shapes=[pltpu.VMEM((2, d_model, LANE), F32)],
        compiler_params=_params(("arbitrary", "arbitrary")),
        name="adaln_mod",
    )(cb, w_ada, b_ada.reshape(depth, 1, n_out))
    return out.reshape(depth, 2, N_MOD, d_model)


SPLIT = 2


def _row_chunks(n, parts):
    step = n // parts
    return [slice(i * step, (i + 1) * step) for i in range(parts)]


def _ffn_stage(xs, who, mod_ref, g_ref, wgu_ref, wd_ref, k, gi):
    f = wd_ref.shape[0]
    hs = [_pre(x, g_ref[gi:gi + 1, :], mod_ref, who, k).astype(BF16) for x in xs]
    gus = [_dot(h, wgu_ref[...]) for h in hs]
    acts = [(gu[:, :f] * _sigmoid(gu[:, :f]) * gu[:, f:]).astype(BF16) for gu in gus]
    ys = [_dot(a, wd_ref[...]) for a in acts]
    return [_post(x, y, g_ref[gi + 1:gi + 2, :], mod_ref, who, k, 0.5) for x, y in zip(xs, ys)]


def _inproj_stage(xs, rows, who, mod_ref, g_ref, win_ref, cs_ref, u_ref, p_ref, q_ref):
    d_ssm = u_ref.shape[1]
    head = cs_ref.shape[0]
    hs = [_pre(x, g_ref[2:3, :], mod_ref, who, 1).astype(BF16) for x in xs]
    hhs = [_dot(h, win_ref[...]) for h in hs]
    for r, hh in zip(rows, hhs):
        u_ref[r, :] = hh[:, :d_ssm].astype(BF16)
        for n in range(N_FNET_HEADS):
            lo = d_ssm + n * head
            pq = _dot(hh[:, lo:lo + head].astype(BF16), cs_ref[...])
            p_ref[r, n * head:(n + 1) * head] = pq[:, :head].astype(BF16)
            q_ref[r, n * head:(n + 1) * head] = pq[:, head:].astype(BF16)


def _outproj_stage(xs, yss, yfs, who, mod_ref, g_ref, wglu_ref, wfm_ref, wout_ref):
    head = wfm_ref.shape[1]
    hs = [_gelu_tanh(ys.astype(F32)) for ys in yss]
    gls = [_dot(h.astype(BF16), wglu_ref[...]) for h in hs]
    mixed = []
    for h, gl, yf in zip(hs, gls, yfs):
        yf = yf.astype(BF16)
        parts = [(h * _sigmoid(gl)).astype(BF16)]
        parts += [_dot(yf[:, n * head:(n + 1) * head], wfm_ref[n]).astype(BF16) for n in range(N_FNET_HEADS)]
        mixed.append(jnp.concatenate(parts, axis=1))
    ys = [_dot(m, wout_ref[...]) for m in mixed]
    return [_post(x, y, g_ref[3:4, :], mod_ref, who, 1, 1.0) for x, y in zip(xs, ys)]


def _cast_chunks(src_refs, dst_refs):
    for src_ref, dst_ref in zip(src_refs, dst_refs):
        dst_ref[...] = src_ref[...].astype(BF16)


def _layer_in_kernel(*refs, first):
    if first:
        (x_ref, xc_ref, renc_ref, cenc_ref, mod_ref, g_ref, wgu_ref, wd_ref, win_ref, cs_ref, ng_ref, nd_ref,
         o_ref, u_ref, p_ref, q_ref, oc_ref, uc_ref, pc_ref, qc_ref, ngo_ref, ndo_ref, xs_ref) = refs
        tm = x_ref.shape[0]
        half = renc_ref.shape[1]
        r0 = pl.program_id(0) * (tm // GRID_W)
        xs_ref[:, half:] = x_ref[:, half:] + cenc_ref[...]
        for q in range(tm // GRID_W):
            xs_ref[q * GRID_W:(q + 1) * GRID_W, :half] = (
                x_ref[q * GRID_W:(q + 1) * GRID_W, :half] + renc_ref[pl.ds(r0 + q, 1), :])
        src_ref = xs_ref
    else:
        (x_ref, xc_ref, mod_ref, g_ref, wgu_ref, wd_ref, win_ref, cs_ref, ng_ref, nd_ref,
         o_ref, u_ref, p_ref, q_ref, oc_ref, uc_ref, pc_ref, qc_ref, ngo_ref, ndo_ref) = refs
        src_ref = x_ref
    _cast_chunks((ng_ref, nd_ref), (ngo_ref, ndo_ref))
    rows = _row_chunks(x_ref.shape[0], SPLIT)
    x1 = _ffn_stage([src_ref[r, :] for r in rows], LAT, mod_ref, g_ref, wgu_ref, wd_ref, 0, 0)
    for r, v in zip(rows, x1):
        o_ref[r, :] = v
    _inproj_stage(x1, rows, LAT, mod_ref, g_ref, win_ref, cs_ref, u_ref, p_ref, q_ref)

    def _():
        rows_c = _row_chunks(xc_ref.shape[0], 1)
        xc1 = _ffn_stage([xc_ref[...]], CTX, mod_ref, g_ref, wgu_ref, wd_ref, 0, 0)
        oc_ref[...] = xc1[0]
        _inproj_stage(xc1, rows_c, CTX, mod_ref, g_ref, win_ref, cs_ref, uc_ref, pc_ref, qc_ref)
    _on_last_step(_)


def _layer_out_kernel(*refs, with_ctx):
    if with_ctx:
        (x_ref, ys_ref, yf_ref, xc_ref, ysc_ref, yfc_ref, mod_ref, g_ref, wglu_ref, wfm_ref, wout_ref,
         wgu_ref, wd_ref, ng_ref, nd_ref, o_ref, oc_ref, ngo_ref, ndo_ref) = refs
        _cast_chunks((ng_ref, nd_ref), (ngo_ref, ndo_ref))
    else:
        x_ref, ys_ref, yf_ref, mod_ref, g_ref, wglu_ref, wfm_ref, wout_ref, wgu_ref, wd_ref, o_ref = refs
    rows = _row_chunks(x_ref.shape[0], SPLIT)
    x2 = _outproj_stage([x_ref[r, :] for r in rows], [ys_ref[r, :] for r in rows], [yf_ref[r, :] for r in rows],
                        LAT, mod_ref, g_ref, wglu_ref, wfm_ref, wout_ref)
    x3 = _ffn_stage(x2, LAT, mod_ref, g_ref, wgu_ref, wd_ref, 2, 4)
    for r, v in zip(rows, x3):
        o_ref[r, :] = v
    if with_ctx:
        def _():
            xc2 = _outproj_stage([xc_ref[...]], [ysc_ref[...]], [yfc_ref[...]], CTX, mod_ref, g_ref, wglu_ref,
                                 wfm_ref, wout_ref)
            oc_ref[...] = _ffn_stage(xc2, CTX, mod_ref, g_ref, wgu_ref, wd_ref, 2, 4)[0]
        _on_last_step(_)


def _tile_spec(cols):
    return pl.BlockSpec((TM, cols), lambda i: (i, 0))


def _whole_spec(shape):
    nd = len(shape)
    return pl.BlockSpec(shape, lambda *_: (0,) * nd)


def _layer_spec(arr, li):
    nd = arr.ndim - 1
    return pl.BlockSpec((None,) + arr.shape[1:], lambda *_: (li,) + (0,) * nd, pipeline_mode=pl.Buffered(1))


def _cast_specs(arr, li, nt):
    rows, cols = arr.shape[1:]
    k = 1
    while (rows * k) % nt or (rows * k // nt) % 16:
        k *= 2
    rc = rows * k // nt
    assert rows % rc == 0 and nt % k == 0
    return (pl.BlockSpec((None, rc, cols), lambda i: (li, i // k, 0)), pl.BlockSpec((rc, cols), lambda i: (i // k, 0)),
            jax.ShapeDtypeStruct((rows, cols), BF16))


def _layer_in(xl, xc, pos, li, mods, gains, wgu, wd, win, cs, nxt, *, d_ssm):
    l, d = xl.shape
    lc = xc.shape[0]
    first = pos is not None
    d_f = win.shape[2] - d_ssm
    widths = (d, d_ssm, d_f, d_f)
    dtypes = (F32, BF16, BF16, BF16)
    stacked = [mods, gains]
    nt = l // TM
    casts = [_cast_specs(a, li, nt) for a in nxt]
    return pl.pallas_call(
        functools.partial(_layer_in_kernel, first=first),
        out_shape=[jax.ShapeDtypeStruct((l, wd_), dt_) for wd_, dt_ in zip(widths, dtypes)]
        + [jax.ShapeDtypeStruct((lc, wd_), dt_) for wd_, dt_ in zip(widths, dtypes)] + [c[2] for c in casts],
        grid=(nt,),
        in_specs=([_tile_spec(d), _whole_spec(xc.shape)] + ([_const_spec(a.shape) for a in pos] if first else [])
                  + [_layer_spec(a, li) for a in stacked] + [_const_spec(wgu.shape), _const_spec(wd.shape)]
                  + [_layer_spec(win, li), _const_spec(cs.shape)] + [c[0] for c in casts]),
        out_specs=([_tile_spec(wd_) for wd_ in widths] + [_whole_spec((lc, wd_)) for wd_ in widths]
                   + [c[1] for c in casts]),
        scratch_shapes=[pltpu.VMEM((TM, d), F32)] if first else [],
        compiler_params=_params(("arbitrary",)),
        name="layer_in",
    )(xl, xc, *(pos if first else ()), *stacked, wgu, wd, win, cs, *nxt)


def _layer_out(lat, ctx, li, mods, gains, wglu, wfm, wout, wgu, wd, nxt):
    xl, ys, yf = lat
    l, d = xl.shape
    with_ctx = ctx is not None
    ctx = list(ctx) if with_ctx else []
    stacked = [mods, gains, wglu, wfm, wout]
    nt = l // TM
    casts = [_cast_specs(a, li + 1, nt) for a in nxt] if with_ctx else []
    outs = pl.pallas_call(
        functools.partial(_layer_out_kernel, with_ctx=with_ctx),
        out_shape=([jax.ShapeDtypeStruct(xl.shape, F32)] + ([jax.ShapeDtypeStruct(ctx[0].shape, F32)] if with_ctx else [])
                   + [c[2] for c in casts]),
        grid=(nt,),
        in_specs=([_tile_spec(d), _tile_spec(ys.shape[1]), _tile_spec(yf.shape[1])]
                  + [_whole_spec(a.shape) for a in ctx] + [_layer_spec(a, li) for a in stacked]
                  + [_const_spec(wgu.shape), _const_spec(wd.shape)] + [c[0] for c in casts]),
        out_specs=([_tile_spec(d)] + ([_whole_spec(ctx[0].shape)] if with_ctx else []) + [c[1] for c in casts]),
        compiler_params=_params(("arbitrary",)),
        name="layer_out",
    )(xl, ys, yf, *ctx, *stacked, wgu, wd, *(nxt if with_ctx else ()))
    return tuple(outs) if with_ctx else (outs[0], None, None, None)


NSEGS = 16
PAIRS = 4


def _cmul_add(ar, ai, xr, xi, zr, zi):
    return ar * xr - ai * xi + zr, ar * xi + ai * xr + zi


def _dot_nt(a, b):
    def nt(x, y):
        return lax.dot_general(x, y, (((1,), (1,)), ((), ())), preferred_element_type=F32)
    a_hi, b_hi = a.astype(BF16), b.astype(BF16)
    a_lo, b_lo = (a - a_hi.astype(F32)).astype(BF16), (b - b_hi.astype(F32)).astype(BF16)
    return nt(a_hi, b_hi) + (nt(a_hi, b_lo) + nt(a_lo, b_hi))


def _s5_taps_kernel(e_re_ref, e_im_ref, ct_re_ref, ct_im_ref, kr_ref):
    t = CHUNK
    nh = ct_re_ref.shape[1]
    g8 = LANE // nh
    gw = e_re_ref.shape[3] // g8
    same = (lax.broadcasted_iota(jnp.int32, (LANE, g8 * gw), 0) // nh
            == lax.broadcasted_iota(jnp.int32, (LANE, g8 * gw), 1) // gw)
    taps = []
    for d in range(2):
        cre = jnp.where(same, jnp.concatenate([ct_re_ref[d]] * g8, axis=0), 0.0)
        cim = jnp.where(same, jnp.concatenate([ct_im_ref[d]] * g8, axis=0), 0.0)
        er = jnp.concatenate([e_re_ref[m, d] for m in range(t)], axis=0)
        ei = jnp.concatenate([e_im_ref[m, d] for m in range(t)], axis=0)
        taps.append(_dot_nt(er, cre) - _dot_nt(ei, cim))
    blk = (lax.broadcasted_iota(jnp.int32, (LANE, LANE), 0) // nh
           == lax.broadcasted_iota(jnp.int32, (LANE, LANE), 1) // nh)
    for lag in range(-(t - 1), t):
        if lag > 0:
            src = taps[0][lag * nh:(lag + 1) * nh]
        elif lag < 0:
            src = taps[1][-lag * nh:(1 - lag) * nh]
        else:
            src = taps[0][:nh] + taps[1][:nh]
        kr_ref[0, lag + t - 1] = jnp.where(blk, jnp.concatenate([src] * g8, axis=0), 0.0).astype(BF16)


def _s5_state_kernel(u_ref, uc_ref, e_re_ref, e_im_ref, mu_ref, d_ref, xp_ref, wz_ref, z_ref, xs_ref, *, seglen, ncc):
    w = LANE
    nc = d_ref.shape[1]
    nlat = NSEGS * seglen
    nv = NSEGS // SUBLANE
    nh = e_re_ref.shape[2]
    ns = w // 2

    @pl.when(pl.program_id(1) == 0)
    def _():
        def fold(i, carry):
            r0 = pl.multiple_of(i * NSEGS, NSEGS)
            tiles = [u_ref[pl.ds(pl.multiple_of((s * seglen + i) * CHUNK, CHUNK), CHUNK), :] for s in range(NSEGS)]
            d_ref[0, pl.ds(r0, NSEGS), :] = jnp.concatenate(tiles, axis=0).reshape(NSEGS, CHUNK * w)
            return carry

        lax.fori_loop(0, seglen, fold, 0)
        d_ref[0, nlat:nlat + ncc, :] = uc_ref[...].reshape(ncc, CHUNK * w)

    wz_ref[...] = jnp.zeros(wz_ref.shape, BF16)
    first = (lax.broadcasted_iota(jnp.int32, (nh, 4 * w), 1) % w) < ns
    for pb in range(PAIRS):
        pair = pl.program_id(1) * PAIRS + pb
        lanes = slice(pb * w, (pb + 1) * w)
        for i in range(CHUNK):
            ez = jnp.concatenate([e_re_ref[CHUNK - 1 - i, 0, :, lanes], e_im_ref[CHUNK - 1 - i, 0, :, lanes],
                                  e_re_ref[i, 1, :, lanes], e_im_ref[i, 1, :, lanes]], axis=1).astype(BF16)
            both = jnp.concatenate([jnp.where(first, ez, jnp.zeros_like(ez)),
                                    jnp.where(first, jnp.zeros_like(ez), ez)], axis=0)
            r0 = pl.multiple_of(i * w + pair * 2 * nh, 2 * nh)
            wz_ref[pl.ds(r0, 2 * nh), pb * 4 * w:(pb + 1) * 4 * w] = both

    for pb in range(PAIRS):
        zz = _dot(d_ref[0], wz_ref[:, pb * 4 * w:(pb + 1) * 4 * w])
        for c in range(4):
            z_ref[pb * 4 + c] = zz[:, c * w:(c + 1) * w]

    mu = [[mu_ref[0, pb, 0, c:c + 1, :] for c in range(4)] for pb in range(PAIRS)]
    mun = [[mu_ref[0, pb, 1, c:c + 1, :] for c in range(4)] for pb in range(PAIRS)]

    pre = []
    for pb in range(PAIRS):
        zc = [z_ref[pb * 4 + c, nlat:nlat + ncc, :] for c in range(4)]
        sf = (jnp.zeros((1, w), F32), jnp.zeros((1, w), F32))
        sb = (jnp.zeros((1, w), F32), jnp.zeros((1, w), F32))
        for j in range(ncc):
            jb = ncc - 1 - j
            xs_ref[pb * 4 + 0, nlat + j:nlat + j + 1, :] = sf[0]
            xs_ref[pb * 4 + 1, nlat + j:nlat + j + 1, :] = sf[1]
            xs_ref[pb * 4 + 2, nlat + jb:nlat + jb + 1, :] = sb[0]
            xs_ref[pb * 4 + 3, nlat + jb:nlat + jb + 1, :] = sb[1]
            sf = _cmul_add(mu[pb][0], mu[pb][1], sf[0], sf[1], zc[0][j:j + 1], zc[1][j:j + 1])
            sb = _cmul_add(mu[pb][2], mu[pb][3], sb[0], sb[1], zc[2][jb:jb + 1], zc[3][jb:jb + 1])
        pre.append((sf, sb))

    mub = [[jnp.broadcast_to(m, (SUBLANE, w)) for m in mu[pb]] for pb in range(PAIRS)]

    def rows(i, v):
        return pl.ds(pl.multiple_of(i * NSEGS + v * SUBLANE, SUBLANE), SUBLANE)

    def step(i, st):
        ib = seglen - 1 - i
        new = []
        for pb in range(PAIRS):
            for v in range(nv):
                k = (pb * nv + v) * 4
                m = mub[pb]
                fr, fi = _cmul_add(m[0], m[1], st[k], st[k + 1],
                                   z_ref[pb * 4 + 0, rows(i, v), :], z_ref[pb * 4 + 1, rows(i, v), :])
                br, bi = _cmul_add(m[2], m[3], st[k + 2], st[k + 3],
                                   z_ref[pb * 4 + 2, rows(ib, v), :], z_ref[pb * 4 + 3, rows(ib, v), :])
                new += [fr, fi, br, bi]
        return tuple(new)

    zero = jnp.zeros((SUBLANE, w), F32)
    fin = lax.fori_loop(0, seglen, step, (zero,) * (PAIRS * nv * 4))

    carry = []
    for pb in range(PAIRS):
        cf, cb = pre[pb]
        rows_f, rows_b = [None] * NSEGS, [None] * NSEGS
        for s in range(NSEGS):
            sr = NSEGS - 1 - s
            rows_f[s] = cf
            rows_b[sr] = cb
            kf = (pb * nv + s // SUBLANE) * 4
            kb = (pb * nv + sr // SUBLANE) * 4
            sl, srl = s % SUBLANE, sr % SUBLANE
            cf = _cmul_add(mun[pb][0], mun[pb][1], cf[0], cf[1], fin[kf][sl:sl + 1, :], fin[kf + 1][sl:sl + 1, :])
            cb = _cmul_add(mun[pb][2], mun[pb][3], cb[0], cb[1],
                           fin[kb + 2][srl:srl + 1, :], fin[kb + 3][srl:srl + 1, :])
        for v in range(nv):
            seg = slice(v * SUBLANE, (v + 1) * SUBLANE)
            carry += [jnp.concatenate([r[0] for r in rows_f[seg]], axis=0),
                      jnp.concatenate([r[1] for r in rows_f[seg]], axis=0),
                      jnp.concatenate([r[0] for r in rows_b[seg]], axis=0),
                      jnp.concatenate([r[1] for r in rows_b[seg]], axis=0)]

    def step2(i, st):
        ib = seglen - 1 - i
        for pb in range(PAIRS):
            for v in range(nv):
                k = (pb * nv + v) * 4
                xs_ref[pb * 4 + 0, rows(i, v), :] = st[k]
                xs_ref[pb * 4 + 1, rows(i, v), :] = st[k + 1]
                xs_ref[pb * 4 + 2, rows(ib, v), :] = st[k + 2]
                xs_ref[pb * 4 + 3, rows(ib, v), :] = st[k + 3]
        return step(i, st)

    lax.fori_loop(0, seglen, step2, tuple(carry))
    for k in range(PAIRS * 4):
        xp_ref[0, :, k * w:(k + 1) * w] = xs_ref[k].astype(BF16)


def _s5_readout_kernel(d_ref, xp_ref, kr_ref, syt_ref, rm_ref, dsk_ref, y_ref, yc_ref, bt_ref, wy_ref, yb_ref,
                       sy_ref, *, seglen):
    w = LANE
    cw = 4 * w
    fw = CHUNK * w
    ncg = fw // cw

    @pl.when(pl.program_id(1) == 0)
    def _():
        for i in range(CHUNK):
            for j in range(CHUNK):
                bt_ref[i * w:(i + 1) * w, j * w:(j + 1) * w] = kr_ref[0, j - i + CHUNK - 1]
        for b in range(syt_ref.shape[3] // w):
            for dc in range(4):
                r0 = (b * 4 + dc) * w
                sy_ref[r0:r0 + w, :] = syt_ref[dc // 2, dc % 2, :, b * w:(b + 1) * w].T.astype(BF16)
        nrow = sy_ref.shape[0]
        nh = rm_ref.shape[0] // CHUNK
        row = lax.broadcasted_iota(jnp.int32, (nrow, cw), 0)
        row_g = 2 * (row // cw) + (row % w) // (w // 2)
        col_g = (lax.broadcasted_iota(jnp.int32, (nrow, cw), 1) % w) // nh
        same = row_g == col_g
        for cg in range(ncg):
            cols = slice(cg * cw, (cg + 1) * cw)
            wy_ref[:, cols] = jnp.where(same, _dot(sy_ref[...], rm_ref[:, cols]), 0.0).astype(BF16)

    nrows = d_ref.shape[1]
    for cg in range(ncg):
        cols = slice(cg * cw, (cg + 1) * cw)
        yb_ref[:, cols] = (_dot(d_ref[0], bt_ref[:, cols]) + _dot(xp_ref[0], wy_ref[:, cols])
                           + d_ref[0, :, cols].astype(F32) * dsk_ref[0, :, cols])

    ngrp = nrows // NSEGS

    def unfold(j, carry):
        gi = pl.program_id(1) * ngrp + j
        r0 = pl.multiple_of(j * NSEGS, NSEGS)
        tok = yb_ref[pl.ds(r0, NSEGS), :].astype(BF16).reshape(NSEGS * CHUNK, w)

        @pl.when(gi < seglen)
        def _():
            for s in range(NSEGS):
                t0 = pl.multiple_of((s * seglen + gi) * CHUNK, CHUNK)
                y_ref[pl.ds(t0, CHUNK), :] = tok[s * CHUNK:(s + 1) * CHUNK]

        @pl.when(gi >= seglen)
        def _():
            t0 = pl.multiple_of((gi - seglen) * NSEGS * CHUNK, NSEGS * CHUNK)
            yc_ref[pl.ds(t0, NSEGS * CHUNK), :] = tok

        return carry

    lax.fori_loop(0, ngrp, unfold, 0)


def _s5_tables(lam_re, lam_im, log_dt, b_re, b_im, c_re, c_im, d_skip, *, seglen):
    t = CHUNK
    ng, ns = lam_re.shape[1], lam_re.shape[2]
    nh = b_re.shape[-1]
    gp = ng * ns
    g8 = LANE // nh
    na = ng // g8
    nb = g8 // 2
    assert 2 * ns == LANE and nb % PAIRS == 0
    dt = jnp.exp(log_dt)[..., None]
    a, b = (lam_re * dt).reshape(2, gp), (lam_im * dt).reshape(2, gp)

    def lam_pow(m):
        mm = jnp.asarray(m, F32).reshape(-1, 1, 1)
        mag = jnp.exp(a[None] * mm)
        return mag * jnp.cos(b[None] * mm), mag * jnp.sin(b[None] * mm)

    pr, pi = lam_pow(np.arange(t + 1))
    lr, li = lam_re.reshape(2, gp), lam_im.reshape(2, gp)
    nr, ni = pr[1] - 1.0, pi[1]
    den = lr * lr + li * li
    qr, qi = (nr * lr + ni * li) / den, (ni * lr - nr * li) / den
    bt_re = b_re.transpose(0, 3, 1, 2).reshape(2, nh, gp)
    bt_im = b_im.transpose(0, 3, 1, 2).reshape(2, nh, gp)
    bb_re = qr[:, None] * bt_re - qi[:, None] * bt_im
    bb_im = qr[:, None] * bt_im + qi[:, None] * bt_re
    e_re = pr[:t, :, None] * bb_re[None] - pi[:t, :, None] * bb_im[None]
    e_im = pr[:t, :, None] * bb_im[None] + pi[:t, :, None] * bb_re[None]
    ct_re = c_re.transpose(0, 2, 1, 3).reshape(2, nh, gp)
    ct_im = c_im.transpose(0, 2, 1, 3).reshape(2, nh, gp)

    sl = slice(1, t + 1)
    pw_re = jnp.stack([pr[sl, 0], pr[sl, 1][::-1]])[:, :, None]
    pw_im = jnp.stack([pi[sl, 0], pi[sl, 1][::-1]])[:, :, None]
    wy_re = ct_re[:, None] * pw_re - ct_im[:, None] * pw_im
    wy_im = ct_re[:, None] * pw_im + ct_im[:, None] * pw_re
    syt = jnp.stack([wy_re, -wy_im], axis=1).reshape(2, 2, t * nh, gp)
    mr, mi = lam_pow([t, t * seglen])
    mu = jnp.stack([mr, mi], axis=2).reshape(2, 4, na, nb, LANE).transpose(2, 3, 0, 1, 4)
    dsk = jnp.tile(d_skip.reshape(na, 1, LANE), (1, 1, t))
    return e_re, e_im, ct_re, ct_im, syt, mu, dsk


def _s5_operators(params, seglen):
    e_re, e_im, ct_re, ct_im, syt, mu, dsk = jax.vmap(functools.partial(_s5_tables, seglen=seglen))(*params)
    depth, t, _, nh, gp = e_re.shape
    na = mu.shape[1]
    tw = gp // na
    nlag = 2 * CHUNK - 1
    kr = pl.pallas_call(
        _s5_taps_kernel,
        out_shape=jax.ShapeDtypeStruct((depth, na, nlag, LANE, LANE), BF16),
        grid=(depth, na),
        in_specs=[pl.BlockSpec((None, t, 2, nh, tw), lambda l, i: (l, 0, 0, 0, i)),
                  pl.BlockSpec((None, t, 2, nh, tw), lambda l, i: (l, 0, 0, 0, i)),
                  pl.BlockSpec((None, 2, nh, tw), lambda l, i: (l, 0, 0, i)),
                  pl.BlockSpec((None, 2, nh, tw), lambda l, i: (l, 0, 0, i))],
        out_specs=pl.BlockSpec((None, 1, nlag, LANE, LANE), lambda l, i: (l, i, 0, 0, 0)),
        compiler_params=_params(("arbitrary", "arbitrary")),
        name="s5_taps",
    )(e_re, e_im, ct_re, ct_im)
    r_mat = np.kron(np.eye(CHUNK, dtype=np.float32), np.tile(np.eye(nh, dtype=np.float32), (1, LANE // nh)))
    return kr, e_re, e_im, syt, jnp.asarray(r_mat, F32).astype(BF16), mu, dsk


def _s5(u, uc, li, ops):
    kr, e_re, e_im, syt, r_mat, mu, dsk = ops
    l, d_ssm = u.shape
    lc = uc.shape[0]
    na, nb = mu.shape[1], mu.shape[2]
    fw = CHUNK * LANE
    nlat, ncc = l // CHUNK, lc // CHUNK
    nc = nlat + ncc
    seglen = nlat // NSEGS
    sw = PAIRS * 4 * LANE
    assert seglen * NSEGS == nlat and ncc * CHUNK == lc
    d, xp = pl.pallas_call(
        functools.partial(_s5_state_kernel, seglen=seglen, ncc=ncc),
        out_shape=(jax.ShapeDtypeStruct((na, nc, fw), BF16), jax.ShapeDtypeStruct((na, nc, nb * 4 * LANE), BF16)),
        grid=(na, nb // PAIRS),
        in_specs=[
            pl.BlockSpec((l, LANE), lambda a, b: (0, a)),
            pl.BlockSpec((lc, LANE), lambda a, b: (0, a)),
            pl.BlockSpec((None,) + e_re.shape[1:4] + (PAIRS * LANE,), lambda a, b: (li, 0, 0, 0, a * (nb // PAIRS) + b)),
            pl.BlockSpec((None,) + e_im.shape[1:4] + (PAIRS * LANE,), lambda a, b: (li, 0, 0, 0, a * (nb // PAIRS) + b)),
            pl.BlockSpec((None, 1, PAIRS, 2, 4, LANE), lambda a, b: (li, a, b, 0, 0, 0)),
        ],
        out_specs=(pl.BlockSpec((1, nc, fw), lambda a, b: (a, 0, 0)),
                   pl.BlockSpec((1, nc, sw), lambda a, b: (a, 0, b))),
        scratch_shapes=[pltpu.VMEM((fw, sw), BF16),
                        pltpu.VMEM((PAIRS * 4, nc, LANE), F32), pltpu.VMEM((PAIRS * 4, nc, LANE), F32)],
        compiler_params=_params(("arbitrary", "arbitrary")),
        name="s5_state",
    )(u, uc, e_re, e_im, mu)
    rbs = max(r for r in range(NSEGS, min(nc, 512) + 1, NSEGS) if nc % r == 0)
    nrb = nc // rbs
    assert NSEGS % 16 == 0 and ncc % NSEGS == 0
    return pl.pallas_call(
        functools.partial(_s5_readout_kernel, seglen=seglen),
        out_shape=(jax.ShapeDtypeStruct((l, d_ssm), BF16), jax.ShapeDtypeStruct((lc, d_ssm), BF16)),
        grid=(na, nrb),
        in_specs=[
            pl.BlockSpec((1, rbs, fw), lambda a, r: (a, r, 0)),
            pl.BlockSpec((1, rbs, nb * 4 * LANE), lambda a, r: (a, r, 0)),
            pl.BlockSpec((None, 1) + kr.shape[2:], lambda a, r: (li, a, 0, 0, 0)),
            pl.BlockSpec((None,) + syt.shape[1:4] + (nb * LANE,), lambda a, r: (li, 0, 0, 0, a)),
            _const_spec(r_mat.shape),
            pl.BlockSpec((None, 1, 1, fw), lambda a, r: (li, a, 0, 0)),
        ],
        out_specs=(pl.BlockSpec((l, LANE), lambda a, r: (0, a)), pl.BlockSpec((lc, LANE), lambda a, r: (0, a))),
        scratch_shapes=[pltpu.VMEM((fw, fw), BF16), pltpu.VMEM((nb * 4 * LANE, fw), BF16),
                        pltpu.VMEM((rbs, fw), F32), pltpu.VMEM((nb * 4 * LANE, syt.shape[3]), BF16)],
        compiler_params=_params(("arbitrary", "arbitrary")),
        name="s5_readout",
    )(d, xp, kr, syt, r_mat, dsk)


def _fft1_kernel(p_ref, q_ref, m_ref, o_ref):
    r, nb, c = p_ref.shape
    p2 = p_ref[...].reshape(r * nb, c).reshape(r, nb * c)
    q2 = q_ref[...].reshape(r * nb, c).reshape(r, nb * c)
    y = _dot(m_ref[...], jnp.concatenate([p2, q2], axis=0)).astype(BF16)
    o_ref[...] = y.reshape(2 * r * nb, c).reshape(2 * r, nb, c)


def _fft2_kernel(yr_ref, yi_ref, ca_ref, sa_ref, cb_ref, sb_ref, o_ref, *, kb):
    cb, sb = cb_ref[...], sb_ref[...]
    r, _, c = o_ref.shape
    outs = []
    for k in range(kb):
        ca, sa = ca_ref[k:k + 1, :], sa_ref[k:k + 1, :]
        g = jnp.concatenate([ca * cb - sa * sb, sa * cb + ca * sb], axis=1).astype(BF16)
        rhs = jnp.concatenate([yr_ref[k], yi_ref[k]], axis=0)
        outs.append(_dot(g, rhs).astype(BF16))
    o_ref[...] = jnp.concatenate(outs, axis=1).reshape(r * kb, c).reshape(r, kb, c)


def _fft_ctx_kernel(p_ref, q_ref, m_ref, o_ref):
    rhs = jnp.concatenate([p_ref[...], q_ref[...]], axis=0)
    o_ref[...] = _dot(m_ref[...], rhs).astype(BF16)


def _fnet_tables(l, lc):
    r = math.isqrt(l)
    assert r * r == l
    k = np.arange(r)
    th = 2.0 * np.pi * (np.outer(k, k) % r) / r
    c, s = np.cos(th), np.sin(th)
    m1 = np.block([[c, -s], [-s, -c]])
    ta = 2.0 * np.pi * np.outer(k, k) / l
    scale = 1.0 / math.sqrt(l)
    tw2 = tuple(jnp.asarray(v, F32) for v in (np.cos(ta), np.sin(ta), c * scale, s * scale))
    kc = np.arange(lc)
    thc = 2.0 * np.pi * (np.outer(kc, kc) % lc) / lc
    mc = np.concatenate([np.cos(thc), -np.sin(thc)], axis=1) / math.sqrt(lc)
    return (jnp.asarray(m1, F32).astype(BF16), tw2, jnp.asarray(mc, F32).astype(BF16))


def _fnet_lat(p, q, m1, tw2):
    l, c = p.shape
    r = m1.shape[0] // 2
    assert r * r == l
    p3 = p.reshape(r, r, c)
    q3 = q.reshape(r, r, c)
    nb = 16
    y1 = pl.pallas_call(
        _fft1_kernel,
        out_shape=jax.ShapeDtypeStruct((2 * r, r, c), BF16),
        grid=(r // nb,),
        in_specs=[pl.BlockSpec((r, nb, c), lambda j: (0, j, 0)),
                  pl.BlockSpec((r, nb, c), lambda j: (0, j, 0)),
                  _const_spec(m1.shape)],
        out_specs=pl.BlockSpec((2 * r, nb, c), lambda j: (0, j, 0)),
        compiler_params=_params(("arbitrary",)),
        name="fft_stage1",
    )(p3, q3, m1)
    kb = 16
    ca, sa, cb, sb = tw2
    out = pl.pallas_call(
        functools.partial(_fft2_kernel, kb=kb),
        out_shape=jax.ShapeDtypeStruct((r, r, c), BF16),
        grid=(r // kb,),
        in_specs=[pl.BlockSpec((kb, r, c), lambda j: (j, 0, 0)),
                  pl.BlockSpec((kb, r, c), lambda j: (r // kb + j, 0, 0)),
                  pl.BlockSpec((kb, r), lambda j: (j, 0)),
                  pl.BlockSpec((kb, r), lambda j: (j, 0)),
                  _const_spec(cb.shape), _const_spec(sb.shape)],
        out_specs=pl.BlockSpec((r, kb, c), lambda j: (0, j, 0)),
        compiler_params=_params(("arbitrary",)),
        name="fft_stage2",
    )(y1, y1, ca, sa, cb, sb)
    return out.reshape(l, c)


def _fnet_ctx(pc, qc, mc):
    return pl.pallas_call(
        _fft_ctx_kernel,
        out_shape=jax.ShapeDtypeStruct(pc.shape, BF16),
        grid=(1,),
        in_specs=[_whole_spec(pc.shape), _whole_spec(qc.shape), _const_spec(mc.shape)],
        out_specs=_whole_spec(pc.shape),
        compiler_params=_params(("arbitrary",)),
        name="fft_ctx",
    )(pc, qc, mc)


def _pos_tables(l, d):
    quarter = d // 4
    omega = 1.0 / (POS_BASE ** (jnp.arange(quarter, dtype=F32) / quarter))

    def enc(pv):
        ang = pv[:, None] * omega[None, :]
        return jnp.concatenate([jnp.sin(ang), jnp.cos(ang)], axis=-1)

    renc = enc(jnp.arange(l // GRID_W, dtype=F32))
    cenc = jnp.tile(enc(jnp.arange(GRID_W, dtype=F32)), (TM // GRID_W, 1))
    return renc, cenc


def kernel(x, c, ctx, c_ctx, w_ada, b_ada, norm_g, ffn1_gu, ffn1_down, ffn2_gu, ffn2_down, w_in, w_out,
           ssm_lam_re, ssm_lam_im, ssm_log_dt, ssm_b_re, ssm_b_im, ssm_c_re, ssm_c_im, ssm_d, w_glu, w_fmix):
    bsz, l, d = x.shape
    lc = ctx.shape[1]
    depth = w_ada.shape[0]
    d_ssm = w_glu.shape[1]
    head = w_fmix.shape[2]
    assert bsz == 1 and l % TM == 0 and TM % GRID_W == 0
    seglen = (l // CHUNK) // NSEGS

    mods = _modulation(c_ctx, c, w_ada, b_ada)
    pos = _pos_tables(l, d)
    m1, tw2, mc = _fnet_tables(l, lc)
    kc = np.arange(head)
    thc = 2.0 * np.pi * (np.outer(kc, kc) % head) / head
    cs = jnp.asarray(np.concatenate([np.cos(thc), np.sin(thc)], axis=1) / math.sqrt(head), F32).astype(BF16)
    win, wout, wglu, wfm = (a.astype(BF16) for a in (w_in, w_out, w_glu, w_fmix))
    wgu1, wd1 = ffn1_gu[0].astype(BF16), ffn1_down[0].astype(BF16)

    ops = _s5_operators((ssm_lam_re, ssm_lam_im, ssm_log_dt, ssm_b_re, ssm_b_im, ssm_c_re, ssm_c_im, ssm_d), seglen)

    xl, xc = x[0], ctx[0]
    for li in range(depth):
        last = li == depth - 1
        xl, u, p, q, xc, uc, pc, qc, wgu2, wd2 = _layer_in(xl, xc, pos if li == 0 else None, li, mods, norm_g, wgu1, wd1,
                                                           win, cs, (ffn2_gu, ffn2_down), d_ssm=d_ssm)
        ys, ysc = _s5(u, uc, li, ops)
        yf = _fnet_lat(p, q, m1, tw2)
        ctx_in = None if last else (xc, ysc, _fnet_ctx(pc, qc, mc))
        xl, xc, wgu1, wd1 = _layer_out((xl, ys, yf), ctx_in, li, mods, norm_g, wglu, wfm, wout, wgu2, wd2,
                                       (ffn1_gu, ffn1_down))
    return xl[None]
```
